```python
import math
import jax, jax.numpy as jnp
from jax import lax
import numpy as np

D_MODEL = 1024
BATCH = 4
SEQ = 4096
DEPTH = 4
DEC_BATCH = 32
DEC_SEQ = 1
PAST_LEN = 8192
PAGE_SIZE = 128

N_A = DEPTH // 2
N_B = DEPTH - N_A
GLA_HEADS = 4
GLA_DK = D_MODEL // 2 // GLA_HEADS
GLA_DV = (3 * D_MODEL // 4) // GLA_HEADS
GLA_QK = GLA_HEADS * GLA_DK
GLA_V = GLA_HEADS * GLA_DV
GLA_RANK = 16
GLA_TAU = 16.0
GLA_CHUNK = 64
HEAD_DIM = 64
DIL_KV_HEADS = 4
DIL_PAIRS = ((128, 1), (512, 4), (2048, 16))
N_GROUPS = len(DIL_PAIRS)
WIN_MAX = max(w for w, _ in DIL_PAIRS)
B_Q = N_GROUPS * DIL_KV_HEADS * HEAD_DIM
B_KV = DIL_KV_HEADS * HEAD_DIM
MEM_TOKENS = 256
MEM_HEADS = 4
MEM_Q = MEM_HEADS * HEAD_DIM
A_SPLITS = [GLA_QK, 2 * GLA_QK, 2 * GLA_QK + GLA_V, 2 * GLA_QK + 2 * GLA_V,
            2 * GLA_QK + 2 * GLA_V + GLA_RANK]
A_IN = 2 * GLA_QK + 2 * GLA_V + GLA_RANK + MEM_Q
A_OUT = GLA_V + MEM_Q
B_IN = B_Q + MEM_Q
B_OUT = B_KV + MEM_Q
D_FF = -(-8 * D_MODEL // (3 * 256)) * 256
ROPE_THETA = 10000.0
EPS = 1e-6
ATTN_SCALE = HEAD_DIM ** -0.5

kernel_name = "yoco_gla_dilated_window_decoder_step"


def rms_norm(x, g):
    xf = x.astype(jnp.float32)
    y = xf * lax.rsqrt(jnp.mean(xf * xf, axis=-1, keepdims=True) + EPS)
    return (y * g.astype(jnp.float32)).astype(x.dtype)


def rope(x, pos):
    half = x.shape[-1] // 2
    inv = ROPE_THETA ** (-jnp.arange(half, dtype=jnp.float32) / half)
    ang = pos.astype(jnp.float32)[:, None] * inv[None, :]
    cos = jnp.cos(ang)[None, :, None, :]
    sin = jnp.sin(ang)[None, :, None, :]
    xf = x.astype(jnp.float32)
    x1, x2 = xf[..., :half], xf[..., half:]
    return jnp.concatenate([x1 * cos - x2 * sin, x2 * cos + x1 * sin], axis=-1).astype(x.dtype)


def swiglu(x, w_gu, w_down):
    g, u = jnp.split(x @ w_gu, 2, axis=-1)
    return (jax.nn.silu(g) * u) @ w_down


def gla_chunked(q, k, v, log_a, h0):
    Bn, T, H, DK = q.shape
    DV = v.shape[-1]
    C = GLA_CHUNK if T >= GLA_CHUNK else T
    pad = (-T) % C
    q, k, v, log_a = [t.astype(jnp.float32) for t in (q, k, v, log_a)]
    if pad:
        pw = ((0, 0), (0, pad), (0, 0), (0, 0))
        q, k, v, log_a = [jnp.pad(t, pw) for t in (q, k, v, log_a)]
    nc = (T + pad) // C

    def to_chunks(t):
        return jnp.moveaxis(t.reshape(Bn, nc, C, *t.shape[2:]), 1, 0)

    causal = jnp.asarray(np.tril(np.ones((C, C), dtype=bool)))[None, :, :, None, None]

    def step(h, inp):
        qc, kc, vc, ac = inp
        b = jnp.cumsum(ac, axis=1)
        diff = b[:, :, None] - b[:, None, :]
        decay = jnp.exp(jnp.where(causal, diff, -jnp.inf))
        scores = jnp.einsum('bihd,bjhd,bijhd->bhij', qc, kc, decay)
        o = (jnp.einsum('bhij,bjhe->bihe', scores, vc)
             + jnp.einsum('bihd,bhde->bihe', qc * jnp.exp(b), h))
        b_last = b[:, -1]
        k_dec = kc * jnp.exp(b_last[:, None] - b)
        h_new = jnp.exp(b_last)[..., None] * h + jnp.einsum('bjhd,bjhe->bhde', k_dec, vc)
        return h_new, o

    h_T, o = lax.scan(step, h0.astype(jnp.float32), tuple(to_chunks(t) for t in (q, k, v, log_a)))
    o = jnp.moveaxis(o, 0, 1).reshape(Bn, nc * C, H, DV)[:, :T]
    return o, h_T.astype(h0.dtype)


def dilated_band_attn(q, k, v, dil, n_keys):
    Bn, S, H, Dh = q.shape
    n = n_keys
    L = S // dil
    nb = -(-L // n)
    Lp = nb * n

    def resid(t):
        t = jnp.swapaxes(t.reshape(Bn, L, dil, H, Dh), 1, 2)
        t = jnp.pad(t, ((0, 0), (0, 0), (0, Lp - L), (0, 0), (0, 0)))
        return t.reshape(Bn, dil, nb, n, H, Dh)

    qb, kb, vb = resid(q), resid(k), resid(v)

    def with_prev(t):
        prev = jnp.pad(t, ((0, 0), (0, 0), (1, 0), (0, 0), (0, 0), (0, 0)))[:, :, :-1]
        return jnp.concatenate([prev, t], axis=3)

    kk, vv = with_prev(kb), with_prev(vb)
    i = np.arange(n)[:, None]
    j = np.arange(2 * n)[None, :]
    dist = i + n - j
    local = (dist >= 0) & (dist <= n)
    blk = np.arange(nb)[:, None, None]
    valid = jnp.asarray(local[None] & ~((blk == 0) & (j[None] < n)))
    s = jnp.einsum('brnqhd,brnkhd->brnhqk', qb, kk, preferred_element_type=jnp.float32) * ATTN_SCALE
    s = jnp.where(valid[None, None, :, None], s, -jnp.inf)
    m = jnp.max(s, axis=-1, keepdims=True)
    p = jnp.exp(s - m)
    l = jnp.sum(p, axis=-1)
    o = jnp.einsum('brnhqk,brnkhd->brnqhd', p, vv.astype(jnp.float32))
    o = o / jnp.swapaxes(l, 3, 4)[..., None]
    lse = jnp.swapaxes(m[..., 0] + jnp.log(l), 3, 4)

    def back(t):
        t = t.reshape(Bn, dil, Lp, *t.shape[4:])[:, :, :L]
        t = jnp.swapaxes(t, 1, 2)
        return t.reshape(Bn, S, *t.shape[3:])

    return back(o), back(lse)


def dilated_gather_attn(q, kbuf, vbuf, dil, n_keys):
    T = q.shape[1]
    Lb = kbuf.shape[1] - T
    idx = Lb + np.arange(T)[:, None] - dil * np.arange(n_keys + 1)[None, :]
    valid = jnp.asarray(idx >= 0)
    idx = np.maximum(idx, 0)
    kg = kbuf[:, idx]
    vg = vbuf[:, idx]
    s = jnp.einsum('bthd,btjhd->bthj', q, kg, preferred_element_type=jnp.float32) * ATTN_SCALE
    s = jnp.where(valid[None, :, None, :], s, -jnp.inf)
    m = jnp.max(s, axis=-1, keepdims=True)
    p = jnp.exp(s - m)
    l = jnp.sum(p, axis=-1)
    o = jnp.einsum('bthj,btjhd->bthd', p, vg.astype(jnp.float32)) / l[..., None]
    return o, m[..., 0] + jnp.log(l)


def mem_kv(mem, g_norm, w_mem_kv, g_mk):
    Bn, M, _ = mem.shape
    mk, mv = jnp.split(rms_norm(mem, g_norm) @ w_mem_kv, 2, axis=-1)
    mk = rms_norm(mk.reshape(Bn, M, MEM_HEADS, HEAD_DIM), g_mk)
    return mk, mv.reshape(Bn, M, MEM_HEADS, HEAD_DIM)


def mem_attend(mq, g_mq, mk, mv):
    Bn, T, _ = mq.shape
    q = rms_norm(mq.reshape(Bn, T, MEM_HEADS, HEAD_DIM), g_mq)
    s = jnp.einsum('bthd,bmhd->bhtm', q, mk, preferred_element_type=jnp.float32) * ATTN_SCALE
    p = jax.nn.softmax(s, axis=-1)
    o = jnp.einsum('bhtm,bmhd->bthd', p, mv.astype(jnp.float32))
    return o.reshape(Bn, T, MEM_Q)


def shared_kv(h, kv_norm, w_kv, g_k, pos):
    Bn, T, _ = h.shape
    k, v = jnp.split(rms_norm(h, kv_norm) @ w_kv, 2, axis=-1)
    k = rope(rms_norm(k.reshape(Bn, T, DIL_KV_HEADS, HEAD_DIM), g_k), pos)
    return k, v.reshape(Bn, T, DIL_KV_HEADS, HEAD_DIM)


def trunk(x, pos, gla_h0, mem_k, mem_v, win_k, win_v,
          norm1, norm2, a_w_in, a_w_gate2, a_b_gate, a_g_onorm, a_w_out,
          kv_norm, w_kv, g_k, b_w_in, b_g_q, b_w_out, g_mem_q, w_ffn_gu, w_ffn_down):
    Bn, T, _ = x.shape
    h = x
    gla_states = []
    k_sh = v_sh = kbuf = vbuf = None
    for li in range(DEPTH):
        hn = rms_norm(h, norm1[li])
        if li < N_A:
            q, k, v, g, glr, mq = jnp.split(hn @ a_w_in[li], A_SPLITS, axis=-1)
            log_a = jax.nn.log_sigmoid((glr @ a_w_gate2[li] + a_b_gate[li]).astype(jnp.float32)) / GLA_TAU
            o, h_T = gla_chunked(
                (q * GLA_DK ** -0.5).reshape(Bn, T, GLA_HEADS, GLA_DK),
                k.reshape(Bn, T, GLA_HEADS, GLA_DK),
                v.reshape(Bn, T, GLA_HEADS, GLA_DV),
                log_a.reshape(Bn, T, GLA_HEADS, GLA_DK),
                gla_h0[li])
            gla_states.append(h_T)
            o = rms_norm(o, a_g_onorm[li]).astype(h.dtype) * jax.nn.silu(g.reshape(Bn, T, GLA_HEADS, GLA_DV))
            tok = o.reshape(Bn, T, GLA_V)
            w_out = a_w_out[li]
        else:
            bi = li - N_A
            qd, mq = jnp.split(hn @ b_w_in[bi], [B_Q], axis=-1)
            qd = rope(rms_norm(qd.reshape(Bn, T, N_GROUPS * DIL_KV_HEADS, HEAD_DIM), b_g_q[bi]), pos)
            qd = qd.reshape(Bn, T, N_GROUPS, DIL_KV_HEADS, HEAD_DIM)
            outs, lses = [], []
            for gi, (win, dil) in enumerate(DIL_PAIRS):
                if win_k is None:
                    o_g, l_g = dilated_band_attn(qd[:, :, gi], k_sh, v_sh, dil, win // dil)
                else:
                    o_g, l_g = dilated_gather_attn(qd[:, :, gi], kbuf, vbuf, dil, win // dil)
                outs.append(o_g)
                lses.append(l_g)
            wts = jax.nn.softmax(jnp.stack(lses, 0), axis=0)
            o = jnp.einsum('gbth,gbthd->bthd', wts, jnp.stack(outs, 0))
            tok = o.reshape(Bn, T, B_KV).astype(h.dtype)
            w_out = b_w_out[bi]
        mem_o = mem_attend(mq, g_mem_q[li], mem_k[li], mem_v[li]).astype(h.dtype)
        h = h + jnp.concatenate([tok, mem_o], axis=-1) @ w_out
        h = h + swiglu(rms_norm(h, norm2[li]), w_ffn_gu[li], w_ffn_down[li])
        if li == N_A - 1:
            k_sh, v_sh = shared_kv(h, kv_norm, w_kv, g_k, pos)
            if win_k is not None:
                kbuf = jnp.concatenate([win_k, k_sh.astype(win_k.dtype)], axis=1)
                vbuf = jnp.concatenate([win_v, v_sh.astype(win_v.dtype)], axis=1)
    return h, jnp.stack(gla_states, 0), k_sh, v_sh


def setup_inputs(seed: int = 0) -> dict:
    key = jax.random.key(seed)
    ks = iter(jax.random.split(key, 32))

    def nrm(shape, scale):
        return jax.random.normal(next(ks), shape, jnp.float32) * scale

    def gain(shape):
        return 1.0 + nrm(shape, 0.05)

    win_rows = min(WIN_MAX, PAST_LEN)
    return {
        "x_prompt": nrm((BATCH, SEQ, D_MODEL), 1.0),
        "x_sample": nrm((DEC_BATCH, DEC_SEQ, D_MODEL), 1.0),
        "state_gla": nrm((N_A, DEC_BATCH, GLA_HEADS, GLA_DK, GLA_DV), 0.5),
        "cache_win_k": nrm((DEC_BATCH, win_rows, DIL_KV_HEADS, HEAD_DIM), 1.0),
        "cache_win_v": nrm((DEC_BATCH, win_rows, DIL_KV_HEADS, HEAD_DIM), 1.0),
        "cache_mem_k": nrm((DEPTH, DEC_BATCH, MEM_TOKENS, MEM_HEADS, HEAD_DIM), 1.0),
        "cache_mem_v": nrm((DEPTH, DEC_BATCH, MEM_TOKENS, MEM_HEADS, HEAD_DIM), 1.0),
        "mem_prompt": nrm((BATCH, MEM_TOKENS, D_MODEL), 1.0),
        "norm1": gain((DEPTH, D_MODEL)),
        "norm2": gain((DEPTH, D_MODEL)),
        "a_w_in": nrm((N_A, D_MODEL, A_IN), D_MODEL ** -0.5),
        "a_w_gate2": nrm((N_A, GLA_RANK, GLA_QK), GLA_RANK ** -0.5),
        "a_b_gate": nrm((N_A, GLA_QK), 0.1),
        "a_g_onorm": gain((N_A, GLA_DV)),
        "a_w_out": nrm((N_A, A_OUT, D_MODEL), A_OUT ** -0.5),
        "kv_norm": gain((D_MODEL,)),
        "w_kv": nrm((D_MODEL, 2 * B_KV), D_MODEL ** -0.5),
        "g_k": gain((HEAD_DIM,)),
        "b_w_in": nrm((N_B, D_MODEL, B_IN), D_MODEL ** -0.5),
        "b_g_q": gain((N_B, HEAD_DIM)),
        "b_w_out": nrm((N_B, B_OUT, D_MODEL), B_OUT ** -0.5),
        "mem_norm": gain((DEPTH, D_MODEL)),
        "w_mem_kv": nrm((DEPTH, D_MODEL, 2 * MEM_Q), D_MODEL ** -0.5),
        "g_mem_q": gain((DEPTH, HEAD_DIM)),
        "g_mem_k": gain((DEPTH, HEAD_DIM)),
        "w_ffn_gu": nrm((DEPTH, D_MODEL, 2 * D_FF), D_MODEL ** -0.5),
        "w_ffn_down": nrm((DEPTH, D_FF, D_MODEL), D_FF ** -0.5),
    }


def reference(x_prompt, x_sample, state_gla, cache_win_k, cache_win_v, cache_mem_k, cache_mem_v,
              mem_prompt, norm1, norm2, a_w_in, a_w_gate2, a_b_gate, a_g_onorm, a_w_out,
              kv_norm, w_kv, g_k, b_w_in, b_g_q, b_w_out, mem_norm, w_mem_kv, g_mem_q, g_mem_k,
              w_ffn_gu, w_ffn_down):
    weights = (norm1, norm2, a_w_in, a_w_gate2, a_b_gate, a_g_onorm, a_w_out,
               kv_norm, w_kv, g_k, b_w_in, b_g_q, b_w_out, g_mem_q, w_ffn_gu, w_ffn_down)

    mks, mvs = [], []
    for li in range(DEPTH):
        mk, mv = mem_kv(mem_prompt, mem_norm[li], w_mem_kv[li], g_mem_k[li])
        mks.append(mk)
        mvs.append(mv)
    mem_k_p = jnp.stack(mks, 0)
    mem_v_p = jnp.stack(mvs, 0)
    h0_p = jnp.zeros((N_A, x_prompt.shape[0], GLA_HEADS, GLA_DK, GLA_DV), x_prompt.dtype)
    pos_p = jnp.arange(x_prompt.shape[1], dtype=jnp.int32)
    y_prompt, gla_p, k_p, v_p = trunk(x_prompt, pos_p, h0_p, mem_k_p, mem_v_p, None, None, *weights)
    keep = min(WIN_MAX, x_prompt.shape[1])
    win_k_p = k_p[:, -keep:]
    win_v_p = v_p[:, -keep:]

    pos_s = PAST_LEN + jnp.arange(x_sample.shape[1], dtype=jnp.int32)
    y_sample, gla_s, k_s, v_s = trunk(x_sample, pos_s, state_gla, cache_mem_k, cache_mem_v,
                                      cache_win_k, cache_win_v, *weights)
    return (y_prompt, y_sample, gla_p, gla_s, win_k_p, win_v_p, k_s, v_s, mem_k_p, mem_v_p)
```

```python
import functools

import numpy as np
import jax
import jax.numpy as jnp
from jax import lax
from jax.experimental import pallas as pl
from jax.experimental.pallas import tpu as pltpu

F32 = jnp.float32
BF16 = jnp.bfloat16

D_MODEL = 1024
DEPTH = 4
N_A = 2
GLA_HEADS = 4
GLA_DK = 128
GLA_DV = 192
GLA_DVP = 256
GLA_QK = GLA_HEADS * GLA_DK
GLA_V = GLA_HEADS * GLA_DV
GLA_VP = GLA_HEADS * GLA_DVP
GLA_RANK = 16
GLA_RANKP = 128
GLA_TAU = 16.0
GLA_CHUNK = 64
HEAD_DIM = 64
N_HEADS = 4
HW = N_HEADS * HEAD_DIM
DIL_PAIRS = ((128, 1), (512, 4), (2048, 16))
N_GROUPS = 3
BAND = 128
WIN_MAX = 2048
MEM_TOKENS = 256
D_FF = 2816
FF_TILE = 256
ROPE_THETA = 10000.0
EPS = 1e-6
ATTN_SCALE = HEAD_DIM ** -0.5
NEG = -1e30

A_COL_Q = 0
A_COL_K = GLA_QK
A_COL_V = 2 * GLA_QK
A_COL_G = A_COL_V + GLA_VP
A_COL_MQ = A_COL_G + GLA_VP
A_COL_GLR = A_COL_MQ + HW
A_IN_P = A_COL_GLR + GLA_RANKP

V7X_VMEM_BYTES = 64 * 1024 * 1024
VMEM_CAP = 56 * 1024 * 1024


def _cparams(sem, vmem_bytes):
    return pltpu.CompilerParams(
        dimension_semantics=sem,
        vmem_limit_bytes=int(min(max(vmem_bytes, 16 * 1024 * 1024), VMEM_CAP)))


def _bdot(a, b):
    return jnp.dot(a.astype(BF16), b.astype(BF16), preferred_element_type=F32)


def _bdot_nt(a, b):
    return lax.dot_general(a.astype(BF16), b.astype(BF16), (((1,), (1,)), ((), ())),
                           preferred_element_type=F32)


def _bdot_tn(a, b):
    return lax.dot_general(a.astype(BF16), b.astype(BF16), (((0,), (0,)), ((), ())),
                           preferred_element_type=F32)


def _split(x):
    hi = x.astype(BF16)
    lo = (x - hi.astype(F32)).astype(BF16)
    return hi, lo


def _dot_exact_rhs(x, m):
    hi, lo = _split(x)
    return (jnp.dot(hi, m, preferred_element_type=F32)
            + jnp.dot(lo, m, preferred_element_type=F32))


def _dot_exact_lhs(m, x):
    hi, lo = _split(x)
    return (jnp.dot(m, hi, preferred_element_type=F32)
            + jnp.dot(m, lo, preferred_element_type=F32))


def _silu(x):
    return x * jax.nn.sigmoid(x)


def _log_sigmoid(z):
    return jnp.minimum(z, 0.0) - jnp.log1p(jnp.exp(-jnp.abs(z)))


def _gla_tables():
    c = GLA_CHUNK
    i = np.arange(c)[:, None]
    t = np.arange(c)[None, :]
    blocks = [(t <= i), (t > i)]
    masks = []
    for lvl in range(6):
        s = c >> lvl
        half = s // 2
        mid = (i // s) * s + half - 1
        second = (i % s) >= half
        m = np.where(second, (t > mid) & (t <= i), (t > i) & (t <= mid))
        blocks.append(m)
        j = t
        masks.append(((i // s) == (j // s)) & ((i % s) >= half) & ((j % s) < half))
    masks.append(i == t)
    mall = np.concatenate(blocks, axis=0).astype(np.float32)
    lmask = np.stack(masks, axis=0).astype(np.float32)
    assert np.array_equal(lmask.sum(0), (t <= i).astype(np.float32))
    return mall, lmask


def _head_tables():
    lane = np.arange(HW)
    group = (lane[:, None] // HEAD_DIM == lane[None, :] // HEAD_DIM).astype(np.float32)
    hmask = (lane[None, :] // HEAD_DIM == np.arange(N_HEADS)[:, None]).astype(np.float32)
    expand = np.zeros((128, HW), np.float32)
    expand[:N_HEADS] = hmask
    return group, hmask.reshape(N_HEADS, 1, HW), expand


def _band_mask():
    i = np.arange(BAND)[:, None]
    j = np.arange(2 * BAND)[None, :]
    dist = i + BAND - j
    return ((dist >= 0) & (dist <= BAND)).astype(np.float32)


def _rope_tables(pos):
    half = HEAD_DIM // 2
    inv = ROPE_THETA ** (-jnp.arange(half, dtype=F32) / half)
    ang = pos.astype(F32)[:, None] * inv[None, :]
    cos = jnp.cos(ang)
    sin = jnp.sin(ang)
    cos64 = jnp.concatenate([cos, cos], axis=-1)
    sin64 = jnp.concatenate([-sin, sin], axis=-1)
    return jnp.tile(cos64, (1, N_HEADS)), jnp.tile(sin64, (1, N_HEADS))


def _norm_matmul_kernel(x_ref, g_ref, w_ref, o_ref):
    x = x_ref[...]
    ms = jnp.mean(x * x, axis=-1, keepdims=True)
    y = x * lax.rsqrt(ms + EPS) * g_ref[...]
    o_ref[...] = jnp.dot(y.astype(BF16), w_ref[...], preferred_element_type=F32)


def _norm_matmul(x, g, w, *, tm, tn, name):
    n, d = x.shape
    nout = w.shape[1]
    vmem = 2 * (tm * d * 4 + d * tn * 2 + tm * tn * 4) + 2 * tm * d * 4 + tm * tn * 4
    return pl.pallas_call(
        _norm_matmul_kernel,
        grid=(n // tm, nout // tn),
        in_specs=[pl.BlockSpec((tm, d), lambda i, j: (i, 0)),
                  pl.BlockSpec((1, d), lambda i, j: (0, 0)),
                  pl.BlockSpec((d, tn), lambda i, j: (0, j))],
        out_specs=pl.BlockSpec((tm, tn), lambda i, j: (i, j)),
        out_shape=jax.ShapeDtypeStruct((n, nout), F32),
        compiler_params=_cparams(("parallel", "parallel"), vmem),
        name=name,
    )(x, g.reshape(1, d), w)


def _headnorm_kernel(x_ref, g_ref, grp_ref, cos_ref, sin_ref, o_ref, *, rope):
    width = x_ref.shape[1]
    grp = grp_ref[...]
    lane = lax.broadcasted_iota(jnp.int32, (1, HW), 1)
    first_half = (lane & (HEAD_DIM - 1)) < (HEAD_DIM // 2)
    for c in range(width // HW):
        sl = slice(c * HW, (c + 1) * HW)
        x = x_ref[:, sl]
        ss = _dot_exact_rhs(x * x, grp)
        y = x * lax.rsqrt(ss * (1.0 / HEAD_DIM) + EPS) * g_ref[:, sl]
        if rope:
            fwd = pltpu.roll(y, HW - HEAD_DIM // 2, 1)
            bwd = pltpu.roll(y, HEAD_DIM // 2, 1)
            rot = jnp.where(first_half, fwd, bwd)
            y = y * cos_ref[...] + rot * sin_ref[...]
        o_ref[:, sl] = y


def _headnorm(x, gain, tabs, *, width, rope, tr, name, pos_blocks=1):
    n = x.shape[0]
    grp, cos, sin = tabs
    g = jnp.tile(gain.reshape(1, HEAD_DIM), (1, width // HEAD_DIM))
    trow = cos.shape[0] if cos.shape[0] == 1 else tr
    if cos.shape[0] == 1:
        tab_map = lambda i: (0, 0)
    else:
        tab_map = lambda i: (i % pos_blocks, 0)
    vmem = 2 * (2 * tr * width * 4 + 2 * trow * HW * 4) + 6 * tr * HW * 4 + HW * HW * 4
    return pl.pallas_call(
        functools.partial(_headnorm_kernel, rope=rope),
        grid=(n // tr,),
        in_specs=[pl.BlockSpec((tr, width), lambda i: (i, 0)),
                  pl.BlockSpec((1, width), lambda i: (0, 0)),
                  pl.BlockSpec((HW, HW), lambda i: (0, 0)),
                  pl.BlockSpec((trow, HW), tab_map),
                  pl.BlockSpec((trow, HW), tab_map)],
        out_specs=pl.BlockSpec((tr, width), lambda i: (i, 0)),
        out_shape=jax.ShapeDtypeStruct((n, width), F32),
        compiler_params=_cparams(("parallel",), vmem),
        name=name,
    )(x, g, grp, cos, sin)


def _gla_kernel(q_ref, k_ref, v_ref, g_ref, glr_ref, wg2_ref, bg_ref, gon_ref, mall_ref, lmask_ref,
                h0_ref, tok_ref, hout_ref, ht_s, *, n_chunks):
    t = pl.program_id(2)

    @pl.when(t == 0)
    def _():
        ht_s[...] = h0_ref[0]

    z = _bdot(glr_ref[...], wg2_ref[...]) + bg_ref[...]
    la = _log_sigmoid(z) * (1.0 / GLA_TAU)
    mall = mall_ref[...]
    ht = ht_s[...]
    outs = []
    c_rows = GLA_CHUNK
    for c in range(n_chunks):
        rows = slice(c * c_rows, (c + 1) * c_rows)
        e = jnp.exp(_dot_exact_lhs(mall, la[rows]))
        qc = q_ref[rows, :] * (GLA_DK ** -0.5)
        kc = k_ref[rows, :]
        vc = v_ref[rows, :].astype(BF16)
        s = lmask_ref[6] * _bdot_nt(qc, kc)
        for lvl in range(6):
            el = e[(2 + lvl) * c_rows:(3 + lvl) * c_rows]
            s = s + lmask_ref[lvl] * _bdot_nt(qc * el, kc * el)
        eb = e[0:c_rows]
        esuf = e[c_rows:2 * c_rows]
        outs.append(_bdot(s, vc) + _bdot_nt(qc * eb, ht))
        ht = ht * eb[c_rows - 1:c_rows, :] + _bdot_tn(vc, kc * esuf)
    ht_s[...] = ht
    o = jnp.concatenate(outs, axis=0)
    ms = jnp.sum(o * o, axis=-1, keepdims=True) * (1.0 / GLA_DV)
    on = o * lax.rsqrt(ms + EPS) * gon_ref[...]
    tok_ref[...] = (on * _silu(g_ref[...])).astype(tok_ref.dtype)

    @pl.when(t == pl.num_programs(2) - 1)
    def _():
        hout_ref[0] = ht


def _gla_prompt(proj, h0t, wg2, bg, gon, mall, lmask, *, batch, seq, tb):
    n = proj.shape[0]
    nt = seq // tb
    row = lambda b, h, t: b * nt + t
    cq, ck = A_COL_Q // GLA_DK, A_COL_K // GLA_DK
    cv, cg = A_COL_V // GLA_DVP, A_COL_G // GLA_DVP
    cl = A_COL_GLR // GLA_RANKP
    vmem = (2 * tb * (2 * GLA_DK + 2 * GLA_DVP + GLA_RANKP) * 4 + 2 * tb * GLA_DVP * 2
            + 6 * GLA_DVP * GLA_DK * 4 + 16 * tb * GLA_DVP * 4)
    return pl.pallas_call(
        functools.partial(_gla_kernel, n_chunks=tb // GLA_CHUNK),
        grid=(batch, GLA_HEADS, nt),
        in_specs=[
            pl.BlockSpec((tb, GLA_DK), lambda b, h, t: (row(b, h, t), cq + h)),
            pl.BlockSpec((tb, GLA_DK), lambda b, h, t: (row(b, h, t), ck + h)),
            pl.BlockSpec((tb, GLA_DVP), lambda b, h, t: (row(b, h, t), cv + h)),
            pl.BlockSpec((tb, GLA_DVP), lambda b, h, t: (row(b, h, t), cg + h)),
            pl.BlockSpec((tb, GLA_RANKP), lambda b, h, t: (row(b, h, t), cl)),
            pl.BlockSpec((GLA_RANKP, GLA_DK), lambda b, h, t: (0, h)),
            pl.BlockSpec((1, GLA_DK), lambda b, h, t: (0, h)),
            pl.BlockSpec((1, GLA_DVP), lambda b, h, t: (0, 0)),
            pl.BlockSpec(mall.shape, lambda b, h, t: (0, 0)),
            pl.BlockSpec(lmask.shape, lambda b, h, t: (0, 0, 0)),
            pl.BlockSpec((1, GLA_DVP, GLA_DK), lambda b, h, t: (b * GLA_HEADS + h, 0, 0)),
        ],
        out_specs=[
            pl.BlockSpec((tb, GLA_DVP), lambda b, h, t: (row(b, h, t), h)),
            pl.BlockSpec((1, GLA_DVP, GLA_DK), lambda b, h, t: (b * GLA_HEADS + h, 0, 0)),
        ],
        out_shape=[jax.ShapeDtypeStruct((n, GLA_VP), BF16),
                   jax.ShapeDtypeStruct((batch * GLA_HEADS, GLA_DVP, GLA_DK), F32)],
        scratch_shapes=[pltpu.VMEM((GLA_DVP, GLA_DK), F32)],
        compiler_params=_cparams(("parallel", "parallel", "arbitrary"), vmem),
        name="gla_prompt",
    )(proj, proj, proj, proj, proj, wg2, bg, gon, mall, lmask, h0t)


def _gla_step_kernel(q_ref, k_ref, v_ref, g_ref, glr_ref, wg2_ref, bg_ref, gon_ref, st_ref,
                     tok_ref, so_ref):
    glr = jnp.broadcast_to(glr_ref[0], (8, GLA_RANKP))
    z = _bdot(glr, wg2_ref[...])[0:1] + bg_ref[...]
    a = jnp.exp(_log_sigmoid(z) * (1.0 / GLA_TAU))
    q = q_ref[0] * (GLA_DK ** -0.5)
    k = k_ref[0]
    v = v_ref[0]
    g = g_ref[0]
    r = lax.broadcasted_iota(jnp.int32, (GLA_DK, GLA_DK), 0)
    c = lax.broadcasted_iota(jnp.int32, (GLA_DK, GLA_DK), 1)
    eye = r == c

    def column(row):
        return jnp.sum(jnp.where(eye, jnp.broadcast_to(row, (GLA_DK, GLA_DK)), 0.0),
                       axis=1, keepdims=True)

    gon = gon_ref[...][:, :GLA_DV]
    for h in range(GLA_HEADS):
        ks = slice(h * GLA_DK, (h + 1) * GLA_DK)
        vs = slice(h * GLA_DVP, h * GLA_DVP + GLA_DV)
        s_new = column(a[:, ks]) * st_ref[0, h] + column(k[:, ks]) * v[:, vs]
        so_ref[0, h] = s_new
        o = jnp.sum(column(q[:, ks]) * s_new, axis=0, keepdims=True)
        ms = jnp.sum(o * o, axis=-1, keepdims=True) * (1.0 / GLA_DV)
        tok = o * lax.rsqrt(ms + EPS) * gon * _silu(g[:, vs])
        tok_ref[0, :, vs] = tok.astype(tok_ref.dtype)
        tok_ref[0, :, h * GLA_DVP + GLA_DV:(h + 1) * GLA_DVP] = jnp.zeros(
            (1, GLA_DVP - GLA_DV), tok_ref.dtype)


def _gla_step(proj3, state, wg2, bg, gon):
    bsz = proj3.shape[0]
    col = lambda w, c: pl.BlockSpec((1, 1, w), lambda b: (b, 0, c))
    st_spec = pl.BlockSpec((1, GLA_HEADS, GLA_DK, GLA_DV), lambda b: (b, 0, 0, 0))
    return pl.pallas_call(
        _gla_step_kernel,
        grid=(bsz,),
        in_specs=[col(GLA_QK, A_COL_Q // GLA_QK), col(GLA_QK, A_COL_K // GLA_QK),
                  col(GLA_VP, A_COL_V // GLA_VP), col(GLA_VP, A_COL_G // GLA_VP),
                  col(GLA_RANKP, A_COL_GLR // GLA_RANKP),
                  pl.BlockSpec((GLA_RANKP, GLA_QK), lambda b: (0, 0)),
                  pl.BlockSpec((1, GLA_QK), lambda b: (0, 0)),
                  pl.BlockSpec((1, GLA_DVP), lambda b: (0, 0)),
                  st_spec],
        out_specs=[pl.BlockSpec((1, 1, GLA_VP), lambda b: (b, 0, 0)), st_spec],
        out_shape=[jax.ShapeDtypeStruct((bsz, 1, GLA_VP), F32),
                   jax.ShapeDtypeStruct(state.shape, F32)],
        compiler_params=_cparams(("parallel",), 24 * 1024 * 1024),
        name="gla_step",
    )(proj3, proj3, proj3, proj3, proj3, wg2, bg, gon, state)


def _mem_attn_kernel(q_ref, k_ref, v_ref, gq_ref, grp_ref, hmask_ref, o_ref):
    tq = q_ref.shape[1]
    q = q_ref[0]
    if tq < 8:
        q = jnp.broadcast_to(q, (8, HW))
    ss = _dot_exact_rhs(q * q, grp_ref[...])
    qn = q * lax.rsqrt(ss * (1.0 / HEAD_DIM) + EPS) * gq_ref[...] * ATTN_SCALE
    k = k_ref[0].astype(BF16)
    v = v_ref[0].astype(BF16)
    out = jnp.zeros(q.shape, F32)
    for h in range(N_HEADS):
        hm = hmask_ref[h]
        s = _bdot_nt(qn * hm, k)
        m = jnp.max(s, axis=-1, keepdims=True)
        p = jnp.exp(s - m)
        l = jnp.sum(p, axis=-1, keepdims=True)
        out = out + hm * (_bdot(p, v) / l)
    o_ref[0] = out[0:tq].astype(o_ref.dtype)


def _mem_attn(q3, qcol, k3, kcol, v3, vcol, gq, grp, hmask, *, tq):
    bsz, t, _ = q3.shape
    g = jnp.tile(gq.reshape(1, HEAD_DIM), (1, N_HEADS))
    vmem = 2 * (tq * HW * 4 + 2 * MEM_TOKENS * HW * 4 + tq * HW * 2) + 16 * max(tq, 8) * HW * 4
    return pl.pallas_call(
        _mem_attn_kernel,
        grid=(bsz, t // tq),
        in_specs=[pl.BlockSpec((1, tq, HW), lambda b, i: (b, i, qcol)),
                  pl.BlockSpec((1, MEM_TOKENS, HW), lambda b, i: (b, 0, kcol)),
                  pl.BlockSpec((1, MEM_TOKENS, HW), lambda b, i: (b, 0, vcol)),
                  pl.BlockSpec((1, HW), lambda b, i: (0, 0)),
                  pl.BlockSpec((HW, HW), lambda b, i: (0, 0)),
                  pl.BlockSpec((N_HEADS, 1, HW), lambda b, i: (0, 0, 0))],
        out_specs=pl.BlockSpec((1, tq, HW), lambda b, i: (b, i, 0)),
        out_shape=jax.ShapeDtypeStruct((bsz, t, HW), BF16 if tq >= 16 else F32),
        compiler_params=_cparams(("parallel", "parallel"), vmem),
        name="mem_attn",
    )(q3, k3, v3, g, grp, hmask)


def _band_attn_kernel(q0_ref, q1_ref, k0_ref, k1_ref, v0_ref, v1_ref, band_ref, hmask_ref,
                      expand_ref, o_ref, og_s, ls_s, *, seq):
    g = pl.program_id(1)
    n_units = seq // BAND

    def run_group(gi, dil):
        nb = seq // dil // BAND
        shift = nb.bit_length() - 1
        band_ok = band_ref[...] > 0.5
        is_cur = (lax.broadcasted_iota(jnp.int32, (BAND, 2 * BAND), 1) >= BAND).astype(jnp.int32)
        lane = lax.broadcasted_iota(jnp.int32, (1, 128), 1)

        def idx(start):
            if dil == 1:
                return pl.ds(pl.multiple_of(start, BAND), BAND)
            return pl.ds(start, BAND, stride=dil)

        def rows(ref0, ref1, start):
            return jnp.concatenate([ref0[idx(start), :], ref1[idx(start), :]], axis=1)

        def unit(u, carry):
            r = lax.shift_right_logical(u, shift)
            nblk = u & (nb - 1)
            qs = r + dil * BAND * nblk
            ks = r + dil * BAND * jnp.maximum(nblk - 1, 0)
            q = rows(q0_ref, q1_ref, qs) * ATTN_SCALE
            kk = jnp.concatenate([rows(k0_ref, k1_ref, ks), rows(k0_ref, k1_ref, qs)],
                                 axis=0).astype(BF16)
            vv = jnp.concatenate([rows(v0_ref, v1_ref, ks), rows(v0_ref, v1_ref, qs)],
                                 axis=0).astype(BF16)
            prev_ok = jnp.where(nblk > 0, 1, 0)
            valid = band_ok & ((is_cur + prev_ok) > 0)
            out = jnp.zeros((BAND, HW), F32)
            lse_t = jnp.zeros((BAND, 128), F32)
            for h in range(N_HEADS):
                hm = hmask_ref[h]
                s = jnp.where(valid, _bdot_nt(q * hm, kk), NEG)
                m = jnp.max(s, axis=-1, keepdims=True)
                p = jnp.exp(s - m)
                l = jnp.sum(p, axis=-1, keepdims=True)
                out = out + hm * (_bdot(p, vv) / l)
                lse_t = lse_t + jnp.where(lane == h, m + jnp.log(l), 0.0)
            og_s[2 * gi, idx(qs), :] = out[:, :128]
            og_s[2 * gi + 1, idx(qs), :] = out[:, 128:]
            ls_s[gi, idx(qs), :] = lse_t
            return carry

        lax.fori_loop(0, n_units, unit, 0)

    for gi, (_, dil) in enumerate(DIL_PAIRS):
        pl.when(g == gi)(functools.partial(run_group, gi, dil))

    @pl.when(g == N_GROUPS - 1)
    def _merge():
        expand = expand_ref[...]
        tr = 512
        for c in range(seq // tr):
            rows = slice(c * tr, (c + 1) * tr)
            l0, l1, l2 = ls_s[0, rows, :], ls_s[1, rows, :], ls_s[2, rows, :]
            m = jnp.maximum(jnp.maximum(l0, l1), l2)
            es = (jnp.exp(l0 - m), jnp.exp(l1 - m), jnp.exp(l2 - m))
            inv = 1.0 / (es[0] + es[1] + es[2])
            acc = jnp.zeros((tr, HW), F32)
            for gi in range(N_GROUPS):
                o_g = jnp.concatenate([og_s[2 * gi, rows, :], og_s[2 * gi + 1, rows, :]], axis=1)
                acc = acc + _dot_exact_rhs(es[gi] * inv, expand) * o_g
            o_ref[rows, :] = acc.astype(o_ref.dtype)


def _band_attn(qd, k_sh, kv, band, hmask, expand, *, batch, seq):
    n = qd.shape[0]
    once = pl.Buffered(1)
    half = lambda cmap: pl.BlockSpec((seq, 128), cmap, pipeline_mode=once)
    vmem = (6 * seq * 128 * 4 + 2 * seq * HW * 2 + 9 * seq * 128 * 4 + 8 * 1024 * 1024)
    return pl.pallas_call(
        functools.partial(_band_attn_kernel, seq=seq),
        grid=(batch, N_GROUPS),
        in_specs=[half(lambda b, g: (b, 2 * g)), half(lambda b, g: (b, 2 * g + 1)),
                  half(lambda b, g: (b, 0)), half(lambda b, g: (b, 1)),
                  half(lambda b, g: (b, 2)), half(lambda b, g: (b, 3)),
                  pl.BlockSpec((BAND, 2 * BAND), lambda b, g: (0, 0)),
                  pl.BlockSpec((N_HEADS, 1, HW), lambda b, g: (0, 0, 0)),
                  pl.BlockSpec((128, HW), lambda b, g: (0, 0))],
        out_specs=pl.BlockSpec((seq, HW), lambda b, g: (b, 0)),
        out_shape=jax.ShapeDtypeStruct((n, HW), BF16),
        scratch_shapes=[pltpu.VMEM((2 * N_GROUPS, seq, 128), F32),
                        pltpu.VMEM((N_GROUPS, seq, 128), F32)],
        compiler_params=_cparams(("parallel", "arbitrary"), vmem),
        name="band_attn",
    )(qd, qd, k_sh, k_sh, kv, kv, band, hmask, expand)


def _step_attn_kernel(q_ref, kn_ref, vn_ref, kc0_ref, kc1_ref, vc0_ref, vc1_ref, hmask_ref, o_ref,
                      *, cache_len):
    kn = kn_ref[0]
    vn = vn_ref[0]
    outs, lses = [], []
    for gi, (_, dil) in enumerate(DIL_PAIRS):
        start = cache_len - dil * BAND
        idx = pl.ds(start, BAND) if dil == 1 else pl.ds(start, BAND, stride=dil)
        kc = jnp.concatenate([kc0_ref[idx, :], kc1_ref[idx, :]], axis=1).astype(BF16)
        vc = jnp.concatenate([vc0_ref[idx, :], vc1_ref[idx, :]], axis=1).astype(BF16)
        q = q_ref[0][:, gi * HW:(gi + 1) * HW] * ATTN_SCALE
        out = jnp.zeros((1, HW), F32)
        lse = jnp.zeros((1, HW), F32)
        for h in range(N_HEADS):
            hm = hmask_ref[h]
            qh = q * hm
            s_c = _bdot_nt(jnp.broadcast_to(qh, (8, HW)), kc)[0:1]
            s_n = jnp.sum(qh * kn, axis=-1, keepdims=True)
            m = jnp.maximum(jnp.max(s_c, axis=-1, keepdims=True), s_n)
            p_c = jnp.exp(s_c - m)
            p_n = jnp.exp(s_n - m)
            l = jnp.sum(p_c, axis=-1, keepdims=True) + p_n
            o = (_bdot(jnp.broadcast_to(p_c, (8, BAND)), vc)[0:1] + p_n * vn) / l
            out = out + hm * o
            lse = lse + hm * (m + jnp.log(l))
        outs.append(out)
        lses.append(lse)
    m = jnp.maximum(jnp.maximum(lses[0], lses[1]), lses[2])
    es = [jnp.exp(x - m) for x in lses]
    inv = 1.0 / (es[0] + es[1] + es[2])
    acc = es[0] * inv * outs[0] + es[1] * inv * outs[1] + es[2] * inv * outs[2]
    o_ref[0] = acc.astype(o_ref.dtype)


def _step_attn(q3, kn3, vn3, kc, vc, hmask):
    bsz, cache_len, _ = kc.shape
    assert cache_len >= max(d for _, d in DIL_PAIRS) * BAND
    vmem = 4 * cache_len * HW * 4 + 4 * 1024 * 1024
    return pl.pallas_call(
        functools.partial(_step_attn_kernel, cache_len=cache_len),
        grid=(bsz,),
        in_specs=[pl.BlockSpec((1, 1, N_GROUPS * HW), lambda b: (b, 0, 0)),
                  pl.BlockSpec((1, 1, HW), lambda b: (b, 0, 0)),
                  pl.BlockSpec((1, 1, HW), lambda b: (b, 0, 0)),
                  pl.BlockSpec((None, cache_len, 128), lambda b: (b, 0, 0)),
                  pl.BlockSpec((None, cache_len, 128), lambda b: (b, 0, 1)),
                  pl.BlockSpec((None, cache_len, 128), lambda b: (b, 0, 0)),
                  pl.BlockSpec((None, cache_len, 128), lambda b: (b, 0, 1)),
                  pl.BlockSpec((N_HEADS, 1, HW), lambda b: (0, 0, 0))],
        out_specs=pl.BlockSpec((1, 1, HW), lambda b: (b, 0, 0)),
        out_shape=jax.ShapeDtypeStruct((bsz, 1, HW), F32),
        compiler_params=_cparams(("parallel",), vmem),
        name="step_attn",
    )(q3, kn3, vn3, kc, kc, vc, vc, hmask)


def _out_ffn_kernel(h_ref, tok_ref, mem_ref, wo1_ref, wo2_ref, g2_ref, wgu_ref, wdn_ref, o_ref):
    h1 = (h_ref[...]
          + jnp.dot(tok_ref[...], wo1_ref[...], preferred_element_type=F32)
          + jnp.dot(mem_ref[...], wo2_ref[...], preferred_element_type=F32))
    ms = jnp.mean(h1 * h1, axis=-1, keepdims=True)
    hn = (h1 * lax.rsqrt(ms + EPS) * g2_ref[...]).astype(BF16)
    gate = jnp.dot(hn, wgu_ref[:, :D_FF], preferred_element_type=F32)
    up = jnp.dot(hn, wgu_ref[:, D_FF:], preferred_element_type=F32)
    act = (_silu(gate) * up).astype(BF16)
    o_ref[...] = h1 + jnp.dot(act, wdn_ref[...], preferred_element_type=F32)


def _out_ffn(h, tok, mem, wo1, wo2, g2, wgu, wdn, *, tm):
    n, d = h.shape
    wt = tok.shape[1]
    once = pl.Buffered(1)
    const = lambda shape: pl.BlockSpec(shape, lambda i: (0, 0), pipeline_mode=once)
    vmem = ((wt + HW) * d * 2 + 3 * D_FF * d * 2
            + 2 * tm * (2 * d * 4 + (wt + HW) * 2) + 3 * tm * d * 4 + 3 * tm * D_FF * 4
            + 4 * 1024 * 1024)
    return pl.pallas_call(
        _out_ffn_kernel,
        grid=(n // tm,),
        in_specs=[pl.BlockSpec((tm, d), lambda i: (i, 0)),
                  pl.BlockSpec((tm, wt), lambda i: (i, 0)),
                  pl.BlockSpec((tm, HW), lambda i: (i, 0)),
                  const((wt, d)), const((HW, d)), const((1, d)),
                  const((d, 2 * D_FF)), const((D_FF, d))],
        out_specs=pl.BlockSpec((tm, d), lambda i: (i, 0)),
        out_shape=jax.ShapeDtypeStruct((n, d), F32),
        compiler_params=_cparams(("parallel",), vmem),
        name="out_ffn",
    )(h, tok, mem, wo1, wo2, g2.reshape(1, d), wgu, wdn)


def _pad_heads_cols(w):
    d = w.shape[0]
    return jnp.pad(w.reshape(d, GLA_HEADS, GLA_DV),
                   ((0, 0), (0, 0), (0, GLA_DVP - GLA_DV))).reshape(d, GLA_VP)


def _pack_a_in(w):
    q = w[:, :GLA_QK]
    k = w[:, GLA_QK:2 * GLA_QK]
    v = w[:, 2 * GLA_QK:2 * GLA_QK + GLA_V]
    g = w[:, 2 * GLA_QK + GLA_V:2 * GLA_QK + 2 * GLA_V]
    glr = w[:, 2 * GLA_QK + 2 * GLA_V:2 * GLA_QK + 2 * GLA_V + GLA_RANK]
    mq = w[:, 2 * GLA_QK + 2 * GLA_V + GLA_RANK:]
    glr = jnp.pad(glr, ((0, 0), (0, GLA_RANKP - GLA_RANK)))
    return jnp.concatenate([q, k, _pad_heads_cols(v), _pad_heads_cols(g), mq, glr],
                           axis=1).astype(BF16)


def _pack_a_out(w):
    tok = w[:GLA_V].reshape(GLA_HEADS, GLA_DV, D_MODEL)
    tok = jnp.pad(tok, ((0, 0), (0, GLA_DVP - GLA_DV), (0, 0))).reshape(GLA_VP, D_MODEL)
    return tok.astype(BF16), w[GLA_V:].astype(BF16)


def kernel(x_prompt, x_sample, state_gla, cache_win_k, cache_win_v, cache_mem_k, cache_mem_v,
           mem_prompt, norm1, norm2, a_w_in, a_w_gate2, a_b_gate, a_g_onorm, a_w_out, kv_norm, w_kv,
           g_k, b_w_in, b_g_q, b_w_out, mem_norm, w_mem_kv, g_mem_q, g_mem_k, w_ffn_gu, w_ffn_down):
    batch, seq, d = x_prompt.shape
    dec_b = x_sample.shape[0]
    past_len = 8192
    assert d == D_MODEL and seq % 512 == 0 and x_sample.shape[1] == 1

    mall_np, lmask_np = _gla_tables()
    grp_np, hmask_np, expand_np = _head_tables()
    mall = jnp.asarray(mall_np, BF16)
    lmask = jnp.asarray(lmask_np, F32)
    grp = jnp.asarray(grp_np, BF16)
    hmask = jnp.asarray(hmask_np, F32)
    expand = jnp.asarray(expand_np, BF16)
    band = jnp.asarray(_band_mask(), F32)

    cos_p, sin_p = _rope_tables(jnp.arange(seq, dtype=jnp.int32))
    cos_s, sin_s = _rope_tables(past_len + jnp.arange(1, dtype=jnp.int32))
    ones_t = jnp.ones((1, HW), F32)
    zeros_t = jnp.zeros((1, HW), F32)

    wa_in = [_pack_a_in(a_w_in[i]) for i in range(N_A)]
    wa_out = [_pack_a_out(a_w_out[i]) for i in range(N_A)]
    wg2 = [jnp.pad(a_w_gate2[i], ((0, GLA_RANKP - GLA_RANK), (0, 0))).astype(BF16)
           for i in range(N_A)]
    bg = [a_b_gate[i].reshape(1, GLA_QK) for i in range(N_A)]
    gon = [jnp.pad(a_g_onorm[i], (0, GLA_DVP - GLA_DV)).reshape(1, GLA_DVP) for i in range(N_A)]
    wb_in = [b_w_in[i].astype(BF16) for i in range(DEPTH - N_A)]
    wb_out = [(b_w_out[i][:HW].astype(BF16), b_w_out[i][HW:].astype(BF16))
              for i in range(DEPTH - N_A)]
    w_kv_b = w_kv.astype(BF16)
    w_mem_b = w_mem_kv.astype(BF16)
    wgu = w_ffn_gu.astype(BF16)
    wdn = w_ffn_down.astype(BF16)

    mem_rows = mem_prompt.reshape(batch * MEM_TOKENS, d)
    mem_kv_p, mem_k_p = [], []
    for li in range(DEPTH):
        kv = _norm_matmul(mem_rows, mem_norm[li], w_mem_b[li], tm=512, tn=2 * HW, name="mem_kv")
        mk = _headnorm(kv, g_mem_k[li], (grp, ones_t, zeros_t), width=HW, rope=False, tr=512,
                       name="mem_knorm")
        mem_kv_p.append(kv.reshape(batch, MEM_TOKENS, 2 * HW))
        mem_k_p.append(mk.reshape(batch, MEM_TOKENS, HW))

    def trunk(x2, bsz, t, prompt):
        n = bsz * t
        tm = 512 if prompt else n
        h = x2
        states = []
        k_sh = kv_sh = None
        for li in range(DEPTH):
            if li < N_A:
                proj = _norm_matmul(h, norm1[li], wa_in[li], tm=tm, tn=A_IN_P // 3, name="a_in")
                proj3 = proj.reshape(bsz, t, A_IN_P)
                mq_col = A_COL_MQ // HW
                if prompt:
                    h0t = jnp.zeros((bsz * GLA_HEADS, GLA_DVP, GLA_DK), F32)
                    tok, st = _gla_prompt(proj, h0t, wg2[li], bg[li], gon[li], mall, lmask,
                                          batch=bsz, seq=t, tb=512)
                    st = jnp.swapaxes(st.reshape(bsz, GLA_HEADS, GLA_DVP, GLA_DK), 2, 3)
                    states.append(st[..., :GLA_DV])
                else:
                    tok, st = _gla_step(proj3, state_gla[li], wg2[li], bg[li], gon[li])
                    tok = tok.reshape(n, GLA_VP).astype(BF16)
                    states.append(st)
                wo1, wo2 = wa_out[li]
            else:
                bi = li - N_A
                proj = _norm_matmul(h, norm1[li], wb_in[bi], tm=tm, tn=D_MODEL, name="b_in")
                proj3 = proj.reshape(bsz, t, D_MODEL)
                mq_col = (N_GROUPS * HW) // HW
                if prompt:
                    qd = _headnorm(proj, b_g_q[bi], (grp, cos_p, sin_p), width=N_GROUPS * HW,
                                   rope=True, tr=512, name="q_rope", pos_blocks=t // 512)
                    tok = _band_attn(qd, k_sh, kv_sh, band, hmask, expand, batch=bsz, seq=t)
                else:
                    qd = _headnorm(proj, b_g_q[bi], (grp, cos_s, sin_s), width=N_GROUPS * HW,
                                   rope=True, tr=n, name="q_rope")
                    tok = _step_attn(qd.reshape(bsz, 1, N_GROUPS * HW), k_sh.reshape(bsz, 1, HW),
                                     kv_sh[:, HW:].reshape(bsz, 1, HW),
                                     cache_win_k.reshape(bsz, -1, HW),
                                     cache_win_v.reshape(bsz, -1, HW), hmask)
                    tok = tok.reshape(n, HW).astype(BF16)
                wo1, wo2 = wb_out[bi]
            if prompt:
                mem_o = _mem_attn(proj3, mq_col, mem_k_p[li], 0, mem_kv_p[li], 1, g_mem_q[li],
                                  grp, hmask, tq=512)
            else:
                mem_o = _mem_attn(proj3, mq_col, cache_mem_k[li].reshape(bsz, MEM_TOKENS, HW), 0,
                                  cache_mem_v[li].reshape(bsz, MEM_TOKENS, HW), 0, g_mem_q[li],
                                  grp, hmask, tq=1)
            h = _out_ffn(h, tok, mem_o.reshape(n, HW).astype(BF16), wo1, wo2, norm2[li], wgu[li],
                         wdn[li], tm=tm)
            if li == N_A - 1:
                kv_sh = _norm_matmul(h, kv_norm, w_kv_b, tm=tm, tn=2 * HW, name="shared_kv")
                tabs = (grp, cos_p, sin_p) if prompt else (grp, cos_s, sin_s)
                k_sh = _headnorm(kv_sh, g_k, tabs, width=HW, rope=True, tr=tm, name="k_rope",
                                 pos_blocks=max(t // tm, 1))
        return h, states, k_sh, kv_sh[:, HW:]

    y_p, gla_p, k_p, v_p = trunk(x_prompt.reshape(batch * seq, d), batch, seq, True)
    y_s, gla_s, k_s, v_s = trunk(x_sample.reshape(dec_b, d), dec_b, 1, False)

    keep = min(WIN_MAX, seq)
    k_p4 = k_p.reshape(batch, seq, N_HEADS, HEAD_DIM)
    v_p4 = v_p.reshape(batch, seq, N_HEADS, HEAD_DIM)
    mem_k_out = jnp.stack(mem_k_p, 0).reshape(DEPTH, batch, MEM_TOKENS, N_HEADS, HEAD_DIM)
    mem_v_out = jnp.stack([kv[..., HW:] for kv in mem_kv_p], 0).reshape(
        DEPTH, batch, MEM_TOKENS, N_HEADS, HEAD_DIM)
    return (y_p.reshape(batch, seq, d),
            y_s.reshape(dec_b, 1, d),
            jnp.stack(gla_p, 0),
            jnp.stack(gla_s, 0),
            k_p4[:, -keep:],
            v_p4[:, -keep:],
            k_s.reshape(dec_b, 1, N_HEADS, HEAD_DIM),
            v_s.reshape(dec_b, 1, N_HEADS, HEAD_DIM),
            mem_k_out,
            mem_v_out)
```

```python
import functools

import numpy as np
import jax
import jax.numpy as jnp
from jax import lax
from jax.experimental import pallas as pl
from jax.experimental.pallas import tpu as pltpu

F32 = jnp.float32
BF16 = jnp.bfloat16

D_MODEL = 1024
DEPTH = 4
N_A = 2
GLA_HEADS = 4
GLA_DK = 128
GLA_DV = 192
GLA_DVP = 256
GLA_QK = GLA_HEADS * GLA_DK
GLA_V = GLA_HEADS * GLA_DV
GLA_VP = GLA_HEADS * GLA_DVP
GLA_RANK = 16
GLA_RANKP = 128
GLA_TAU = 16.0
GLA_CHUNK = 64
HEAD_DIM = 64
N_HEADS = 4
HW = N_HEADS * HEAD_DIM
DIL_PAIRS = ((128, 1), (512, 4), (2048, 16))
N_GROUPS = 3
BAND = 128
WIN_MAX = 2048
MEM_TOKENS = 256
D_FF = 2816
FF_TILE = 256
ROPE_THETA = 10000.0
EPS = 1e-6
ATTN_SCALE = HEAD_DIM ** -0.5
NEG = -1e30

A_COL_Q = 0
A_COL_K = GLA_QK
A_COL_V = 2 * GLA_QK
A_COL_G = A_COL_V + GLA_VP
A_COL_MQ = A_COL_G + GLA_VP
A_COL_GLR = A_COL_MQ + HW
A_IN_P = A_COL_GLR + GLA_RANKP

V7X_VMEM_BYTES = 64 * 1024 * 1024
VMEM_CAP = 56 * 1024 * 1024


def _cparams(sem, vmem_bytes):
    return pltpu.CompilerParams(
        dimension_semantics=sem,
        vmem_limit_bytes=int(min(max(vmem_bytes, 16 * 1024 * 1024), VMEM_CAP)))


def _bdot(a, b):
    return jnp.dot(a.astype(BF16), b.astype(BF16), preferred_element_type=F32)


def _bdot_nt(a, b):
    return lax.dot_general(a.astype(BF16), b.astype(BF16), (((1,), (1,)), ((), ())),
                           preferred_element_type=F32)


def _bdot_tn(a, b):
    return lax.dot_general(a.astype(BF16), b.astype(BF16), (((0,), (0,)), ((), ())),
                           preferred_element_type=F32)


def _split(x):
    hi = x.astype(BF16)
    lo = (x - hi.astype(F32)).astype(BF16)
    return hi, lo


def _dot_exact_rhs(x, m):
    hi, lo = _split(x)
    return (jnp.dot(hi, m, preferred_element_type=F32)
            + jnp.dot(lo, m, preferred_element_type=F32))


def _dot_exact_lhs(m, x):
    hi, lo = _split(x)
    return (jnp.dot(m, hi, preferred_element_type=F32)
            + jnp.dot(m, lo, preferred_element_type=F32))


def _silu(x):
    return x * jax.nn.sigmoid(x)


def _log_sigmoid(z):
    return jnp.minimum(z, 0.0) - jnp.log1p(jnp.exp(-jnp.abs(z)))


def _gla_tables():
    c = GLA_CHUNK
    i = np.arange(c)[:, None]
    t = np.arange(c)[None, :]
    blocks = [(t <= i), (t > i)]
    masks = []
    for lvl in range(6):
        s = c >> lvl
        half = s // 2
        mid = (i // s) * s + half - 1
        second = (i % s) >= half
        m = np.where(second, (t > mid) & (t <= i), (t > i) & (t <= mid))
        blocks.append(m)
        j = t
        masks.append(((i // s) == (j // s)) & ((i % s) >= half) & ((j % s) < half))
    masks.append(i == t)
    mall = np.concatenate(blocks, axis=0).astype(np.float32)
    lmask = np.stack(masks, axis=0).astype(np.float32)
    assert np.array_equal(lmask.sum(0), (t <= i).astype(np.float32))
    return mall, lmask


def _head_tables():
    lane = np.arange(HW)
    group = (lane[:, None] // HEAD_DIM == lane[None, :] // HEAD_DIM).astype(np.float32)
    hmask = (lane[None, :] // HEAD_DIM == np.arange(N_HEADS)[:, None]).astype(np.float32)
    expand = np.zeros((128, HW), np.float32)
    expand[:N_HEADS] = hmask
    return group, hmask.reshape(N_HEADS, 1, HW), expand


def _band_bias():
    i = np.arange(BAND)[:, None]
    j = np.arange(2 * BAND)[None, :]
    dist = i + BAND - j
    ok = (dist >= 0) & (dist <= BAND)
    first = ok & (j >= BAND)
    return np.where(np.stack([first, ok], 0), 0.0, NEG).astype(np.float32)


def _rope_tables(pos):
    half = HEAD_DIM // 2
    inv = ROPE_THETA ** (-jnp.arange(half, dtype=F32) / half)
    ang = pos.astype(F32)[:, None] * inv[None, :]
    cos = jnp.cos(ang)
    sin = jnp.sin(ang)
    cos64 = jnp.concatenate([cos, cos], axis=-1)
    sin64 = jnp.concatenate([-sin, sin], axis=-1)
    return jnp.tile(cos64, (1, N_HEADS)), jnp.tile(sin64, (1, N_HEADS))


def _norm_matmul_kernel(x_ref, g_ref, w_ref, o_ref):
    x = x_ref[...]
    ms = jnp.mean(x * x, axis=-1, keepdims=True)
    y = x * lax.rsqrt(ms + EPS) * g_ref[...]
    o_ref[...] = jnp.dot(y.astype(BF16), w_ref[...], preferred_element_type=F32)


def _norm_matmul(x, g, w, *, tm, tn, name):
    n, d = x.shape
    nout = w.shape[1]
    vmem = 2 * (tm * d * 4 + d * tn * 2 + tm * tn * 4) + 2 * tm * d * 4 + tm * tn * 4
    return pl.pallas_call(
        _norm_matmul_kernel,
        grid=(n // tm, nout // tn),
        in_specs=[pl.BlockSpec((tm, d), lambda i, j: (i, 0)),
                  pl.BlockSpec((1, d), lambda i, j: (0, 0)),
                  pl.BlockSpec((d, tn), lambda i, j: (0, j))],
        out_specs=pl.BlockSpec((tm, tn), lambda i, j: (i, j)),
        out_shape=jax.ShapeDtypeStruct((n, nout), F32),
        compiler_params=_cparams(("parallel", "parallel"), vmem),
        name=name,
    )(x, g.reshape(1, d), w)


def _headnorm_kernel(x_ref, g_ref, grp_ref, cos_ref, sin_ref, o_ref, *, rope):
    width = x_ref.shape[1]
    grp = grp_ref[...]
    lane = lax.broadcasted_iota(jnp.int32, (1, HW), 1)
    first_half = (lane & (HEAD_DIM - 1)) < (HEAD_DIM // 2)
    for c in range(width // HW):
        sl = slice(c * HW, (c + 1) * HW)
        x = x_ref[:, sl]
        ss = _dot_exact_rhs(x * x, grp)
        y = x * lax.rsqrt(ss * (1.0 / HEAD_DIM) + EPS) * g_ref[:, sl]
        if rope:
            fwd = pltpu.roll(y, HW - HEAD_DIM // 2, 1)
            bwd = pltpu.roll(y, HEAD_DIM // 2, 1)
            rot = jnp.where(first_half, fwd, bwd)
            y = y * cos_ref[...] + rot * sin_ref[...]
        o_ref[:, sl] = y


def _headnorm(x, gain, tabs, *, width, rope, tr, name, pos_blocks=1):
    n = x.shape[0]
    grp, cos, sin = tabs
    g = jnp.tile(gain.reshape(1, HEAD_DIM), (1, width // HEAD_DIM))
    trow = cos.shape[0] if cos.shape[0] == 1 else tr
    if cos.shape[0] == 1:
        tab_map = lambda i: (0, 0)
    else:
        tab_map = lambda i: (i % pos_blocks, 0)
    vmem = 2 * (2 * tr * width * 4 + 2 * trow * HW * 4) + 6 * tr * HW * 4 + HW * HW * 4
    return pl.pallas_call(
        functools.partial(_headnorm_kernel, rope=rope),
        grid=(n // tr,),
        in_specs=[pl.BlockSpec((tr, width), lambda i: (i, 0)),
                  pl.BlockSpec((1, width), lambda i: (0, 0)),
                  pl.BlockSpec((HW, HW), lambda i: (0, 0)),
                  pl.BlockSpec((trow, HW), tab_map),
                  pl.BlockSpec((trow, HW), tab_map)],
        out_specs=pl.BlockSpec((tr, width), lambda i: (i, 0)),
        out_shape=jax.ShapeDtypeStruct((n, width), F32),
        compiler_params=_cparams(("parallel",), vmem),
        name=name,
    )(x, g, grp, cos, sin)


def _gla_kernel(q_ref, k_ref, v_ref, g_ref, glr_ref, wg2_ref, bg_ref, gon_ref, mall_ref, lmask_ref,
                h0_ref, tok_ref, hout_ref, ht_s, *, n_chunks):
    t = pl.program_id(2)

    @pl.when(t == 0)
    def _():
        ht_s[...] = h0_ref[0]

    z = _bdot(glr_ref[...], wg2_ref[...]) + bg_ref[...]
    la = _log_sigmoid(z) * (1.0 / GLA_TAU)
    cr = GLA_CHUNK
    chunk = lambda c: slice(c * cr, (c + 1) * cr)
    la_cat = jnp.concatenate([la[chunk(c)] for c in range(n_chunks)], axis=1)
    e_all = jnp.exp(_dot_exact_lhs(mall_ref[...], la_cat))

    def decay_rows(blk):
        return jnp.concatenate(
            [e_all[blk * cr:(blk + 1) * cr, c * GLA_DK:(c + 1) * GLA_DK] for c in range(n_chunks)],
            axis=0)

    q = q_ref[...] * (GLA_DK ** -0.5)
    k = k_ref[...]
    v = v_ref[...].astype(BF16)
    eb = decay_rows(0)
    qb = (q * eb).astype(BF16)
    kdec = (k * decay_rows(1)).astype(BF16)
    q_lv = [q.astype(BF16)]
    k_lv = [k.astype(BF16)]
    for lvl in range(6):
        el = decay_rows(2 + lvl)
        q_lv.append((q * el).astype(BF16))
        k_lv.append((k * el).astype(BF16))
    o_intra, kv_new = [], []
    for c in range(n_chunks):
        r = chunk(c)
        s = lmask_ref[6] * _bdot_nt(q_lv[0][r], k_lv[0][r])
        for lvl in range(6):
            s = s + lmask_ref[lvl] * _bdot_nt(q_lv[1 + lvl][r], k_lv[1 + lvl][r])
        o_intra.append(_bdot(s, v[r]))
        kv_new.append(_bdot_tn(v[r], kdec[r]))
    ht = ht_s[...]
    outs = []
    for c in range(n_chunks):
        outs.append(o_intra[c] + _bdot_nt(qb[chunk(c)], ht))
        ht = ht * eb[c * cr + cr - 1:(c + 1) * cr, :] + kv_new[c]
    ht_s[...] = ht
    o = jnp.concatenate(outs, axis=0)
    ms = jnp.sum(o * o, axis=-1, keepdims=True) * (1.0 / GLA_DV)
    on = o * lax.rsqrt(ms + EPS) * gon_ref[...]
    tok_ref[...] = (on * _silu(g_ref[...])).astype(tok_ref.dtype)

    @pl.when(t == pl.num_programs(2) - 1)
    def _():
        hout_ref[0] = ht


def _gla_prompt(proj, h0t, wg2, bg, gon, mall, lmask, *, batch, seq, tb):
    n = proj.shape[0]
    nt = seq // tb
    row = lambda b, h, t: b * nt + t
    cq, ck = A_COL_Q // GLA_DK, A_COL_K // GLA_DK
    cv, cg = A_COL_V // GLA_DVP, A_COL_G // GLA_DVP
    cl = A_COL_GLR // GLA_RANKP
    vmem = (2 * tb * (2 * GLA_DK + 2 * GLA_DVP + GLA_RANKP) * 4 + 2 * tb * GLA_DVP * 2
            + 6 * GLA_DVP * GLA_DK * 4 + 16 * tb * GLA_DVP * 4)
    return pl.pallas_call(
        functools.partial(_gla_kernel, n_chunks=tb // GLA_CHUNK),
        grid=(batch, GLA_HEADS, nt),
        in_specs=[
            pl.BlockSpec((tb, GLA_DK), lambda b, h, t: (row(b, h, t), cq + h)),
            pl.BlockSpec((tb, GLA_DK), lambda b, h, t: (row(b, h, t), ck + h)),
            pl.BlockSpec((tb, GLA_DVP), lambda b, h, t: (row(b, h, t), cv + h)),
            pl.BlockSpec((tb, GLA_DVP), lambda b, h, t: (row(b, h, t), cg + h)),
            pl.BlockSpec((tb, GLA_RANKP), lambda b, h, t: (row(b, h, t), cl)),
            pl.BlockSpec((GLA_RANKP, GLA_DK), lambda b, h, t: (0, h)),
            pl.BlockSpec((1, GLA_DK), lambda b, h, t: (0, h)),
            pl.BlockSpec((1, GLA_DVP), lambda b, h, t: (0, 0)),
            pl.BlockSpec(mall.shape, lambda b, h, t: (0, 0)),
            pl.BlockSpec(lmask.shape, lambda b, h, t: (0, 0, 0)),
            pl.BlockSpec((1, GLA_DVP, GLA_DK), lambda b, h, t: (b * GLA_HEADS + h, 0, 0)),
        ],
        out_specs=[
            pl.BlockSpec((tb, GLA_DVP), lambda b, h, t: (row(b, h, t), h)),
            pl.BlockSpec((1, GLA_DVP, GLA_DK), lambda b, h, t: (b * GLA_HEADS + h, 0, 0)),
        ],
        out_shape=[jax.ShapeDtypeStruct((n, GLA_VP), BF16),
                   jax.ShapeDtypeStruct((batch * GLA_HEADS, GLA_DVP, GLA_DK), F32)],
        scratch_shapes=[pltpu.VMEM((GLA_DVP, GLA_DK), F32)],
        compiler_params=_cparams(("parallel", "parallel", "arbitrary"), vmem),
        name="gla_prompt",
    )(proj, proj, proj, proj, proj, wg2, bg, gon, mall, lmask, h0t)


def _gla_step_kernel(q_ref, k_ref, v_ref, g_ref, glr_ref, wg2_ref, bg_ref, gon_ref, st_ref,
                     tok_ref, so_ref):
    glr = jnp.broadcast_to(glr_ref[0], (8, GLA_RANKP))
    z = _bdot(glr, wg2_ref[...])[0:1] + bg_ref[...]
    a = jnp.exp(_log_sigmoid(z) * (1.0 / GLA_TAU))
    q = q_ref[0] * (GLA_DK ** -0.5)
    k = k_ref[0]
    v = v_ref[0]
    g = g_ref[0]
    r = lax.broadcasted_iota(jnp.int32, (GLA_DK, GLA_DK), 0)
    c = lax.broadcasted_iota(jnp.int32, (GLA_DK, GLA_DK), 1)
    eye = r == c

    def column(row):
        return jnp.sum(jnp.where(eye, jnp.broadcast_to(row, (GLA_DK, GLA_DK)), 0.0),
                       axis=1, keepdims=True)

    gon = gon_ref[...][:, :GLA_DV]
    for h in range(GLA_HEADS):
        ks = slice(h * GLA_DK, (h + 1) * GLA_DK)
        vs = slice(h * GLA_DVP, h * GLA_DVP + GLA_DV)
        s_new = column(a[:, ks]) * st_ref[0, h] + column(k[:, ks]) * v[:, vs]
        so_ref[0, h] = s_new
        o = jnp.sum(column(q[:, ks]) * s_new, axis=0, keepdims=True)
        ms = jnp.sum(o * o, axis=-1, keepdims=True) * (1.0 / GLA_DV)
        tok = o * lax.rsqrt(ms + EPS) * gon * _silu(g[:, vs])
        tok_ref[0, :, vs] = tok.astype(tok_ref.dtype)
        tok_ref[0, :, h * GLA_DVP + GLA_DV:(h + 1) * GLA_DVP] = jnp.zeros(
            (1, GLA_DVP - GLA_DV), tok_ref.dtype)


def _gla_step(proj3, state, wg2, bg, gon):
    bsz = proj3.shape[0]
    col = lambda w, c: pl.BlockSpec((1, 1, w), lambda b: (b, 0, c))
    st_spec = pl.BlockSpec((1, GLA_HEADS, GLA_DK, GLA_DV), lambda b: (b, 0, 0, 0))
    return pl.pallas_call(
        _gla_step_kernel,
        grid=(bsz,),
        in_specs=[col(GLA_QK, A_COL_Q // GLA_QK), col(GLA_QK, A_COL_K // GLA_QK),
                  col(GLA_VP, A_COL_V // GLA_VP), col(GLA_VP, A_COL_G // GLA_VP),
                  col(GLA_RANKP, A_COL_GLR // GLA_RANKP),
                  pl.BlockSpec((GLA_RANKP, GLA_QK), lambda b: (0, 0)),
                  pl.BlockSpec((1, GLA_QK), lambda b: (0, 0)),
                  pl.BlockSpec((1, GLA_DVP), lambda b: (0, 0)),
                  st_spec],
        out_specs=[pl.BlockSpec((1, 1, GLA_VP), lambda b: (b, 0, 0)), st_spec],
        out_shape=[jax.ShapeDtypeStruct((bsz, 1, GLA_VP), F32),
                   jax.ShapeDtypeStruct(state.shape, F32)],
        compiler_params=_cparams(("parallel",), 24 * 1024 * 1024),
        name="gla_step",
    )(proj3, proj3, proj3, proj3, proj3, wg2, bg, gon, state)


def _mem_attn_kernel(q_ref, k_ref, v_ref, gq_ref, grp_ref, hmask_ref, o_ref):
    tq = q_ref.shape[1]
    q = q_ref[0]
    if tq < 8:
        q = jnp.broadcast_to(q, (8, HW))
    ss = _dot_exact_rhs(q * q, grp_ref[...])
    qn = q * lax.rsqrt(ss * (1.0 / HEAD_DIM) + EPS) * gq_ref[...] * ATTN_SCALE
    k = k_ref[0].astype(BF16)
    v = v_ref[0].astype(BF16)
    out = jnp.zeros(q.shape, F32)
    for h in range(N_HEADS):
        hm = hmask_ref[h]
        s = _bdot_nt(qn * hm, k)
        m = jnp.max(s, axis=-1, keepdims=True)
        p = jnp.exp(s - m)
        l = jnp.sum(p, axis=-1, keepdims=True)
        out = out + hm * (_bdot(p, v) / l)
    o_ref[0] = out[0:tq].astype(o_ref.dtype)


def _mem_attn(q3, qcol, k3, kcol, v3, vcol, gq, grp, hmask, *, tq):
    bsz, t, _ = q3.shape
    g = jnp.tile(gq.reshape(1, HEAD_DIM), (1, N_HEADS))
    vmem = 2 * (tq * HW * 4 + 2 * MEM_TOKENS * HW * 4 + tq * HW * 2) + 16 * max(tq, 8) * HW * 4
    return pl.pallas_call(
        _mem_attn_kernel,
        grid=(bsz, t // tq),
        in_specs=[pl.BlockSpec((1, tq, HW), lambda b, i: (b, i, qcol)),
                  pl.BlockSpec((1, MEM_TOKENS, HW), lambda b, i: (b, 0, kcol)),
                  pl.BlockSpec((1, MEM_TOKENS, HW), lambda b, i: (b, 0, vcol)),
                  pl.BlockSpec((1, HW), lambda b, i: (0, 0)),
                  pl.BlockSpec((HW, HW), lambda b, i: (0, 0)),
                  pl.BlockSpec((N_HEADS, 1, HW), lambda b, i: (0, 0, 0))],
        out_specs=pl.BlockSpec((1, tq, HW), lambda b, i: (b, i, 0)),
        out_shape=jax.ShapeDtypeStruct((bsz, t, HW), BF16 if tq >= 16 else F32),
        compiler_params=_cparams(("parallel", "parallel"), vmem),
        name="mem_attn",
    )(q3, k3, v3, g, grp, hmask)


def _band_attn_kernel(q0_ref, q1_ref, k0_ref, k1_ref, v0_ref, v1_ref, bias_ref, hmask_ref,
                      expand_ref, o_ref, og_s, ls_s, *, seq):
    g = pl.program_id(1)
    n_units = seq // BAND

    def run_group(gi, dil):
        nb = seq // dil // BAND
        shift = nb.bit_length() - 1
        lane = lax.broadcasted_iota(jnp.int32, (1, 128), 1)
        ones_cols = jnp.ones((2 * BAND, 128), BF16)

        def idx(start):
            if dil == 1:
                return pl.ds(pl.multiple_of(start, BAND), BAND)
            return pl.ds(start, BAND, stride=dil)

        def rows(ref0, ref1, start):
            return jnp.concatenate([ref0[idx(start), :], ref1[idx(start), :]], axis=1)

        def unit(u, carry):
            r = lax.shift_right_logical(u, shift)
            nblk = u & (nb - 1)
            qs = r + dil * BAND * nblk
            ks = r + dil * BAND * jnp.maximum(nblk - 1, 0)
            q = rows(q0_ref, q1_ref, qs) * ATTN_SCALE
            kk = jnp.concatenate([rows(k0_ref, k1_ref, ks), rows(k0_ref, k1_ref, qs)],
                                 axis=0).astype(BF16)
            vv = jnp.concatenate([rows(v0_ref, v1_ref, ks), rows(v0_ref, v1_ref, qs)],
                                 axis=0).astype(BF16)
            q4 = jnp.concatenate([q * hmask_ref[h] for h in range(N_HEADS)], axis=0)
            bias = bias_ref[jnp.minimum(nblk, 1)]
            s = _bdot_nt(q4, kk).reshape(N_HEADS, BAND, 2 * BAND) + bias[None]
            s = s.reshape(N_HEADS * BAND, 2 * BAND)
            m = jnp.max(s, axis=-1, keepdims=True)
            p = jnp.exp(s - m)
            oe = _bdot(p, jnp.concatenate([vv, ones_cols], axis=1))
            l = oe[:, HW:]
            o4 = oe[:, :HW] / jnp.concatenate([l, l], axis=1)
            lse4 = m + jnp.log(l)
            out = jnp.zeros((BAND, HW), F32)
            lse_t = jnp.zeros((BAND, 128), F32)
            for h in range(N_HEADS):
                hr = slice(h * BAND, (h + 1) * BAND)
                out = out + hmask_ref[h] * o4[hr]
                lse_t = lse_t + jnp.where(lane == h, lse4[hr], 0.0)
            og_s[2 * gi, idx(qs), :] = out[:, :128]
            og_s[2 * gi + 1, idx(qs), :] = out[:, 128:]
            ls_s[gi, idx(qs), :] = lse_t
            return carry

        lax.fori_loop(0, n_units, unit, 0, unroll=2)

    for gi, (_, dil) in enumerate(DIL_PAIRS):
        pl.when(g == gi)(functools.partial(run_group, gi, dil))

    @pl.when(g == N_GROUPS - 1)
    def _merge():
        expand = expand_ref[...]
        tr = 512
        for c in range(seq // tr):
            rows = slice(c * tr, (c + 1) * tr)
            l0, l1, l2 = ls_s[0, rows, :], ls_s[1, rows, :], ls_s[2, rows, :]
            m = jnp.maximum(jnp.maximum(l0, l1), l2)
            es = (jnp.exp(l0 - m), jnp.exp(l1 - m), jnp.exp(l2 - m))
            inv = 1.0 / (es[0] + es[1] + es[2])
            acc = jnp.zeros((tr, HW), F32)
            for gi in range(N_GROUPS):
                o_g = jnp.concatenate([og_s[2 * gi, rows, :], og_s[2 * gi + 1, rows, :]], axis=1)
                acc = acc + _dot_exact_rhs(es[gi] * inv, expand) * o_g
            o_ref[rows, :] = acc.astype(o_ref.dtype)


def _band_attn(qd, k_sh, kv, band, hmask, expand, *, batch, seq):
    n = qd.shape[0]
    once = pl.Buffered(1)
    half = lambda cmap: pl.BlockSpec((seq, 128), cmap, pipeline_mode=once)
    vmem = (6 * seq * 128 * 4 + 2 * seq * HW * 2 + 9 * seq * 128 * 4 + 8 * 1024 * 1024)
    return pl.pallas_call(
        functools.partial(_band_attn_kernel, seq=seq),
        grid=(batch, N_GROUPS),
        in_specs=[half(lambda b, g: (b, 2 * g)), half(lambda b, g: (b, 2 * g + 1)),
                  half(lambda b, g: (b, 0)), half(lambda b, g: (b, 1)),
                  half(lambda b, g: (b, 2)), half(lambda b, g: (b, 3)),
                  pl.BlockSpec((2, BAND, 2 * BAND), lambda b, g: (0, 0, 0)),
                  pl.BlockSpec((N_HEADS, 1, HW), lambda b, g: (0, 0, 0)),
                  pl.BlockSpec((128, HW), lambda b, g: (0, 0))],
        out_specs=pl.BlockSpec((seq, HW), lambda b, g: (b, 0)),
        out_shape=jax.ShapeDtypeStruct((n, HW), BF16),
        scratch_shapes=[pltpu.VMEM((2 * N_GROUPS, seq, 128), F32),
                        pltpu.VMEM((N_GROUPS, seq, 128), F32)],
        compiler_params=_cparams(("parallel", "arbitrary"), vmem),
        name="band_attn",
    )(qd, qd, k_sh, k_sh, kv, kv, band, hmask, expand)


def _step_attn_kernel(q_ref, kn_ref, vn_ref, kc0_ref, kc1_ref, vc0_ref, vc1_ref, hmask_ref, o_ref,
                      *, cache_len):
    kn = kn_ref[0]
    vn = vn_ref[0]
    outs, lses = [], []
    for gi, (_, dil) in enumerate(DIL_PAIRS):
        start = cache_len - dil * BAND
        idx = pl.ds(start, BAND) if dil == 1 else pl.ds(start, BAND, stride=dil)
        kc = jnp.concatenate([kc0_ref[idx, :], kc1_ref[idx, :]], axis=1).astype(BF16)
        vc = jnp.concatenate([vc0_ref[idx, :], vc1_ref[idx, :]], axis=1).astype(BF16)
        q = q_ref[0][:, gi * HW:(gi + 1) * HW] * ATTN_SCALE
        out = jnp.zeros((1, HW), F32)
        lse = jnp.zeros((1, HW), F32)
        for h in range(N_HEADS):
            hm = hmask_ref[h]
            qh = q * hm
            s_c = _bdot_nt(jnp.broadcast_to(qh, (8, HW)), kc)[0:1]
            s_n = jnp.sum(qh * kn, axis=-1, keepdims=True)
            m = jnp.maximum(jnp.max(s_c, axis=-1, keepdims=True), s_n)
            p_c = jnp.exp(s_c - m)
            p_n = jnp.exp(s_n - m)
            l = jnp.sum(p_c, axis=-1, keepdims=True) + p_n
            o = (_bdot(jnp.broadcast_to(p_c, (8, BAND)), vc)[0:1] + p_n * vn) / l
            out = out + hm * o
            lse = lse + hm * (m + jnp.log(l))
        outs.append(out)
        lses.append(lse)
    m = jnp.maximum(jnp.maximum(lses[0], lses[1]), lses[2])
    es = [jnp.exp(x - m) for x in lses]
    inv = 1.0 / (es[0] + es[1] + es[2])
    acc = es[0] * inv * outs[0] + es[1] * inv * outs[1] + es[2] * inv * outs[2]
    o_ref[0] = acc.astype(o_ref.dtype)


def _step_attn(q3, kn3, vn3, kc, vc, hmask):
    bsz, cache_len, _ = kc.shape
    assert cache_len >= max(d for _, d in DIL_PAIRS) * BAND
    vmem = 4 * cache_len * HW * 4 + 4 * 1024 * 1024
    return pl.pallas_call(
        functools.partial(_step_attn_kernel, cache_len=cache_len),
        grid=(bsz,),
        in_specs=[pl.BlockSpec((1, 1, N_GROUPS * HW), lambda b: (b, 0, 0)),
                  pl.BlockSpec((1, 1, HW), lambda b: (b, 0, 0)),
                  pl.BlockSpec((1, 1, HW), lambda b: (b, 0, 0)),
                  pl.BlockSpec((None, cache_len, 128), lambda b: (b, 0, 0)),
                  pl.BlockSpec((None, cache_len, 128), lambda b: (b, 0, 1)),
                  pl.BlockSpec((None, cache_len, 128), lambda b: (b, 0, 0)),
                  pl.BlockSpec((None, cache_len, 128), lambda b: (b, 0, 1)),
                  pl.BlockSpec((N_HEADS, 1, HW), lambda b: (0, 0, 0))],
        out_specs=pl.BlockSpec((1, 1, HW), lambda b: (b, 0, 0)),
        out_shape=jax.ShapeDtypeStruct((bsz, 1, HW), F32),
        compiler_params=_cparams(("parallel",), vmem),
        name="step_attn",
    )(q3, kn3, vn3, kc, kc, vc, vc, hmask)


def _out_ffn_kernel(h_ref, tok_ref, mem_ref, wo1_ref, wo2_ref, g2_ref, wgu_ref, wdn_ref, o_ref):
    h1 = (h_ref[...]
          + jnp.dot(tok_ref[...], wo1_ref[...], preferred_element_type=F32)
          + jnp.dot(mem_ref[...], wo2_ref[...], preferred_element_type=F32))
    ms = jnp.mean(h1 * h1, axis=-1, keepdims=True)
    hn = (h1 * lax.rsqrt(ms + EPS) * g2_ref[...]).astype(BF16)
    gate = jnp.dot(hn, wgu_ref[:, :D_FF], preferred_element_type=F32)
    up = jnp.dot(hn, wgu_ref[:, D_FF:], preferred_element_type=F32)
    act = (_silu(gate) * up).astype(BF16)
    o_ref[...] = h1 + jnp.dot(act, wdn_ref[...], preferred_element_type=F32)


def _out_ffn(h, tok, mem, wo1, wo2, g2, wgu, wdn, *, tm):
    n, d = h.shape
    wt = tok.shape[1]
    once = pl.Buffered(1)
    const = lambda shape: pl.BlockSpec(shape, lambda i: (0, 0), pipeline_mode=once)
    vmem = ((wt + HW) * d * 2 + 3 * D_FF * d * 2
            + 2 * tm * (2 * d * 4 + (wt + HW) * 2) + 3 * tm * d * 4 + 3 * tm * D_FF * 4
            + 4 * 1024 * 1024)
    return pl.pallas_call(
        _out_ffn_kernel,
        grid=(n // tm,),
        in_specs=[pl.BlockSpec((tm, d), lambda i: (i, 0)),
                  pl.BlockSpec((tm, wt), lambda i: (i, 0)),
                  pl.BlockSpec((tm, HW), lambda i: (i, 0)),
                  const((wt, d)), const((HW, d)), const((1, d)),
                  const((d, 2 * D_FF)), const((D_FF, d))],
        out_specs=pl.BlockSpec((tm, d), lambda i: (i, 0)),
        out_shape=jax.ShapeDtypeStruct((n, d), F32),
        compiler_params=_cparams(("parallel",), vmem),
        name="out_ffn",
    )(h, tok, mem, wo1, wo2, g2.reshape(1, d), wgu, wdn)


def _pad_heads_cols(w):
    d = w.shape[0]
    return jnp.pad(w.reshape(d, GLA_HEADS, GLA_DV),
                   ((0, 0), (0, 0), (0, GLA_DVP - GLA_DV))).reshape(d, GLA_VP)


def _pack_a_in(w):
    q = w[:, :GLA_QK]
    k = w[:, GLA_QK:2 * GLA_QK]
    v = w[:, 2 * GLA_QK:2 * GLA_QK + GLA_V]
    g = w[:, 2 * GLA_QK + GLA_V:2 * GLA_QK + 2 * GLA_V]
    glr = w[:, 2 * GLA_QK + 2 * GLA_V:2 * GLA_QK + 2 * GLA_V + GLA_RANK]
    mq = w[:, 2 * GLA_QK + 2 * GLA_V + GLA_RANK:]
    glr = jnp.pad(glr, ((0, 0), (0, GLA_RANKP - GLA_RANK)))
    return jnp.concatenate([q, k, _pad_heads_cols(v), _pad_heads_cols(g), mq, glr],
                           axis=1).astype(BF16)


def _pack_a_out(w):
    tok = w[:GLA_V].reshape(GLA_HEADS, GLA_DV, D_MODEL)
    tok = jnp.pad(tok, ((0, 0), (0, GLA_DVP - GLA_DV), (0, 0))).reshape(GLA_VP, D_MODEL)
    return tok.astype(BF16), w[GLA_V:].astype(BF16)


def kernel(x_prompt, x_sample, state_gla, cache_win_k, cache_win_v, cache_mem_k, cache_mem_v,
           mem_prompt, norm1, norm2, a_w_in, a_w_gate2, a_b_gate, a_g_onorm, a_w_out, kv_norm, w_kv,
           g_k, b_w_in, b_g_q, b_w_out, mem_norm, w_mem_kv, g_mem_q, g_mem_k, w_ffn_gu, w_ffn_down):
    batch, seq, d = x_prompt.shape
    dec_b = x_sample.shape[0]
    past_len = 8192
    assert d == D_MODEL and seq % 512 == 0 and x_sample.shape[1] == 1

    mall_np, lmask_np = _gla_tables()
    grp_np, hmask_np, expand_np = _head_tables()
    mall = jnp.asarray(mall_np, BF16)
    lmask = jnp.asarray(lmask_np, F32)
    grp = jnp.asarray(grp_np, BF16)
    hmask = jnp.asarray(hmask_np, F32)
    expand = jnp.asarray(expand_np, BF16)
    band = jnp.asarray(_band_bias(), F32)

    cos_p, sin_p = _rope_tables(jnp.arange(seq, dtype=jnp.int32))
    cos_s, sin_s = _rope_tables(past_len + jnp.arange(1, dtype=jnp.int32))
    ones_t = jnp.ones((1, HW), F32)
    zeros_t = jnp.zeros((1, HW), F32)

    wa_in = [_pack_a_in(a_w_in[i]) for i in range(N_A)]
    wa_out = [_pack_a_out(a_w_out[i]) for i in range(N_A)]
    wg2 = [jnp.pad(a_w_gate2[i], ((0, GLA_RANKP - GLA_RANK), (0, 0))).astype(BF16)
           for i in range(N_A)]
    bg = [a_b_gate[i].reshape(1, GLA_QK) for i in range(N_A)]
    gon = [jnp.pad(a_g_onorm[i], (0, GLA_DVP - GLA_DV)).reshape(1, GLA_DVP) for i in range(N_A)]
    wb_in = [b_w_in[i].astype(BF16) for i in range(DEPTH - N_A)]
    wb_out = [(b_w_out[i][:HW].astype(BF16), b_w_out[i][HW:].astype(BF16))
              for i in range(DEPTH - N_A)]
    w_kv_b = w_kv.astype(BF16)
    w_mem_b = w_mem_kv.astype(BF16)
    wgu = w_ffn_gu.astype(BF16)
    wdn = w_ffn_down.astype(BF16)

    mem_rows = mem_prompt.reshape(batch * MEM_TOKENS, d)
    mem_kv_p, mem_k_p = [], []
    for li in range(DEPTH):
        kv = _norm_matmul(mem_rows, mem_norm[li], w_mem_b[li], tm=512, tn=2 * HW, name="mem_kv")
        mk = _headnorm(kv, g_mem_k[li], (grp, ones_t, zeros_t), width=HW, rope=False, tr=512,
                       name="mem_knorm")
        mem_kv_p.append(kv.reshape(batch, MEM_TOKENS, 2 * HW))
        mem_k_p.append(mk.reshape(batch, MEM_TOKENS, HW))

    def trunk(x2, bsz, t, prompt):
        n = bsz * t
        tm = 512 if prompt else n
        h = x2
        states = []
        k_sh = kv_sh = None
        for li in range(DEPTH):
            if li < N_A:
                proj = _norm_matmul(h, norm1[li], wa_in[li], tm=tm, tn=A_IN_P // 3, name="a_in")
                proj3 = proj.reshape(bsz, t, A_IN_P)
                mq_col = A_COL_MQ // HW
                if prompt:
                    h0t = jnp.zeros((bsz * GLA_HEADS, GLA_DVP, GLA_DK), F32)
                    tok, st = _gla_prompt(proj, h0t, wg2[li], bg[li], gon[li], mall, lmask,
                                          batch=bsz, seq=t, tb=512)
                    st = jnp.swapaxes(st.reshape(bsz, GLA_HEADS, GLA_DVP, GLA_DK), 2, 3)
                    states.append(st[..., :GLA_DV])
                else:
                    tok, st = _gla_step(proj3, state_gla[li], wg2[li], bg[li], gon[li])
                    tok = tok.reshape(n, GLA_VP).astype(BF16)
                    states.append(st)
                wo1, wo2 = wa_out[li]
            else:
                bi = li - N_A
                proj = _norm_matmul(h, norm1[li], wb_in[bi], tm=tm, tn=D_MODEL, name="b_in")
                proj3 = proj.reshape(bsz, t, D_MODEL)
                mq_col = (N_GROUPS * HW) // HW
                if prompt:
                    qd = _headnorm(proj, b_g_q[bi], (grp, cos_p, sin_p), width=N_GROUPS * HW,
                                   rope=True, tr=512, name="q_rope", pos_blocks=t // 512)
                    tok = _band_attn(qd, k_sh, kv_sh, band, hmask, expand, batch=bsz, seq=t)
                else:
                    qd = _headnorm(proj, b_g_q[bi], (grp, cos_s, sin_s), width=N_GROUPS * HW,
                                   rope=True, tr=n, name="q_rope")
                    tok = _step_attn(qd.reshape(bsz, 1, N_GROUPS * HW), k_sh.reshape(bsz, 1, HW),
                                     kv_sh[:, HW:].reshape(bsz, 1, HW),
                                     cache_win_k.reshape(bsz, -1, HW),
                                     cache_win_v.reshape(bsz, -1, HW), hmask)
                    tok = tok.reshape(n, HW).astype(BF16)
                wo1, wo2 = wb_out[bi]
            if prompt:
                mem_o = _mem_attn(proj3, mq_col, mem_k_p[li], 0, mem_kv_p[li], 1, g_mem_q[li],
                                  grp, hmask, tq=512)
            else:
                mem_o = _mem_attn(proj3, mq_col, cache_mem_k[li].reshape(bsz, MEM_TOKENS, HW), 0,
                                  cache_mem_v[li].reshape(bsz, MEM_TOKENS, HW), 0, g_mem_q[li],
                                  grp, hmask, tq=1)
            h = _out_ffn(h, tok, mem_o.reshape(n, HW).astype(BF16), wo1, wo2, norm2[li], wgu[li],
                         wdn[li], tm=tm)
            if li == N_A - 1:
                kv_sh = _norm_matmul(h, kv_norm, w_kv_b, tm=tm, tn=2 * HW, name="shared_kv")
                tabs = (grp, cos_p, sin_p) if prompt else (grp, cos_s, sin_s)
                k_sh = _headnorm(kv_sh, g_k, tabs, width=HW, rope=True, tr=tm, name="k_rope",
                                 pos_blocks=max(t // tm, 1))
        return h, states, k_sh, kv_sh[:, HW:]

    y_p, gla_p, k_p, v_p = trunk(x_prompt.reshape(batch * seq, d), batch, seq, True)
    y_s, gla_s, k_s, v_s = trunk(x_sample.reshape(dec_b, d), dec_b, 1, False)

    keep = min(WIN_MAX, seq)
    k_p4 = k_p.reshape(batch, seq, N_HEADS, HEAD_DIM)
    v_p4 = v_p.reshape(batch, seq, N_HEADS, HEAD_DIM)
    mem_k_out = jnp.stack(mem_k_p, 0).reshape(DEPTH, batch, MEM_TOKENS, N_HEADS, HEAD_DIM)
    mem_v_out = jnp.stack([kv[..., HW:] for kv in mem_kv_p], 0).reshape(
        DEPTH, batch, MEM_TOKENS, N_HEADS, HEAD_DIM)
    return (y_p.reshape(batch, seq, d),
            y_s.reshape(dec_b, 1, d),
            jnp.stack(gla_p, 0),
            jnp.stack(gla_s, 0),
            k_p4[:, -keep:],
            v_p4[:, -keep:],
            k_s.reshape(dec_b, 1, N_HEADS, HEAD_DIM),
            v_s.reshape(dec_b, 1, N_HEADS, HEAD_DIM),
            mem_k_out,
            mem_v_out)
```

```python
import functools

import numpy as np
import jax
import jax.numpy as jnp
from jax import lax
from jax.experimental import pallas as pl
from jax.experimental.pallas import tpu as pltpu

F32 = jnp.float32
BF16 = jnp.bfloat16

D_MODEL = 1024
DEPTH = 4
N_A = 2
GLA_HEADS = 4
GLA_DK = 128
GLA_DV = 192
GLA_DVP = 256
GLA_QK = GLA_HEADS * GLA_DK
GLA_V = GLA_HEADS * GLA_DV
GLA_VP = GLA_HEADS * GLA_DVP
GLA_RANK = 16
GLA_RANKP = 128
GLA_TAU = 16.0
GLA_CHUNK = 64
HEAD_DIM = 64
N_HEADS = 4
HW = N_HEADS * HEAD_DIM
DIL_PAIRS = ((128, 1), (512, 4), (2048, 16))
N_GROUPS = 3
BAND = 128
WIN_MAX = 2048
MEM_TOKENS = 256
D_FF = 2816
FF_TILE = 256
ROPE_THETA = 10000.0
EPS = 1e-6
ATTN_SCALE = HEAD_DIM ** -0.5
NEG = -1e30

A_COL_Q = 0
A_COL_K = GLA_QK
A_COL_V = 2 * GLA_QK
A_COL_G = A_COL_V + GLA_VP
A_COL_MQ = A_COL_G + GLA_VP
A_COL_GLR = A_COL_MQ + HW
A_IN_P = A_COL_GLR + GLA_RANKP

V7X_VMEM_BYTES = 64 * 1024 * 1024
VMEM_CAP = 56 * 1024 * 1024


def _cparams(sem, vmem_bytes):
    return pltpu.CompilerParams(
        dimension_semantics=sem,
        vmem_limit_bytes=int(min(max(vmem_bytes, 16 * 1024 * 1024), VMEM_CAP)))


def _bdot(a, b):
    return jnp.dot(a.astype(BF16), b.astype(BF16), preferred_element_type=F32)


def _bdot_nt(a, b):
    return lax.dot_general(a.astype(BF16), b.astype(BF16), (((1,), (1,)), ((), ())),
                           preferred_element_type=F32)


def _bdot_tn(a, b):
    return lax.dot_general(a.astype(BF16), b.astype(BF16), (((0,), (0,)), ((), ())),
                           preferred_element_type=F32)


def _split(x):
    hi = x.astype(BF16)
    lo = (x - hi.astype(F32)).astype(BF16)
    return hi, lo


def _dot_exact_rhs(x, m):
    hi, lo = _split(x)
    return (jnp.dot(hi, m, preferred_element_type=F32)
            + jnp.dot(lo, m, preferred_element_type=F32))


def _dot_exact_lhs(m, x):
    hi, lo = _split(x)
    return (jnp.dot(m, hi, preferred_element_type=F32)
            + jnp.dot(m, lo, preferred_element_type=F32))


def _silu(x):
    return x * jax.nn.sigmoid(x)


def _log_sigmoid(z):
    return jnp.minimum(z, 0.0) - jnp.log1p(jnp.exp(-jnp.abs(z)))


def _gla_tables():
    c = GLA_CHUNK
    i = np.arange(c)[:, None]
    t = np.arange(c)[None, :]
    blocks = [(t <= i), (t > i)]
    masks = []
    for lvl in range(6):
        s = c >> lvl
        half = s // 2
        mid = (i // s) * s + half - 1
        second = (i % s) >= half
        m = np.where(second, (t > mid) & (t <= i), (t > i) & (t <= mid))
        blocks.append(m)
        j = t
        masks.append(((i // s) == (j // s)) & ((i % s) >= half) & ((j % s) < half))
    masks.append(i == t)
    mall = np.concatenate(blocks, axis=0).astype(np.float32)
    lmask = np.stack(masks, axis=0).astype(np.float32)
    assert np.array_equal(lmask.sum(0), (t <= i).astype(np.float32))
    return mall, lmask


def _head_tables():
    lane = np.arange(HW)
    group = (lane[:, None] // HEAD_DIM == lane[None, :] // HEAD_DIM).astype(np.float32)
    hmask = (lane[None, :] // HEAD_DIM == np.arange(N_HEADS)[:, None]).astype(np.float32)
    expand = np.zeros((128, HW), np.float32)
    expand[:N_HEADS] = hmask
    return group, hmask.reshape(N_HEADS, 1, HW), expand


def _band_bias():
    i = np.arange(BAND)[:, None]
    j = np.arange(2 * BAND)[None, :]
    dist = i + BAND - j
    ok = (dist >= 0) & (dist <= BAND)
    first = ok & (j >= BAND)
    return np.where(np.stack([first, ok], 0), 0.0, NEG).astype(np.float32)


def _rope_tables(pos):
    half = HEAD_DIM // 2
    inv = ROPE_THETA ** (-jnp.arange(half, dtype=F32) / half)
    ang = pos.astype(F32)[:, None] * inv[None, :]
    cos = jnp.cos(ang)
    sin = jnp.sin(ang)
    cos64 = jnp.concatenate([cos, cos], axis=-1)
    sin64 = jnp.concatenate([-sin, sin], axis=-1)
    return jnp.tile(cos64, (1, N_HEADS)), jnp.tile(sin64, (1, N_HEADS))


def _norm_matmul_kernel(x_ref, g_ref, w_ref, o_ref):
    x = x_ref[...]
    ms = jnp.mean(x * x, axis=-1, keepdims=True)
    y = x * lax.rsqrt(ms + EPS) * g_ref[...]
    o_ref[...] = jnp.dot(y.astype(BF16), w_ref[...], preferred_element_type=F32)


def _norm_matmul(x, g, gi, w, wi, *, tm, tn, name):
    n, d = x.shape
    nout = w.shape[2]
    vmem = 2 * (tm * d * 4 + d * tn * 2 + tm * tn * 4) + 2 * tm * d * 4 + tm * tn * 4
    return pl.pallas_call(
        _norm_matmul_kernel,
        grid=(n // tm, nout // tn),
        in_specs=[pl.BlockSpec((tm, d), lambda i, j: (i, 0)),
                  pl.BlockSpec((None, 1, d), lambda i, j: (gi, 0, 0)),
                  pl.BlockSpec((None, d, tn), lambda i, j: (wi, 0, j))],
        out_specs=pl.BlockSpec((tm, tn), lambda i, j: (i, j)),
        out_shape=jax.ShapeDtypeStruct((n, nout), F32),
        compiler_params=_cparams(("parallel", "parallel"), vmem),
        name=name,
    )(x, g, w)


def _headnorm_kernel(x_ref, g_ref, grp_ref, cos_ref, sin_ref, o_ref, *, rope):
    width = x_ref.shape[1]
    grp = grp_ref[...]
    lane = lax.broadcasted_iota(jnp.int32, (1, HW), 1)
    first_half = (lane & (HEAD_DIM - 1)) < (HEAD_DIM // 2)
    for c in range(width // HW):
        sl = slice(c * HW, (c + 1) * HW)
        x = x_ref[:, sl]
        ss = _dot_exact_rhs(x * x, grp)
        y = x * lax.rsqrt(ss * (1.0 / HEAD_DIM) + EPS) * g_ref[:, sl]
        if rope:
            fwd = pltpu.roll(y, HW - HEAD_DIM // 2, 1)
            bwd = pltpu.roll(y, HEAD_DIM // 2, 1)
            rot = jnp.where(first_half, fwd, bwd)
            y = y * cos_ref[...] + rot * sin_ref[...]
        o_ref[:, sl] = y


def _headnorm(x, gain, tabs, *, width, rope, tr, name, pos_blocks=1):
    n = x.shape[0]
    grp, cos, sin = tabs
    g = jnp.tile(gain.reshape(1, HEAD_DIM), (1, width // HEAD_DIM))
    trow = cos.shape[0] if cos.shape[0] == 1 else tr
    if cos.shape[0] == 1:
        tab_map = lambda i: (0, 0)
    else:
        tab_map = lambda i: (i % pos_blocks, 0)
    vmem = 2 * (2 * tr * width * 4 + 2 * trow * HW * 4) + 6 * tr * HW * 4 + HW * HW * 4
    return pl.pallas_call(
        functools.partial(_headnorm_kernel, rope=rope),
        grid=(n // tr,),
        in_specs=[pl.BlockSpec((tr, width), lambda i: (i, 0)),
                  pl.BlockSpec((1, width), lambda i: (0, 0)),
                  pl.BlockSpec((HW, HW), lambda i: (0, 0)),
                  pl.BlockSpec((trow, HW), tab_map),
                  pl.BlockSpec((trow, HW), tab_map)],
        out_specs=pl.BlockSpec((tr, width), lambda i: (i, 0)),
        out_shape=jax.ShapeDtypeStruct((n, width), F32),
        compiler_params=_cparams(("parallel",), vmem),
        name=name,
    )(x, g, grp, cos, sin)


def _gla_kernel(q_ref, k_ref, v_ref, g_ref, glr_ref, wg2_ref, bg_ref, gon_ref, mall_ref, lmask_ref,
                h0_ref, tok_ref, hout_ref, ht_s, *, n_chunks):
    t = pl.program_id(2)

    @pl.when(t == 0)
    def _():
        ht_s[...] = h0_ref[0]

    z = _bdot(glr_ref[...], wg2_ref[...]) + bg_ref[...]
    la = _log_sigmoid(z) * (1.0 / GLA_TAU)
    cr = GLA_CHUNK
    chunk = lambda c: slice(c * cr, (c + 1) * cr)
    la_cat = jnp.concatenate([la[chunk(c)] for c in range(n_chunks)], axis=1)
    e_all = jnp.exp(_dot_exact_lhs(mall_ref[...], la_cat))

    def decay_rows(blk):
        return jnp.concatenate(
            [e_all[blk * cr:(blk + 1) * cr, c * GLA_DK:(c + 1) * GLA_DK] for c in range(n_chunks)],
            axis=0)

    q = q_ref[...] * (GLA_DK ** -0.5)
    k = k_ref[...]
    v = v_ref[...].astype(BF16)
    eb = decay_rows(0)
    qb = (q * eb).astype(BF16)
    kdec = (k * decay_rows(1)).astype(BF16)
    q_lv = [q.astype(BF16)]
    k_lv = [k.astype(BF16)]
    for lvl in range(6):
        el = decay_rows(2 + lvl)
        q_lv.append((q * el).astype(BF16))
        k_lv.append((k * el).astype(BF16))
    o_intra, kv_new = [], []
    for c in range(n_chunks):
        r = chunk(c)
        s = lmask_ref[6] * _bdot_nt(q_lv[0][r], k_lv[0][r])
        for lvl in range(6):
            s = s + lmask_ref[lvl] * _bdot_nt(q_lv[1 + lvl][r], k_lv[1 + lvl][r])
        o_intra.append(_bdot(s, v[r]))
        kv_new.append(_bdot_tn(v[r], kdec[r]))
    ht = ht_s[...]
    outs = []
    for c in range(n_chunks):
        outs.append(o_intra[c] + _bdot_nt(qb[chunk(c)], ht))
        ht = ht * eb[c * cr + cr - 1:(c + 1) * cr, :] + kv_new[c]
    ht_s[...] = ht
    o = jnp.concatenate(outs, axis=0)
    ms = jnp.sum(o * o, axis=-1, keepdims=True) * (1.0 / GLA_DV)
    on = o * lax.rsqrt(ms + EPS) * gon_ref[...]
    tok_ref[...] = (on * _silu(g_ref[...])).astype(tok_ref.dtype)

    @pl.when(t == pl.num_programs(2) - 1)
    def _():
        hout_ref[0] = ht


def _gla_prompt(proj, h0t, wg2, bg, gon, mall, lmask, *, batch, seq, tb):
    n = proj.shape[0]
    nt = seq // tb
    row = lambda b, h, t: b * nt + t
    cq, ck = A_COL_Q // GLA_DK, A_COL_K // GLA_DK
    cv, cg = A_COL_V // GLA_DVP, A_COL_G // GLA_DVP
    cl = A_COL_GLR // GLA_RANKP
    vmem = (2 * tb * (2 * GLA_DK + 2 * GLA_DVP + GLA_RANKP) * 4 + 2 * tb * GLA_DVP * 2
            + 6 * GLA_DVP * GLA_DK * 4 + 16 * tb * GLA_DVP * 4)
    return pl.pallas_call(
        functools.partial(_gla_kernel, n_chunks=tb // GLA_CHUNK),
        grid=(batch, GLA_HEADS, nt),
        in_specs=[
            pl.BlockSpec((tb, GLA_DK), lambda b, h, t: (row(b, h, t), cq + h)),
            pl.BlockSpec((tb, GLA_DK), lambda b, h, t: (row(b, h, t), ck + h)),
            pl.BlockSpec((tb, GLA_DVP), lambda b, h, t: (row(b, h, t), cv + h)),
            pl.BlockSpec((tb, GLA_DVP), lambda b, h, t: (row(b, h, t), cg + h)),
            pl.BlockSpec((tb, GLA_RANKP), lambda b, h, t: (row(b, h, t), cl)),
            pl.BlockSpec((GLA_RANKP, GLA_DK), lambda b, h, t: (0, h)),
            pl.BlockSpec((1, GLA_DK), lambda b, h, t: (0, h)),
            pl.BlockSpec((1, GLA_DVP), lambda b, h, t: (0, 0)),
            pl.BlockSpec(mall.shape, lambda b, h, t: (0, 0)),
            pl.BlockSpec(lmask.shape, lambda b, h, t: (0, 0, 0)),
            pl.BlockSpec((1, GLA_DVP, GLA_DK), lambda b, h, t: (b * GLA_HEADS + h, 0, 0)),
        ],
        out_specs=[
            pl.BlockSpec((tb, GLA_DVP), lambda b, h, t: (row(b, h, t), h)),
            pl.BlockSpec((1, GLA_DVP, GLA_DK), lambda b, h, t: (b * GLA_HEADS + h, 0, 0)),
        ],
        out_shape=[jax.ShapeDtypeStruct((n, GLA_VP), BF16),
                   jax.ShapeDtypeStruct((batch * GLA_HEADS, GLA_DVP, GLA_DK), F32)],
        scratch_shapes=[pltpu.VMEM((GLA_DVP, GLA_DK), F32)],
        compiler_params=_cparams(("parallel", "parallel", "arbitrary"), vmem),
        name="gla_prompt",
    )(proj, proj, proj, proj, proj, wg2, bg, gon, mall, lmask, h0t)


STEP_ROWS = 8


def _gla_step_kernel(q_ref, k_ref, v_ref, g_ref, glr_ref, wg2_ref, bg_ref, gon_ref, st_ref, *rest):
    tok_ref, so_ref = rest[-2:]
    nb = q_ref.shape[0]
    z = _bdot(glr_ref[...], wg2_ref[...]) + bg_ref[...]
    a = jnp.exp(_log_sigmoid(z) * (1.0 / GLA_TAU))
    q = q_ref[...] * (GLA_DK ** -0.5)
    k = k_ref[...]
    v = v_ref[...]
    g = g_ref[...]
    gon = gon_ref[...][:, :GLA_DV]
    row = lax.broadcasted_iota(jnp.int32, (nb, 1), 0)
    for h in range(GLA_HEADS):
        ks = slice(h * GLA_DK, (h + 1) * GLA_DK)
        vs = slice(h * GLA_DVP, h * GLA_DVP + GLA_DV)
        o_rows = jnp.zeros((nb, GLA_DV), F32)
        for b in range(nb):
            only_b = row == b
            outer = _bdot_tn(jnp.where(only_b, v[:, vs], 0.0), k[:, ks])
            s_new = st_ref[0, b, h] * a[b:b + 1, ks] + outer
            so_ref[0, b, h] = s_new
            o_rows = o_rows + jnp.where(only_b, _bdot_nt(q[:, ks], s_new), 0.0)
        ms = jnp.sum(o_rows * o_rows, axis=-1, keepdims=True) * (1.0 / GLA_DV)
        tok_ref[:, vs] = o_rows * lax.rsqrt(ms + EPS) * gon * _silu(g[:, vs])
        tok_ref[:, h * GLA_DVP + GLA_DV:(h + 1) * GLA_DVP] = jnp.zeros(
            (nb, GLA_DVP - GLA_DV), F32)


def _gla_step(proj, state_t, li, prev_out, wg2, bg, gon):
    bsz = proj.shape[0]
    nb = STEP_ROWS
    col = lambda w, c: pl.BlockSpec((nb, w), lambda i: (i, c))
    st_spec = pl.BlockSpec((1, nb, GLA_HEADS, GLA_DV, GLA_DK), lambda i: (li, i, 0, 0, 0))
    in_specs = [col(GLA_QK, A_COL_Q // GLA_QK), col(GLA_QK, A_COL_K // GLA_QK),
                col(GLA_VP, A_COL_V // GLA_VP), col(GLA_VP, A_COL_G // GLA_VP),
                col(GLA_RANKP, A_COL_GLR // GLA_RANKP),
                pl.BlockSpec((GLA_RANKP, GLA_QK), lambda i: (0, 0)),
                pl.BlockSpec((1, GLA_QK), lambda i: (0, 0)),
                pl.BlockSpec((1, GLA_DVP), lambda i: (0, 0)),
                st_spec]
    args = [proj, proj, proj, proj, proj, wg2, bg, gon, state_t]
    aliases = {}
    if prev_out is not None:
        in_specs.append(pl.BlockSpec(memory_space=pl.ANY))
        args.append(prev_out)
        aliases = {len(args) - 1: 1}
    return pl.pallas_call(
        _gla_step_kernel,
        grid=(bsz // nb,),
        in_specs=in_specs,
        out_specs=[pl.BlockSpec((nb, GLA_VP), lambda i: (i, 0)), st_spec],
        out_shape=[jax.ShapeDtypeStruct((bsz, GLA_VP), F32),
                   jax.ShapeDtypeStruct(state_t.shape, F32)],
        input_output_aliases=aliases,
        compiler_params=_cparams(("parallel",), 32 * 1024 * 1024),
        name="gla_step",
    )(*args)


def _mem_qnorm(q, gq_ref, grp_ref):
    ss = _dot_exact_rhs(q * q, grp_ref[...])
    return q * lax.rsqrt(ss * (1.0 / HEAD_DIM) + EPS) * gq_ref[...] * ATTN_SCALE


def _mem_attn_kernel(q_ref, kt_ref, vt_ref, gq_ref, grp_ref, hmask_ref, o_ref):
    qn = _mem_qnorm(q_ref[0], gq_ref, grp_ref)
    kt = kt_ref[0, 0].astype(BF16)
    vt = vt_ref[0, 0].astype(BF16)
    out = jnp.zeros(qn.shape, F32)
    for h in range(N_HEADS):
        hm = hmask_ref[h]
        s = _bdot(qn * hm, kt)
        m = jnp.max(s, axis=-1, keepdims=True)
        p = jnp.exp(s - m)
        l = jnp.sum(p, axis=-1, keepdims=True)
        out = out + hm * (_bdot_nt(p, vt) / l)
    o_ref[0] = out.astype(o_ref.dtype)


def _mem_attn(q3, qcol, kt, vt, li, gq, grp, hmask, *, tq):
    bsz, t, _ = q3.shape
    g = jnp.tile(gq.reshape(1, HEAD_DIM), (1, N_HEADS))
    vmem = 2 * (tq * HW * 4 + 2 * MEM_TOKENS * HW * 4 + tq * HW * 2) + 16 * tq * HW * 4
    kv_spec = pl.BlockSpec((1, 1, HW, MEM_TOKENS), lambda b, i: (li, b, 0, 0))
    return pl.pallas_call(
        _mem_attn_kernel,
        grid=(bsz, t // tq),
        in_specs=[pl.BlockSpec((1, tq, HW), lambda b, i: (b, i, qcol)),
                  kv_spec, kv_spec,
                  pl.BlockSpec((1, HW), lambda b, i: (0, 0)),
                  pl.BlockSpec((HW, HW), lambda b, i: (0, 0)),
                  pl.BlockSpec((N_HEADS, 1, HW), lambda b, i: (0, 0, 0))],
        out_specs=pl.BlockSpec((1, tq, HW), lambda b, i: (b, i, 0)),
        out_shape=jax.ShapeDtypeStruct((bsz, t, HW), BF16),
        compiler_params=_cparams(("parallel", "parallel"), vmem),
        name="mem_attn",
    )(q3, kt, vt, g, grp, hmask)


def _mem_step_kernel(q_ref, kt_ref, vt_ref, gq_ref, grp_ref, hrow_ref, o_ref):
    nb = q_ref.shape[0]
    qn = _mem_qnorm(q_ref[...], gq_ref, grp_ref)
    hrow = hrow_ref[...]
    row = lax.broadcasted_iota(jnp.int32, (nb, 1), 0)
    out = jnp.zeros((nb, HW), F32)
    for b in range(nb):
        q4 = hrow * qn[b:b + 1]
        s = _bdot(q4, kt_ref[0, b])
        m = jnp.max(s, axis=-1, keepdims=True)
        p = jnp.exp(s - m)
        l = jnp.sum(p, axis=-1, keepdims=True)
        o4 = _bdot_nt(p, vt_ref[0, b]) / l
        o_b = jnp.sum(hrow * o4, axis=0, keepdims=True)
        out = out + jnp.where(row == b, o_b, 0.0)
    o_ref[...] = out


def _mem_step(q, qcol, kt, vt, li, gq, grp, hrow):
    bsz = q.shape[0]
    nb = STEP_ROWS
    g = jnp.tile(gq.reshape(1, HEAD_DIM), (1, N_HEADS))
    kv_spec = pl.BlockSpec((1, nb, HW, MEM_TOKENS), lambda i: (li, i, 0, 0))
    return pl.pallas_call(
        _mem_step_kernel,
        grid=(bsz // nb,),
        in_specs=[pl.BlockSpec((nb, HW), lambda i: (i, qcol)),
                  kv_spec, kv_spec,
                  pl.BlockSpec((1, HW), lambda i: (0, 0)),
                  pl.BlockSpec((HW, HW), lambda i: (0, 0)),
                  pl.BlockSpec((8, HW), lambda i: (0, 0))],
        out_specs=pl.BlockSpec((nb, HW), lambda i: (i, 0)),
        out_shape=jax.ShapeDtypeStruct((bsz, HW), F32),
        compiler_params=_cparams(("parallel",), 24 * 1024 * 1024),
        name="mem_step",
    )(q, kt, vt, g, grp, hrow)


def _mem_kv_kernel(x_ref, gn_ref, w_ref, gk_ref, grp_ref, kt_ref, vt_ref):
    x = x_ref[0]
    ms = jnp.mean(x * x, axis=-1, keepdims=True)
    y = x * lax.rsqrt(ms + EPS) * gn_ref[0]
    kv = jnp.dot(y.astype(BF16), w_ref[0], preferred_element_type=F32)
    k = kv[:, :HW]
    ss = _dot_exact_rhs(k * k, grp_ref[...])
    k = k * lax.rsqrt(ss * (1.0 / HEAD_DIM) + EPS) * gk_ref[0]
    kt_ref[0, 0] = k.T
    vt_ref[0, 0] = kv[:, HW:].T


def _mem_kv(mem, gn, w, gk, grp):
    bsz, m, d = mem.shape
    nl = w.shape[0]
    out = jax.ShapeDtypeStruct((nl, bsz, HW, m), F32)
    o_spec = pl.BlockSpec((1, 1, HW, m), lambda l, b: (l, b, 0, 0))
    return pl.pallas_call(
        _mem_kv_kernel,
        grid=(nl, bsz),
        in_specs=[pl.BlockSpec((1, m, d), lambda l, b: (b, 0, 0)),
                  pl.BlockSpec((1, 1, d), lambda l, b: (l, 0, 0)),
                  pl.BlockSpec((1, d, 2 * HW), lambda l, b: (l, 0, 0)),
                  pl.BlockSpec((1, 1, HW), lambda l, b: (l, 0, 0)),
                  pl.BlockSpec((HW, HW), lambda l, b: (0, 0))],
        out_specs=[o_spec, o_spec],
        out_shape=[out, out],
        compiler_params=_cparams(("parallel", "parallel"), 24 * 1024 * 1024),
        name="mem_kv",
    )(mem, gn.reshape(nl, 1, d), w, jnp.tile(gk, (1, N_HEADS)).reshape(nl, 1, HW), grp)


def _band_attn_kernel(q0_ref, q1_ref, k0_ref, k1_ref, v0_ref, v1_ref, bias_ref, hmask_ref,
                      expand_ref, o_ref, og_s, ls_s, *, seq):
    g = pl.program_id(1)
    n_units = seq // BAND

    def run_group(gi, dil):
        nb = seq // dil // BAND
        shift = nb.bit_length() - 1
        lane = lax.broadcasted_iota(jnp.int32, (1, 128), 1)
        ones_cols = jnp.ones((2 * BAND, 128), BF16)

        def idx(start):
            if dil == 1:
                return pl.ds(pl.multiple_of(start, BAND), BAND)
            return pl.ds(start, BAND, stride=dil)

        def rows(ref0, ref1, start):
            return jnp.concatenate([ref0[idx(start), :], ref1[idx(start), :]], axis=1)

        def unit(u, carry):
            r = lax.shift_right_logical(u, shift)
            nblk = u & (nb - 1)
            qs = r + dil * BAND * nblk
            ks = r + dil * BAND * jnp.maximum(nblk - 1, 0)
            q = rows(q0_ref, q1_ref, qs) * ATTN_SCALE
            kk = jnp.concatenate([rows(k0_ref, k1_ref, ks), rows(k0_ref, k1_ref, qs)],
                                 axis=0).astype(BF16)
            vv = jnp.concatenate([rows(v0_ref, v1_ref, ks), rows(v0_ref, v1_ref, qs)],
                                 axis=0).astype(BF16)
            q4 = jnp.concatenate([q * hmask_ref[h] for h in range(N_HEADS)], axis=0)
            bias = bias_ref[jnp.minimum(nblk, 1)]
            s = _bdot_nt(q4, kk).reshape(N_HEADS, BAND, 2 * BAND) + bias[None]
            s = s.reshape(N_HEADS * BAND, 2 * BAND)
            m = jnp.max(s, axis=-1, keepdims=True)
            p = jnp.exp(s - m)
            oe = _bdot(p, jnp.concatenate([vv, ones_cols], axis=1))
            l = oe[:, HW:]
            o4 = oe[:, :HW] / jnp.concatenate([l, l], axis=1)
            lse4 = m + jnp.log(l)
            out = jnp.zeros((BAND, HW), F32)
            lse_t = jnp.zeros((BAND, 128), F32)
            for h in range(N_HEADS):
                hr = slice(h * BAND, (h + 1) * BAND)
                out = out + hmask_ref[h] * o4[hr]
                lse_t = lse_t + jnp.where(lane == h, lse4[hr], 0.0)
            og_s[2 * gi, idx(qs), :] = out[:, :128]
            og_s[2 * gi + 1, idx(qs), :] = out[:, 128:]
            ls_s[gi, idx(qs), :] = lse_t
            return carry

        lax.fori_loop(0, n_units, unit, 0, unroll=2)

    for gi, (_, dil) in enumerate(DIL_PAIRS):
        pl.when(g == gi)(functools.partial(run_group, gi, dil))

    @pl.when(g == N_GROUPS - 1)
    def _merge():
        expand = expand_ref[...]
        tr = 512
        for c in range(seq // tr):
            rows = slice(c * tr, (c + 1) * tr)
            l0, l1, l2 = ls_s[0, rows, :], ls_s[1, rows, :], ls_s[2, rows, :]
            m = jnp.maximum(jnp.maximum(l0, l1), l2)
            es = (jnp.exp(l0 - m), jnp.exp(l1 - m), jnp.exp(l2 - m))
            inv = 1.0 / (es[0] + es[1] + es[2])
            acc = jnp.zeros((tr, HW), F32)
            for gi in range(N_GROUPS):
                o_g = jnp.concatenate([og_s[2 * gi, rows, :], og_s[2 * gi + 1, rows, :]], axis=1)
                acc = acc + _dot_exact_rhs(es[gi] * inv, expand) * o_g
            o_ref[rows, :] = acc.astype(o_ref.dtype)


def _band_attn(qd, k_sh, kv, band, hmask, expand, *, batch, seq):
    n = qd.shape[0]
    once = pl.Buffered(1)
    half = lambda cmap: pl.BlockSpec((seq, 128), cmap, pipeline_mode=once)
    vmem = (6 * seq * 128 * 4 + 2 * seq * HW * 2 + 9 * seq * 128 * 4 + 8 * 1024 * 1024)
    return pl.pallas_call(
        functools.partial(_band_attn_kernel, seq=seq),
        grid=(batch, N_GROUPS),
        in_specs=[half(lambda b, g: (b, 2 * g)), half(lambda b, g: (b, 2 * g + 1)),
                  half(lambda b, g: (b, 0)), half(lambda b, g: (b, 1)),
                  half(lambda b, g: (b, 2)), half(lambda b, g: (b, 3)),
                  pl.BlockSpec((2, BAND, 2 * BAND), lambda b, g: (0, 0, 0)),
                  pl.BlockSpec((N_HEADS, 1, HW), lambda b, g: (0, 0, 0)),
                  pl.BlockSpec((128, HW), lambda b, g: (0, 0))],
        out_specs=pl.BlockSpec((seq, HW), lambda b, g: (b, 0)),
        out_shape=jax.ShapeDtypeStruct((n, HW), BF16),
        scratch_shapes=[pltpu.VMEM((2 * N_GROUPS, seq, 128), F32),
                        pltpu.VMEM((N_GROUPS, seq, 128), F32)],
        compiler_params=_cparams(("parallel", "arbitrary"), vmem),
        name="band_attn",
    )(qd, qd, k_sh, k_sh, kv, kv, band, hmask, expand)


def _step_bias(cache_len):
    t = np.arange(cache_len)
    rows = np.zeros((16, cache_len), np.float32)
    for gi, (_, dil) in enumerate(DIL_PAIRS):
        keep = (t >= cache_len - dil * BAND) & ((cache_len - t) % dil == 0)
        rows[4 * gi:4 * gi + 4] = np.where(keep, 0.0, NEG)[None]
    return rows


def _step_attn_kernel(q_ref, kn_ref, vn_ref, kt_ref, vt_ref, bias_ref, hrow_ref, o_ref):
    j = pl.program_id(1)
    q = q_ref[pl.ds(j, 1), :] * ATTN_SCALE
    kn = kn_ref[pl.ds(j, 1), :]
    vn = vn_ref[pl.ds(j, 1), :]
    hrow = hrow_ref[...]
    row = lax.broadcasted_iota(jnp.int32, (16, 1), 0)
    q16 = hrow * jnp.where(row < 4, q[:, :HW], jnp.where(row < 8, q[:, HW:2 * HW], q[:, 2 * HW:]))
    s = _bdot(q16, kt_ref[0]) + bias_ref[...]
    s_n = jnp.sum(q16 * kn, axis=-1, keepdims=True)
    m = jnp.maximum(jnp.max(s, axis=-1, keepdims=True), s_n)
    p = jnp.exp(s - m)
    p_n = jnp.exp(s_n - m)
    l = jnp.sum(p, axis=-1, keepdims=True) + p_n
    o16 = (_bdot_nt(p, vt_ref[0]) + p_n * vn) / l
    lse = m + jnp.log(l)
    lg = [lse[4 * g:4 * g + 4] for g in range(N_GROUPS)]
    mm = jnp.maximum(jnp.maximum(lg[0], lg[1]), lg[2])
    es = [jnp.exp(x - mm) for x in lg]
    inv = 1.0 / (es[0] + es[1] + es[2])
    acc = jnp.zeros((4, HW), F32)
    for g in range(N_GROUPS):
        acc = acc + (es[g] * inv) * (hrow[0:4] * o16[4 * g:4 * g + 4])
    o_ref[pl.ds(j, 1), :] = jnp.sum(acc, axis=0, keepdims=True)


def _step_attn(q, kn, kvn, kt, vt, bias, hrow16):
    bsz, _, cache_len = kt.shape
    assert cache_len >= max(d for _, d in DIL_PAIRS) * BAND
    nb = STEP_ROWS
    rows = lambda w: pl.BlockSpec((nb, w), lambda i, j: (i, 0))
    cache = pl.BlockSpec((1, HW, cache_len), lambda i, j: (i * nb + j, 0, 0))
    vmem = 4 * HW * cache_len * 4 + 8 * 16 * cache_len * 4 + 4 * 1024 * 1024
    return pl.pallas_call(
        _step_attn_kernel,
        grid=(bsz // nb, nb),
        in_specs=[rows(N_GROUPS * HW), rows(HW),
                  pl.BlockSpec((nb, HW), lambda i, j: (i, 1)), cache, cache,
                  pl.BlockSpec((16, cache_len), lambda i, j: (0, 0)),
                  pl.BlockSpec((16, HW), lambda i, j: (0, 0))],
        out_specs=rows(HW),
        out_shape=jax.ShapeDtypeStruct((bsz, HW), F32),
        compiler_params=_cparams(("parallel", "arbitrary"), vmem),
        name="step_attn",
    )(q, kn, kvn, kt, vt, bias, hrow16)


def _out_ffn_kernel(h_ref, tok_ref, mem_ref, wo1_ref, wo2_ref, g2_ref, wgu_ref, wdn_ref, o_ref):
    h1 = (h_ref[...]
          + jnp.dot(tok_ref[...].astype(BF16), wo1_ref[...], preferred_element_type=F32)
          + jnp.dot(mem_ref[...].astype(BF16), wo2_ref[...], preferred_element_type=F32))
    ms = jnp.mean(h1 * h1, axis=-1, keepdims=True)
    hn = (h1 * lax.rsqrt(ms + EPS) * g2_ref[...]).astype(BF16)
    gate = jnp.dot(hn, wgu_ref[:, :D_FF], preferred_element_type=F32)
    up = jnp.dot(hn, wgu_ref[:, D_FF:], preferred_element_type=F32)
    act = (_silu(gate) * up).astype(BF16)
    o_ref[...] = h1 + jnp.dot(act, wdn_ref[...], preferred_element_type=F32)


def _out_ffn(h, tok, mem, wo, woi, g2, wgu, wdn, li, *, tm):
    n, d = h.shape
    wt = tok.shape[1]
    assert wo.shape[1] == wt + HW and wt % HW == 0
    once = pl.Buffered(1)
    vmem = ((wt + HW) * d * 2 + 3 * D_FF * d * 2
            + 2 * tm * (2 * d * 4 + (wt + HW) * 4) + 3 * tm * d * 4 + 3 * tm * D_FF * 4
            + 4 * 1024 * 1024)
    return pl.pallas_call(
        _out_ffn_kernel,
        grid=(n // tm,),
        in_specs=[pl.BlockSpec((tm, d), lambda i: (i, 0)),
                  pl.BlockSpec((tm, wt), lambda i: (i, 0)),
                  pl.BlockSpec((tm, HW), lambda i: (i, 0)),
                  pl.BlockSpec((None, wt, d), lambda i: (woi, 0, 0), pipeline_mode=once),
                  pl.BlockSpec((None, HW, d), lambda i: (woi, wt // HW, 0), pipeline_mode=once),
                  pl.BlockSpec((None, 1, d), lambda i: (li, 0, 0), pipeline_mode=once),
                  pl.BlockSpec((None, d, 2 * D_FF), lambda i: (li, 0, 0), pipeline_mode=once),
                  pl.BlockSpec((None, D_FF, d), lambda i: (li, 0, 0), pipeline_mode=once)],
        out_specs=pl.BlockSpec((tm, d), lambda i: (i, 0)),
        out_shape=jax.ShapeDtypeStruct((n, d), F32),
        compiler_params=_cparams(("parallel",), vmem),
        name="out_ffn",
    )(h, tok, mem, wo, wo, g2, wgu, wdn)


def _tail_transpose_kernel(x_ref, o_ref):
    o_ref[0] = x_ref[...].T


def _tail_transpose(x, col, *, batch, seq, keep, tr):
    per, nk = seq // tr, keep // tr
    return pl.pallas_call(
        _tail_transpose_kernel,
        grid=(batch, nk),
        in_specs=[pl.BlockSpec((tr, HW), lambda b, j: (b * per + per - nk + j, col))],
        out_specs=pl.BlockSpec((1, HW, tr), lambda b, j: (b, 0, j)),
        out_shape=jax.ShapeDtypeStruct((batch, HW, keep), F32),
        compiler_params=_cparams(("parallel", "parallel"), 16 * 1024 * 1024),
        name="tail_transpose",
    )(x)


def _pad_heads_cols(w):
    d = w.shape[0]
    return jnp.pad(w.reshape(d, GLA_HEADS, GLA_DV),
                   ((0, 0), (0, 0), (0, GLA_DVP - GLA_DV))).reshape(d, GLA_VP)


def _pack_a_in(w):
    q = w[:, :GLA_QK]
    k = w[:, GLA_QK:2 * GLA_QK]
    v = w[:, 2 * GLA_QK:2 * GLA_QK + GLA_V]
    g = w[:, 2 * GLA_QK + GLA_V:2 * GLA_QK + 2 * GLA_V]
    glr = w[:, 2 * GLA_QK + 2 * GLA_V:2 * GLA_QK + 2 * GLA_V + GLA_RANK]
    mq = w[:, 2 * GLA_QK + 2 * GLA_V + GLA_RANK:]
    glr = jnp.pad(glr, ((0, 0), (0, GLA_RANKP - GLA_RANK)))
    return jnp.concatenate([q, k, _pad_heads_cols(v), _pad_heads_cols(g), mq, glr],
                           axis=1).astype(BF16)


def _pack_a_out(w):
    tok = w[:GLA_V].reshape(GLA_HEADS, GLA_DV, D_MODEL)
    tok = jnp.pad(tok, ((0, 0), (0, GLA_DVP - GLA_DV), (0, 0))).reshape(GLA_VP, D_MODEL)
    return jnp.concatenate([tok, w[GLA_V:]], axis=0).astype(BF16)


def kernel(x_prompt, x_sample, state_gla, cache_win_k, cache_win_v, cache_mem_k, cache_mem_v,
           mem_prompt, norm1, norm2, a_w_in, a_w_gate2, a_b_gate, a_g_onorm, a_w_out, kv_norm, w_kv,
           g_k, b_w_in, b_g_q, b_w_out, mem_norm, w_mem_kv, g_mem_q, g_mem_k, w_ffn_gu, w_ffn_down):
    batch, seq, d = x_prompt.shape
    dec_b = x_sample.shape[0]
    past_len = 8192
    assert d == D_MODEL and seq % 512 == 0 and x_sample.shape[1] == 1

    mall_np, lmask_np = _gla_tables()
    grp_np, hmask_np, expand_np = _head_tables()
    mall = jnp.asarray(mall_np, BF16)
    lmask = jnp.asarray(lmask_np, F32)
    grp = jnp.asarray(grp_np, BF16)
    hmask = jnp.asarray(hmask_np, F32)
    expand = jnp.asarray(expand_np, BF16)
    band = jnp.asarray(_band_bias(), F32)

    hrow8 = jnp.asarray(np.concatenate([hmask_np[:, 0], np.zeros((4, HW), np.float32)], 0))
    hrow16 = jnp.asarray(np.concatenate([hmask_np[:, 0]] * N_GROUPS
                                        + [np.zeros((4, HW), np.float32)], 0))

    cos_p, sin_p = _rope_tables(jnp.arange(seq, dtype=jnp.int32))
    cos_s, sin_s = _rope_tables(past_len + jnp.arange(1, dtype=jnp.int32))

    wa_in = jnp.stack([_pack_a_in(a_w_in[i]) for i in range(N_A)], 0)
    wa_out = jnp.stack([_pack_a_out(a_w_out[i]) for i in range(N_A)], 0)
    wg2 = [jnp.pad(a_w_gate2[i], ((0, GLA_RANKP - GLA_RANK), (0, 0))).astype(BF16)
           for i in range(N_A)]
    bg = [a_b_gate[i].reshape(1, GLA_QK) for i in range(N_A)]
    gon = [jnp.pad(a_g_onorm[i], (0, GLA_DVP - GLA_DV)).reshape(1, GLA_DVP) for i in range(N_A)]
    wb_in = b_w_in.astype(BF16)
    wb_out = b_w_out.astype(BF16)
    w_kv_b = w_kv.astype(BF16).reshape(1, d, 2 * HW)
    w_mem_b = w_mem_kv.astype(BF16)
    wgu = w_ffn_gu.astype(BF16)
    wdn = w_ffn_down.astype(BF16)
    n1 = norm1.reshape(DEPTH, 1, d)
    n2 = norm2.reshape(DEPTH, 1, d)
    nkv = kv_norm.reshape(1, 1, d)

    mem_kt_p, mem_vt_p = _mem_kv(mem_prompt, mem_norm, w_mem_b, g_mem_k, grp)

    cache_len = cache_win_k.shape[1]
    win_kt = jnp.transpose(cache_win_k, (0, 2, 3, 1)).reshape(dec_b, HW, cache_len)
    win_vt = jnp.transpose(cache_win_v, (0, 2, 3, 1)).reshape(dec_b, HW, cache_len)
    mem_kt_s = jnp.transpose(cache_mem_k, (0, 1, 3, 4, 2)).reshape(DEPTH, dec_b, HW, MEM_TOKENS)
    mem_vt_s = jnp.transpose(cache_mem_v, (0, 1, 3, 4, 2)).reshape(DEPTH, dec_b, HW, MEM_TOKENS)
    state_t = jnp.swapaxes(state_gla, 3, 4)
    step_bias = jnp.asarray(_step_bias(cache_len), F32)

    def trunk(x2, bsz, t, prompt):
        n = bsz * t
        tm = 512 if prompt else n
        h = x2
        states = []
        state_out = None
        k_sh = kv_sh = None
        for li in range(DEPTH):
            if li < N_A:
                proj = _norm_matmul(h, n1, li, wa_in, li, tm=tm, tn=A_IN_P // 3, name="a_in")
                mq_col = A_COL_MQ // HW
                if prompt:
                    h0t = jnp.zeros((bsz * GLA_HEADS, GLA_DVP, GLA_DK), F32)
                    tok, st = _gla_prompt(proj, h0t, wg2[li], bg[li], gon[li], mall, lmask,
                                          batch=bsz, seq=t, tb=512)
                    states.append(st.reshape(bsz, GLA_HEADS, GLA_DVP, GLA_DK)[:, :, :GLA_DV])
                else:
                    tok, state_out = _gla_step(proj, state_t, li, state_out, wg2[li], bg[li],
                                               gon[li])
                wo, woi = wa_out, li
            else:
                bi = li - N_A
                proj = _norm_matmul(h, n1, li, wb_in, bi, tm=tm, tn=D_MODEL, name="b_in")
                mq_col = (N_GROUPS * HW) // HW
                if prompt:
                    qd = _headnorm(proj, b_g_q[bi], (grp, cos_p, sin_p), width=N_GROUPS * HW,
                                   rope=True, tr=512, name="q_rope", pos_blocks=t // 512)
                    tok = _band_attn(qd, k_sh, kv_sh, band, hmask, expand, batch=bsz, seq=t)
                else:
                    qd = _headnorm(proj, b_g_q[bi], (grp, cos_s, sin_s), width=N_GROUPS * HW,
                                   rope=True, tr=n, name="q_rope")
                    tok = _step_attn(qd, k_sh, kv_sh, win_kt, win_vt, step_bias, hrow16)
                wo, woi = wb_out, bi
            if prompt:
                mem_o = _mem_attn(proj.reshape(bsz, t, proj.shape[1]), mq_col, mem_kt_p, mem_vt_p,
                                  li, g_mem_q[li], grp, hmask, tq=512).reshape(n, HW)
            else:
                mem_o = _mem_step(proj, mq_col, mem_kt_s, mem_vt_s, li, g_mem_q[li], grp, hrow8)
            h = _out_ffn(h, tok, mem_o, wo, woi, n2, wgu, wdn, li, tm=tm)
            if li == N_A - 1:
                kv_sh = _norm_matmul(h, nkv, 0, w_kv_b, 0, tm=tm, tn=2 * HW, name="shared_kv")
                tabs = (grp, cos_p, sin_p) if prompt else (grp, cos_s, sin_s)
                k_sh = _headnorm(kv_sh, g_k, tabs, width=HW, rope=True, tr=tm, name="k_rope",
                                 pos_blocks=max(t // tm, 1))
        return h, states, state_out, k_sh, kv_sh

    y_p, gla_p, _, k_p, kv_p = trunk(x_prompt.reshape(batch * seq, d), batch, seq, True)
    y_s, _, gla_s, k_s, kv_s = trunk(x_sample.reshape(dec_b, d), dec_b, 1, False)

    keep = min(WIN_MAX, seq)

    def window_out(x, col):
        xt = _tail_transpose(x, col, batch=batch, seq=seq, keep=keep, tr=512)
        return jnp.transpose(xt.reshape(batch, N_HEADS, HEAD_DIM, keep), (0, 3, 1, 2))

    def mem_out(xt):
        return jnp.transpose(xt.reshape(DEPTH, batch, N_HEADS, HEAD_DIM, MEM_TOKENS),
                             (0, 1, 4, 2, 3))

    return (y_p.reshape(batch, seq, d),
            y_s.reshape(dec_b, 1, d),
            jnp.swapaxes(jnp.stack(gla_p, 0), 3, 4),
            jnp.swapaxes(gla_s, 3, 4),
            window_out(k_p, 0),
            window_out(kv_p, 1),
            k_s.reshape(dec_b, 1, N_HEADS, HEAD_DIM),
            kv_s[:, HW:].reshape(dec_b, 1, N_HEADS, HEAD_DIM),
            mem_out(mem_kt_p),
            mem_out(mem_vt_p))
```

```python
import functools

import numpy as np
import jax
import jax.numpy as jnp
from jax import lax
from jax.experimental import pallas as pl
from jax.experimental.pallas import tpu as pltpu

F32 = jnp.float32
BF16 = jnp.bfloat16

D_MODEL = 1024
DEPTH = 4
N_A = 2
GLA_HEADS = 4
GLA_DK = 128
GLA_DV = 192
GLA_DVP = 256
GLA_QK = GLA_HEADS * GLA_DK
GLA_V = GLA_HEADS * GLA_DV
GLA_VP = GLA_HEADS * GLA_DVP
GLA_RANK = 16
GLA_RANKP = 128
GLA_TAU = 16.0
GLA_CHUNK = 64
HEAD_DIM = 64
N_HEADS = 4
HW = N_HEADS * HEAD_DIM
DIL_PAIRS = ((128, 1), (512, 4), (2048, 16))
N_GROUPS = 3
BAND = 128
WIN_MAX = 2048
MEM_TOKENS = 256
D_FF = 2816
FF_TILE = 256
ROPE_THETA = 10000.0
EPS = 1e-6
ATTN_SCALE = HEAD_DIM ** -0.5
NEG = -1e30

A_COL_Q = 0
A_COL_K = GLA_QK
A_COL_V = 2 * GLA_QK
A_COL_G = A_COL_V + GLA_VP
A_COL_MQ = A_COL_G + GLA_VP
A_COL_GLR = A_COL_MQ + HW
A_IN_P = A_COL_GLR + GLA_RANKP

V7X_VMEM_BYTES = 64 * 1024 * 1024
VMEM_CAP = 56 * 1024 * 1024


def _cparams(sem, vmem_bytes):
    return pltpu.CompilerParams(
        dimension_semantics=sem,
        vmem_limit_bytes=int(min(max(vmem_bytes, 16 * 1024 * 1024), VMEM_CAP)))


def _bdot(a, b):
    return jnp.dot(a.astype(BF16), b.astype(BF16), preferred_element_type=F32)


def _bdot_nt(a, b):
    return lax.dot_general(a.astype(BF16), b.astype(BF16), (((1,), (1,)), ((), ())),
                           preferred_element_type=F32)


def _bdot_tn(a, b):
    return lax.dot_general(a.astype(BF16), b.astype(BF16), (((0,), (0,)), ((), ())),
                           preferred_element_type=F32)


def _split(x):
    hi = x.astype(BF16)
    lo = (x - hi.astype(F32)).astype(BF16)
    return hi, lo


def _dot_exact_rhs(x, m):
    hi, lo = _split(x)
    return (jnp.dot(hi, m, preferred_element_type=F32)
            + jnp.dot(lo, m, preferred_element_type=F32))


def _dot_exact_lhs(m, x):
    hi, lo = _split(x)
    return (jnp.dot(m, hi, preferred_element_type=F32)
            + jnp.dot(m, lo, preferred_element_type=F32))


def _silu(x):
    return x * jax.nn.sigmoid(x)


def _log_sigmoid(z):
    return jnp.minimum(z, 0.0) - jnp.log1p(jnp.exp(-jnp.abs(z)))


def _gla_tables():
    c = GLA_CHUNK
    i = np.arange(c)[:, None]
    t = np.arange(c)[None, :]
    blocks = [(t <= i), (t > i)]
    masks = []
    for lvl in range(6):
        s = c >> lvl
        half = s // 2
        mid = (i // s) * s + half - 1
        second = (i % s) >= half
        m = np.where(second, (t > mid) & (t <= i), (t > i) & (t <= mid))
        blocks.append(m)
        j = t
        masks.append(((i // s) == (j // s)) & ((i % s) >= half) & ((j % s) < half))
    masks.append(i == t)
    mall = np.concatenate(blocks, axis=0).astype(np.float32)
    lmask = np.stack(masks, axis=0).astype(np.float32)
    assert np.array_equal(lmask.sum(0), (t <= i).astype(np.float32))
    return mall, lmask


def _head_tables():
    lane = np.arange(HW)
    group = (lane[:, None] // HEAD_DIM == lane[None, :] // HEAD_DIM).astype(np.float32)
    hmask = (lane[None, :] // HEAD_DIM == np.arange(N_HEADS)[:, None]).astype(np.float32)
    expand = np.zeros((128, HW), np.float32)
    expand[:N_HEADS] = hmask
    return group, hmask.reshape(N_HEADS, 1, HW), expand


def _band_bias():
    i = np.arange(BAND)[:, None]
    j = np.arange(2 * BAND)[None, :]
    dist = i + BAND - j
    ok = (dist >= 0) & (dist <= BAND)
    first = ok & (j >= BAND)
    return np.where(np.stack([first, ok], 0), 0.0, NEG).astype(np.float32)


def _rope_tables(pos):
    half = HEAD_DIM // 2
    inv = ROPE_THETA ** (-jnp.arange(half, dtype=F32) / half)
    ang = pos.astype(F32)[:, None] * inv[None, :]
    cos = jnp.cos(ang)
    sin = jnp.sin(ang)
    cos64 = jnp.concatenate([cos, cos], axis=-1)
    sin64 = jnp.concatenate([-sin, sin], axis=-1)
    return jnp.tile(cos64, (1, N_HEADS)), jnp.tile(sin64, (1, N_HEADS))


def _norm_matmul_kernel(x_ref, g_ref, w_ref, o_ref):
    x = x_ref[...]
    ms = jnp.mean(x * x, axis=-1, keepdims=True)
    y = x * lax.rsqrt(ms + EPS) * g_ref[...]
    o_ref[...] = jnp.dot(y.astype(BF16), w_ref[...], preferred_element_type=F32)


def _norm_matmul(x, g, gi, w, wi, *, tm, tn, name):
    n, d = x.shape
    nout = w.shape[2]
    vmem = 2 * (tm * d * 4 + d * tn * 2 + tm * tn * 4) + 2 * tm * d * 4 + tm * tn * 4
    return pl.pallas_call(
        _norm_matmul_kernel,
        grid=(n // tm, nout // tn),
        in_specs=[pl.BlockSpec((tm, d), lambda i, j: (i, 0)),
                  pl.BlockSpec((None, 1, d), lambda i, j: (gi, 0, 0)),
                  pl.BlockSpec((None, d, tn), lambda i, j: (wi, 0, j))],
        out_specs=pl.BlockSpec((tm, tn), lambda i, j: (i, j)),
        out_shape=jax.ShapeDtypeStruct((n, nout), F32),
        compiler_params=_cparams(("parallel", "parallel"), vmem),
        name=name,
    )(x, g, w)


def _norm_matmul_rope_kernel(x_ref, g_ref, w_ref, hg_ref, grp_ref, cos_ref, sin_ref, o_ref, *,
                             rope_width):
    x = x_ref[...]
    ms = jnp.mean(x * x, axis=-1, keepdims=True)
    xn = x * lax.rsqrt(ms + EPS) * g_ref[...]
    y = jnp.dot(xn.astype(BF16), w_ref[...], preferred_element_type=F32)
    grp = grp_ref[...]
    lane = lax.broadcasted_iota(jnp.int32, (1, HW), 1)
    first_half = (lane & (HEAD_DIM - 1)) < (HEAD_DIM // 2)
    for c in range(rope_width // HW):
        sl = slice(c * HW, (c + 1) * HW)
        yc = y[:, sl]
        ss = _dot_exact_rhs(yc * yc, grp)
        yc = yc * lax.rsqrt(ss * (1.0 / HEAD_DIM) + EPS) * hg_ref[...]
        fwd = pltpu.roll(yc, HW - HEAD_DIM // 2, 1)
        bwd = pltpu.roll(yc, HEAD_DIM // 2, 1)
        rot = jnp.where(first_half, fwd, bwd)
        o_ref[:, sl] = yc * cos_ref[...] + rot * sin_ref[...]
    o_ref[:, rope_width:] = y[:, rope_width:]


def _norm_matmul_rope(x, g, gi, w, wi, head_gain, tabs, *, rope_width, tm, name, pos_blocks=1):
    n, d = x.shape
    nout = w.shape[2]
    grp, cos, sin = tabs
    hg = jnp.tile(head_gain.reshape(1, HEAD_DIM), (1, N_HEADS))
    trow = 1 if cos.shape[0] == 1 else tm
    tab_map = (lambda i: (0, 0)) if cos.shape[0] == 1 else (lambda i: (i % pos_blocks, 0))
    vmem = (2 * (tm * d * 4 + tm * nout * 4 + 2 * trow * HW * 4) + d * nout * 2
            + 2 * tm * d * 4 + 3 * tm * nout * 4)
    return pl.pallas_call(
        functools.partial(_norm_matmul_rope_kernel, rope_width=rope_width),
        grid=(n // tm,),
        in_specs=[pl.BlockSpec((tm, d), lambda i: (i, 0)),
                  pl.BlockSpec((None, 1, d), lambda i: (gi, 0, 0)),
                  pl.BlockSpec((None, d, nout), lambda i: (wi, 0, 0),
                               pipeline_mode=pl.Buffered(1)),
                  pl.BlockSpec((1, HW), lambda i: (0, 0)),
                  pl.BlockSpec((HW, HW), lambda i: (0, 0)),
                  pl.BlockSpec((trow, HW), tab_map),
                  pl.BlockSpec((trow, HW), tab_map)],
        out_specs=pl.BlockSpec((tm, nout), lambda i: (i, 0)),
        out_shape=jax.ShapeDtypeStruct((n, nout), F32),
        compiler_params=_cparams(("parallel",), vmem),
        name=name,
    )(x, g, w, hg, grp, cos, sin)


def _gla_block(q, k, v, g, la, ht, gon, mall_ref, lmask_ref, n_chunks):
    cr = GLA_CHUNK
    chunk = lambda c: slice(c * cr, (c + 1) * cr)
    la_cat = jnp.concatenate([la[chunk(c)] for c in range(n_chunks)], axis=1)
    e_all = jnp.exp(_dot_exact_lhs(mall_ref[...], la_cat))

    def decay_rows(blk):
        return jnp.concatenate(
            [e_all[blk * cr:(blk + 1) * cr, c * GLA_DK:(c + 1) * GLA_DK] for c in range(n_chunks)],
            axis=0)

    eb = decay_rows(0)
    qb = (q * eb).astype(BF16)
    kdec = (k * decay_rows(1)).astype(BF16)
    q_lv = [q.astype(BF16)]
    k_lv = [k.astype(BF16)]
    for lvl in range(6):
        el = decay_rows(2 + lvl)
        q_lv.append((q * el).astype(BF16))
        k_lv.append((k * el).astype(BF16))
    o_intra, kv_new = [], []
    for c in range(n_chunks):
        r = chunk(c)
        s = lmask_ref[6] * _bdot_nt(q_lv[0][r], k_lv[0][r])
        for lvl in range(6):
            s = s + lmask_ref[lvl] * _bdot_nt(q_lv[1 + lvl][r], k_lv[1 + lvl][r])
        o_intra.append(_bdot(s, v[r]))
        kv_new.append(_bdot_tn(v[r], kdec[r]))
    outs = []
    for c in range(n_chunks):
        outs.append(o_intra[c] + _bdot_nt(qb[chunk(c)], ht))
        ht = ht * eb[c * cr + cr - 1:(c + 1) * cr, :] + kv_new[c]
    o = jnp.concatenate(outs, axis=0)
    ms = jnp.sum(o * o, axis=-1, keepdims=True) * (1.0 / GLA_DV)
    return o * lax.rsqrt(ms + EPS) * gon * _silu(g), ht


def _gla_kernel(x_ref, g1_ref, w_ref, wg2_ref, bg_ref, gon_ref, mall_ref, lmask_ref, h0_ref,
                tok_ref, mq_ref, hout_ref, proj_s, ht_s, *, n_chunks):
    t = pl.program_id(1)

    @pl.when(t == 0)
    def _():
        ht_s[...] = h0_ref[0]

    x = x_ref[...]
    ms = jnp.mean(x * x, axis=-1, keepdims=True)
    xn = (x * lax.rsqrt(ms + EPS) * g1_ref[...]).astype(BF16)
    proj_s[...] = jnp.dot(xn, w_ref[...], preferred_element_type=F32)
    mq_ref[...] = proj_s[:, A_COL_MQ:A_COL_MQ + HW]
    z = _bdot(proj_s[:, A_COL_GLR:A_COL_GLR + GLA_RANKP], wg2_ref[...]) + bg_ref[...]
    la = _log_sigmoid(z) * (1.0 / GLA_TAU)
    gon = gon_ref[...]
    for h in range(GLA_HEADS):
        kc = slice(h * GLA_DK, (h + 1) * GLA_DK)
        vc = slice(h * GLA_DVP, (h + 1) * GLA_DVP)
        q = proj_s[:, A_COL_Q + h * GLA_DK:A_COL_Q + (h + 1) * GLA_DK] * (GLA_DK ** -0.5)
        k = proj_s[:, A_COL_K + h * GLA_DK:A_COL_K + (h + 1) * GLA_DK]
        v = proj_s[:, A_COL_V + h * GLA_DVP:A_COL_V + (h + 1) * GLA_DVP].astype(BF16)
        g = proj_s[:, A_COL_G + h * GLA_DVP:A_COL_G + (h + 1) * GLA_DVP]
        tok, ht = _gla_block(q, k, v, g, la[:, kc], ht_s[h], gon, mall_ref, lmask_ref, n_chunks)
        tok_ref[:, vc] = tok.astype(tok_ref.dtype)
        ht_s[h] = ht

    @pl.when(t == pl.num_programs(1) - 1)
    def _():
        hout_ref[0] = ht_s[...]


def _gla_prompt(x, g1, w_in, li, h0t, wg2, bg, gon, mall, lmask, *, batch, seq, tb):
    n, d = x.shape
    nt = seq // tb
    const2 = lambda shape: pl.BlockSpec(shape, lambda b, t: (0,) * len(shape))
    st_spec = pl.BlockSpec((1, GLA_HEADS, GLA_DVP, GLA_DK), lambda b, t: (b, 0, 0, 0))
    vmem = (2 * tb * d * 4 + d * A_IN_P * 2 + tb * A_IN_P * 4 + 2 * tb * (GLA_VP * 2 + HW * 4)
            + 6 * GLA_HEADS * GLA_DVP * GLA_DK * 4 + 24 * tb * GLA_DVP * 4)
    return pl.pallas_call(
        functools.partial(_gla_kernel, n_chunks=tb // GLA_CHUNK),
        grid=(batch, nt),
        in_specs=[
            pl.BlockSpec((tb, d), lambda b, t: (b * nt + t, 0)),
            pl.BlockSpec((None, 1, d), lambda b, t: (li, 0, 0)),
            pl.BlockSpec((None, d, A_IN_P), lambda b, t: (li, 0, 0), pipeline_mode=pl.Buffered(1)),
            const2((GLA_RANKP, GLA_QK)), const2((1, GLA_QK)), const2((1, GLA_DVP)),
            const2(mall.shape), const2(lmask.shape), st_spec,
        ],
        out_specs=[
            pl.BlockSpec((tb, GLA_VP), lambda b, t: (b * nt + t, 0)),
            pl.BlockSpec((tb, HW), lambda b, t: (b * nt + t, 0)),
            st_spec,
        ],
        out_shape=[jax.ShapeDtypeStruct((n, GLA_VP), BF16),
                   jax.ShapeDtypeStruct((n, HW), F32),
                   jax.ShapeDtypeStruct((batch, GLA_HEADS, GLA_DVP, GLA_DK), F32)],
        scratch_shapes=[pltpu.VMEM((tb, A_IN_P), F32),
                        pltpu.VMEM((GLA_HEADS, GLA_DVP, GLA_DK), F32)],
        compiler_params=_cparams(("parallel", "arbitrary"), vmem),
        name="gla_prompt",
    )(x, g1, w_in, wg2, bg, gon, mall, lmask, h0t)


STEP_ROWS = 8


def _gla_step_kernel(q_ref, k_ref, v_ref, g_ref, glr_ref, wg2_ref, bg_ref, gon_ref, st_ref, *rest):
    tok_ref, so_ref = rest[-2:]
    nb = q_ref.shape[0]
    z = _bdot(glr_ref[...], wg2_ref[...]) + bg_ref[...]
    a = jnp.exp(_log_sigmoid(z) * (1.0 / GLA_TAU))
    q = q_ref[...] * (GLA_DK ** -0.5)
    k = k_ref[...]
    v = v_ref[...]
    g = g_ref[...]
    gon = gon_ref[...][:, :GLA_DV]
    row = lax.broadcasted_iota(jnp.int32, (nb, 1), 0)
    for h in range(GLA_HEADS):
        ks = slice(h * GLA_DK, (h + 1) * GLA_DK)
        vs = slice(h * GLA_DVP, h * GLA_DVP + GLA_DV)
        o_rows = jnp.zeros((nb, GLA_DV), F32)
        for b in range(nb):
            only_b = row == b
            outer = _bdot_tn(jnp.where(only_b, v[:, vs], 0.0), k[:, ks])
            s_new = st_ref[0, b, h] * a[b:b + 1, ks] + outer
            so_ref[0, b, h] = s_new
            o_rows = o_rows + jnp.where(only_b, _bdot_nt(q[:, ks], s_new), 0.0)
        ms = jnp.sum(o_rows * o_rows, axis=-1, keepdims=True) * (1.0 / GLA_DV)
        tok_ref[:, vs] = o_rows * lax.rsqrt(ms + EPS) * gon * _silu(g[:, vs])
        tok_ref[:, h * GLA_DVP + GLA_DV:(h + 1) * GLA_DVP] = jnp.zeros(
            (nb, GLA_DVP - GLA_DV), F32)


def _gla_step(proj, state_t, li, prev_out, wg2, bg, gon):
    bsz = proj.shape[0]
    nb = STEP_ROWS
    col = lambda w, c: pl.BlockSpec((nb, w), lambda i: (i, c))
    st_spec = pl.BlockSpec((1, nb, GLA_HEADS, GLA_DV, GLA_DK), lambda i: (li, i, 0, 0, 0))
    in_specs = [col(GLA_QK, A_COL_Q // GLA_QK), col(GLA_QK, A_COL_K // GLA_QK),
                col(GLA_VP, A_COL_V // GLA_VP), col(GLA_VP, A_COL_G // GLA_VP),
                col(GLA_RANKP, A_COL_GLR // GLA_RANKP),
                pl.BlockSpec((GLA_RANKP, GLA_QK), lambda i: (0, 0)),
                pl.BlockSpec((1, GLA_QK), lambda i: (0, 0)),
                pl.BlockSpec((1, GLA_DVP), lambda i: (0, 0)),
                st_spec]
    args = [proj, proj, proj, proj, proj, wg2, bg, gon, state_t]
    aliases = {}
    if prev_out is not None:
        in_specs.append(pl.BlockSpec(memory_space=pl.ANY))
        args.append(prev_out)
        aliases = {len(args) - 1: 1}
    return pl.pallas_call(
        _gla_step_kernel,
        grid=(bsz // nb,),
        in_specs=in_specs,
        out_specs=[pl.BlockSpec((nb, GLA_VP), lambda i: (i, 0)), st_spec],
        out_shape=[jax.ShapeDtypeStruct((bsz, GLA_VP), F32),
                   jax.ShapeDtypeStruct(state_t.shape, F32)],
        input_output_aliases=aliases,
        compiler_params=_cparams(("parallel",), 32 * 1024 * 1024),
        name="gla_step",
    )(*args)


def _mem_qnorm(q, gq_ref, grp_ref):
    ss = _dot_exact_rhs(q * q, grp_ref[...])
    return q * lax.rsqrt(ss * (1.0 / HEAD_DIM) + EPS) * gq_ref[...] * ATTN_SCALE


def _mem_attn_kernel(q_ref, kt_ref, vt_ref, gq_ref, grp_ref, hmask_ref, o_ref):
    qn = _mem_qnorm(q_ref[0], gq_ref, grp_ref)
    tq = qn.shape[0]
    kt = kt_ref[0, 0].astype(BF16)
    vt = jnp.concatenate([vt_ref[0, 0].astype(BF16), jnp.ones((128, MEM_TOKENS), BF16)], axis=0)
    q4 = jnp.concatenate([qn * hmask_ref[h] for h in range(N_HEADS)], axis=0)
    s = _bdot(q4, kt)
    m = jnp.max(s, axis=-1, keepdims=True)
    oe = _bdot_nt(jnp.exp(s - m), vt)
    l = oe[:, HW:]
    o4 = oe[:, :HW] / jnp.concatenate([l, l], axis=1)
    out = jnp.zeros(qn.shape, F32)
    for h in range(N_HEADS):
        out = out + hmask_ref[h] * o4[h * tq:(h + 1) * tq]
    o_ref[0] = out.astype(o_ref.dtype)


def _mem_attn(q3, qcol, kt, vt, li, gq, grp, hmask, *, tq):
    bsz, t, _ = q3.shape
    g = jnp.tile(gq.reshape(1, HEAD_DIM), (1, N_HEADS))
    vmem = 2 * (tq * HW * 4 + 2 * MEM_TOKENS * HW * 4 + tq * HW * 2) + 16 * tq * HW * 4
    kv_spec = pl.BlockSpec((1, 1, HW, MEM_TOKENS), lambda b, i: (li, b, 0, 0))
    return pl.pallas_call(
        _mem_attn_kernel,
        grid=(bsz, t // tq),
        in_specs=[pl.BlockSpec((1, tq, HW), lambda b, i: (b, i, qcol)),
                  kv_spec, kv_spec,
                  pl.BlockSpec((1, HW), lambda b, i: (0, 0)),
                  pl.BlockSpec((HW, HW), lambda b, i: (0, 0)),
                  pl.BlockSpec((N_HEADS, 1, HW), lambda b, i: (0, 0, 0))],
        out_specs=pl.BlockSpec((1, tq, HW), lambda b, i: (b, i, 0)),
        out_shape=jax.ShapeDtypeStruct((bsz, t, HW), BF16),
        compiler_params=_cparams(("parallel", "parallel"), vmem),
        name="mem_attn",
    )(q3, kt, vt, g, grp, hmask)


def _mem_step_kernel(q_ref, kt_ref, vt_ref, gq_ref, grp_ref, hrow_ref, o_ref):
    nb = q_ref.shape[0]
    qn = _mem_qnorm(q_ref[...], gq_ref, grp_ref)
    hrow = hrow_ref[...]
    row = lax.broadcasted_iota(jnp.int32, (nb, 1), 0)
    out = jnp.zeros((nb, HW), F32)
    for b in range(nb):
        q4 = hrow * qn[b:b + 1]
        s = _bdot(q4, kt_ref[0, b])
        m = jnp.max(s, axis=-1, keepdims=True)
        p = jnp.exp(s - m)
        l = jnp.sum(p, axis=-1, keepdims=True)
        o4 = _bdot_nt(p, vt_ref[0, b]) / l
        o_b = jnp.sum(hrow * o4, axis=0, keepdims=True)
        out = out + jnp.where(row == b, o_b, 0.0)
    o_ref[...] = out


def _mem_step(q, qcol, kt, vt, li, gq, grp, hrow):
    bsz = q.shape[0]
    nb = STEP_ROWS
    g = jnp.tile(gq.reshape(1, HEAD_DIM), (1, N_HEADS))
    kv_spec = pl.BlockSpec((1, nb, HW, MEM_TOKENS), lambda i: (li, i, 0, 0))
    return pl.pallas_call(
        _mem_step_kernel,
        grid=(bsz // nb,),
        in_specs=[pl.BlockSpec((nb, HW), lambda i: (i, qcol)),
                  kv_spec, kv_spec,
                  pl.BlockSpec((1, HW), lambda i: (0, 0)),
                  pl.BlockSpec((HW, HW), lambda i: (0, 0)),
                  pl.BlockSpec((8, HW), lambda i: (0, 0))],
        out_specs=pl.BlockSpec((nb, HW), lambda i: (i, 0)),
        out_shape=jax.ShapeDtypeStruct((bsz, HW), F32),
        compiler_params=_cparams(("parallel",), 24 * 1024 * 1024),
        name="mem_step",
    )(q, kt, vt, g, grp, hrow)


def _mem_kv_kernel(x_ref, gn_ref, w_ref, gk_ref, grp_ref, kt_ref, vt_ref):
    x = x_ref[0]
    ms = jnp.mean(x * x, axis=-1, keepdims=True)
    y = x * lax.rsqrt(ms + EPS) * gn_ref[0]
    kv = jnp.dot(y.astype(BF16), w_ref[0], preferred_element_type=F32)
    k = kv[:, :HW]
    ss = _dot_exact_rhs(k * k, grp_ref[...])
    k = k * lax.rsqrt(ss * (1.0 / HEAD_DIM) + EPS) * gk_ref[0]
    kt_ref[0, 0] = k.T
    vt_ref[0, 0] = kv[:, HW:].T


def _mem_kv(mem, gn, w, gk, grp):
    bsz, m, d = mem.shape
    nl = w.shape[0]
    out = jax.ShapeDtypeStruct((nl, bsz, HW, m), F32)
    o_spec = pl.BlockSpec((1, 1, HW, m), lambda l, b: (l, b, 0, 0))
    return pl.pallas_call(
        _mem_kv_kernel,
        grid=(nl, bsz),
        in_specs=[pl.BlockSpec((1, m, d), lambda l, b: (b, 0, 0)),
                  pl.BlockSpec((1, 1, d), lambda l, b: (l, 0, 0)),
                  pl.BlockSpec((1, d, 2 * HW), lambda l, b: (l, 0, 0)),
                  pl.BlockSpec((1, 1, HW), lambda l, b: (l, 0, 0)),
                  pl.BlockSpec((HW, HW), lambda l, b: (0, 0))],
        out_specs=[o_spec, o_spec],
        out_shape=[out, out],
        compiler_params=_cparams(("parallel", "parallel"), 24 * 1024 * 1024),
        name="mem_kv",
    )(mem, gn.reshape(nl, 1, d), w, jnp.tile(gk, (1, N_HEADS)).reshape(nl, 1, HW), grp)


def _band_attn_kernel(q0_ref, q1_ref, k0_ref, k1_ref, v0_ref, v1_ref, bias_ref, hmask_ref,
                      expand_ref, o_ref, og_s, ls_s, *, seq):
    g = pl.program_id(1)
    n_units = seq // BAND

    def run_group(gi, dil):
        nb = seq // dil // BAND
        shift = nb.bit_length() - 1
        lane = lax.broadcasted_iota(jnp.int32, (1, 128), 1)
        ones_cols = jnp.ones((2 * BAND, 128), BF16)

        def idx(start):
            if dil == 1:
                return pl.ds(pl.multiple_of(start, BAND), BAND)
            return pl.ds(start, BAND, stride=dil)

        def rows(ref0, ref1, start):
            return jnp.concatenate([ref0[idx(start), :], ref1[idx(start), :]], axis=1)

        def unit(u, carry):
            r = lax.shift_right_logical(u, shift)
            nblk = u & (nb - 1)
            qs = r + dil * BAND * nblk
            ks = r + dil * BAND * jnp.maximum(nblk - 1, 0)
            q = rows(q0_ref, q1_ref, qs) * ATTN_SCALE
            kk = jnp.concatenate([rows(k0_ref, k1_ref, ks), rows(k0_ref, k1_ref, qs)],
                                 axis=0).astype(BF16)
            vv = jnp.concatenate([rows(v0_ref, v1_ref, ks), rows(v0_ref, v1_ref, qs)],
                                 axis=0).astype(BF16)
            q4 = jnp.concatenate([q * hmask_ref[h] for h in range(N_HEADS)], axis=0)
            bias = bias_ref[jnp.minimum(nblk, 1)]
            s = _bdot_nt(q4, kk).reshape(N_HEADS, BAND, 2 * BAND) + bias[None]
            s = s.reshape(N_HEADS * BAND, 2 * BAND)
            m = jnp.max(s, axis=-1, keepdims=True)
            p = jnp.exp(s - m)
            oe = _bdot(p, jnp.concatenate([vv, ones_cols], axis=1))
            l = oe[:, HW:]
            o4 = oe[:, :HW] / jnp.concatenate([l, l], axis=1)
            lse4 = m + jnp.log(l)
            out = jnp.zeros((BAND, HW), F32)
            lse_t = jnp.zeros((BAND, 128), F32)
            for h in range(N_HEADS):
                hr = slice(h * BAND, (h + 1) * BAND)
                out = out + hmask_ref[h] * o4[hr]
                lse_t = lse_t + jnp.where(lane == h, lse4[hr], 0.0)
            og_s[2 * gi, idx(qs), :] = out[:, :128]
            og_s[2 * gi + 1, idx(qs), :] = out[:, 128:]
            ls_s[gi, idx(qs), :] = lse_t
            return carry

        lax.fori_loop(0, n_units, unit, 0, unroll=2)

    for gi, (_, dil) in enumerate(DIL_PAIRS):
        pl.when(g == gi)(functools.partial(run_group, gi, dil))

    @pl.when(g == N_GROUPS - 1)
    def _merge():
        expand = expand_ref[...]
        tr = 512
        for c in range(seq // tr):
            rows = slice(c * tr, (c + 1) * tr)
            l0, l1, l2 = ls_s[0, rows, :], ls_s[1, rows, :], ls_s[2, rows, :]
            m = jnp.maximum(jnp.maximum(l0, l1), l2)
            es = (jnp.exp(l0 - m), jnp.exp(l1 - m), jnp.exp(l2 - m))
            inv = 1.0 / (es[0] + es[1] + es[2])
            acc = jnp.zeros((tr, HW), F32)
            for gi in range(N_GROUPS):
                o_g = jnp.concatenate([og_s[2 * gi, rows, :], og_s[2 * gi + 1, rows, :]], axis=1)
                acc = acc + _dot_exact_rhs(es[gi] * inv, expand) * o_g
            o_ref[rows, :] = acc.astype(o_ref.dtype)


def _band_attn(qd, kv, band, hmask, expand, *, batch, seq):
    n = qd.shape[0]
    once = pl.Buffered(1)
    half = lambda cmap: pl.BlockSpec((seq, 128), cmap, pipeline_mode=once)
    vmem = (6 * seq * 128 * 4 + 2 * seq * HW * 2 + 9 * seq * 128 * 4 + 8 * 1024 * 1024)
    return pl.pallas_call(
        functools.partial(_band_attn_kernel, seq=seq),
        grid=(batch, N_GROUPS),
        in_specs=[half(lambda b, g: (b, 2 * g)), half(lambda b, g: (b, 2 * g + 1)),
                  half(lambda b, g: (b, 0)), half(lambda b, g: (b, 1)),
                  half(lambda b, g: (b, 2)), half(lambda b, g: (b, 3)),
                  pl.BlockSpec((2, BAND, 2 * BAND), lambda b, g: (0, 0, 0)),
                  pl.BlockSpec((N_HEADS, 1, HW), lambda b, g: (0, 0, 0)),
                  pl.BlockSpec((128, HW), lambda b, g: (0, 0))],
        out_specs=pl.BlockSpec((seq, HW), lambda b, g: (b, 0)),
        out_shape=jax.ShapeDtypeStruct((n, HW), BF16),
        scratch_shapes=[pltpu.VMEM((2 * N_GROUPS, seq, 128), F32),
                        pltpu.VMEM((N_GROUPS, seq, 128), F32)],
        compiler_params=_cparams(("parallel", "arbitrary"), vmem),
        name="band_attn",
    )(qd, qd, kv, kv, kv, kv, band, hmask, expand)


def _step_bias(cache_len):
    t = np.arange(cache_len)
    rows = np.zeros((16, cache_len), np.float32)
    for gi, (_, dil) in enumerate(DIL_PAIRS):
        keep = (t >= cache_len - dil * BAND) & ((cache_len - t) % dil == 0)
        rows[4 * gi:4 * gi + 4] = np.where(keep, 0.0, NEG)[None]
    return rows


def _step_attn_kernel(q_ref, kn_ref, vn_ref, kt_ref, vt_ref, bias_ref, hrow_ref, o_ref):
    j = pl.program_id(1)
    q = q_ref[pl.ds(j, 1), :] * ATTN_SCALE
    kn = kn_ref[pl.ds(j, 1), :]
    vn = vn_ref[pl.ds(j, 1), :]
    hrow = hrow_ref[...]
    row = lax.broadcasted_iota(jnp.int32, (16, 1), 0)
    q16 = hrow * jnp.where(row < 4, q[:, :HW], jnp.where(row < 8, q[:, HW:2 * HW], q[:, 2 * HW:]))
    s = _bdot(q16, kt_ref[0]) + bias_ref[...]
    s_n = jnp.sum(q16 * kn, axis=-1, keepdims=True)
    m = jnp.maximum(jnp.max(s, axis=-1, keepdims=True), s_n)
    p = jnp.exp(s - m)
    p_n = jnp.exp(s_n - m)
    l = jnp.sum(p, axis=-1, keepdims=True) + p_n
    o16 = (_bdot_nt(p, vt_ref[0]) + p_n * vn) / l
    lse = m + jnp.log(l)
    lg = [lse[4 * g:4 * g + 4] for g in range(N_GROUPS)]
    mm = jnp.maximum(jnp.maximum(lg[0], lg[1]), lg[2])
    es = [jnp.exp(x - mm) for x in lg]
    inv = 1.0 / (es[0] + es[1] + es[2])
    acc = jnp.zeros((4, HW), F32)
    for g in range(N_GROUPS):
        acc = acc + (es[g] * inv) * (hrow[0:4] * o16[4 * g:4 * g + 4])
    o_ref[pl.ds(j, 1), :] = jnp.sum(acc, axis=0, keepdims=True)


def _step_attn(q, kn, kvn, kt, vt, bias, hrow16):
    bsz, _, cache_len = kt.shape
    assert cache_len >= max(d for _, d in DIL_PAIRS) * BAND
    nb = STEP_ROWS
    rows = lambda w: pl.BlockSpec((nb, w), lambda i, j: (i, 0))
    cache = pl.BlockSpec((1, HW, cache_len), lambda i, j: (i * nb + j, 0, 0))
    vmem = 4 * HW * cache_len * 4 + 8 * 16 * cache_len * 4 + 4 * 1024 * 1024
    return pl.pallas_call(
        _step_attn_kernel,
        grid=(bsz // nb, nb),
        in_specs=[rows(N_GROUPS * HW), rows(HW),
                  pl.BlockSpec((nb, HW), lambda i, j: (i, 1)), cache, cache,
                  pl.BlockSpec((16, cache_len), lambda i, j: (0, 0)),
                  pl.BlockSpec((16, HW), lambda i, j: (0, 0))],
        out_specs=rows(HW),
        out_shape=jax.ShapeDtypeStruct((bsz, HW), F32),
        compiler_params=_cparams(("parallel", "arbitrary"), vmem),
        name="step_attn",
    )(q, kn, kvn, kt, vt, bias, hrow16)


def _out_ffn_kernel(h_ref, tok_ref, mem_ref, wo1_ref, wo2_ref, g2_ref, wgu_ref, wdn_ref, o_ref):
    h1 = (h_ref[...]
          + jnp.dot(tok_ref[...].astype(BF16), wo1_ref[...], preferred_element_type=F32)
          + jnp.dot(mem_ref[...].astype(BF16), wo2_ref[...], preferred_element_type=F32))
    ms = jnp.mean(h1 * h1, axis=-1, keepdims=True)
    hn = (h1 * lax.rsqrt(ms + EPS) * g2_ref[...]).astype(BF16)
    gate = jnp.dot(hn, wgu_ref[:, :D_FF], preferred_element_type=F32)
    up = jnp.dot(hn, wgu_ref[:, D_FF:], preferred_element_type=F32)
    act = (_silu(gate) * up).astype(BF16)
    o_ref[...] = h1 + jnp.dot(act, wdn_ref[...], preferred_element_type=F32)


def _out_ffn(h, tok, mem, wo, woi, g2, wgu, wdn, li, *, tm):
    n, d = h.shape
    wt = tok.shape[1]
    assert wo.shape[1] == wt + HW and wt % HW == 0
    once = pl.Buffered(1)
    vmem = ((wt + HW) * d * 2 + 3 * D_FF * d * 2
            + 2 * tm * (2 * d * 4 + (wt + HW) * 4) + 3 * tm * d * 4 + 3 * tm * D_FF * 4
            + 4 * 1024 * 1024)
    return pl.pallas_call(
        _out_ffn_kernel,
        grid=(n // tm,),
        in_specs=[pl.BlockSpec((tm, d), lambda i: (i, 0)),
                  pl.BlockSpec((tm, wt), lambda i: (i, 0)),
                  pl.BlockSpec((tm, HW), lambda i: (i, 0)),
                  pl.BlockSpec((None, wt, d), lambda i: (woi, 0, 0), pipeline_mode=once),
                  pl.BlockSpec((None, HW, d), lambda i: (woi, wt // HW, 0), pipeline_mode=once),
                  pl.BlockSpec((None, 1, d), lambda i: (li, 0, 0), pipeline_mode=once),
                  pl.BlockSpec((None, d, 2 * D_FF), lambda i: (li, 0, 0), pipeline_mode=once),
                  pl.BlockSpec((None, D_FF, d), lambda i: (li, 0, 0), pipeline_mode=once)],
        out_specs=pl.BlockSpec((tm, d), lambda i: (i, 0)),
        out_shape=jax.ShapeDtypeStruct((n, d), F32),
        compiler_params=_cparams(("parallel",), vmem),
        name="out_ffn",
    )(h, tok, mem, wo, wo, g2, wgu, wdn)


def _tail_transpose_kernel(x_ref, o_ref):
    o_ref[0] = x_ref[...].T


def _tail_transpose(x, col, *, batch, seq, keep, tr):
    per, nk = seq // tr, keep // tr
    return pl.pallas_call(
        _tail_transpose_kernel,
        grid=(batch, nk),
        in_specs=[pl.BlockSpec((tr, HW), lambda b, j: (b * per + per - nk + j, col))],
        out_specs=pl.BlockSpec((1, HW, tr), lambda b, j: (b, 0, j)),
        out_shape=jax.ShapeDtypeStruct((batch, HW, keep), F32),
        compiler_params=_cparams(("parallel", "parallel"), 16 * 1024 * 1024),
        name="tail_transpose",
    )(x)


def _pad_heads_cols(w):
    d = w.shape[0]
    return jnp.pad(w.reshape(d, GLA_HEADS, GLA_DV),
                   ((0, 0), (0, 0), (0, GLA_DVP - GLA_DV))).reshape(d, GLA_VP)


def _pack_a_in(w):
    q = w[:, :GLA_QK]
    k = w[:, GLA_QK:2 * GLA_QK]
    v = w[:, 2 * GLA_QK:2 * GLA_QK + GLA_V]
    g = w[:, 2 * GLA_QK + GLA_V:2 * GLA_QK + 2 * GLA_V]
    glr = w[:, 2 * GLA_QK + 2 * GLA_V:2 * GLA_QK + 2 * GLA_V + GLA_RANK]
    mq = w[:, 2 * GLA_QK + 2 * GLA_V + GLA_RANK:]
    glr = jnp.pad(glr, ((0, 0), (0, GLA_RANKP - GLA_RANK)))
    return jnp.concatenate([q, k, _pad_heads_cols(v), _pad_heads_cols(g), mq, glr],
                           axis=1).astype(BF16)


def _pack_a_out(w):
    tok = w[:GLA_V].reshape(GLA_HEADS, GLA_DV, D_MODEL)
    tok = jnp.pad(tok, ((0, 0), (0, GLA_DVP - GLA_DV), (0, 0))).reshape(GLA_VP, D_MODEL)
    return jnp.concatenate([tok, w[GLA_V:]], axis=0).astype(BF16)


def kernel(x_prompt, x_sample, state_gla, cache_win_k, cache_win_v, cache_mem_k, cache_mem_v,
           mem_prompt, norm1, norm2, a_w_in, a_w_gate2, a_b_gate, a_g_onorm, a_w_out, kv_norm, w_kv,
           g_k, b_w_in, b_g_q, b_w_out, mem_norm, w_mem_kv, g_mem_q, g_mem_k, w_ffn_gu, w_ffn_down):
    batch, seq, d = x_prompt.shape
    dec_b = x_sample.shape[0]
    past_len = 8192
    assert d == D_MODEL and seq % 512 == 0 and x_sample.shape[1] == 1

    mall_np, lmask_np = _gla_tables()
    grp_np, hmask_np, expand_np = _head_tables()
    mall = jnp.asarray(mall_np, BF16)
    lmask = jnp.asarray(lmask_np, F32)
    grp = jnp.asarray(grp_np, BF16)
    hmask = jnp.asarray(hmask_np, F32)
    expand = jnp.asarray(expand_np, BF16)
    band = jnp.asarray(_band_bias(), F32)

    hrow8 = jnp.asarray(np.concatenate([hmask_np[:, 0], np.zeros((4, HW), np.float32)], 0))
    hrow16 = jnp.asarray(np.concatenate([hmask_np[:, 0]] * N_GROUPS
                                        + [np.zeros((4, HW), np.float32)], 0))

    cos_p, sin_p = _rope_tables(jnp.arange(seq, dtype=jnp.int32))
    cos_s, sin_s = _rope_tables(past_len + jnp.arange(1, dtype=jnp.int32))

    wa_in = jnp.stack([_pack_a_in(a_w_in[i]) for i in range(N_A)], 0)
    wa_out = jnp.stack([_pack_a_out(a_w_out[i]) for i in range(N_A)], 0)
    wg2 = [jnp.pad(a_w_gate2[i], ((0, GLA_RANKP - GLA_RANK), (0, 0))).astype(BF16)
           for i in range(N_A)]
    bg = [a_b_gate[i].reshape(1, GLA_QK) for i in range(N_A)]
    gon = [jnp.pad(a_g_onorm[i], (0, GLA_DVP - GLA_DV)).reshape(1, GLA_DVP) for i in range(N_A)]
    wb_in = b_w_in.astype(BF16)
    wb_out = b_w_out.astype(BF16)
    w_kv_b = w_kv.astype(BF16).reshape(1, d, 2 * HW)
    w_mem_b = w_mem_kv.astype(BF16)
    wgu = w_ffn_gu.astype(BF16)
    wdn = w_ffn_down.astype(BF16)
    n1 = norm1.reshape(DEPTH, 1, d)
    n2 = norm2.reshape(DEPTH, 1, d)
    nkv = kv_norm.reshape(1, 1, d)

    mem_kt_p, mem_vt_p = _mem_kv(mem_prompt, mem_norm, w_mem_b, g_mem_k, grp)

    cache_len = cache_win_k.shape[1]
    win_kt = jnp.transpose(cache_win_k, (0, 2, 3, 1)).reshape(dec_b, HW, cache_len)
    win_vt = jnp.transpose(cache_win_v, (0, 2, 3, 1)).reshape(dec_b, HW, cache_len)
    mem_kt_s = jnp.transpose(cache_mem_k, (0, 1, 3, 4, 2)).reshape(DEPTH, dec_b, HW, MEM_TOKENS)
    mem_vt_s = jnp.transpose(cache_mem_v, (0, 1, 3, 4, 2)).reshape(DEPTH, dec_b, HW, MEM_TOKENS)
    state_t = jnp.swapaxes(state_gla, 3, 4)
    step_bias = jnp.asarray(_step_bias(cache_len), F32)

    def trunk(x2, bsz, t, prompt):
        n = bsz * t
        tm = 512 if prompt else n
        h = x2
        states = []
        state_out = None
        kv_sh = None
        tabs = (grp, cos_p, sin_p) if prompt else (grp, cos_s, sin_s)
        pos_blocks = max(t // tm, 1)
        for li in range(DEPTH):
            if li < N_A:
                if prompt:
                    h0t = jnp.zeros((bsz, GLA_HEADS, GLA_DVP, GLA_DK), F32)
                    tok, proj, st = _gla_prompt(h, n1, wa_in, li, h0t, wg2[li], bg[li], gon[li],
                                                mall, lmask, batch=bsz, seq=t, tb=512)
                    states.append(st[:, :, :GLA_DV])
                    mq_col = 0
                else:
                    proj = _norm_matmul(h, n1, li, wa_in, li, tm=tm, tn=A_IN_P // 3, name="a_in")
                    tok, state_out = _gla_step(proj, state_t, li, state_out, wg2[li], bg[li],
                                               gon[li])
                    mq_col = A_COL_MQ // HW
                wo, woi = wa_out, li
            else:
                bi = li - N_A
                proj = _norm_matmul_rope(h, n1, li, wb_in, bi, b_g_q[bi], tabs,
                                         rope_width=N_GROUPS * HW, tm=tm, name="b_in",
                                         pos_blocks=pos_blocks)
                mq_col = (N_GROUPS * HW) // HW
                if prompt:
                    tok = _band_attn(proj, kv_sh, band, hmask, expand, batch=bsz, seq=t)
                else:
                    tok = _step_attn(proj, kv_sh, kv_sh, win_kt, win_vt, step_bias, hrow16)
                wo, woi = wb_out, bi
            if prompt:
                mem_o = _mem_attn(proj.reshape(bsz, t, proj.shape[1]), mq_col, mem_kt_p, mem_vt_p,
                                  li, g_mem_q[li], grp, hmask, tq=512).reshape(n, HW)
            else:
                mem_o = _mem_step(proj, mq_col, mem_kt_s, mem_vt_s, li, g_mem_q[li], grp, hrow8)
            h = _out_ffn(h, tok, mem_o, wo, woi, n2, wgu, wdn, li, tm=tm)
            if li == N_A - 1:
                kv_sh = _norm_matmul_rope(h, nkv, 0, w_kv_b, 0, g_k, tabs, rope_width=HW, tm=tm,
                                          name="shared_kv", pos_blocks=pos_blocks)
        return h, states, state_out, kv_sh

    y_p, gla_p, _, kv_p = trunk(x_prompt.reshape(batch * seq, d), batch, seq, True)
    y_s, _, gla_s, kv_s = trunk(x_sample.reshape(dec_b, d), dec_b, 1, False)

    keep = min(WIN_MAX, seq)

    def window_out(x, col):
        xt = _tail_transpose(x, col, batch=batch, seq=seq, keep=keep, tr=512)
        return jnp.transpose(xt.reshape(batch, N_HEADS, HEAD_DIM, keep), (0, 3, 1, 2))

    def mem_out(xt):
        return jnp.transpose(xt.reshape(DEPTH, batch, N_HEADS, HEAD_DIM, MEM_TOKENS),
                             (0, 1, 4, 2, 3))

    return (y_p.reshape(batch, seq, d),
            y_s.reshape(dec_b, 1, d),
            jnp.swapaxes(jnp.stack(gla_p, 0), 3, 4),
            jnp.swapaxes(gla_s, 3, 4),
            window_out(kv_p, 0),
            window_out(kv_p, 1),
            kv_s[:, :HW].reshape(dec_b, 1, N_HEADS, HEAD_DIM),
            kv_s[:, HW:].reshape(dec_b, 1, N_HEADS, HEAD_DIM),
            mem_out(mem_kt_p),
            mem_out(mem_vt_p))
```

```python
import functools

import numpy as np
import jax
import jax.numpy as jnp
from jax import lax
from jax.experimental import pallas as pl
from jax.experimental.pallas import tpu as pltpu

F32 = jnp.float32
BF16 = jnp.bfloat16

D_MODEL = 1024
DEPTH = 4
N_A = 2
GLA_HEADS = 4
GLA_DK = 128
GLA_DV = 192
GLA_DVP = 256
GLA_QK = GLA_HEADS * GLA_DK
GLA_V = GLA_HEADS * GLA_DV
GLA_VP = GLA_HEADS * GLA_DVP
GLA_RANK = 16
GLA_RANKP = 128
GLA_TAU = 16.0
GLA_CHUNK = 64
HEAD_DIM = 64
N_HEADS = 4
HW = N_HEADS * HEAD_DIM
DIL_PAIRS = ((128, 1), (512, 4), (2048, 16))
N_GROUPS = 3
BAND = 128
WIN_MAX = 2048
MEM_TOKENS = 256
D_FF = 2816
FF_TILE = 256
ROPE_THETA = 10000.0
EPS = 1e-6
ATTN_SCALE = HEAD_DIM ** -0.5
NEG = -1e30
LOG2E = 1.4426950408889634
LN2 = 0.6931471805599453

A_COL_Q = 0
A_COL_K = GLA_QK
A_COL_V = 2 * GLA_QK
A_COL_G = A_COL_V + GLA_VP
A_COL_MQ = A_COL_G + GLA_VP
A_COL_GLR = A_COL_MQ + HW
A_IN_P = A_COL_GLR + GLA_RANKP

V7X_VMEM_BYTES = 64 * 1024 * 1024
VMEM_CAP = 56 * 1024 * 1024


def _cparams(sem, vmem_bytes):
    return pltpu.CompilerParams(
        dimension_semantics=sem,
        vmem_limit_bytes=int(min(max(vmem_bytes, 16 * 1024 * 1024), VMEM_CAP)))


def _bdot(a, b):
    return jnp.dot(a.astype(BF16), b.astype(BF16), preferred_element_type=F32)


def _bdot_nt(a, b):
    return lax.dot_general(a.astype(BF16), b.astype(BF16), (((1,), (1,)), ((), ())),
                           preferred_element_type=F32)


def _bdot_tn(a, b):
    return lax.dot_general(a.astype(BF16), b.astype(BF16), (((0,), (0,)), ((), ())),
                           preferred_element_type=F32)


def _split(x):
    hi = x.astype(BF16)
    lo = (x - hi.astype(F32)).astype(BF16)
    return hi, lo


def _dot_exact_rhs(x, m):
    hi, lo = _split(x)
    return (jnp.dot(hi, m, preferred_element_type=F32)
            + jnp.dot(lo, m, preferred_element_type=F32))


def _dot_exact_lhs(m2, x):
    hi, lo = _split(x)
    return jnp.dot(m2, jnp.concatenate([hi, lo], axis=0), preferred_element_type=F32)


def _silu(x):
    return x * jax.nn.sigmoid(x)


def _log_sigmoid(z):
    return jnp.minimum(z, 0.0) - jnp.log1p(jnp.exp(-jnp.abs(z)))


def _gla_tables():
    c = GLA_CHUNK
    i = np.arange(c)[:, None]
    t = np.arange(c)[None, :]
    blocks = [(t <= i), (t > i)]
    masks = []
    for lvl in range(6):
        s = c >> lvl
        half = s // 2
        mid = (i // s) * s + half - 1
        second = (i % s) >= half
        m = np.where(second, (t > mid) & (t <= i), (t > i) & (t <= mid))
        blocks.append(m)
        j = t
        masks.append(((i // s) == (j // s)) & ((i % s) >= half) & ((j % s) < half))
    masks.append(i == t)
    mall = np.concatenate(blocks, axis=0).astype(np.float32)
    mall = np.concatenate([mall, mall], axis=1)
    lmask = np.stack(masks, axis=0).astype(np.float32)
    assert np.array_equal(lmask.sum(0), (t <= i).astype(np.float32))
    return mall, lmask


def _head_tables():
    lane = np.arange(HW)
    group = (lane[:, None] // HEAD_DIM == lane[None, :] // HEAD_DIM).astype(np.float32)
    hmask = (lane[None, :] // HEAD_DIM == np.arange(N_HEADS)[:, None]).astype(np.float32)
    expand = np.zeros((128, HW), np.float32)
    expand[:N_HEADS] = hmask
    return group, hmask.reshape(N_HEADS, 1, HW), expand


def _band_bias():
    i = np.arange(BAND)[:, None]
    j = np.arange(2 * BAND)[None, :]
    dist = i + BAND - j
    ok = (dist >= 0) & (dist <= BAND)
    first = ok & (j >= BAND)
    return np.where(np.stack([first, ok], 0), 0.0, NEG).astype(np.float32)


def _rope_tables(pos):
    half = HEAD_DIM // 2
    inv = ROPE_THETA ** (-jnp.arange(half, dtype=F32) / half)
    ang = pos.astype(F32)[:, None] * inv[None, :]
    cos = jnp.cos(ang)
    sin = jnp.sin(ang)
    cos64 = jnp.concatenate([cos, cos], axis=-1)
    sin64 = jnp.concatenate([-sin, sin], axis=-1)
    return jnp.tile(cos64, (1, N_HEADS)), jnp.tile(sin64, (1, N_HEADS))


def _norm_matmul_kernel(x_ref, g_ref, w_ref, o_ref):
    x = x_ref[...]
    ms = jnp.mean(x * x, axis=-1, keepdims=True)
    y = x * lax.rsqrt(ms + EPS) * g_ref[...]
    o_ref[...] = jnp.dot(y.astype(BF16), w_ref[...], preferred_element_type=F32)


def _norm_matmul(x, g, gi, w, wi, *, tm, tn, name):
    n, d = x.shape
    nout = w.shape[2]
    vmem = 2 * (tm * d * 4 + d * tn * 2 + tm * tn * 4) + 2 * tm * d * 4 + tm * tn * 4
    return pl.pallas_call(
        _norm_matmul_kernel,
        grid=(n // tm, nout // tn),
        in_specs=[pl.BlockSpec((tm, d), lambda i, j: (i, 0)),
                  pl.BlockSpec((None, 1, d), lambda i, j: (gi, 0, 0)),
                  pl.BlockSpec((None, d, tn), lambda i, j: (wi, 0, j))],
        out_specs=pl.BlockSpec((tm, tn), lambda i, j: (i, j)),
        out_shape=jax.ShapeDtypeStruct((n, nout), F32),
        compiler_params=_cparams(("parallel", "parallel"), vmem),
        name=name,
    )(x, g, w)


def _norm_matmul_rope_kernel(x_ref, g_ref, w_ref, hg_ref, grp_ref, cos_ref, sin_ref, o_ref, *,
                             rope_width):
    x = x_ref[...]
    ms = jnp.mean(x * x, axis=-1, keepdims=True)
    xn = x * lax.rsqrt(ms + EPS) * g_ref[...]
    y = jnp.dot(xn.astype(BF16), w_ref[...], preferred_element_type=F32)
    grp = grp_ref[...]
    lane = lax.broadcasted_iota(jnp.int32, (1, HW), 1)
    first_half = (lane & (HEAD_DIM - 1)) < (HEAD_DIM // 2)
    for c in range(rope_width // HW):
        sl = slice(c * HW, (c + 1) * HW)
        yc = y[:, sl]
        ss = _dot_exact_rhs(yc * yc, grp)
        yc = yc * lax.rsqrt(ss * (1.0 / HEAD_DIM) + EPS) * hg_ref[...]
        fwd = pltpu.roll(yc, HW - HEAD_DIM // 2, 1)
        bwd = pltpu.roll(yc, HEAD_DIM // 2, 1)
        rot = jnp.where(first_half, fwd, bwd)
        o_ref[:, sl] = yc * cos_ref[...] + rot * sin_ref[...]
    o_ref[:, rope_width:] = y[:, rope_width:]


def _norm_matmul_rope(x, g, gi, w, wi, head_gain, tabs, *, rope_width, tm, name, pos_blocks=1):
    n, d = x.shape
    nout = w.shape[2]
    grp, cos, sin = tabs
    hg = jnp.tile(head_gain.reshape(1, HEAD_DIM), (1, N_HEADS))
    trow = 1 if cos.shape[0] == 1 else tm
    tab_map = (lambda i: (0, 0)) if cos.shape[0] == 1 else (lambda i: (i % pos_blocks, 0))
    vmem = (2 * (tm * d * 4 + tm * nout * 4 + 2 * trow * HW * 4) + d * nout * 2
            + 2 * tm * d * 4 + 3 * tm * nout * 4)
    return pl.pallas_call(
        functools.partial(_norm_matmul_rope_kernel, rope_width=rope_width),
        grid=(n // tm,),
        in_specs=[pl.BlockSpec((tm, d), lambda i: (i, 0)),
                  pl.BlockSpec((None, 1, d), lambda i: (gi, 0, 0)),
                  pl.BlockSpec((None, d, nout), lambda i: (wi, 0, 0),
                               pipeline_mode=pl.Buffered(1)),
                  pl.BlockSpec((1, HW), lambda i: (0, 0)),
                  pl.BlockSpec((HW, HW), lambda i: (0, 0)),
                  pl.BlockSpec((trow, HW), tab_map),
                  pl.BlockSpec((trow, HW), tab_map)],
        out_specs=pl.BlockSpec((tm, nout), lambda i: (i, 0)),
        out_shape=jax.ShapeDtypeStruct((n, nout), F32),
        compiler_params=_cparams(("parallel",), vmem),
        name=name,
    )(x, g, w, hg, grp, cos, sin)


def _gla_block(q, k, v, g, la, ht, gon, mall_ref, lmask_ref, n_chunks):
    cr = GLA_CHUNK
    chunk = lambda c: slice(c * cr, (c + 1) * cr)
    la_cat = jnp.concatenate([la[chunk(c)] for c in range(n_chunks)], axis=1)
    e_all = jnp.exp2(_dot_exact_lhs(mall_ref[...], la_cat * LOG2E))

    def decay_rows(blk):
        return jnp.concatenate(
            [e_all[blk * cr:(blk + 1) * cr, c * GLA_DK:(c + 1) * GLA_DK] for c in range(n_chunks)],
            axis=0)

    eb = decay_rows(0)
    qb = (q * eb).astype(BF16)
    kdec = (k * decay_rows(1)).astype(BF16)
    q_lv = [q.astype(BF16)]
    k_lv = [k.astype(BF16)]
    for lvl in range(6):
        el = decay_rows(2 + lvl)
        q_lv.append((q * el).astype(BF16))
        k_lv.append((k * el).astype(BF16))
    o_intra, kv_new = [], []
    for c in range(n_chunks):
        r = chunk(c)
        s = lmask_ref[6] * _bdot_nt(q_lv[0][r], k_lv[0][r])
        for lvl in range(6):
            s = s + lmask_ref[lvl] * _bdot_nt(q_lv[1 + lvl][r], k_lv[1 + lvl][r])
        o_intra.append(_bdot(s, v[r]))
        kv_new.append(_bdot_tn(v[r], kdec[r]))
    outs = []
    for c in range(n_chunks):
        outs.append(o_intra[c] + _bdot_nt(qb[chunk(c)], ht))
        ht = ht * eb[c * cr + cr - 1:(c + 1) * cr, :] + kv_new[c]
    o = jnp.concatenate(outs, axis=0)
    ms = jnp.sum(o * o, axis=-1, keepdims=True) * (1.0 / GLA_DV)
    return o * lax.rsqrt(ms + EPS) * gon * _silu(g), ht


def _gla_kernel(x_ref, g1_ref, w_ref, wg2_ref, bg_ref, gon_ref, mall_ref, lmask_ref, h0_ref,
                tok_ref, mq_ref, hout_ref, proj_s, ht_s, *, n_chunks):
    t = pl.program_id(1)

    @pl.when(t == 0)
    def _():
        ht_s[...] = h0_ref[0]

    x = x_ref[...]
    ms = jnp.mean(x * x, axis=-1, keepdims=True)
    xn = (x * lax.rsqrt(ms + EPS) * g1_ref[...]).astype(BF16)
    proj_s[...] = jnp.dot(xn, w_ref[...], preferred_element_type=F32)
    mq_ref[...] = proj_s[:, A_COL_MQ:A_COL_MQ + HW]
    z = _bdot(proj_s[:, A_COL_GLR:A_COL_GLR + GLA_RANKP], wg2_ref[...]) + bg_ref[...]
    la = _log_sigmoid(z) * (1.0 / GLA_TAU)
    gon = gon_ref[...]
    for h in range(GLA_HEADS):
        kc = slice(h * GLA_DK, (h + 1) * GLA_DK)
        vc = slice(h * GLA_DVP, (h + 1) * GLA_DVP)
        q = proj_s[:, A_COL_Q + h * GLA_DK:A_COL_Q + (h + 1) * GLA_DK] * (GLA_DK ** -0.5)
        k = proj_s[:, A_COL_K + h * GLA_DK:A_COL_K + (h + 1) * GLA_DK]
        v = proj_s[:, A_COL_V + h * GLA_DVP:A_COL_V + (h + 1) * GLA_DVP].astype(BF16)
        g = proj_s[:, A_COL_G + h * GLA_DVP:A_COL_G + (h + 1) * GLA_DVP]
        tok, ht = _gla_block(q, k, v, g, la[:, kc], ht_s[h], gon, mall_ref, lmask_ref, n_chunks)
        tok_ref[:, vc] = tok.astype(tok_ref.dtype)
        ht_s[h] = ht

    @pl.when(t == pl.num_programs(1) - 1)
    def _():
        hout_ref[0] = ht_s[...]


def _gla_prompt(x, g1, w_in, li, h0t, wg2, bg, gon, mall, lmask, *, batch, seq, tb):
    n, d = x.shape
    nt = seq // tb
    const2 = lambda shape: pl.BlockSpec(shape, lambda b, t: (0,) * len(shape))
    st_spec = pl.BlockSpec((1, GLA_HEADS, GLA_DVP, GLA_DK), lambda b, t: (b, 0, 0, 0))
    vmem = (2 * tb * d * 4 + d * A_IN_P * 2 + tb * A_IN_P * 4 + 2 * tb * (GLA_VP * 2 + HW * 4)
            + 6 * GLA_HEADS * GLA_DVP * GLA_DK * 4 + 24 * tb * GLA_DVP * 4)
    return pl.pallas_call(
        functools.partial(_gla_kernel, n_chunks=tb // GLA_CHUNK),
        grid=(batch, nt),
        in_specs=[
            pl.BlockSpec((tb, d), lambda b, t: (b * nt + t, 0)),
            pl.BlockSpec((None, 1, d), lambda b, t: (li, 0, 0)),
            pl.BlockSpec((None, d, A_IN_P), lambda b, t: (li, 0, 0), pipeline_mode=pl.Buffered(1)),
            const2((GLA_RANKP, GLA_QK)), const2((1, GLA_QK)), const2((1, GLA_DVP)),
            const2(mall.shape), const2(lmask.shape), st_spec,
        ],
        out_specs=[
            pl.BlockSpec((tb, GLA_VP), lambda b, t: (b * nt + t, 0)),
            pl.BlockSpec((tb, HW), lambda b, t: (b * nt + t, 0)),
            st_spec,
        ],
        out_shape=[jax.ShapeDtypeStruct((n, GLA_VP), BF16),
                   jax.ShapeDtypeStruct((n, HW), F32),
                   jax.ShapeDtypeStruct((batch, GLA_HEADS, GLA_DVP, GLA_DK), F32)],
        scratch_shapes=[pltpu.VMEM((tb, A_IN_P), F32),
                        pltpu.VMEM((GLA_HEADS, GLA_DVP, GLA_DK), F32)],
        compiler_params=_cparams(("parallel", "arbitrary"), vmem),
        name="gla_prompt",
    )(x, g1, w_in, wg2, bg, gon, mall, lmask, h0t)


STEP_ROWS = 8


def _gla_step_kernel(q_ref, k_ref, v_ref, g_ref, glr_ref, wg2_ref, bg_ref, gon_ref, st_ref, *rest):
    tok_ref, so_ref = rest[-2:]
    nb = q_ref.shape[0]
    z = _bdot(glr_ref[...], wg2_ref[...]) + bg_ref[...]
    a = jnp.exp(_log_sigmoid(z) * (1.0 / GLA_TAU))
    q = q_ref[...] * (GLA_DK ** -0.5)
    k = k_ref[...]
    v = v_ref[...]
    g = g_ref[...]
    gon = gon_ref[...][:, :GLA_DV]
    row = lax.broadcasted_iota(jnp.int32, (nb, 1), 0)
    for h in range(GLA_HEADS):
        ks = slice(h * GLA_DK, (h + 1) * GLA_DK)
        vs = slice(h * GLA_DVP, h * GLA_DVP + GLA_DV)
        o_rows = jnp.zeros((nb, GLA_DV), F32)
        for b in range(nb):
            only_b = row == b
            outer = _bdot_tn(jnp.where(only_b, v[:, vs], 0.0), k[:, ks])
            s_new = st_ref[0, b, h] * a[b:b + 1, ks] + outer
            so_ref[0, b, h] = s_new
            o_rows = o_rows + jnp.where(only_b, _bdot_nt(q[:, ks], s_new), 0.0)
        ms = jnp.sum(o_rows * o_rows, axis=-1, keepdims=True) * (1.0 / GLA_DV)
        tok_ref[:, vs] = o_rows * lax.rsqrt(ms + EPS) * gon * _silu(g[:, vs])
        tok_ref[:, h * GLA_DVP + GLA_DV:(h + 1) * GLA_DVP] = jnp.zeros(
            (nb, GLA_DVP - GLA_DV), F32)


def _gla_step(proj, state_t, li, prev_out, wg2, bg, gon):
    bsz = proj.shape[0]
    nb = STEP_ROWS
    col = lambda w, c: pl.BlockSpec((nb, w), lambda i: (i, c))
    st_spec = pl.BlockSpec((1, nb, GLA_HEADS, GLA_DV, GLA_DK), lambda i: (li, i, 0, 0, 0))
    in_specs = [col(GLA_QK, A_COL_Q // GLA_QK), col(GLA_QK, A_COL_K // GLA_QK),
                col(GLA_VP, A_COL_V // GLA_VP), col(GLA_VP, A_COL_G // GLA_VP),
                col(GLA_RANKP, A_COL_GLR // GLA_RANKP),
                pl.BlockSpec((GLA_RANKP, GLA_QK), lambda i: (0, 0)),
                pl.BlockSpec((1, GLA_QK), lambda i: (0, 0)),
                pl.BlockSpec((1, GLA_DVP), lambda i: (0, 0)),
                st_spec]
    args = [proj, proj, proj, proj, proj, wg2, bg, gon, state_t]
    aliases = {}
    if prev_out is not None:
        in_specs.append(pl.BlockSpec(memory_space=pl.ANY))
        args.append(prev_out)
        aliases = {len(args) - 1: 1}
    return pl.pallas_call(
        _gla_step_kernel,
        grid=(bsz // nb,),
        in_specs=in_specs,
        out_specs=[pl.BlockSpec((nb, GLA_VP), lambda i: (i, 0)), st_spec],
        out_shape=[jax.ShapeDtypeStruct((bsz, GLA_VP), F32),
                   jax.ShapeDtypeStruct(state_t.shape, F32)],
        input_output_aliases=aliases,
        compiler_params=_cparams(("parallel",), 32 * 1024 * 1024),
        name="gla_step",
    )(*args)


def _mem_qnorm(q, gq_ref, grp_ref):
    ss = _dot_exact_rhs(q * q, grp_ref[...])
    return q * lax.rsqrt(ss * (1.0 / HEAD_DIM) + EPS) * gq_ref[...] * ATTN_SCALE


def _mem_attn_kernel(q_ref, kt_ref, vt_ref, gq_ref, grp_ref, hmask_ref, o_ref):
    qn = _mem_qnorm(q_ref[0], gq_ref, grp_ref)
    tq = qn.shape[0]
    kt = kt_ref[0, 0].astype(BF16)
    vt = jnp.concatenate([vt_ref[0, 0].astype(BF16), jnp.ones((128, MEM_TOKENS), BF16)], axis=0)
    q4 = jnp.concatenate([qn * hmask_ref[h] for h in range(N_HEADS)], axis=0)
    s = _bdot(q4, kt)
    m = jnp.max(s, axis=-1, keepdims=True)
    oe = _bdot_nt(jnp.exp(s - m), vt)
    l = oe[:, HW:]
    o4 = oe[:, :HW] / jnp.concatenate([l, l], axis=1)
    out = jnp.zeros(qn.shape, F32)
    for h in range(N_HEADS):
        out = out + hmask_ref[h] * o4[h * tq:(h + 1) * tq]
    o_ref[0] = out.astype(o_ref.dtype)


def _mem_attn(q3, qcol, kt, vt, li, gq, grp, hmask, *, tq):
    bsz, t, _ = q3.shape
    g = jnp.tile(gq.reshape(1, HEAD_DIM), (1, N_HEADS))
    vmem = 2 * (tq * HW * 4 + 2 * MEM_TOKENS * HW * 4 + tq * HW * 2) + 16 * tq * HW * 4
    kv_spec = pl.BlockSpec((1, 1, HW, MEM_TOKENS), lambda b, i: (li, b, 0, 0))
    return pl.pallas_call(
        _mem_attn_kernel,
        grid=(bsz, t // tq),
        in_specs=[pl.BlockSpec((1, tq, HW), lambda b, i: (b, i, qcol)),
                  kv_spec, kv_spec,
                  pl.BlockSpec((1, HW), lambda b, i: (0, 0)),
                  pl.BlockSpec((HW, HW), lambda b, i: (0, 0)),
                  pl.BlockSpec((N_HEADS, 1, HW), lambda b, i: (0, 0, 0))],
        out_specs=pl.BlockSpec((1, tq, HW), lambda b, i: (b, i, 0)),
        out_shape=jax.ShapeDtypeStruct((bsz, t, HW), BF16),
        compiler_params=_cparams(("parallel", "parallel"), vmem),
        name="mem_attn",
    )(q3, kt, vt, g, grp, hmask)


def _mem_step_kernel(q_ref, kt_ref, vt_ref, gq_ref, grp_ref, hrow_ref, o_ref):
    nb = q_ref.shape[0]
    qn = _mem_qnorm(q_ref[...], gq_ref, grp_ref)
    hrow = hrow_ref[...]
    row = lax.broadcasted_iota(jnp.int32, (nb, 1), 0)
    out = jnp.zeros((nb, HW), F32)
    for b in range(nb):
        q4 = hrow * qn[b:b + 1]
        s = _bdot(q4, kt_ref[0, b])
        m = jnp.max(s, axis=-1, keepdims=True)
        p = jnp.exp(s - m)
        l = jnp.sum(p, axis=-1, keepdims=True)
        o4 = _bdot_nt(p, vt_ref[0, b]) / l
        o_b = jnp.sum(hrow * o4, axis=0, keepdims=True)
        out = out + jnp.where(row == b, o_b, 0.0)
    o_ref[...] = out


def _mem_step(q, qcol, kt, vt, li, gq, grp, hrow):
    bsz = q.shape[0]
    nb = STEP_ROWS
    g = jnp.tile(gq.reshape(1, HEAD_DIM), (1, N_HEADS))
    kv_spec = pl.BlockSpec((1, nb, HW, MEM_TOKENS), lambda i: (li, i, 0, 0))
    return pl.pallas_call(
        _mem_step_kernel,
        grid=(bsz // nb,),
        in_specs=[pl.BlockSpec((nb, HW), lambda i: (i, qcol)),
                  kv_spec, kv_spec,
                  pl.BlockSpec((1, HW), lambda i: (0, 0)),
                  pl.BlockSpec((HW, HW), lambda i: (0, 0)),
                  pl.BlockSpec((8, HW), lambda i: (0, 0))],
        out_specs=pl.BlockSpec((nb, HW), lambda i: (i, 0)),
        out_shape=jax.ShapeDtypeStruct((bsz, HW), F32),
        compiler_params=_cparams(("parallel",), 24 * 1024 * 1024),
        name="mem_step",
    )(q, kt, vt, g, grp, hrow)


def _mem_kv_kernel(x_ref, gn_ref, w_ref, gk_ref, grp_ref, kt_ref, vt_ref):
    x = x_ref[0]
    ms = jnp.mean(x * x, axis=-1, keepdims=True)
    y = x * lax.rsqrt(ms + EPS) * gn_ref[0]
    kv = jnp.dot(y.astype(BF16), w_ref[0], preferred_element_type=F32)
    k = kv[:, :HW]
    ss = _dot_exact_rhs(k * k, grp_ref[...])
    k = k * lax.rsqrt(ss * (1.0 / HEAD_DIM) + EPS) * gk_ref[0]
    kt_ref[0, 0] = k.T
    vt_ref[0, 0] = kv[:, HW:].T


def _mem_kv(mem, gn, w, gk, grp):
    bsz, m, d = mem.shape
    nl = w.shape[0]
    out = jax.ShapeDtypeStruct((nl, bsz, HW, m), F32)
    o_spec = pl.BlockSpec((1, 1, HW, m), lambda l, b: (l, b, 0, 0))
    return pl.pallas_call(
        _mem_kv_kernel,
        grid=(nl, bsz),
        in_specs=[pl.BlockSpec((1, m, d), lambda l, b: (b, 0, 0)),
                  pl.BlockSpec((1, 1, d), lambda l, b: (l, 0, 0)),
                  pl.BlockSpec((1, d, 2 * HW), lambda l, b: (l, 0, 0)),
                  pl.BlockSpec((1, 1, HW), lambda l, b: (l, 0, 0)),
                  pl.BlockSpec((HW, HW), lambda l, b: (0, 0))],
        out_specs=[o_spec, o_spec],
        out_shape=[out, out],
        compiler_params=_cparams(("parallel", "parallel"), 24 * 1024 * 1024),
        name="mem_kv",
    )(mem, gn.reshape(nl, 1, d), w, jnp.tile(gk, (1, N_HEADS)).reshape(nl, 1, HW), grp)


def _band_attn_kernel(q0_ref, q1_ref, k0_ref, k1_ref, v0_ref, v1_ref, bias_ref, hmask_ref,
                      expand_ref, o_ref, og_s, ls_s, *, seq):
    g = pl.program_id(1)
    n_units = seq // BAND

    def run_group(gi, dil):
        nb = seq // dil // BAND
        shift = nb.bit_length() - 1
        lane = lax.broadcasted_iota(jnp.int32, (1, 128), 1)
        low_lanes = lane < HEAD_DIM
        ones_cols = jnp.ones((2 * BAND, 128), BF16)

        def idx(start):
            if dil == 1:
                return pl.ds(pl.multiple_of(start, BAND), BAND)
            return pl.ds(start, BAND, stride=dil)

        def rows(ref0, ref1, start):
            return jnp.concatenate([ref0[idx(start), :], ref1[idx(start), :]], axis=1)

        def unit(u, carry):
            r = lax.shift_right_logical(u, shift)
            nblk = u & (nb - 1)
            qs = r + dil * BAND * nblk
            ks = r + dil * BAND * jnp.maximum(nblk - 1, 0)
            q = rows(q0_ref, q1_ref, qs) * (ATTN_SCALE * LOG2E)
            kk = jnp.concatenate([rows(k0_ref, k1_ref, ks), rows(k0_ref, k1_ref, qs)],
                                 axis=0).astype(BF16)
            vv = jnp.concatenate([rows(v0_ref, v1_ref, ks), rows(v0_ref, v1_ref, qs)],
                                 axis=0).astype(BF16)
            q4 = jnp.concatenate([q * hmask_ref[h] for h in range(N_HEADS)], axis=0)
            bias = bias_ref[jnp.minimum(nblk, 1)]
            s = _bdot_nt(q4, kk).reshape(N_HEADS, BAND, 2 * BAND) + bias[None]
            s = s.reshape(N_HEADS * BAND, 2 * BAND)
            m = jnp.max(s, axis=-1, keepdims=True)
            p = jnp.exp2(s - m)
            oe = _bdot(p, jnp.concatenate([vv, ones_cols], axis=1))
            hd = [slice(h * BAND, (h + 1) * BAND) for h in range(N_HEADS)]
            for half in range(2):
                ha, hb = hd[2 * half], hd[2 * half + 1]
                cols = slice(half * 128, (half + 1) * 128)
                o_h = jnp.where(low_lanes, oe[ha, cols], oe[hb, cols])
                l_h = jnp.where(low_lanes, oe[ha, HW:], oe[hb, HW:])
                og_s[2 * gi + half, idx(qs), :] = o_h / l_h
            l_t = oe[hd[0], HW:]
            m_t = jnp.broadcast_to(m[hd[0]], (BAND, 128))
            for h in range(1, N_HEADS):
                l_t = jnp.where(lane == h, oe[hd[h], HW:], l_t)
                m_t = jnp.where(lane == h, m[hd[h]], m_t)
            ls_s[gi, idx(qs), :] = (m_t + jnp.log2(l_t)) * LN2
            return carry

        lax.fori_loop(0, n_units, unit, 0, unroll=4)

    for gi, (_, dil) in enumerate(DIL_PAIRS):
        pl.when(g == gi)(functools.partial(run_group, gi, dil))

    @pl.when(g == N_GROUPS - 1)
    def _merge():
        expand = expand_ref[...]
        tr = 512
        for c in range(seq // tr):
            rows = slice(c * tr, (c + 1) * tr)
            l0, l1, l2 = ls_s[0, rows, :], ls_s[1, rows, :], ls_s[2, rows, :]
            m = jnp.maximum(jnp.maximum(l0, l1), l2)
            es = (jnp.exp(l0 - m), jnp.exp(l1 - m), jnp.exp(l2 - m))
            inv = 1.0 / (es[0] + es[1] + es[2])
            acc = jnp.zeros((tr, HW), F32)
            for gi in range(N_GROUPS):
                o_g = jnp.concatenate([og_s[2 * gi, rows, :], og_s[2 * gi + 1, rows, :]], axis=1)
                acc = acc + _dot_exact_rhs(es[gi] * inv, expand) * o_g
            o_ref[rows, :] = acc.astype(o_ref.dtype)


def _band_attn(qd, kv, band, hmask, expand, *, batch, seq):
    n = qd.shape[0]
    once = pl.Buffered(1)
    half = lambda cmap: pl.BlockSpec((seq, 128), cmap, pipeline_mode=once)
    vmem = (6 * seq * 128 * 4 + 2 * seq * HW * 2 + 9 * seq * 128 * 4 + 8 * 1024 * 1024)
    return pl.pallas_call(
        functools.partial(_band_attn_kernel, seq=seq),
        grid=(batch, N_GROUPS),
        in_specs=[half(lambda b, g: (b, 2 * g)), half(lambda b, g: (b, 2 * g + 1)),
                  half(lambda b, g: (b, 0)), half(lambda b, g: (b, 1)),
                  half(lambda b, g: (b, 2)), half(lambda b, g: (b, 3)),
                  pl.BlockSpec((2, BAND, 2 * BAND), lambda b, g: (0, 0, 0)),
                  pl.BlockSpec((N_HEADS, 1, HW), lambda b, g: (0, 0, 0)),
                  pl.BlockSpec((128, HW), lambda b, g: (0, 0))],
        out_specs=pl.BlockSpec((seq, HW), lambda b, g: (b, 0)),
        out_shape=jax.ShapeDtypeStruct((n, HW), BF16),
        scratch_shapes=[pltpu.VMEM((2 * N_GROUPS, seq, 128), F32),
                        pltpu.VMEM((N_GROUPS, seq, 128), F32)],
        compiler_params=_cparams(("parallel", "arbitrary"), vmem),
        name="band_attn",
    )(qd, qd, kv, kv, kv, kv, band, hmask, expand)


def _step_bias(cache_len):
    t = np.arange(cache_len)
    rows = np.zeros((16, cache_len), np.float32)
    for gi, (_, dil) in enumerate(DIL_PAIRS):
        keep = (t >= cache_len - dil * BAND) & ((cache_len - t) % dil == 0)
        rows[4 * gi:4 * gi + 4] = np.where(keep, 0.0, NEG)[None]
    return rows


def _step_attn_kernel(q_ref, kn_ref, vn_ref, kt_ref, vt_ref, bias_ref, hrow_ref, o_ref):
    j = pl.program_id(1)
    q = q_ref[pl.ds(j, 1), :] * ATTN_SCALE
    kn = kn_ref[pl.ds(j, 1), :]
    vn = vn_ref[pl.ds(j, 1), :]
    hrow = hrow_ref[...]
    row = lax.broadcasted_iota(jnp.int32, (16, 1), 0)
    q16 = hrow * jnp.where(row < 4, q[:, :HW], jnp.where(row < 8, q[:, HW:2 * HW], q[:, 2 * HW:]))
    s = _bdot(q16, kt_ref[0]) + bias_ref[...]
    s_n = jnp.sum(q16 * kn, axis=-1, keepdims=True)
    m = jnp.maximum(jnp.max(s, axis=-1, keepdims=True), s_n)
    p = jnp.exp(s - m)
    p_n = jnp.exp(s_n - m)
    l = jnp.sum(p, axis=-1, keepdims=True) + p_n
    o16 = (_bdot_nt(p, vt_ref[0]) + p_n * vn) / l
    lse = m + jnp.log(l)
    lg = [lse[4 * g:4 * g + 4] for g in range(N_GROUPS)]
    mm = jnp.maximum(jnp.maximum(lg[0], lg[1]), lg[2])
    es = [jnp.exp(x - mm) for x in lg]
    inv = 1.0 / (es[0] + es[1] + es[2])
    acc = jnp.zeros((4, HW), F32)
    for g in range(N_GROUPS):
        acc = acc + (es[g] * inv) * (hrow[0:4] * o16[4 * g:4 * g + 4])
    o_ref[pl.ds(j, 1), :] = jnp.sum(acc, axis=0, keepdims=True)


def _step_attn(q, kn, kvn, kt, vt, bias, hrow16):
    bsz, _, cache_len = kt.shape
    assert cache_len >= max(d for _, d in DIL_PAIRS) * BAND
    nb = STEP_ROWS
    rows = lambda w: pl.BlockSpec((nb, w), lambda i, j: (i, 0))
    cache = pl.BlockSpec((1, HW, cache_len), lambda i, j: (i * nb + j, 0, 0))
    vmem = 4 * HW * cache_len * 4 + 8 * 16 * cache_len * 4 + 4 * 1024 * 1024
    return pl.pallas_call(
        _step_attn_kernel,
        grid=(bsz // nb, nb),
        in_specs=[rows(N_GROUPS * HW), rows(HW),
                  pl.BlockSpec((nb, HW), lambda i, j: (i, 1)), cache, cache,
                  pl.BlockSpec((16, cache_len), lambda i, j: (0, 0)),
                  pl.BlockSpec((16, HW), lambda i, j: (0, 0))],
        out_specs=rows(HW),
        out_shape=jax.ShapeDtypeStruct((bsz, HW), F32),
        compiler_params=_cparams(("parallel", "arbitrary"), vmem),
        name="step_attn",
    )(q, kn, kvn, kt, vt, bias, hrow16)


def _out_ffn_kernel(h_ref, tok_ref, mem_ref, wo1_ref, wo2_ref, g2_ref, wgu_ref, wdn_ref, o_ref):
    h1 = (h_ref[...]
          + jnp.dot(tok_ref[...].astype(BF16), wo1_ref[...], preferred_element_type=F32)
          + jnp.dot(mem_ref[...].astype(BF16), wo2_ref[...], preferred_element_type=F32))
    ms = jnp.mean(h1 * h1, axis=-1, keepdims=True)
    hn = (h1 * lax.rsqrt(ms + EPS) * g2_ref[...]).astype(BF16)
    gate = jnp.dot(hn, wgu_ref[:, :D_FF], preferred_element_type=F32)
    up = jnp.dot(hn, wgu_ref[:, D_FF:], preferred_element_type=F32)
    act = (_silu(gate) * up).astype(BF16)
    o_ref[...] = h1 + jnp.dot(act, wdn_ref[...], preferred_element_type=F32)


def _out_ffn(h, tok, mem, wo, woi, g2, wgu, wdn, li, *, tm):
    n, d = h.shape
    wt = tok.shape[1]
    assert wo.shape[1] == wt + HW and wt % HW == 0
    once = pl.Buffered(1)
    vmem = ((wt + HW) * d * 2 + 3 * D_FF * d * 2
            + 2 * tm * (2 * d * 4 + (wt + HW) * 4) + 3 * tm * d * 4 + 3 * tm * D_FF * 4
            + 4 * 1024 * 1024)
    in_specs = [pl.BlockSpec((tm, d), lambda i: (i, 0)),
                pl.BlockSpec((tm, wt), lambda i: (i, 0)),
                pl.BlockSpec((tm, HW), lambda i: (i, 0)),
                pl.BlockSpec((None, wt, d), lambda i: (woi, 0, 0), pipeline_mode=once),
                pl.BlockSpec((None, HW, d), lambda i: (woi, wt // HW, 0), pipeline_mode=once),
                pl.BlockSpec((None, 1, d), lambda i: (li, 0, 0), pipeline_mode=once),
                pl.BlockSpec((None, d, 2 * D_FF), lambda i: (li, 0, 0), pipeline_mode=once),
                pl.BlockSpec((None, D_FF, d), lambda i: (li, 0, 0), pipeline_mode=once)]
    args = [h, tok, mem, wo, wo, g2, wgu, wdn]
    return pl.pallas_call(
        _out_ffn_kernel,
        grid=(n // tm,),
        in_specs=in_specs,
        out_specs=pl.BlockSpec((tm, d), lambda i: (i, 0)),
        out_shape=jax.ShapeDtypeStruct((n, d), F32),
        compiler_params=_cparams(("parallel",), vmem),
        name="out_ffn",
    )(*args)


def _tail_transpose_kernel(x_ref, o_ref):
    o_ref[0] = x_ref[...].T


def _tail_transpose(x, col, *, batch, seq, keep, tr):
    per, nk = seq // tr, keep // tr
    return pl.pallas_call(
        _tail_transpose_kernel,
        grid=(batch, nk),
        in_specs=[pl.BlockSpec((tr, HW), lambda b, j: (b * per + per - nk + j, col))],
        out_specs=pl.BlockSpec((1, HW, tr), lambda b, j: (b, 0, j)),
        out_shape=jax.ShapeDtypeStruct((batch, HW, keep), F32),
        compiler_params=_cparams(("parallel", "parallel"), 16 * 1024 * 1024),
        name="tail_transpose",
    )(x)


def _pad_heads_cols(w):
    d = w.shape[0]
    return jnp.pad(w.reshape(d, GLA_HEADS, GLA_DV),
                   ((0, 0), (0, 0), (0, GLA_DVP - GLA_DV))).reshape(d, GLA_VP)


def _pack_a_in(w):
    q = w[:, :GLA_QK]
    k = w[:, GLA_QK:2 * GLA_QK]
    v = w[:, 2 * GLA_QK:2 * GLA_QK + GLA_V]
    g = w[:, 2 * GLA_QK + GLA_V:2 * GLA_QK + 2 * GLA_V]
    glr = w[:, 2 * GLA_QK + 2 * GLA_V:2 * GLA_QK + 2 * GLA_V + GLA_RANK]
    mq = w[:, 2 * GLA_QK + 2 * GLA_V + GLA_RANK:]
    glr = jnp.pad(glr, ((0, 0), (0, GLA_RANKP - GLA_RANK)))
    return jnp.concatenate([q, k, _pad_heads_cols(v), _pad_heads_cols(g), mq, glr],
                           axis=1).astype(BF16)


def _pack_a_out(w):
    tok = w[:GLA_V].reshape(GLA_HEADS, GLA_DV, D_MODEL)
    tok = jnp.pad(tok, ((0, 0), (0, GLA_DVP - GLA_DV), (0, 0))).reshape(GLA_VP, D_MODEL)
    return jnp.concatenate([tok, w[GLA_V:]], axis=0).astype(BF16)


def kernel(x_prompt, x_sample, state_gla, cache_win_k, cache_win_v, cache_mem_k, cache_mem_v,
           mem_prompt, norm1, norm2, a_w_in, a_w_gate2, a_b_gate, a_g_onorm, a_w_out, kv_norm, w_kv,
           g_k, b_w_in, b_g_q, b_w_out, mem_norm, w_mem_kv, g_mem_q, g_mem_k, w_ffn_gu, w_ffn_down):
    batch, seq, d = x_prompt.shape
    dec_b = x_sample.shape[0]
    past_len = 8192
    assert d == D_MODEL and seq % 512 == 0 and x_sample.shape[1] == 1

    mall_np, lmask_np = _gla_tables()
    grp_np, hmask_np, expand_np = _head_tables()
    mall = jnp.asarray(mall_np, BF16)
    lmask = jnp.asarray(lmask_np, F32)
    grp = jnp.asarray(grp_np, BF16)
    hmask = jnp.asarray(hmask_np, F32)
    expand = jnp.asarray(expand_np, BF16)
    band = jnp.asarray(_band_bias(), F32)

    hrow8 = jnp.asarray(np.concatenate([hmask_np[:, 0], np.zeros((4, HW), np.float32)], 0))
    hrow16 = jnp.asarray(np.concatenate([hmask_np[:, 0]] * N_GROUPS
                                        + [np.zeros((4, HW), np.float32)], 0))

    cos_p, sin_p = _rope_tables(jnp.arange(seq, dtype=jnp.int32))
    cos_s, sin_s = _rope_tables(past_len + jnp.arange(1, dtype=jnp.int32))

    wa_in = jnp.stack([_pack_a_in(a_w_in[i]) for i in range(N_A)], 0)
    wa_out = jnp.stack([_pack_a_out(a_w_out[i]) for i in range(N_A)], 0)
    wg2 = [jnp.pad(a_w_gate2[i], ((0, GLA_RANKP - GLA_RANK), (0, 0))).astype(BF16)
           for i in range(N_A)]
    bg = [a_b_gate[i].reshape(1, GLA_QK) for i in range(N_A)]
    gon = [jnp.pad(a_g_onorm[i], (0, GLA_DVP - GLA_DV)).reshape(1, GLA_DVP) for i in range(N_A)]
    wb_in = b_w_in.astype(BF16)
    wb_out = b_w_out.astype(BF16)
    w_kv_b = w_kv.astype(BF16).reshape(1, d, 2 * HW)
    w_mem_b = w_mem_kv.astype(BF16)
    wgu = w_ffn_gu.astype(BF16)
    wdn = w_ffn_down.astype(BF16)
    n1 = norm1.reshape(DEPTH, 1, d)
    n2 = norm2.reshape(DEPTH, 1, d)
    nkv = kv_norm.reshape(1, 1, d)

    mem_kt_p, mem_vt_p = _mem_kv(mem_prompt, mem_norm, w_mem_b, g_mem_k, grp)

    cache_len = cache_win_k.shape[1]
    win_kt = jnp.transpose(cache_win_k, (0, 2, 3, 1)).reshape(dec_b, HW, cache_len)
    win_vt = jnp.transpose(cache_win_v, (0, 2, 3, 1)).reshape(dec_b, HW, cache_len)
    mem_kt_s = jnp.transpose(cache_mem_k, (0, 1, 3, 4, 2)).reshape(DEPTH, dec_b, HW, MEM_TOKENS)
    mem_vt_s = jnp.transpose(cache_mem_v, (0, 1, 3, 4, 2)).reshape(DEPTH, dec_b, HW, MEM_TOKENS)
    state_t = jnp.swapaxes(state_gla, 3, 4)
    step_bias = jnp.asarray(_step_bias(cache_len), F32)

    def trunk(x2, bsz, t, prompt):
        n = bsz * t
        tm = 512 if prompt else n
        h = x2
        states = []
        state_out = None
        kv_sh = None
        tabs = (grp, cos_p, sin_p) if prompt else (grp, cos_s, sin_s)
        pos_blocks = max(t // tm, 1)
        for li in range(DEPTH):
            if li < N_A:
                if prompt:
                    h0t = jnp.zeros((bsz, GLA_HEADS, GLA_DVP, GLA_DK), F32)
                    tok, proj, st = _gla_prompt(h, n1, wa_in, li, h0t, wg2[li], bg[li], gon[li],
                                                mall, lmask, batch=bsz, seq=t, tb=512)
                    states.append(st[:, :, :GLA_DV])
                    mq_col = 0
                else:
                    proj = _norm_matmul(h, n1, li, wa_in, li, tm=tm, tn=A_IN_P // 3, name="a_in")
                    tok, state_out = _gla_step(proj, state_t, li, state_out, wg2[li], bg[li],
                                               gon[li])
                    mq_col = A_COL_MQ // HW
                wo, woi = wa_out, li
            else:
                bi = li - N_A
                proj = _norm_matmul_rope(h, n1, li, wb_in, bi, b_g_q[bi], tabs,
                                         rope_width=N_GROUPS * HW, tm=tm, name="b_in",
                                         pos_blocks=pos_blocks)
                mq_col = (N_GROUPS * HW) // HW
                if prompt:
                    tok = _band_attn(proj, kv_sh, band, hmask, expand, batch=bsz, seq=t)
                else:
                    tok = _step_attn(proj, kv_sh, kv_sh, win_kt, win_vt, step_bias, hrow16)
                wo, woi = wb_out, bi
            if prompt:
                mem_o = _mem_attn(proj.reshape(bsz, t, proj.shape[1]), mq_col, mem_kt_p, mem_vt_p,
                                  li, g_mem_q[li], grp, hmask, tq=512).reshape(n, HW)
            else:
                mem_o = _mem_step(proj, mq_col, mem_kt_s, mem_vt_s, li, g_mem_q[li], grp, hrow8)
            h = _out_ffn(h, tok, mem_o, wo, woi, n2, wgu, wdn, li, tm=tm)
            if li == N_A - 1:
                kv_sh = _norm_matmul_rope(h, nkv, 0, w_kv_b, 0, g_k, tabs, rope_width=HW, tm=tm,
                                          name="shared_kv", pos_blocks=pos_blocks)
        return h, states, state_out, kv_sh

    y_p, gla_p, _, kv_p = trunk(x_prompt.reshape(batch * seq, d), batch, seq, True)
    y_s, _, gla_s, kv_s = trunk(x_sample.reshape(dec_b, d), dec_b, 1, False)

    keep = min(WIN_MAX, seq)

    def window_out(x, col):
        xt = _tail_transpose(x, col, batch=batch, seq=seq, keep=keep, tr=512)
        return jnp.transpose(xt.reshape(batch, N_HEADS, HEAD_DIM, keep), (0, 3, 1, 2))

    def mem_out(xt):
        return jnp.transpose(xt.reshape(DEPTH, batch, N_HEADS, HEAD_DIM, MEM_TOKENS),
                             (0, 1, 4, 2, 3))

    return (y_p.reshape(batch, seq, d),
            y_s.reshape(dec_b, 1, d),
            jnp.swapaxes(jnp.stack(gla_p, 0), 3, 4),
            jnp.swapaxes(gla_s, 3, 4),
            window_out(kv_p, 0),
            window_out(kv_p, 1),
            kv_s[:, :HW].reshape(dec_b, 1, N_HEADS, HEAD_DIM),
            kv_s[:, HW:].reshape(dec_b, 1, N_HEADS, HEAD_DIM),
            mem_out(mem_kt_p),
            mem_out(mem_vt_p))
```

```python
import functools

import numpy as np
import jax
import jax.numpy as jnp
from jax import lax
from jax.experimental import pallas as pl
from jax.experimental.pallas import tpu as pltpu

F32 = jnp.float32
BF16 = jnp.bfloat16

D_MODEL = 1024
DEPTH = 4
N_A = 2
GLA_HEADS = 4
GLA_DK = 128
GLA_DV = 192
GLA_DVP = 256
GLA_QK = GLA_HEADS * GLA_DK
GLA_V = GLA_HEADS * GLA_DV
GLA_VP = GLA_HEADS * GLA_DVP
GLA_RANK = 16
GLA_RANKP = 128
GLA_TAU = 16.0
GLA_CHUNK = 64
HEAD_DIM = 64
N_HEADS = 4
HW = N_HEADS * HEAD_DIM
DIL_PAIRS = ((128, 1), (512, 4), (2048, 16))
N_GROUPS = 3
BAND = 128
WIN_MAX = 2048
MEM_TOKENS = 256
D_FF = 2816
FF_TILE = 256
ROPE_THETA = 10000.0
EPS = 1e-6
ATTN_SCALE = HEAD_DIM ** -0.5
NEG = -1e30
LOG2E = 1.4426950408889634
LN2 = 0.6931471805599453

A_COL_Q = 0
A_COL_K = GLA_QK
A_COL_V = 2 * GLA_QK
A_COL_G = A_COL_V + GLA_VP
A_COL_MQ = A_COL_G + GLA_VP
A_COL_GLR = A_COL_MQ + HW
A_IN_P = A_COL_GLR + GLA_RANKP

V7X_VMEM_BYTES = 64 * 1024 * 1024
VMEM_CAP = 56 * 1024 * 1024


def _cparams(sem, vmem_bytes):
    return pltpu.CompilerParams(
        dimension_semantics=sem,
        vmem_limit_bytes=int(min(max(vmem_bytes, 16 * 1024 * 1024), VMEM_CAP)))


def _bdot(a, b):
    return jnp.dot(a.astype(BF16), b.astype(BF16), preferred_element_type=F32)


def _bdot_nt(a, b):
    return lax.dot_general(a.astype(BF16), b.astype(BF16), (((1,), (1,)), ((), ())),
                           preferred_element_type=F32)


def _bdot_tn(a, b):
    return lax.dot_general(a.astype(BF16), b.astype(BF16), (((0,), (0,)), ((), ())),
                           preferred_element_type=F32)


def _split(x):
    hi = x.astype(BF16)
    lo = (x - hi.astype(F32)).astype(BF16)
    return hi, lo


def _dot_exact_rhs(x, m):
    hi, lo = _split(x)
    return (jnp.dot(hi, m, preferred_element_type=F32)
            + jnp.dot(lo, m, preferred_element_type=F32))


def _dot_exact_lhs(m2, x):
    hi, lo = _split(x)
    return jnp.dot(m2, jnp.concatenate([hi, lo], axis=0), preferred_element_type=F32)


def _silu(x):
    return x * jax.nn.sigmoid(x)


def _log_sigmoid(z):
    return jnp.minimum(z, 0.0) - jnp.log1p(jnp.exp(-jnp.abs(z)))


def _gla_tables():
    c = GLA_CHUNK
    i = np.arange(c)[:, None]
    t = np.arange(c)[None, :]
    blocks = [(t <= i), (t > i)]
    masks = []
    for lvl in range(6):
        s = c >> lvl
        half = s // 2
        mid = (i // s) * s + half - 1
        second = (i % s) >= half
        m = np.where(second, (t > mid) & (t <= i), (t > i) & (t <= mid))
        blocks.append(m)
        j = t
        masks.append(((i // s) == (j // s)) & ((i % s) >= half) & ((j % s) < half))
    masks.append(i == t)
    mall = np.concatenate(blocks, axis=0).astype(np.float32)
    mall = np.concatenate([mall, mall], axis=1)
    lmask = np.stack(masks, axis=0).astype(np.float32)
    assert np.array_equal(lmask.sum(0), (t <= i).astype(np.float32))
    return mall, lmask


def _head_tables():
    lane = np.arange(HW)
    group = (lane[:, None] // HEAD_DIM == lane[None, :] // HEAD_DIM).astype(np.float32)
    hmask = (lane[None, :] // HEAD_DIM == np.arange(N_HEADS)[:, None]).astype(np.float32)
    return group, hmask.reshape(N_HEADS, 1, HW)


def _band_bias():
    i = np.arange(BAND)[:, None]
    j = np.arange(2 * BAND)[None, :]
    dist = i + BAND - j
    ok = (dist >= 0) & (dist <= BAND)
    first = ok & (j >= BAND)
    return np.where(np.stack([first, ok], 0), 0.0, NEG).astype(np.float32)


def _rope_tables(pos):
    half = HEAD_DIM // 2
    inv = np.float32(ROPE_THETA) ** (-np.arange(half, dtype=np.float32) / np.float32(half))
    ang = np.asarray(pos, np.float32)[:, None] * inv.astype(np.float32)[None, :]
    cos = np.cos(ang).astype(np.float32)
    sin = np.sin(ang).astype(np.float32)
    cos64 = np.concatenate([cos, cos], axis=-1)
    sin64 = np.concatenate([-sin, sin], axis=-1)
    return jnp.asarray(np.tile(cos64, (1, N_HEADS))), jnp.asarray(np.tile(sin64, (1, N_HEADS)))


def _norm_matmul_kernel(x_ref, g_ref, w_ref, o_ref):
    x = x_ref[...]
    ms = jnp.mean(x * x, axis=-1, keepdims=True)
    y = x * lax.rsqrt(ms + EPS) * g_ref[...]
    o_ref[...] = jnp.dot(y.astype(BF16), w_ref[...], preferred_element_type=F32)


def _norm_matmul(x, g, gi, w, wi, *, tm, tn, name):
    n, d = x.shape
    nout = w.shape[2]
    vmem = 2 * (tm * d * 4 + d * tn * 2 + tm * tn * 4) + 2 * tm * d * 4 + tm * tn * 4
    return pl.pallas_call(
        _norm_matmul_kernel,
        grid=(n // tm, nout // tn),
        in_specs=[pl.BlockSpec((tm, d), lambda i, j: (i, 0)),
                  pl.BlockSpec((None, 1, d), lambda i, j: (gi, 0, 0)),
                  pl.BlockSpec((None, d, tn), lambda i, j: (wi, 0, j))],
        out_specs=pl.BlockSpec((tm, tn), lambda i, j: (i, j)),
        out_shape=jax.ShapeDtypeStruct((n, nout), F32),
        compiler_params=_cparams(("parallel", "parallel"), vmem),
        name=name,
    )(x, g, w)


def _norm_matmul_rope_kernel(x_ref, g_ref, w_ref, hg_ref, grp_ref, cos_ref, sin_ref, *rest,
                             rope_width, cast):
    if cast:
        wgu_i, wdn_i, o_ref, wgu_o, wdn_o = rest
        wgu_o[...] = wgu_i[...].astype(BF16)
        wdn_o[...] = wdn_i[...].astype(BF16)
    else:
        o_ref, = rest
    x = x_ref[...]
    ms = jnp.mean(x * x, axis=-1, keepdims=True)
    xn = x * lax.rsqrt(ms + EPS) * g_ref[...]
    y = jnp.dot(xn.astype(BF16), w_ref[...], preferred_element_type=F32)
    grp = grp_ref[...]
    lane = lax.broadcasted_iota(jnp.int32, (1, HW), 1)
    first_half = (lane & (HEAD_DIM - 1)) < (HEAD_DIM // 2)
    for c in range(rope_width // HW):
        sl = slice(c * HW, (c + 1) * HW)
        yc = y[:, sl]
        ss = _dot_exact_rhs(yc * yc, grp)
        yc = yc * lax.rsqrt(ss * (1.0 / HEAD_DIM) + EPS) * hg_ref[...]
        fwd = pltpu.roll(yc, HW - HEAD_DIM // 2, 1)
        bwd = pltpu.roll(yc, HEAD_DIM // 2, 1)
        rot = jnp.where(first_half, fwd, bwd)
        o_ref[:, sl] = yc * cos_ref[...] + rot * sin_ref[...]
    o_ref[:, rope_width:] = y[:, rope_width:]


def _norm_matmul_rope(x, g, gi, w, wi, head_gain, tabs, *, rope_width, tm, name, pos_blocks=1,
                      cast=None):
    n, d = x.shape
    nout = w.shape[2]
    grp, cos, sin = tabs
    hg = jnp.tile(head_gain.reshape(1, HEAD_DIM), (1, N_HEADS))
    trow = 1 if cos.shape[0] == 1 else tm
    tab_map = (lambda i: (0, 0)) if cos.shape[0] == 1 else (lambda i: (i % pos_blocks, 0))
    vmem = (2 * (tm * d * 4 + tm * nout * 4 + 2 * trow * HW * 4) + d * nout * 2
            + 2 * tm * d * 4 + 3 * tm * nout * 4 + (8 * 1024 * 1024 if cast else 0))
    in_specs = [pl.BlockSpec((tm, d), lambda i: (i, 0)),
                pl.BlockSpec((None, 1, d), lambda i: (gi, 0, 0)),
                pl.BlockSpec((None, d, nout), lambda i: (wi, 0, 0), pipeline_mode=pl.Buffered(1)),
                pl.BlockSpec((1, HW), lambda i: (0, 0)),
                pl.BlockSpec((HW, HW), lambda i: (0, 0)),
                pl.BlockSpec((trow, HW), tab_map),
                pl.BlockSpec((trow, HW), tab_map)]
    args = [x, g, w, hg, grp, cos, sin]
    out_specs = [pl.BlockSpec((tm, nout), lambda i: (i, 0))]
    out_shape = [jax.ShapeDtypeStruct((n, nout), F32)]
    if cast is not None:
        wgu32, wdn32, cli = cast
        for w32 in (wgu32, wdn32):
            i_spec, o_spec, o_shape = _cast_specs(w32, cli, n // tm, lambda i: i)
            in_specs.append(i_spec)
            out_specs.append(o_spec)
            out_shape.append(o_shape)
            args.append(w32)
    out = pl.pallas_call(
        functools.partial(_norm_matmul_rope_kernel, rope_width=rope_width, cast=cast is not None),
        grid=(n // tm,),
        in_specs=in_specs,
        out_specs=out_specs,
        out_shape=out_shape,
        compiler_params=_cparams(("arbitrary" if cast is not None else "parallel",), vmem),
        name=name,
    )(*args)
    return out if cast is not None else out[0]


def _gla_block(q, k, v, g, la, ht, gon, mall_ref, lmask_ref, n_chunks):
    cr = GLA_CHUNK
    chunk = lambda c: slice(c * cr, (c + 1) * cr)
    la_cat = jnp.concatenate([la[chunk(c)] for c in range(n_chunks)], axis=1)
    e_all = jnp.exp2(_dot_exact_lhs(mall_ref[...], la_cat * LOG2E))

    def decay_rows(blk):
        return jnp.concatenate(
            [e_all[blk * cr:(blk + 1) * cr, c * GLA_DK:(c + 1) * GLA_DK] for c in range(n_chunks)],
            axis=0)

    eb = decay_rows(0)
    qb = (q * eb).astype(BF16)
    kdec = (k * decay_rows(1)).astype(BF16)
    q_lv = [q.astype(BF16)]
    k_lv = [k.astype(BF16)]
    for lvl in range(6):
        el = decay_rows(2 + lvl)
        q_lv.append((q * el).astype(BF16))
        k_lv.append((k * el).astype(BF16))
    o_intra, kv_new = [], []
    for c in range(n_chunks):
        r = chunk(c)
        s = lmask_ref[6] * _bdot_nt(q_lv[0][r], k_lv[0][r])
        for lvl in range(6):
            s = s + lmask_ref[lvl] * _bdot_nt(q_lv[1 + lvl][r], k_lv[1 + lvl][r])
        o_intra.append(_bdot(s, v[r]))
        kv_new.append(_bdot_tn(v[r], kdec[r]))
    outs = []
    for c in range(n_chunks):
        outs.append(o_intra[c] + _bdot_nt(qb[chunk(c)], ht))
        ht = ht * eb[c * cr + cr - 1:(c + 1) * cr, :] + kv_new[c]
    o = jnp.concatenate(outs, axis=0)
    ms = jnp.sum(o * o, axis=-1, keepdims=True) * (1.0 / GLA_DV)
    return o * lax.rsqrt(ms + EPS) * gon * _silu(g), ht


def _cast_specs(w32, li, n_steps, step_of):
    rows, cols = w32.shape[1:]
    chunk, rep = rows // n_steps, 1
    assert chunk * n_steps == rows
    while (chunk * rep) % 16:
        rep *= 2
    chunk *= rep
    return (pl.BlockSpec((None, chunk, cols), lambda *ids: (li, step_of(*ids) // rep, 0)),
            pl.BlockSpec((chunk, cols), lambda *ids: (step_of(*ids) // rep, 0)),
            jax.ShapeDtypeStruct((rows, cols), BF16))


def _gla_kernel(x_ref, g1_ref, w_ref, wg2_ref, bg_ref, gon_ref, mall_ref, lmask_ref, h0_ref,
                wgu_i, wdn_i, tok_ref, mq_ref, hout_ref, wgu_o, wdn_o, proj_s, ht_s, *, n_chunks):
    t = pl.program_id(1)
    wgu_o[...] = wgu_i[...].astype(BF16)
    wdn_o[...] = wdn_i[...].astype(BF16)

    @pl.when(t == 0)
    def _():
        ht_s[...] = h0_ref[0]

    x = x_ref[...]
    ms = jnp.mean(x * x, axis=-1, keepdims=True)
    xn = (x * lax.rsqrt(ms + EPS) * g1_ref[...]).astype(BF16)
    proj_s[...] = jnp.dot(xn, w_ref[...], preferred_element_type=F32)
    mq_ref[...] = proj_s[:, A_COL_MQ:A_COL_MQ + HW]
    z = _bdot(proj_s[:, A_COL_GLR:A_COL_GLR + GLA_RANKP], wg2_ref[...]) + bg_ref[...]
    la = _log_sigmoid(z) * (1.0 / GLA_TAU)
    gon = gon_ref[...]
    for h in range(GLA_HEADS):
        kc = slice(h * GLA_DK, (h + 1) * GLA_DK)
        vc = slice(h * GLA_DVP, (h + 1) * GLA_DVP)
        q = proj_s[:, A_COL_Q + h * GLA_DK:A_COL_Q + (h + 1) * GLA_DK] * (GLA_DK ** -0.5)
        k = proj_s[:, A_COL_K + h * GLA_DK:A_COL_K + (h + 1) * GLA_DK]
        v = proj_s[:, A_COL_V + h * GLA_DVP:A_COL_V + (h + 1) * GLA_DVP].astype(BF16)
        g = proj_s[:, A_COL_G + h * GLA_DVP:A_COL_G + (h + 1) * GLA_DVP]
        tok, ht = _gla_block(q, k, v, g, la[:, kc], ht_s[h], gon, mall_ref, lmask_ref, n_chunks)
        tok_ref[:, vc] = tok.astype(tok_ref.dtype)
        ht_s[h] = ht

    @pl.when(t == pl.num_programs(1) - 1)
    def _():
        hout_ref[0] = ht_s[...]


def _gla_prompt(x, g1, w_in, li, h0t, wg2, bg, gon, mall, lmask, wgu32, wdn32, *, batch, seq, tb):
    n, d = x.shape
    nt = seq // tb
    const2 = lambda shape: pl.BlockSpec(shape, lambda b, t: (0,) * len(shape))
    st_spec = pl.BlockSpec((1, GLA_HEADS, GLA_DVP, GLA_DK), lambda b, t: (b, 0, 0, 0))
    step_of = lambda b, t: b * nt + t
    gu_in, gu_out, gu_shape = _cast_specs(wgu32, li, batch * nt, step_of)
    dn_in, dn_out, dn_shape = _cast_specs(wdn32, li, batch * nt, step_of)
    vmem = (2 * tb * d * 4 + d * A_IN_P * 2 + tb * A_IN_P * 4 + 2 * tb * (GLA_VP * 2 + HW * 4)
            + 6 * GLA_HEADS * GLA_DVP * GLA_DK * 4 + 24 * tb * GLA_DVP * 4 + 8 * 1024 * 1024)
    return pl.pallas_call(
        functools.partial(_gla_kernel, n_chunks=tb // GLA_CHUNK),
        grid=(batch, nt),
        in_specs=[
            pl.BlockSpec((tb, d), lambda b, t: (b * nt + t, 0)),
            pl.BlockSpec((None, 1, d), lambda b, t: (li, 0, 0)),
            pl.BlockSpec((None, d, A_IN_P), lambda b, t: (li, 0, 0), pipeline_mode=pl.Buffered(1)),
            const2((GLA_RANKP, GLA_QK)), const2((1, GLA_QK)), const2((1, GLA_DVP)),
            const2(mall.shape), const2(lmask.shape), st_spec, gu_in, dn_in,
        ],
        out_specs=[
            pl.BlockSpec((tb, GLA_VP), lambda b, t: (b * nt + t, 0)),
            pl.BlockSpec((tb, HW), lambda b, t: (b * nt + t, 0)),
            st_spec, gu_out, dn_out,
        ],
        out_shape=[jax.ShapeDtypeStruct((n, GLA_VP), BF16),
                   jax.ShapeDtypeStruct((n, HW), F32),
                   jax.ShapeDtypeStruct((batch, GLA_HEADS, GLA_DVP, GLA_DK), F32),
                   gu_shape, dn_shape],
        scratch_shapes=[pltpu.VMEM((tb, A_IN_P), F32),
                        pltpu.VMEM((GLA_HEADS, GLA_DVP, GLA_DK), F32)],
        compiler_params=_cparams(("parallel", "arbitrary"), vmem),
        name="gla_prompt",
    )(x, g1, w_in, wg2, bg, gon, mall, lmask, h0t, wgu32, wdn32)


STEP_ROWS = 8


def _gla_step_kernel(q_ref, k_ref, v_ref, g_ref, glr_ref, wg2_ref, bg_ref, gon_ref, st_ref, *rest,
                     out_layer):
    tok_ref, so_ref = rest[-2:]
    nb = q_ref.shape[0]
    for other in range(so_ref.shape[0]):
        if other != out_layer:
            so_ref[other] = jnp.zeros(so_ref.shape[1:], F32)
    z = _bdot(glr_ref[...], wg2_ref[...]) + bg_ref[...]
    a = jnp.exp(_log_sigmoid(z) * (1.0 / GLA_TAU))
    q = q_ref[...] * (GLA_DK ** -0.5)
    k = k_ref[...]
    v = v_ref[...]
    g = g_ref[...]
    gon = gon_ref[...][:, :GLA_DV]
    row = lax.broadcasted_iota(jnp.int32, (nb, 1), 0)
    for h in range(GLA_HEADS):
        ks = slice(h * GLA_DK, (h + 1) * GLA_DK)
        vs = slice(h * GLA_DVP, h * GLA_DVP + GLA_DV)
        o_rows = jnp.zeros((nb, GLA_DV), F32)
        for b in range(nb):
            only_b = row == b
            outer = _bdot_tn(jnp.where(only_b, v[:, vs], 0.0), k[:, ks])
            s_new = st_ref[0, b, h] * a[b:b + 1, ks] + outer
            so_ref[out_layer, b, h] = s_new
            o_rows = o_rows + jnp.where(only_b, _bdot_nt(q[:, ks], s_new), 0.0)
        ms = jnp.sum(o_rows * o_rows, axis=-1, keepdims=True) * (1.0 / GLA_DV)
        tok_ref[:, vs] = o_rows * lax.rsqrt(ms + EPS) * gon * _silu(g[:, vs])
        tok_ref[:, h * GLA_DVP + GLA_DV:(h + 1) * GLA_DVP] = jnp.zeros(
            (nb, GLA_DVP - GLA_DV), F32)


def _gla_step(proj, state_t, li, prev_out, wg2, bg, gon):
    bsz = proj.shape[0]
    nb = STEP_ROWS
    col = lambda w, c: pl.BlockSpec((nb, w), lambda i: (i, c))
    st_spec = pl.BlockSpec((1, nb, GLA_HEADS, GLA_DV, GLA_DK), lambda i: (li, i, 0, 0, 0))
    in_specs = [col(GLA_QK, A_COL_Q // GLA_QK), col(GLA_QK, A_COL_K // GLA_QK),
                col(GLA_VP, A_COL_V // GLA_VP), col(GLA_VP, A_COL_G // GLA_VP),
                col(GLA_RANKP, A_COL_GLR // GLA_RANKP),
                pl.BlockSpec((GLA_RANKP, GLA_QK), lambda i: (0, 0)),
                pl.BlockSpec((1, GLA_QK), lambda i: (0, 0)),
                pl.BlockSpec((1, GLA_DVP), lambda i: (0, 0)),
                st_spec]
    args = [proj, proj, proj, proj, proj, wg2, bg, gon, state_t]
    if prev_out is None:
        aliases, out_layer = {}, li
        so_spec = pl.BlockSpec((state_t.shape[0], nb, GLA_HEADS, GLA_DV, GLA_DK),
                               lambda i: (0, i, 0, 0, 0))
    else:
        in_specs.append(pl.BlockSpec(memory_space=pl.ANY))
        args.append(prev_out)
        aliases, out_layer, so_spec = {len(args) - 1: 1}, 0, st_spec
    return pl.pallas_call(
        functools.partial(_gla_step_kernel, out_layer=out_layer),
        grid=(bsz // nb,),
        in_specs=in_specs,
        out_specs=[pl.BlockSpec((nb, GLA_VP), lambda i: (i, 0)), so_spec],
        out_shape=[jax.ShapeDtypeStruct((bsz, GLA_VP), F32),
                   jax.ShapeDtypeStruct(state_t.shape, F32)],
        input_output_aliases=aliases,
        compiler_params=_cparams(("parallel",), 32 * 1024 * 1024),
        name="gla_step",
    )(*args)


def _mem_qnorm(q, gq_ref, grp_ref):
    ss = _dot_exact_rhs(q * q, grp_ref[...])
    return q * lax.rsqrt(ss * (1.0 / HEAD_DIM) + EPS) * gq_ref[...] * ATTN_SCALE


def _mem_attn_kernel(q_ref, kt_ref, vt_ref, gq_ref, grp_ref, hmask_ref, o_ref):
    qn = _mem_qnorm(q_ref[0], gq_ref, grp_ref)
    tq = qn.shape[0]
    kt = kt_ref[0, 0].astype(BF16)
    vt = vt_ref[0, 0].astype(BF16)
    q4 = jnp.concatenate([qn * hmask_ref[h] for h in range(N_HEADS)], axis=0)
    s = _bdot(q4, kt)
    m = jnp.max(s, axis=-1, keepdims=True)
    p = jnp.exp(s - m)
    o4 = _bdot_nt(p, vt) / jnp.sum(p, axis=-1, keepdims=True)
    out = jnp.zeros(qn.shape, F32)
    for h in range(N_HEADS):
        out = out + hmask_ref[h] * o4[h * tq:(h + 1) * tq]
    o_ref[0] = out.astype(o_ref.dtype)


def _mem_attn(q3, qcol, kt, vt, li, gq, grp, hmask, *, tq):
    bsz, t, _ = q3.shape
    g = jnp.tile(gq.reshape(1, HEAD_DIM), (1, N_HEADS))
    vmem = 2 * (tq * HW * 4 + 2 * MEM_TOKENS * HW * 4 + tq * HW * 2) + 16 * tq * HW * 4
    kv_spec = pl.BlockSpec((1, 1, HW, MEM_TOKENS), lambda b, i: (li, b, 0, 0))
    return pl.pallas_call(
        _mem_attn_kernel,
        grid=(bsz, t // tq),
        in_specs=[pl.BlockSpec((1, tq, HW), lambda b, i: (b, i, qcol)),
                  kv_spec, kv_spec,
                  pl.BlockSpec((1, HW), lambda b, i: (0, 0)),
                  pl.BlockSpec((HW, HW), lambda b, i: (0, 0)),
                  pl.BlockSpec((N_HEADS, 1, HW), lambda b, i: (0, 0, 0))],
        out_specs=pl.BlockSpec((1, tq, HW), lambda b, i: (b, i, 0)),
        out_shape=jax.ShapeDtypeStruct((bsz, t, HW), BF16),
        compiler_params=_cparams(("parallel", "parallel"), vmem),
        name="mem_attn",
    )(q3, kt, vt, g, grp, hmask)


def _mem_step_kernel(q_ref, kt_ref, vt_ref, gq_ref, grp_ref, hrow_ref, o_ref):
    nb = q_ref.shape[0]
    qn = _mem_qnorm(q_ref[...], gq_ref, grp_ref)
    hrow = hrow_ref[...]
    row = lax.broadcasted_iota(jnp.int32, (nb, 1), 0)
    out = jnp.zeros((nb, HW), F32)
    for b in range(nb):
        q4 = hrow * qn[b:b + 1]
        s = _bdot(q4, kt_ref[0, b])
        m = jnp.max(s, axis=-1, keepdims=True)
        p = jnp.exp(s - m)
        l = jnp.sum(p, axis=-1, keepdims=True)
        o4 = _bdot_nt(p, vt_ref[0, b]) / l
        o_b = jnp.sum(hrow * o4, axis=0, keepdims=True)
        out = out + jnp.where(row == b, o_b, 0.0)
    o_ref[...] = out


def _mem_step(q, qcol, kt, vt, li, gq, grp, hrow):
    bsz = q.shape[0]
    nb = STEP_ROWS
    g = jnp.tile(gq.reshape(1, HEAD_DIM), (1, N_HEADS))
    kv_spec = pl.BlockSpec((1, nb, HW, MEM_TOKENS), lambda i: (li, i, 0, 0))
    return pl.pallas_call(
        _mem_step_kernel,
        grid=(bsz // nb,),
        in_specs=[pl.BlockSpec((nb, HW), lambda i: (i, qcol)),
                  kv_spec, kv_spec,
                  pl.BlockSpec((1, HW), lambda i: (0, 0)),
                  pl.BlockSpec((HW, HW), lambda i: (0, 0)),
                  pl.BlockSpec((8, HW), lambda i: (0, 0))],
        out_specs=pl.BlockSpec((nb, HW), lambda i: (i, 0)),
        out_shape=jax.ShapeDtypeStruct((bsz, HW), F32),
        compiler_params=_cparams(("parallel",), 24 * 1024 * 1024),
        name="mem_step",
    )(q, kt, vt, g, grp, hrow)


def _mem_kv_kernel(x_ref, gn_ref, w_ref, gk_ref, grp_ref, kt_ref, vt_ref):
    x = x_ref[0]
    ms = jnp.mean(x * x, axis=-1, keepdims=True)
    y = x * lax.rsqrt(ms + EPS) * gn_ref[0]
    kv = jnp.dot(y.astype(BF16), w_ref[0], preferred_element_type=F32)
    k = kv[:, :HW]
    ss = _dot_exact_rhs(k * k, grp_ref[...])
    k = k * lax.rsqrt(ss * (1.0 / HEAD_DIM) + EPS) * gk_ref[0]
    kt_ref[0, 0] = k.T
    vt_ref[0, 0] = kv[:, HW:].T


def _mem_kv(mem, gn, w, gk, grp):
    bsz, m, d = mem.shape
    nl = w.shape[0]
    out = jax.ShapeDtypeStruct((nl, bsz, HW, m), F32)
    o_spec = pl.BlockSpec((1, 1, HW, m), lambda l, b: (l, b, 0, 0))
    return pl.pallas_call(
        _mem_kv_kernel,
        grid=(nl, bsz),
        in_specs=[pl.BlockSpec((1, m, d), lambda l, b: (b, 0, 0)),
                  pl.BlockSpec((1, 1, d), lambda l, b: (l, 0, 0)),
                  pl.BlockSpec((1, d, 2 * HW), lambda l, b: (l, 0, 0)),
                  pl.BlockSpec((1, 1, HW), lambda l, b: (l, 0, 0)),
                  pl.BlockSpec((HW, HW), lambda l, b: (0, 0))],
        out_specs=[o_spec, o_spec],
        out_shape=[out, out],
        compiler_params=_cparams(("parallel", "parallel"), 24 * 1024 * 1024),
        name="mem_kv",
    )(mem, gn.reshape(nl, 1, d), w, jnp.tile(gk, (1, N_HEADS)).reshape(nl, 1, HW), grp)


def _band_attn_kernel(q0_ref, q1_ref, k0_ref, k1_ref, v0_ref, v1_ref, bias_ref, hmask_ref,
                      o_ref, og_s, ls_s, *, seq):
    g = pl.program_id(1)
    n_units = seq // BAND

    def run_group(gi, dil):
        nb = seq // dil // BAND
        shift = nb.bit_length() - 1
        lane = lax.broadcasted_iota(jnp.int32, (1, 128), 1)
        low_lanes = lane < HEAD_DIM

        def idx(start):
            if dil == 1:
                return pl.ds(pl.multiple_of(start, BAND), BAND)
            return pl.ds(start, BAND, stride=dil)

        def rows(ref0, ref1, start):
            return jnp.concatenate([ref0[idx(start), :], ref1[idx(start), :]], axis=1)

        def unit(u, carry):
            r = lax.shift_right_logical(u, shift)
            nblk = u & (nb - 1)
            qs = r + dil * BAND * nblk
            ks = r + dil * BAND * jnp.maximum(nblk - 1, 0)
            q = rows(q0_ref, q1_ref, qs) * (ATTN_SCALE * LOG2E)
            kk = jnp.concatenate([rows(k0_ref, k1_ref, ks), rows(k0_ref, k1_ref, qs)],
                                 axis=0).astype(BF16)
            vv = jnp.concatenate([rows(v0_ref, v1_ref, ks), rows(v0_ref, v1_ref, qs)],
                                 axis=0).astype(BF16)
            q4 = jnp.concatenate([q * hmask_ref[h] for h in range(N_HEADS)], axis=0)
            bias = bias_ref[jnp.minimum(nblk, 1)]
            s = _bdot_nt(q4, kk).reshape(N_HEADS, BAND, 2 * BAND) + bias[None]
            s = s.reshape(N_HEADS * BAND, 2 * BAND)
            m = jnp.max(s, axis=-1, keepdims=True)
            p = jnp.exp2(s - m)
            l = jnp.sum(p, axis=-1, keepdims=True)
            oe = _bdot(p, vv)
            hd = [slice(h * BAND, (h + 1) * BAND) for h in range(N_HEADS)]
            for half in range(2):
                ha, hb = hd[2 * half], hd[2 * half + 1]
                cols = slice(half * 128, (half + 1) * 128)
                o_h = jnp.where(low_lanes, oe[ha, cols], oe[hb, cols])
                l_h = jnp.where(low_lanes, l[ha], l[hb])
                m_h = jnp.where(low_lanes, m[ha], m[hb])
                og_s[2 * gi + half, idx(qs), :] = o_h / l_h
                ls_s[2 * gi + half, idx(qs), :] = (m_h + jnp.log2(l_h)) * LN2
            return carry

        lax.fori_loop(0, n_units, unit, 0, unroll=4)

    for gi, (_, dil) in enumerate(DIL_PAIRS):
        pl.when(g == gi)(functools.partial(run_group, gi, dil))

    @pl.when(g == N_GROUPS - 1)
    def _merge():
        tr = 512
        for c in range(seq // tr):
            rows = slice(c * tr, (c + 1) * tr)
            for half in range(2):
                ls = [ls_s[2 * gi + half, rows, :] for gi in range(N_GROUPS)]
                m = jnp.maximum(jnp.maximum(ls[0], ls[1]), ls[2])
                es = [jnp.exp(x - m) for x in ls]
                inv = 1.0 / (es[0] + es[1] + es[2])
                acc = jnp.zeros((tr, 128), F32)
                for gi in range(N_GROUPS):
                    acc = acc + (es[gi] * inv) * og_s[2 * gi + half, rows, :]
                o_ref[rows, half * 128:(half + 1) * 128] = acc.astype(o_ref.dtype)


def _band_attn(qd, kv, band, hmask, *, batch, seq):
    n = qd.shape[0]
    once = pl.Buffered(1)
    half = lambda cmap: pl.BlockSpec((seq, 128), cmap, pipeline_mode=once)
    vmem = (6 * seq * 128 * 4 + 2 * seq * HW * 2 + 12 * seq * 128 * 4 + 8 * 1024 * 1024)
    return pl.pallas_call(
        functools.partial(_band_attn_kernel, seq=seq),
        grid=(batch, N_GROUPS),
        in_specs=[half(lambda b, g: (b, 2 * g)), half(lambda b, g: (b, 2 * g + 1)),
                  half(lambda b, g: (b, 0)), half(lambda b, g: (b, 1)),
                  half(lambda b, g: (b, 2)), half(lambda b, g: (b, 3)),
                  pl.BlockSpec((2, BAND, 2 * BAND), lambda b, g: (0, 0, 0)),
                  pl.BlockSpec((N_HEADS, 1, HW), lambda b, g: (0, 0, 0))],
        out_specs=pl.BlockSpec((seq, HW), lambda b, g: (b, 0)),
        out_shape=jax.ShapeDtypeStruct((n, HW), BF16),
        scratch_shapes=[pltpu.VMEM((2 * N_GROUPS, seq, 128), F32),
                        pltpu.VMEM((2 * N_GROUPS, seq, 128), F32)],
        compiler_params=_cparams(("parallel", "arbitrary"), vmem),
        name="band_attn",
    )(qd, qd, kv, kv, kv, kv, band, hmask)


def _step_bias(cache_len):
    t = np.arange(cache_len)
    rows = np.zeros((16, cache_len), np.float32)
    for gi, (_, dil) in enumerate(DIL_PAIRS):
        keep = (t >= cache_len - dil * BAND) & ((cache_len - t) % dil == 0)
        rows[4 * gi:4 * gi + 4] = np.where(keep, 0.0, NEG)[None]
    return rows


def _step_attn_kernel(q_ref, kn_ref, vn_ref, kt_ref, vt_ref, bias_ref, hrow_ref, o_ref):
    j = pl.program_id(1)
    q = q_ref[pl.ds(j, 1), :] * ATTN_SCALE
    kn = kn_ref[pl.ds(j, 1), :]
    vn = vn_ref[pl.ds(j, 1), :]
    hrow = hrow_ref[...]
    row = lax.broadcasted_iota(jnp.int32, (16, 1), 0)
    q16 = hrow * jnp.where(row < 4, q[:, :HW], jnp.where(row < 8, q[:, HW:2 * HW], q[:, 2 * HW:]))
    s = _bdot(q16, kt_ref[0]) + bias_ref[...]
    s_n = jnp.sum(q16 * kn, axis=-1, keepdims=True)
    m = jnp.maximum(jnp.max(s, axis=-1, keepdims=True), s_n)
    p = jnp.exp(s - m)
    p_n = jnp.exp(s_n - m)
    l = jnp.sum(p, axis=-1, keepdims=True) + p_n
    o16 = (_bdot_nt(p, vt_ref[0]) + p_n * vn) / l
    lse = m + jnp.log(l)
    lg = [lse[4 * g:4 * g + 4] for g in range(N_GROUPS)]
    mm = jnp.maximum(jnp.maximum(lg[0], lg[1]), lg[2])
    es = [jnp.exp(x - mm) for x in lg]
    inv = 1.0 / (es[0] + es[1] + es[2])
    acc = jnp.zeros((4, HW), F32)
    for g in range(N_GROUPS):
        acc = acc + (es[g] * inv) * (hrow[0:4] * o16[4 * g:4 * g + 4])
    o_ref[pl.ds(j, 1), :] = jnp.sum(acc, axis=0, keepdims=True)


def _step_attn(q, kn, kvn, kt, vt, bias, hrow16):
    bsz, _, cache_len = kt.shape
    assert cache_len >= max(d for _, d in DIL_PAIRS) * BAND
    nb = STEP_ROWS
    rows = lambda w: pl.BlockSpec((nb, w), lambda i, j: (i, 0))
    cache = pl.BlockSpec((1, HW, cache_len), lambda i, j: (i * nb + j, 0, 0))
    vmem = 4 * HW * cache_len * 4 + 8 * 16 * cache_len * 4 + 4 * 1024 * 1024
    return pl.pallas_call(
        _step_attn_kernel,
        grid=(bsz // nb, nb),
        in_specs=[rows(N_GROUPS * HW), rows(HW),
                  pl.BlockSpec((nb, HW), lambda i, j: (i, 1)), cache, cache,
                  pl.BlockSpec((16, cache_len), lambda i, j: (0, 0)),
                  pl.BlockSpec((16, HW), lambda i, j: (0, 0))],
        out_specs=rows(HW),
        out_shape=jax.ShapeDtypeStruct((bsz, HW), F32),
        compiler_params=_cparams(("parallel", "arbitrary"), vmem),
        name="step_attn",
    )(q, kn, kvn, kt, vt, bias, hrow16)


def _out_ffn_kernel(h_ref, tok_ref, mem_ref, wo1_ref, wo2_ref, g2_ref, wgu_ref, wdn_ref, o_ref):
    h1 = (h_ref[...]
          + jnp.dot(tok_ref[...].astype(BF16), wo1_ref[...], preferred_element_type=F32)
          + jnp.dot(mem_ref[...].astype(BF16), wo2_ref[...], preferred_element_type=F32))
    ms = jnp.mean(h1 * h1, axis=-1, keepdims=True)
    hn = (h1 * lax.rsqrt(ms + EPS) * g2_ref[...]).astype(BF16)
    gate = jnp.dot(hn, wgu_ref[:, :D_FF], preferred_element_type=F32)
    up = jnp.dot(hn, wgu_ref[:, D_FF:], preferred_element_type=F32)
    act = (_silu(gate) * up).astype(BF16)
    o_ref[...] = h1 + jnp.dot(act, wdn_ref[...], preferred_element_type=F32)


def _out_ffn(h, tok, mem, wo, woi, g2, li, wgu, wdn, *, tm):
    n, d = h.shape
    wt = tok.shape[1]
    assert wo.shape[1] == wt + HW and wt % HW == 0
    once = pl.Buffered(1)
    vmem = ((wt + HW) * d * 2 + 3 * D_FF * d * 2
            + 2 * tm * (2 * d * 4 + (wt + HW) * 4) + 3 * tm * d * 4 + 3 * tm * D_FF * 4
            + 4 * 1024 * 1024)
    in_specs = [pl.BlockSpec((tm, d), lambda i: (i, 0)),
                pl.BlockSpec((tm, wt), lambda i: (i, 0)),
                pl.BlockSpec((tm, HW), lambda i: (i, 0)),
                pl.BlockSpec((None, wt, d), lambda i: (woi, 0, 0), pipeline_mode=once),
                pl.BlockSpec((None, HW, d), lambda i: (woi, wt // HW, 0), pipeline_mode=once),
                pl.BlockSpec((None, 1, d), lambda i: (li, 0, 0), pipeline_mode=once),
                pl.BlockSpec((d, 2 * D_FF), lambda i: (0, 0), pipeline_mode=once),
                pl.BlockSpec((D_FF, d), lambda i: (0, 0), pipeline_mode=once)]
    args = [h, tok, mem, wo, wo, g2, wgu, wdn]
    return pl.pallas_call(
        _out_ffn_kernel,
        grid=(n // tm,),
        in_specs=in_specs,
        out_specs=pl.BlockSpec((tm, d), lambda i: (i, 0)),
        out_shape=jax.ShapeDtypeStruct((n, d), F32),
        compiler_params=_cparams(("parallel",), vmem),
        name="out_ffn",
    )(*args)


def _tail_transpose_kernel(x_ref, o_ref):
    o_ref[0] = x_ref[...].T


def _tail_transpose(x, col, *, batch, seq, keep, tr):
    per, nk = seq // tr, keep // tr
    return pl.pallas_call(
        _tail_transpose_kernel,
        grid=(batch, nk),
        in_specs=[pl.BlockSpec((tr, HW), lambda b, j: (b * per + per - nk + j, col))],
        out_specs=pl.BlockSpec((1, HW, tr), lambda b, j: (b, 0, j)),
        out_shape=jax.ShapeDtypeStruct((batch, HW, keep), F32),
        compiler_params=_cparams(("parallel", "parallel"), 16 * 1024 * 1024),
        name="tail_transpose",
    )(x)


def _pad_heads_cols(w):
    d = w.shape[0]
    return jnp.pad(w.reshape(d, GLA_HEADS, GLA_DV),
                   ((0, 0), (0, 0), (0, GLA_DVP - GLA_DV))).reshape(d, GLA_VP)


def _pack_a_in(w):
    q = w[:, :GLA_QK]
    k = w[:, GLA_QK:2 * GLA_QK]
    v = w[:, 2 * GLA_QK:2 * GLA_QK + GLA_V]
    g = w[:, 2 * GLA_QK + GLA_V:2 * GLA_QK + 2 * GLA_V]
    glr = w[:, 2 * GLA_QK + 2 * GLA_V:2 * GLA_QK + 2 * GLA_V + GLA_RANK]
    mq = w[:, 2 * GLA_QK + 2 * GLA_V + GLA_RANK:]
    glr = jnp.pad(glr, ((0, 0), (0, GLA_RANKP - GLA_RANK)))
    return jnp.concatenate([q, k, _pad_heads_cols(v), _pad_heads_cols(g), mq, glr],
                           axis=1).astype(BF16)


def _pack_a_out(w):
    tok = w[:GLA_V].reshape(GLA_HEADS, GLA_DV, D_MODEL)
    tok = jnp.pad(tok, ((0, 0), (0, GLA_DVP - GLA_DV), (0, 0))).reshape(GLA_VP, D_MODEL)
    return jnp.concatenate([tok, w[GLA_V:]], axis=0).astype(BF16)


def kernel(x_prompt, x_sample, state_gla, cache_win_k, cache_win_v, cache_mem_k, cache_mem_v,
           mem_prompt, norm1, norm2, a_w_in, a_w_gate2, a_b_gate, a_g_onorm, a_w_out, kv_norm, w_kv,
           g_k, b_w_in, b_g_q, b_w_out, mem_norm, w_mem_kv, g_mem_q, g_mem_k, w_ffn_gu, w_ffn_down):
    batch, seq, d = x_prompt.shape
    dec_b = x_sample.shape[0]
    past_len = 8192
    assert d == D_MODEL and seq % 512 == 0 and x_sample.shape[1] == 1

    mall_np, lmask_np = _gla_tables()
    grp_np, hmask_np = _head_tables()
    mall = jnp.asarray(mall_np, BF16)
    lmask = jnp.asarray(lmask_np, F32)
    grp = jnp.asarray(grp_np, BF16)
    hmask = jnp.asarray(hmask_np, F32)
    band = jnp.asarray(_band_bias(), F32)

    hrow8 = jnp.asarray(np.concatenate([hmask_np[:, 0], np.zeros((4, HW), np.float32)], 0))
    hrow16 = jnp.asarray(np.concatenate([hmask_np[:, 0]] * N_GROUPS
                                        + [np.zeros((4, HW), np.float32)], 0))

    cos_p, sin_p = _rope_tables(np.arange(seq))
    cos_s, sin_s = _rope_tables(past_len + np.arange(1))

    wa_in = jnp.stack([_pack_a_in(a_w_in[i]) for i in range(N_A)], 0)
    wa_out = jnp.stack([_pack_a_out(a_w_out[i]) for i in range(N_A)], 0)
    wg2 = [jnp.pad(a_w_gate2[i], ((0, GLA_RANKP - GLA_RANK), (0, 0))).astype(BF16)
           for i in range(N_A)]
    bg = [a_b_gate[i].reshape(1, GLA_QK) for i in range(N_A)]
    gon = [jnp.pad(a_g_onorm[i], (0, GLA_DVP - GLA_DV)).reshape(1, GLA_DVP) for i in range(N_A)]
    wb_in = b_w_in.astype(BF16)
    wb_out = b_w_out.astype(BF16)
    w_kv_b = w_kv.astype(BF16).reshape(1, d, 2 * HW)
    w_mem_b = w_mem_kv.astype(BF16)
    ffn_w = {}
    n1 = norm1.reshape(DEPTH, 1, d)
    n2 = norm2.reshape(DEPTH, 1, d)
    nkv = kv_norm.reshape(1, 1, d)

    mem_kt_p, mem_vt_p = _mem_kv(mem_prompt, mem_norm, w_mem_b, g_mem_k, grp)

    cache_len = cache_win_k.shape[1]
    win_kt = jnp.transpose(cache_win_k, (0, 2, 3, 1)).reshape(dec_b, HW, cache_len)
    win_vt = jnp.transpose(cache_win_v, (0, 2, 3, 1)).reshape(dec_b, HW, cache_len)
    mem_kt_s = jnp.transpose(cache_mem_k, (0, 1, 3, 4, 2)).reshape(DEPTH, dec_b, HW, MEM_TOKENS)
    mem_vt_s = jnp.transpose(cache_mem_v, (0, 1, 3, 4, 2)).reshape(DEPTH, dec_b, HW, MEM_TOKENS)
    state_t = jnp.swapaxes(state_gla, 3, 4)
    step_bias = jnp.asarray(_step_bias(cache_len), F32)

    def trunk(x2, bsz, t, prompt):
        n = bsz * t
        tm = 512 if prompt else n
        h = x2
        states = []
        state_out = None
        kv_sh = None
        tabs = (grp, cos_p, sin_p) if prompt else (grp, cos_s, sin_s)
        pos_blocks = max(t // tm, 1)
        for li in range(DEPTH):
            if li < N_A:
                if prompt:
                    h0t = jnp.zeros((bsz, GLA_HEADS, GLA_DVP, GLA_DK), F32)
                    tok, proj, st, wgu_b, wdn_b = _gla_prompt(
                        h, n1, wa_in, li, h0t, wg2[li], bg[li], gon[li], mall, lmask,
                        w_ffn_gu, w_ffn_down, batch=bsz, seq=t, tb=512)
                    ffn_w[li] = (wgu_b, wdn_b)
                    states.append(st[:, :, :GLA_DV])
                    mq_col = 0
                else:
                    proj = _norm_matmul(h, n1, li, wa_in, li, tm=tm, tn=A_IN_P // 3, name="a_in")
                    tok, state_out = _gla_step(proj, state_t, li, state_out, wg2[li], bg[li],
                                               gon[li])
                    mq_col = A_COL_MQ // HW
                wo, woi = wa_out, li
            else:
                bi = li - N_A
                proj = _norm_matmul_rope(h, n1, li, wb_in, bi, b_g_q[bi], tabs,
                                         rope_width=N_GROUPS * HW, tm=tm, name="b_in",
                                         pos_blocks=pos_blocks,
                                         cast=(w_ffn_gu, w_ffn_down, li) if prompt else None)
                if prompt:
                    proj, wgu_b, wdn_b = proj
                    ffn_w[li] = (wgu_b, wdn_b)
                mq_col = (N_GROUPS * HW) // HW
                if prompt:
                    tok = _band_attn(proj, kv_sh, band, hmask, batch=bsz, seq=t)
                else:
                    tok = _step_attn(proj, kv_sh, kv_sh, win_kt, win_vt, step_bias, hrow16)
                wo, woi = wb_out, bi
            if prompt:
                mem_o = _mem_attn(proj.reshape(bsz, t, proj.shape[1]), mq_col, mem_kt_p, mem_vt_p,
                                  li, g_mem_q[li], grp, hmask, tq=512).reshape(n, HW)
            else:
                mem_o = _mem_step(proj, mq_col, mem_kt_s, mem_vt_s, li, g_mem_q[li], grp, hrow8)
            h = _out_ffn(h, tok, mem_o, wo, woi, n2, li, *ffn_w[li], tm=tm)
            if li == N_A - 1:
                kv_sh = _norm_matmul_rope(h, nkv, 0, w_kv_b, 0, g_k, tabs, rope_width=HW, tm=tm,
                                          name="shared_kv", pos_blocks=pos_blocks)
        return h, states, state_out, kv_sh

    y_p, gla_p, _, kv_p = trunk(x_prompt.reshape(batch * seq, d), batch, seq, True)
    y_s, _, gla_s, kv_s = trunk(x_sample.reshape(dec_b, d), dec_b, 1, False)

    keep = min(WIN_MAX, seq)

    def window_out(x, col):
        xt = _tail_transpose(x, col, batch=batch, seq=seq, keep=keep, tr=512)
        return jnp.transpose(xt.reshape(batch, N_HEADS, HEAD_DIM, keep), (0, 3, 1, 2))

    def mem_out(xt):
        return jnp.transpose(xt.reshape(DEPTH, batch, N_HEADS, HEAD_DIM, MEM_TOKENS),
                             (0, 1, 4, 2, 3))

    return (y_p.reshape(batch, seq, d),
            y_s.reshape(dec_b, 1, d),
            jnp.swapaxes(jnp.stack(gla_p, 0), 3, 4),
            jnp.swapaxes(gla_s, 3, 4),
            window_out(kv_p, 0),
            window_out(kv_p, 1),
            kv_s[:, :HW].reshape(dec_b, 1, N_HEADS, HEAD_DIM),
            kv_s[:, HW:].reshape(dec_b, 1, N_HEADS, HEAD_DIM),
            mem_out(mem_kt_p),
            mem_out(mem_vt_p))
```

```python
import functools

import numpy as np
import jax
import jax.numpy as jnp
from jax import lax
from jax.experimental import pallas as pl
from jax.experimental.pallas import tpu as pltpu

F32 = jnp.float32
BF16 = jnp.bfloat16

D_MODEL = 1024
DEPTH = 4
N_A = 2
GLA_HEADS = 4
GLA_DK = 128
GLA_DV = 192
GLA_DVP = 256
GLA_QK = GLA_HEADS * GLA_DK
GLA_V = GLA_HEADS * GLA_DV
GLA_VP = GLA_HEADS * GLA_DVP
GLA_RANK = 16
GLA_RANKP = 128
GLA_TAU = 16.0
GLA_CHUNK = 64
HEAD_DIM = 64
N_HEADS = 4
HW = N_HEADS * HEAD_DIM
DIL_PAIRS = ((128, 1), (512, 4), (2048, 16))
N_GROUPS = 3
BAND = 128
WIN_MAX = 2048
MEM_TOKENS = 256
D_FF = 2816
FF_TILE = 256
ROPE_THETA = 10000.0
EPS = 1e-6
ATTN_SCALE = HEAD_DIM ** -0.5
NEG = -1e30
ROPE_SUB_ROWS = 512
LOG2E = 1.4426950408889634
LN2 = 0.6931471805599453

A_COL_Q = 0
A_COL_K = GLA_QK
A_COL_V = 2 * GLA_QK
A_COL_G = A_COL_V + GLA_VP
A_COL_MQ = A_COL_G + GLA_VP
A_COL_GLR = A_COL_MQ + HW
A_IN_P = A_COL_GLR + GLA_RANKP

V7X_VMEM_BYTES = 64 * 1024 * 1024
VMEM_CAP = 56 * 1024 * 1024


def _cparams(sem, vmem_bytes):
    return pltpu.CompilerParams(
        dimension_semantics=sem,
        vmem_limit_bytes=int(min(max(vmem_bytes, 16 * 1024 * 1024), VMEM_CAP)))


def _bdot(a, b):
    return jnp.dot(a.astype(BF16), b.astype(BF16), preferred_element_type=F32)


def _bdot_nt(a, b):
    return lax.dot_general(a.astype(BF16), b.astype(BF16), (((1,), (1,)), ((), ())),
                           preferred_element_type=F32)


def _bdot_tn(a, b):
    return lax.dot_general(a.astype(BF16), b.astype(BF16), (((0,), (0,)), ((), ())),
                           preferred_element_type=F32)


def _split(x):
    hi = x.astype(BF16)
    lo = (x - hi.astype(F32)).astype(BF16)
    return hi, lo


def _dot_exact_rhs(x, m):
    hi, lo = _split(x)
    return (jnp.dot(hi, m, preferred_element_type=F32)
            + jnp.dot(lo, m, preferred_element_type=F32))


def _dot_exact_lhs(m2, x):
    hi, lo = _split(x)
    return jnp.dot(m2, jnp.concatenate([hi, lo], axis=0), preferred_element_type=F32)


def _silu(x):
    return x * jax.nn.sigmoid(x)


def _log_sigmoid(z):
    return jnp.minimum(z, 0.0) - jnp.log1p(jnp.exp(-jnp.abs(z)))


def _gla_tables():
    c = GLA_CHUNK
    i = np.arange(c)[:, None]
    t = np.arange(c)[None, :]
    blocks = [(t <= i), (t > i)]
    masks = []
    for lvl in range(6):
        s = c >> lvl
        half = s // 2
        mid = (i // s) * s + half - 1
        second = (i % s) >= half
        m = np.where(second, (t > mid) & (t <= i), (t > i) & (t <= mid))
        blocks.append(m)
        j = t
        masks.append(((i // s) == (j // s)) & ((i % s) >= half) & ((j % s) < half))
    masks.append(i == t)
    mall = np.concatenate(blocks, axis=0).astype(np.float32)
    mall = np.concatenate([mall, mall], axis=1)
    lmask = np.stack(masks, axis=0).astype(np.float32)
    assert np.array_equal(lmask.sum(0), (t <= i).astype(np.float32))
    return mall, lmask


def _head_tables():
    lane = np.arange(HW)
    group = (lane[:, None] // HEAD_DIM == lane[None, :] // HEAD_DIM).astype(np.float32)
    hmask = (lane[None, :] // HEAD_DIM == np.arange(N_HEADS)[:, None]).astype(np.float32)
    return group, hmask.reshape(N_HEADS, 1, HW)


def _band_bias():
    i = np.arange(BAND)[:, None]
    j = np.arange(2 * BAND)[None, :]
    dist = i + BAND - j
    ok = (dist >= 0) & (dist <= BAND)
    first = ok & (j >= BAND)
    return np.where(np.stack([first, ok], 0), 0.0, NEG).astype(np.float32)


def _rope_tables(pos):
    half = HEAD_DIM // 2
    inv = ROPE_THETA ** (-np.arange(half, dtype=np.float64) / half)
    ang = np.asarray(pos, np.float64)[:, None] * inv[None, :]
    cos = np.cos(ang).astype(np.float32)
    sin = np.sin(ang).astype(np.float32)
    cos64 = np.concatenate([cos, cos], axis=-1)
    sin64 = np.concatenate([-sin, sin], axis=-1)
    return jnp.asarray(np.tile(cos64, (1, N_HEADS))), jnp.asarray(np.tile(sin64, (1, N_HEADS)))


def _norm_matmul_kernel(x_ref, g_ref, w_ref, o_ref):
    x = x_ref[...]
    ms = jnp.mean(x * x, axis=-1, keepdims=True)
    y = x * lax.rsqrt(ms + EPS) * g_ref[...]
    o_ref[...] = jnp.dot(y.astype(BF16), w_ref[...], preferred_element_type=F32)


def _norm_matmul(x, g, gi, w, wi, *, tm, tn, name):
    n, d = x.shape
    nout = w.shape[2]
    vmem = 2 * (tm * d * 4 + d * tn * 2 + tm * tn * 4) + 2 * tm * d * 4 + tm * tn * 4
    return pl.pallas_call(
        _norm_matmul_kernel,
        grid=(n // tm, nout // tn),
        in_specs=[pl.BlockSpec((tm, d), lambda i, j: (i, 0)),
                  pl.BlockSpec((None, 1, d), lambda i, j: (gi, 0, 0)),
                  pl.BlockSpec((None, d, tn), lambda i, j: (wi, 0, j))],
        out_specs=pl.BlockSpec((tm, tn), lambda i, j: (i, j)),
        out_shape=jax.ShapeDtypeStruct((n, nout), F32),
        compiler_params=_cparams(("parallel", "parallel"), vmem),
        name=name,
    )(x, g, w)


def _norm_matmul_rope_kernel(x_ref, g_ref, w_ref, hg_ref, grp_ref, cos_ref, sin_ref, *rest,
                             rope_width, cast):
    if cast:
        wgu_i, wdn_i, o_ref, wgu_o, wdn_o = rest
        wgu_o[...] = wgu_i[...].astype(BF16)
        wdn_o[...] = wdn_i[...].astype(BF16)
    else:
        o_ref, = rest
    grp = grp_ref[...]
    lane = lax.broadcasted_iota(jnp.int32, (1, HW), 1)
    first_half = (lane & (HEAD_DIM - 1)) < (HEAD_DIM // 2)
    tm = x_ref.shape[0]
    sub = min(tm, ROPE_SUB_ROWS)
    for r in range(tm // sub):
        rs = slice(r * sub, (r + 1) * sub)
        trs = rs if cos_ref.shape[0] == tm else slice(None)
        x = x_ref[rs, :]
        ms = jnp.mean(x * x, axis=-1, keepdims=True)
        xn = x * lax.rsqrt(ms + EPS) * g_ref[...]
        y = jnp.dot(xn.astype(BF16), w_ref[...], preferred_element_type=F32)
        for c in range(rope_width // HW):
            sl = slice(c * HW, (c + 1) * HW)
            yc = y[:, sl]
            ss = _dot_exact_rhs(yc * yc, grp)
            yc = yc * lax.rsqrt(ss * (1.0 / HEAD_DIM) + EPS) * hg_ref[...]
            fwd = pltpu.roll(yc, HW - HEAD_DIM // 2, 1)
            bwd = pltpu.roll(yc, HEAD_DIM // 2, 1)
            rot = jnp.where(first_half, fwd, bwd)
            o_ref[rs, sl] = yc * cos_ref[trs, :] + rot * sin_ref[trs, :]
        o_ref[rs, rope_width:] = y[:, rope_width:]


def _norm_matmul_rope(x, g, gi, w, wi, head_gain, tabs, *, rope_width, tm, name, pos_blocks=1,
                      cast=None):
    n, d = x.shape
    nout = w.shape[2]
    grp, cos, sin = tabs
    hg = jnp.tile(head_gain.reshape(1, HEAD_DIM), (1, N_HEADS))
    trow = 1 if cos.shape[0] == 1 else tm
    tab_map = (lambda i: (0, 0)) if cos.shape[0] == 1 else (lambda i: (i % pos_blocks, 0))
    vmem = (2 * (tm * d * 4 + tm * nout * 4 + 2 * trow * HW * 4) + d * nout * 2
            + 2 * tm * d * 4 + 3 * tm * nout * 4 + (8 * 1024 * 1024 if cast else 0))
    in_specs = [pl.BlockSpec((tm, d), lambda i: (i, 0)),
                pl.BlockSpec((None, 1, d), lambda i: (gi, 0, 0)),
                pl.BlockSpec((None, d, nout), lambda i: (wi, 0, 0), pipeline_mode=pl.Buffered(1)),
                pl.BlockSpec((1, HW), lambda i: (0, 0)),
                pl.BlockSpec((HW, HW), lambda i: (0, 0)),
                pl.BlockSpec((trow, HW), tab_map),
                pl.BlockSpec((trow, HW), tab_map)]
    args = [x, g, w, hg, grp, cos, sin]
    out_specs = [pl.BlockSpec((tm, nout), lambda i: (i, 0))]
    out_shape = [jax.ShapeDtypeStruct((n, nout), F32)]
    if cast is not None:
        wgu32, wdn32, cli = cast
        for w32 in (wgu32, wdn32):
            i_spec, o_spec, o_shape = _cast_specs(w32, cli, n // tm, lambda i: i)
            in_specs.append(i_spec)
            out_specs.append(o_spec)
            out_shape.append(o_shape)
            args.append(w32)
    out = pl.pallas_call(
        functools.partial(_norm_matmul_rope_kernel, rope_width=rope_width, cast=cast is not None),
        grid=(n // tm,),
        in_specs=in_specs,
        out_specs=out_specs,
        out_shape=out_shape,
        compiler_params=_cparams(("arbitrary" if cast is not None else "parallel",), vmem),
        name=name,
    )(*args)
    return out if cast is not None else out[0]


def _gla_block(q, k, v, g, la, ht, gon, mall_ref, lmask_ref, n_chunks):
    cr = GLA_CHUNK
    chunk = lambda c: slice(c * cr, (c + 1) * cr)
    la_cat = jnp.concatenate([la[chunk(c)] for c in range(n_chunks)], axis=1)
    e_all = jnp.exp2(_dot_exact_lhs(mall_ref[...], la_cat * LOG2E))

    def decay_rows(blk):
        return jnp.concatenate(
            [e_all[blk * cr:(blk + 1) * cr, c * GLA_DK:(c + 1) * GLA_DK] for c in range(n_chunks)],
            axis=0)

    eb = decay_rows(0)
    qb = (q * eb).astype(BF16)
    kdec = (k * decay_rows(1)).astype(BF16)
    q_lv = [q.astype(BF16)]
    k_lv = [k.astype(BF16)]
    for lvl in range(6):
        el = decay_rows(2 + lvl)
        q_lv.append((q * el).astype(BF16))
        k_lv.append((k * el).astype(BF16))
    o_intra, kv_new = [], []
    for c in range(n_chunks):
        r = chunk(c)
        s = lmask_ref[6] * _bdot_nt(q_lv[0][r], k_lv[0][r])
        for lvl in range(6):
            s = s + lmask_ref[lvl] * _bdot_nt(q_lv[1 + lvl][r], k_lv[1 + lvl][r])
        o_intra.append(_bdot(s, v[r]))
        kv_new.append(_bdot_tn(v[r], kdec[r]))
    outs = []
    for c in range(n_chunks):
        outs.append(o_intra[c] + _bdot_nt(qb[chunk(c)], ht))
        ht = ht * eb[c * cr + cr - 1:(c + 1) * cr, :] + kv_new[c]
    o = jnp.concatenate(outs, axis=0)
    ms = jnp.sum(o * o, axis=-1, keepdims=True) * (1.0 / GLA_DV)
    return o * lax.rsqrt(ms + EPS) * gon * _silu(g), ht


def _cast_specs(w32, li, n_steps, step_of):
    rows, cols = w32.shape[1:]
    chunk, rep = rows // n_steps, 1
    assert chunk * n_steps == rows
    while (chunk * rep) % 16:
        rep *= 2
    chunk *= rep
    return (pl.BlockSpec((None, chunk, cols), lambda *ids: (li, step_of(*ids) // rep, 0)),
            pl.BlockSpec((chunk, cols), lambda *ids: (step_of(*ids) // rep, 0)),
            jax.ShapeDtypeStruct((rows, cols), BF16))


def _gla_kernel(x_ref, g1_ref, w_ref, wg2_ref, bg_ref, gon_ref, mall_ref, lmask_ref, h0_ref,
                wgu_i, wdn_i, tok_ref, mq_ref, hout_ref, wgu_o, wdn_o, proj_s, ht_s, *, n_chunks):
    t = pl.program_id(1)
    wgu_o[...] = wgu_i[...].astype(BF16)
    wdn_o[...] = wdn_i[...].astype(BF16)

    @pl.when(t == 0)
    def _():
        ht_s[...] = h0_ref[0]

    x = x_ref[...]
    ms = jnp.mean(x * x, axis=-1, keepdims=True)
    xn = (x * lax.rsqrt(ms + EPS) * g1_ref[...]).astype(BF16)
    proj_s[...] = jnp.dot(xn, w_ref[...], preferred_element_type=F32)
    mq_ref[...] = proj_s[:, A_COL_MQ:A_COL_MQ + HW]
    z = _bdot(proj_s[:, A_COL_GLR:A_COL_GLR + GLA_RANKP], wg2_ref[...]) + bg_ref[...]
    la = _log_sigmoid(z) * (1.0 / GLA_TAU)
    gon = gon_ref[...]
    for h in range(GLA_HEADS):
        kc = slice(h * GLA_DK, (h + 1) * GLA_DK)
        vc = slice(h * GLA_DVP, (h + 1) * GLA_DVP)
        q = proj_s[:, A_COL_Q + h * GLA_DK:A_COL_Q + (h + 1) * GLA_DK] * (GLA_DK ** -0.5)
        k = proj_s[:, A_COL_K + h * GLA_DK:A_COL_K + (h + 1) * GLA_DK]
        v = proj_s[:, A_COL_V + h * GLA_DVP:A_COL_V + (h + 1) * GLA_DVP].astype(BF16)
        g = proj_s[:, A_COL_G + h * GLA_DVP:A_COL_G + (h + 1) * GLA_DVP]
        tok, ht = _gla_block(q, k, v, g, la[:, kc], ht_s[h], gon, mall_ref, lmask_ref, n_chunks)
        tok_ref[:, vc] = tok.astype(tok_ref.dtype)
        ht_s[h] = ht

    @pl.when(t == pl.num_programs(1) - 1)
    def _():
        hout_ref[0] = ht_s[...]


def _gla_prompt(x, g1, w_in, li, h0t, wg2, bg, gon, mall, lmask, wgu32, wdn32, *, batch, seq, tb):
    n, d = x.shape
    nt = seq // tb
    const2 = lambda shape: pl.BlockSpec(shape, lambda b, t: (0,) * len(shape))
    st_spec = pl.BlockSpec((1, GLA_HEADS, GLA_DVP, GLA_DK), lambda b, t: (b, 0, 0, 0))
    step_of = lambda b, t: b * nt + t
    gu_in, gu_out, gu_shape = _cast_specs(wgu32, li, batch * nt, step_of)
    dn_in, dn_out, dn_shape = _cast_specs(wdn32, li, batch * nt, step_of)
    vmem = (2 * tb * d * 4 + d * A_IN_P * 2 + tb * A_IN_P * 4 + 2 * tb * (GLA_VP * 2 + HW * 4)
            + 6 * GLA_HEADS * GLA_DVP * GLA_DK * 4 + 24 * tb * GLA_DVP * 4 + 8 * 1024 * 1024)
    return pl.pallas_call(
        functools.partial(_gla_kernel, n_chunks=tb // GLA_CHUNK),
        grid=(batch, nt),
        in_specs=[
            pl.BlockSpec((tb, d), lambda b, t: (b * nt + t, 0)),
            pl.BlockSpec((None, 1, d), lambda b, t: (li, 0, 0)),
            pl.BlockSpec((None, d, A_IN_P), lambda b, t: (li, 0, 0), pipeline_mode=pl.Buffered(1)),
            const2((GLA_RANKP, GLA_QK)), const2((1, GLA_QK)), const2((1, GLA_DVP)),
            const2(mall.shape), const2(lmask.shape), st_spec, gu_in, dn_in,
        ],
        out_specs=[
            pl.BlockSpec((tb, GLA_VP), lambda b, t: (b * nt + t, 0)),
            pl.BlockSpec((tb, HW), lambda b, t: (b * nt + t, 0)),
            st_spec, gu_out, dn_out,
        ],
        out_shape=[jax.ShapeDtypeStruct((n, GLA_VP), BF16),
                   jax.ShapeDtypeStruct((n, HW), F32),
                   jax.ShapeDtypeStruct((batch, GLA_HEADS, GLA_DVP, GLA_DK), F32),
                   gu_shape, dn_shape],
        scratch_shapes=[pltpu.VMEM((tb, A_IN_P), F32),
                        pltpu.VMEM((GLA_HEADS, GLA_DVP, GLA_DK), F32)],
        compiler_params=_cparams(("parallel", "arbitrary"), vmem),
        name="gla_prompt",
    )(x, g1, w_in, wg2, bg, gon, mall, lmask, h0t, wgu32, wdn32)


STEP_ROWS = 8


def _gla_step_kernel(q_ref, k_ref, v_ref, g_ref, glr_ref, wg2_ref, bg_ref, gon_ref, st_ref, *rest,
                     out_layer):
    tok_ref, so_ref = rest[-2:]
    nb = q_ref.shape[0]
    for other in range(so_ref.shape[0]):
        if other != out_layer:
            so_ref[other] = jnp.zeros(so_ref.shape[1:], F32)
    z = _bdot(glr_ref[...], wg2_ref[...]) + bg_ref[...]
    a = jnp.exp(_log_sigmoid(z) * (1.0 / GLA_TAU))
    q = q_ref[...] * (GLA_DK ** -0.5)
    k = k_ref[...]
    v = v_ref[...]
    g = g_ref[...]
    gon = gon_ref[...][:, :GLA_DV]
    row = lax.broadcasted_iota(jnp.int32, (nb, 1), 0)
    for h in range(GLA_HEADS):
        ks = slice(h * GLA_DK, (h + 1) * GLA_DK)
        vs = slice(h * GLA_DVP, h * GLA_DVP + GLA_DV)
        o_rows = jnp.zeros((nb, GLA_DV), F32)
        for b in range(nb):
            only_b = row == b
            outer = _bdot_tn(jnp.where(only_b, v[:, vs], 0.0), k[:, ks])
            s_new = st_ref[0, b, h] * a[b:b + 1, ks] + outer
            so_ref[out_layer, b, h] = s_new
            o_rows = o_rows + jnp.where(only_b, _bdot_nt(q[:, ks], s_new), 0.0)
        ms = jnp.sum(o_rows * o_rows, axis=-1, keepdims=True) * (1.0 / GLA_DV)
        tok_ref[:, vs] = o_rows * lax.rsqrt(ms + EPS) * gon * _silu(g[:, vs])
        tok_ref[:, h * GLA_DVP + GLA_DV:(h + 1) * GLA_DVP] = jnp.zeros(
            (nb, GLA_DVP - GLA_DV), F32)


def _gla_step(proj, state_t, li, prev_out, wg2, bg, gon):
    bsz = proj.shape[0]
    nb = STEP_ROWS
    col = lambda w, c: pl.BlockSpec((nb, w), lambda i: (i, c))
    st_spec = pl.BlockSpec((1, nb, GLA_HEADS, GLA_DV, GLA_DK), lambda i: (li, i, 0, 0, 0))
    in_specs = [col(GLA_QK, A_COL_Q // GLA_QK), col(GLA_QK, A_COL_K // GLA_QK),
                col(GLA_VP, A_COL_V // GLA_VP), col(GLA_VP, A_COL_G // GLA_VP),
                col(GLA_RANKP, A_COL_GLR // GLA_RANKP),
                pl.BlockSpec((GLA_RANKP, GLA_QK), lambda i: (0, 0)),
                pl.BlockSpec((1, GLA_QK), lambda i: (0, 0)),
                pl.BlockSpec((1, GLA_DVP), lambda i: (0, 0)),
                st_spec]
    args = [proj, proj, proj, proj, proj, wg2, bg, gon, state_t]
    if prev_out is None:
        aliases, out_layer = {}, li
        so_spec = pl.BlockSpec((state_t.shape[0], nb, GLA_HEADS, GLA_DV, GLA_DK),
                               lambda i: (0, i, 0, 0, 0))
    else:
        in_specs.append(pl.BlockSpec(memory_space=pl.ANY))
        args.append(prev_out)
        aliases, out_layer, so_spec = {len(args) - 1: 1}, 0, st_spec
    return pl.pallas_call(
        functools.partial(_gla_step_kernel, out_layer=out_layer),
        grid=(bsz // nb,),
        in_specs=in_specs,
        out_specs=[pl.BlockSpec((nb, GLA_VP), lambda i: (i, 0)), so_spec],
        out_shape=[jax.ShapeDtypeStruct((bsz, GLA_VP), F32),
                   jax.ShapeDtypeStruct(state_t.shape, F32)],
        input_output_aliases=aliases,
        compiler_params=_cparams(("parallel",), 32 * 1024 * 1024),
        name="gla_step",
    )(*args)


def _mem_qnorm(q, gq_ref, grp_ref):
    ss = _dot_exact_rhs(q * q, grp_ref[...])
    return q * lax.rsqrt(ss * (1.0 / HEAD_DIM) + EPS) * gq_ref[...] * ATTN_SCALE


def _mem_attn_kernel(q_ref, kt_ref, vt_ref, gq_ref, grp_ref, hmask_ref, o_ref):
    qn = _mem_qnorm(q_ref[0], gq_ref, grp_ref)
    tq = qn.shape[0]
    kt = kt_ref[0, 0].astype(BF16)
    vt = vt_ref[0, 0].astype(BF16)
    q4 = jnp.concatenate([qn * hmask_ref[h] for h in range(N_HEADS)], axis=0)
    s = _bdot(q4, kt)
    m = jnp.max(s, axis=-1, keepdims=True)
    p = jnp.exp(s - m)
    o4 = _bdot_nt(p, vt) / jnp.sum(p, axis=-1, keepdims=True)
    out = jnp.zeros(qn.shape, F32)
    for h in range(N_HEADS):
        out = out + hmask_ref[h] * o4[h * tq:(h + 1) * tq]
    o_ref[0] = out.astype(o_ref.dtype)


def _mem_attn(q3, qcol, kt, vt, li, gq, grp, hmask, *, tq):
    bsz, t, _ = q3.shape
    g = jnp.tile(gq.reshape(1, HEAD_DIM), (1, N_HEADS))
    vmem = 2 * (tq * HW * 4 + 2 * MEM_TOKENS * HW * 4 + tq * HW * 2) + 16 * tq * HW * 4
    kv_spec = pl.BlockSpec((1, 1, HW, MEM_TOKENS), lambda b, i: (li, b, 0, 0))
    return pl.pallas_call(
        _mem_attn_kernel,
        grid=(bsz, t // tq),
        in_specs=[pl.BlockSpec((1, tq, HW), lambda b, i: (b, i, qcol)),
                  kv_spec, kv_spec,
                  pl.BlockSpec((1, HW), lambda b, i: (0, 0)),
                  pl.BlockSpec((HW, HW), lambda b, i: (0, 0)),
                  pl.BlockSpec((N_HEADS, 1, HW), lambda b, i: (0, 0, 0))],
        out_specs=pl.BlockSpec((1, tq, HW), lambda b, i: (b, i, 0)),
        out_shape=jax.ShapeDtypeStruct((bsz, t, HW), BF16),
        compiler_params=_cparams(("parallel", "parallel"), vmem),
        name="mem_attn",
    )(q3, kt, vt, g, grp, hmask)


def _mem_step_kernel(q_ref, kt_ref, vt_ref, gq_ref, grp_ref, hrow_ref, o_ref):
    nb = q_ref.shape[0]
    qn = _mem_qnorm(q_ref[...], gq_ref, grp_ref)
    hrow = hrow_ref[...]
    row = lax.broadcasted_iota(jnp.int32, (nb, 1), 0)
    out = jnp.zeros((nb, HW), F32)
    for b in range(nb):
        q4 = hrow * qn[b:b + 1]
        s = _bdot(q4, kt_ref[0, b])
        m = jnp.max(s, axis=-1, keepdims=True)
        p = jnp.exp(s - m)
        l = jnp.sum(p, axis=-1, keepdims=True)
        o4 = _bdot_nt(p, vt_ref[0, b]) / l
        o_b = jnp.sum(hrow * o4, axis=0, keepdims=True)
        out = out + jnp.where(row == b, o_b, 0.0)
    o_ref[...] = out


def _mem_step(q, qcol, kt, vt, li, gq, grp, hrow):
    bsz = q.shape[0]
    nb = STEP_ROWS
    g = jnp.tile(gq.reshape(1, HEAD_DIM), (1, N_HEADS))
    kv_spec = pl.BlockSpec((1, nb, HW, MEM_TOKENS), lambda i: (li, i, 0, 0))
    return pl.pallas_call(
        _mem_step_kernel,
        grid=(bsz // nb,),
        in_specs=[pl.BlockSpec((nb, HW), lambda i: (i, qcol)),
                  kv_spec, kv_spec,
                  pl.BlockSpec((1, HW), lambda i: (0, 0)),
                  pl.BlockSpec((HW, HW), lambda i: (0, 0)),
                  pl.BlockSpec((8, HW), lambda i: (0, 0))],
        out_specs=pl.BlockSpec((nb, HW), lambda i: (i, 0)),
        out_shape=jax.ShapeDtypeStruct((bsz, HW), F32),
        compiler_params=_cparams(("parallel",), 24 * 1024 * 1024),
        name="mem_step",
    )(q, kt, vt, g, grp, hrow)


def _mem_kv_kernel(x_ref, gn_ref, w_ref, gk_ref, grp_ref, kt_ref, vt_ref):
    x = x_ref[0]
    ms = jnp.mean(x * x, axis=-1, keepdims=True)
    y = x * lax.rsqrt(ms + EPS) * gn_ref[0]
    kv = jnp.dot(y.astype(BF16), w_ref[0], preferred_element_type=F32)
    k = kv[:, :HW]
    ss = _dot_exact_rhs(k * k, grp_ref[...])
    k = k * lax.rsqrt(ss * (1.0 / HEAD_DIM) + EPS) * gk_ref[0]
    kt_ref[0, 0] = k.T
    vt_ref[0, 0] = kv[:, HW:].T


def _mem_kv(mem, gn, w, gk, grp):
    bsz, m, d = mem.shape
    nl = w.shape[0]
    out = jax.ShapeDtypeStruct((nl, bsz, HW, m), F32)
    o_spec = pl.BlockSpec((1, 1, HW, m), lambda l, b: (l, b, 0, 0))
    return pl.pallas_call(
        _mem_kv_kernel,
        grid=(nl, bsz),
        in_specs=[pl.BlockSpec((1, m, d), lambda l, b: (b, 0, 0)),
                  pl.BlockSpec((1, 1, d), lambda l, b: (l, 0, 0)),
                  pl.BlockSpec((1, d, 2 * HW), lambda l, b: (l, 0, 0)),
                  pl.BlockSpec((1, 1, HW), lambda l, b: (l, 0, 0)),
                  pl.BlockSpec((HW, HW), lambda l, b: (0, 0))],
        out_specs=[o_spec, o_spec],
        out_shape=[out, out],
        compiler_params=_cparams(("parallel", "parallel"), 24 * 1024 * 1024),
        name="mem_kv",
    )(mem, gn.reshape(nl, 1, d), w, jnp.tile(gk, (1, N_HEADS)).reshape(nl, 1, HW), grp)


def _band_attn_kernel(q0_ref, q1_ref, k0_ref, k1_ref, v0_ref, v1_ref, bias_ref, hmask_ref,
                      o_ref, og_s, ls_s, *, seq):
    g = pl.program_id(1)
    n_units = seq // BAND

    def run_group(gi, dil):
        nb = seq // dil // BAND
        shift = nb.bit_length() - 1
        lane = lax.broadcasted_iota(jnp.int32, (1, 128), 1)
        low_lanes = lane < HEAD_DIM

        def idx(start):
            if dil == 1:
                return pl.ds(pl.multiple_of(start, BAND), BAND)
            return pl.ds(start, BAND, stride=dil)

        def rows(ref0, ref1, start):
            return jnp.concatenate([ref0[idx(start), :], ref1[idx(start), :]], axis=1)

        def unit(u, carry):
            r = lax.shift_right_logical(u, shift)
            nblk = u & (nb - 1)
            qs = r + dil * BAND * nblk
            ks = r + dil * BAND * jnp.maximum(nblk - 1, 0)
            q = rows(q0_ref, q1_ref, qs) * (ATTN_SCALE * LOG2E)
            kk = jnp.concatenate([rows(k0_ref, k1_ref, ks), rows(k0_ref, k1_ref, qs)],
                                 axis=0).astype(BF16)
            vv = jnp.concatenate([rows(v0_ref, v1_ref, ks), rows(v0_ref, v1_ref, qs)],
                                 axis=0).astype(BF16)
            q4 = jnp.concatenate([q * hmask_ref[h] for h in range(N_HEADS)], axis=0)
            bias = bias_ref[jnp.minimum(nblk, 1)]
            s = _bdot_nt(q4, kk).reshape(N_HEADS, BAND, 2 * BAND) + bias[None]
            s = s.reshape(N_HEADS * BAND, 2 * BAND)
            m = jnp.max(s, axis=-1, keepdims=True)
            p = jnp.exp2(s - m)
            l = jnp.sum(p, axis=-1, keepdims=True)
            oe = _bdot(p, vv)
            hd = [slice(h * BAND, (h + 1) * BAND) for h in range(N_HEADS)]
            for half in range(2):
                ha, hb = hd[2 * half], hd[2 * half + 1]
                cols = slice(half * 128, (half + 1) * 128)
                o_h = jnp.where(low_lanes, oe[ha, cols], oe[hb, cols])
                l_h = jnp.where(low_lanes, l[ha], l[hb])
                m_h = jnp.where(low_lanes, m[ha], m[hb])
                og_s[2 * gi + half, idx(qs), :] = o_h / l_h
                ls_s[2 * gi + half, idx(qs), :] = (m_h + jnp.log2(l_h)) * LN2
            return carry

        lax.fori_loop(0, n_units, unit, 0, unroll=4)

    for gi, (_, dil) in enumerate(DIL_PAIRS):
        pl.when(g == gi)(functools.partial(run_group, gi, dil))

    @pl.when(g == N_GROUPS - 1)
    def _merge():
        tr = 512
        for c in range(seq // tr):
            rows = slice(c * tr, (c + 1) * tr)
            for half in range(2):
                ls = [ls_s[2 * gi + half, rows, :] for gi in range(N_GROUPS)]
                m = jnp.maximum(jnp.maximum(ls[0], ls[1]), ls[2])
                es = [jnp.exp(x - m) for x in ls]
                inv = 1.0 / (es[0] + es[1] + es[2])
                acc = jnp.zeros((tr, 128), F32)
                for gi in range(N_GROUPS):
                    acc = acc + (es[gi] * inv) * og_s[2 * gi + half, rows, :]
                o_ref[rows, half * 128:(half + 1) * 128] = acc.astype(o_ref.dtype)


def _band_attn(qd, kv, band, hmask, *, batch, seq):
    n = qd.shape[0]
    once = pl.Buffered(1)
    half = lambda cmap: pl.BlockSpec((seq, 128), cmap, pipeline_mode=once)
    qhalf = lambda cmap: pl.BlockSpec((seq, 128), cmap)
    vmem = (8 * seq * 128 * 4 + 2 * seq * HW * 2 + 12 * seq * 128 * 4 + 8 * 1024 * 1024)
    return pl.pallas_call(
        functools.partial(_band_attn_kernel, seq=seq),
        grid=(batch, N_GROUPS),
        in_specs=[qhalf(lambda b, g: (b, 2 * g)), qhalf(lambda b, g: (b, 2 * g + 1)),
                  half(lambda b, g: (b, 0)), half(lambda b, g: (b, 1)),
                  half(lambda b, g: (b, 2)), half(lambda b, g: (b, 3)),
                  pl.BlockSpec((2, BAND, 2 * BAND), lambda b, g: (0, 0, 0)),
                  pl.BlockSpec((N_HEADS, 1, HW), lambda b, g: (0, 0, 0))],
        out_specs=pl.BlockSpec((seq, HW), lambda b, g: (b, 0)),
        out_shape=jax.ShapeDtypeStruct((n, HW), BF16),
        scratch_shapes=[pltpu.VMEM((2 * N_GROUPS, seq, 128), F32),
                        pltpu.VMEM((2 * N_GROUPS, seq, 128), F32)],
        compiler_params=_cparams(("parallel", "arbitrary"), vmem),
        name="band_attn",
    )(qd, qd, kv, kv, kv, kv, band, hmask)


def _step_bias(cache_len):
    t = np.arange(cache_len)
    rows = np.zeros((16, cache_len), np.float32)
    for gi, (_, dil) in enumerate(DIL_PAIRS):
        keep = (t >= cache_len - dil * BAND) & ((cache_len - t) % dil == 0)
        rows[4 * gi:4 * gi + 4] = np.where(keep, 0.0, NEG)[None]
    return rows


def _step_attn_kernel(q_ref, kn_ref, vn_ref, kt_ref, vt_ref, bias_ref, hrow_ref, o_ref):
    j = pl.program_id(1)
    q = q_ref[pl.ds(j, 1), :] * ATTN_SCALE
    kn = kn_ref[pl.ds(j, 1), :]
    vn = vn_ref[pl.ds(j, 1), :]
    hrow = hrow_ref[...]
    row = lax.broadcasted_iota(jnp.int32, (16, 1), 0)
    q16 = hrow * jnp.where(row < 4, q[:, :HW], jnp.where(row < 8, q[:, HW:2 * HW], q[:, 2 * HW:]))
    s = _bdot(q16, kt_ref[0]) + bias_ref[...]
    s_n = jnp.sum(q16 * kn, axis=-1, keepdims=True)
    m = jnp.maximum(jnp.max(s, axis=-1, keepdims=True), s_n)
    p = jnp.exp(s - m)
    p_n = jnp.exp(s_n - m)
    l = jnp.sum(p, axis=-1, keepdims=True) + p_n
    o16 = (_bdot_nt(p, vt_ref[0]) + p_n * vn) / l
    lse = m + jnp.log(l)
    lg = [lse[4 * g:4 * g + 4] for g in range(N_GROUPS)]
    mm = jnp.maximum(jnp.maximum(lg[0], lg[1]), lg[2])
    es = [jnp.exp(x - mm) for x in lg]
    inv = 1.0 / (es[0] + es[1] + es[2])
    acc = jnp.zeros((4, HW), F32)
    for g in range(N_GROUPS):
        acc = acc + (es[g] * inv) * (hrow[0:4] * o16[4 * g:4 * g + 4])
    o_ref[pl.ds(j, 1), :] = jnp.sum(acc, axis=0, keepdims=True)


def _step_attn(q, kn, kvn, kt, vt, bias, hrow16):
    bsz, _, cache_len = kt.shape
    assert cache_len >= max(d for _, d in DIL_PAIRS) * BAND
    nb = STEP_ROWS
    rows = lambda w: pl.BlockSpec((nb, w), lambda i, j: (i, 0))
    cache = pl.BlockSpec((1, HW, cache_len), lambda i, j: (i * nb + j, 0, 0))
    vmem = 4 * HW * cache_len * 4 + 8 * 16 * cache_len * 4 + 4 * 1024 * 1024
    return pl.pallas_call(
        _step_attn_kernel,
        grid=(bsz // nb, nb),
        in_specs=[rows(N_GROUPS * HW), rows(HW),
                  pl.BlockSpec((nb, HW), lambda i, j: (i, 1)), cache, cache,
                  pl.BlockSpec((16, cache_len), lambda i, j: (0, 0)),
                  pl.BlockSpec((16, HW), lambda i, j: (0, 0))],
        out_specs=rows(HW),
        out_shape=jax.ShapeDtypeStruct((bsz, HW), F32),
        compiler_params=_cparams(("parallel", "arbitrary"), vmem),
        name="step_attn",
    )(q, kn, kvn, kt, vt, bias, hrow16)


def _out_ffn_kernel(h_ref, tok_ref, mem_ref, wo1_ref, wo2_ref, g2_ref, wgu_ref, wdn_ref, o_ref):
    h1 = (h_ref[...]
          + jnp.dot(tok_ref[...].astype(BF16), wo1_ref[...], preferred_element_type=F32)
          + jnp.dot(mem_ref[...].astype(BF16), wo2_ref[...], preferred_element_type=F32))
    ms = jnp.mean(h1 * h1, axis=-1, keepdims=True)
    hn = (h1 * lax.rsqrt(ms + EPS) * g2_ref[...]).astype(BF16)
    gate = jnp.dot(hn, wgu_ref[:, :D_FF], preferred_element_type=F32)
    up = jnp.dot(hn, wgu_ref[:, D_FF:], preferred_element_type=F32)
    act = (_silu(gate) * up).astype(BF16)
    o_ref[...] = h1 + jnp.dot(act, wdn_ref[...], preferred_element_type=F32)


def _out_ffn(h, tok, mem, wo, woi, g2, li, wgu, wdn, *, tm):
    n, d = h.shape
    wt = tok.shape[1]
    assert wo.shape[1] == wt + HW and wt % HW == 0
    once = pl.Buffered(1)
    vmem = ((wt + HW) * d * 2 + 3 * D_FF * d * 2
            + 2 * tm * (2 * d * 4 + (wt + HW) * 4) + 3 * tm * d * 4 + 3 * tm * D_FF * 4
            + 4 * 1024 * 1024)
    in_specs = [pl.BlockSpec((tm, d), lambda i: (i, 0)),
                pl.BlockSpec((tm, wt), lambda i: (i, 0)),
                pl.BlockSpec((tm, HW), lambda i: (i, 0)),
                pl.BlockSpec((None, wt, d), lambda i: (woi, 0, 0), pipeline_mode=once),
                pl.BlockSpec((None, HW, d), lambda i: (woi, wt // HW, 0), pipeline_mode=once),
                pl.BlockSpec((None, 1, d), lambda i: (li, 0, 0), pipeline_mode=once),
                pl.BlockSpec((d, 2 * D_FF), lambda i: (0, 0), pipeline_mode=once),
                pl.BlockSpec((D_FF, d), lambda i: (0, 0), pipeline_mode=once)]
    args = [h, tok, mem, wo, wo, g2, wgu, wdn]
    return pl.pallas_call(
        _out_ffn_kernel,
        grid=(n // tm,),
        in_specs=in_specs,
        out_specs=pl.BlockSpec((tm, d), lambda i: (i, 0)),
        out_shape=jax.ShapeDtypeStruct((n, d), F32),
        compiler_params=_cparams(("parallel",), vmem),
        name="out_ffn",
    )(*args)


def _tail_transpose_kernel(x_ref, o_ref):
    o_ref[0] = x_ref[...].T


def _tail_transpose(x, col, *, batch, seq, keep, tr):
    per, nk = seq // tr, keep // tr
    return pl.pallas_call(
        _tail_transpose_kernel,
        grid=(batch, nk),
        in_specs=[pl.BlockSpec((tr, HW), lambda b, j: (b * per + per - nk + j, col))],
        out_specs=pl.BlockSpec((1, HW, tr), lambda b, j: (b, 0, j)),
        out_shape=jax.ShapeDtypeStruct((batch, HW, keep), F32),
        compiler_params=_cparams(("parallel", "parallel"), 16 * 1024 * 1024),
        name="tail_transpose",
    )(x)


def _pack_a_in(w):
    nl, d, _ = w.shape
    wt = jnp.swapaxes(w, 1, 2).astype(BF16)

    def pad_heads(x):
        x = x.reshape(nl, GLA_HEADS, GLA_DV, d)
        return jnp.pad(x, ((0, 0), (0, 0), (0, GLA_DVP - GLA_DV), (0, 0))).reshape(nl, GLA_VP, d)

    o_v, o_g, o_r = 2 * GLA_QK, 2 * GLA_QK + GLA_V, 2 * GLA_QK + 2 * GLA_V
    glr = jnp.pad(wt[:, o_r:o_r + GLA_RANK], ((0, 0), (0, GLA_RANKP - GLA_RANK), (0, 0)))
    packed = jnp.concatenate([wt[:, :o_v], pad_heads(wt[:, o_v:o_g]), pad_heads(wt[:, o_g:o_r]),
                              wt[:, o_r + GLA_RANK:], glr], axis=1)
    return jnp.swapaxes(packed, 1, 2)


def _pack_a_out(w):
    tok = w[:GLA_V].reshape(GLA_HEADS, GLA_DV, D_MODEL)
    tok = jnp.pad(tok, ((0, 0), (0, GLA_DVP - GLA_DV), (0, 0))).reshape(GLA_VP, D_MODEL)
    return jnp.concatenate([tok, w[GLA_V:]], axis=0).astype(BF16)


def kernel(x_prompt, x_sample, state_gla, cache_win_k, cache_win_v, cache_mem_k, cache_mem_v,
           mem_prompt, norm1, norm2, a_w_in, a_w_gate2, a_b_gate, a_g_onorm, a_w_out, kv_norm, w_kv,
           g_k, b_w_in, b_g_q, b_w_out, mem_norm, w_mem_kv, g_mem_q, g_mem_k, w_ffn_gu, w_ffn_down):
    batch, seq, d = x_prompt.shape
    dec_b = x_sample.shape[0]
    past_len = 8192
    assert d == D_MODEL and seq % 512 == 0 and x_sample.shape[1] == 1

    mall_np, lmask_np = _gla_tables()
    grp_np, hmask_np = _head_tables()
    mall = jnp.asarray(mall_np, BF16)
    lmask = jnp.asarray(lmask_np, F32)
    grp = jnp.asarray(grp_np, BF16)
    hmask = jnp.asarray(hmask_np, F32)
    band = jnp.asarray(_band_bias(), F32)

    hrow8 = jnp.asarray(np.concatenate([hmask_np[:, 0], np.zeros((4, HW), np.float32)], 0))
    hrow16 = jnp.asarray(np.concatenate([hmask_np[:, 0]] * N_GROUPS
                                        + [np.zeros((4, HW), np.float32)], 0))

    cos_p, sin_p = _rope_tables(np.arange(seq))
    cos_s, sin_s = _rope_tables(past_len + np.arange(1))

    wa_in = _pack_a_in(a_w_in)
    wa_out = jnp.stack([_pack_a_out(a_w_out[i]) for i in range(N_A)], 0)
    wg2 = [jnp.pad(a_w_gate2[i], ((0, GLA_RANKP - GLA_RANK), (0, 0))).astype(BF16)
           for i in range(N_A)]
    bg = [a_b_gate[i].reshape(1, GLA_QK) for i in range(N_A)]
    gon = [jnp.pad(a_g_onorm[i], (0, GLA_DVP - GLA_DV)).reshape(1, GLA_DVP) for i in range(N_A)]
    wb_in = b_w_in.astype(BF16)
    wb_out = b_w_out.astype(BF16)
    w_kv_b = w_kv.astype(BF16).reshape(1, d, 2 * HW)
    w_mem_b = w_mem_kv.astype(BF16)
    ffn_w = {}
    n1 = norm1.reshape(DEPTH, 1, d)
    n2 = norm2.reshape(DEPTH, 1, d)
    nkv = kv_norm.reshape(1, 1, d)

    mem_kt_p, mem_vt_p = _mem_kv(mem_prompt, mem_norm, w_mem_b, g_mem_k, grp)

    cache_len = cache_win_k.shape[1]
    win_kt = jnp.transpose(cache_win_k, (0, 2, 3, 1)).reshape(dec_b, HW, cache_len)
    win_vt = jnp.transpose(cache_win_v, (0, 2, 3, 1)).reshape(dec_b, HW, cache_len)
    mem_kt_s = jnp.transpose(cache_mem_k, (0, 1, 3, 4, 2)).reshape(DEPTH, dec_b, HW, MEM_TOKENS)
    mem_vt_s = jnp.transpose(cache_mem_v, (0, 1, 3, 4, 2)).reshape(DEPTH, dec_b, HW, MEM_TOKENS)
    state_t = jnp.swapaxes(state_gla, 3, 4)
    step_bias = jnp.asarray(_step_bias(cache_len), F32)

    def trunk(x2, bsz, t, prompt):
        n = bsz * t
        tm = 512 if prompt else n
        h = x2
        states = []
        state_out = None
        kv_sh = None
        tabs = (grp, cos_p, sin_p) if prompt else (grp, cos_s, sin_s)
        pos_blocks = max(t // tm, 1)
        for li in range(DEPTH):
            if li < N_A:
                if prompt:
                    h0t = jnp.zeros((bsz, GLA_HEADS, GLA_DVP, GLA_DK), F32)
                    tok, proj, st, wgu_b, wdn_b = _gla_prompt(
                        h, n1, wa_in, li, h0t, wg2[li], bg[li], gon[li], mall, lmask,
                        w_ffn_gu, w_ffn_down, batch=bsz, seq=t, tb=512)
                    ffn_w[li] = (wgu_b, wdn_b)
                    states.append(st[:, :, :GLA_DV])
                    mq_col = 0
                else:
                    proj = _norm_matmul(h, n1, li, wa_in, li, tm=tm, tn=A_IN_P // 3, name="a_in")
                    tok, state_out = _gla_step(proj, state_t, li, state_out, wg2[li], bg[li],
                                               gon[li])
                    mq_col = A_COL_MQ // HW
                wo, woi = wa_out, li
            else:
                bi = li - N_A
                proj = _norm_matmul_rope(h, n1, li, wb_in, bi, b_g_q[bi], tabs,
                                         rope_width=N_GROUPS * HW, tm=tm, name="b_in",
                                         pos_blocks=pos_blocks,
                                         cast=(w_ffn_gu, w_ffn_down, li) if prompt else None)
                if prompt:
                    proj, wgu_b, wdn_b = proj
                    ffn_w[li] = (wgu_b, wdn_b)
                mq_col = (N_GROUPS * HW) // HW
                if prompt:
                    tok = _band_attn(proj, kv_sh, band, hmask, batch=bsz, seq=t)
                else:
                    tok = _step_attn(proj, kv_sh, kv_sh, win_kt, win_vt, step_bias, hrow16)
                wo, woi = wb_out, bi
            if prompt:
                mem_o = _mem_attn(proj.reshape(bsz, t, proj.shape[1]), mq_col, mem_kt_p, mem_vt_p,
                                  li, g_mem_q[li], grp, hmask, tq=512).reshape(n, HW)
            else:
                mem_o = _mem_step(proj, mq_col, mem_kt_s, mem_vt_s, li, g_mem_q[li], grp, hrow8)
            h = _out_ffn(h, tok, mem_o, wo, woi, n2, li, *ffn_w[li], tm=tm)
            if li == N_A - 1:
                kv_sh = _norm_matmul_rope(h, nkv, 0, w_kv_b, 0, g_k, tabs, rope_width=HW, tm=tm,
                                          name="shared_kv", pos_blocks=pos_blocks)
        return h, states, state_out, kv_sh

    y_p, gla_p, _, kv_p = trunk(x_prompt.reshape(batch * seq, d), batch, seq, True)
    y_s, _, gla_s, kv_s = trunk(x_sample.reshape(dec_b, d), dec_b, 1, False)

    keep = min(WIN_MAX, seq)

    def window_out(x, col):
        xt = _tail_transpose(x, col, batch=batch, seq=seq, keep=keep, tr=512)
        return jnp.transpose(xt.reshape(batch, N_HEADS, HEAD_DIM, keep), (0, 3, 1, 2))

    def mem_out(xt):
        return jnp.transpose(xt.reshape(DEPTH, batch, N_HEADS, HEAD_DIM, MEM_TOKENS),
                             (0, 1, 4, 2, 3))

    return (y_p.reshape(batch, seq, d),
            y_s.reshape(dec_b, 1, d),
            jnp.swapaxes(jnp.stack(gla_p, 0), 3, 4),
            jnp.swapaxes(gla_s, 3, 4),
            window_out(kv_p, 0),
            window_out(kv_p, 1),
            kv_s[:, :HW].reshape(dec_b, 1, N_HEADS, HEAD_DIM),
            kv_s[:, HW:].reshape(dec_b, 1, N_HEADS, HEAD_DIM),
            mem_out(mem_kt_p),
            mem_out(mem_vt_p))
```

```python
import functools

import numpy as np
import jax
import jax.numpy as jnp
from jax import lax
from jax.experimental import pallas as pl
from jax.experimental.pallas import tpu as pltpu

F32 = jnp.float32
BF16 = jnp.bfloat16

D_MODEL = 1024
DEPTH = 4
N_A = 2
GLA_HEADS = 4
GLA_DK = 128
GLA_DV = 192
GLA_DVP = 256
GLA_QK = GLA_HEADS * GLA_DK
GLA_V = GLA_HEADS * GLA_DV
GLA_VP = GLA_HEADS * GLA_DVP
GLA_RANK = 16
GLA_RANKP = 128
GLA_TAU = 16.0
GLA_CHUNK = 64
HEAD_DIM = 64
N_HEADS = 4
HW = N_HEADS * HEAD_DIM
DIL_PAIRS = ((128, 1), (512, 4), (2048, 16))
N_GROUPS = 3
BAND = 128
WIN_MAX = 2048
MEM_TOKENS = 256
D_FF = 2816
ROPE_THETA = 10000.0
EPS = 1e-6
ATTN_SCALE = HEAD_DIM ** -0.5
NEG = -1e30
ROPE_SUB_ROWS = 512
LOG2E = 1.4426950408889634
LN2 = 0.6931471805599453

A_COL_Q = 0
A_COL_K = GLA_QK
A_COL_V = 2 * GLA_QK
A_COL_G = A_COL_V + GLA_VP
A_COL_MQ = A_COL_G + GLA_VP
A_COL_GLR = A_COL_MQ + HW
A_IN_P = A_COL_GLR + GLA_RANKP

V7X_VMEM_BYTES = 64 * 1024 * 1024
VMEM_CAP = 56 * 1024 * 1024


def _cparams(sem, vmem_bytes):
    return pltpu.CompilerParams(
        dimension_semantics=sem,
        vmem_limit_bytes=int(min(max(vmem_bytes, 16 * 1024 * 1024), VMEM_CAP)))


def _bdot(a, b):
    return jnp.dot(a.astype(BF16), b.astype(BF16), preferred_element_type=F32)


def _bdot_nt(a, b):
    return lax.dot_general(a.astype(BF16), b.astype(BF16), (((1,), (1,)), ((), ())),
                           preferred_element_type=F32)


def _bdot_tn(a, b):
    return lax.dot_general(a.astype(BF16), b.astype(BF16), (((0,), (0,)), ((), ())),
                           preferred_element_type=F32)


def _split(x):
    hi = x.astype(BF16)
    lo = (x - hi.astype(F32)).astype(BF16)
    return hi, lo


def _dot_exact_rhs(x, m):
    hi, lo = _split(x)
    return (jnp.dot(hi, m, preferred_element_type=F32)
            + jnp.dot(lo, m, preferred_element_type=F32))


def _dot_exact_lhs(m2, x):
    hi, lo = _split(x)
    return jnp.dot(m2, jnp.concatenate([hi, lo], axis=0), preferred_element_type=F32)


def _silu(x):
    return x * jax.nn.sigmoid(x)


def _log_sigmoid(z):
    return jnp.minimum(z, 0.0) - jnp.log1p(jnp.exp(-jnp.abs(z)))


def _gla_tables():
    c = GLA_CHUNK
    i = np.arange(c)[:, None]
    t = np.arange(c)[None, :]
    blocks = [(t <= i), (t > i)]
    masks = []
    for lvl in range(6):
        s = c >> lvl
        half = s // 2
        mid = (i // s) * s + half - 1
        second = (i % s) >= half
        m = np.where(second, (t > mid) & (t <= i), (t > i) & (t <= mid))
        blocks.append(m)
        j = t
        masks.append(((i // s) == (j // s)) & ((i % s) >= half) & ((j % s) < half))
    masks.append(i == t)
    mall = np.concatenate(blocks, axis=0).astype(np.float32)
    mall = np.concatenate([mall, mall], axis=1)
    lmask = np.stack(masks, axis=0).astype(np.float32)
    assert np.array_equal(lmask.sum(0), (t <= i).astype(np.float32))
    return mall, lmask


def _head_tables():
    lane = np.arange(HW)
    group = (lane[:, None] // HEAD_DIM == lane[None, :] // HEAD_DIM).astype(np.float32)
    hmask = (lane[None, :] // HEAD_DIM == np.arange(N_HEADS)[:, None]).astype(np.float32)
    return group, hmask.reshape(N_HEADS, 1, HW)


def _band_bias():
    i = np.arange(BAND)[:, None]
    j = np.arange(2 * BAND)[None, :]
    dist = i + BAND - j
    ok = (dist >= 0) & (dist <= BAND)
    first = ok & (j >= BAND)
    return np.where(np.stack([first, ok], 0), 0.0, NEG).astype(np.float32)


def _rope_tables(pos):
    half = HEAD_DIM // 2
    inv = ROPE_THETA ** (-np.arange(half, dtype=np.float64) / half)
    ang = np.asarray(pos, np.float64)[:, None] * inv[None, :]
    cos = np.cos(ang).astype(np.float32)
    sin = np.sin(ang).astype(np.float32)
    cos64 = np.concatenate([cos, cos], axis=-1)
    sin64 = np.concatenate([-sin, sin], axis=-1)
    return jnp.asarray(np.tile(cos64, (1, N_HEADS))), jnp.asarray(np.tile(sin64, (1, N_HEADS)))


def _norm_matmul_kernel(x_ref, g_ref, w_ref, o_ref):
    x = x_ref[...]
    ms = jnp.mean(x * x, axis=-1, keepdims=True)
    y = x * lax.rsqrt(ms + EPS) * g_ref[...]
    o_ref[...] = jnp.dot(y.astype(BF16), w_ref[...], preferred_element_type=F32)


def _norm_matmul(x, g, gi, w, wi, *, tm, tn, name):
    n, d = x.shape
    nout = w.shape[2]
    vmem = 2 * (tm * d * 4 + d * tn * 2 + tm * tn * 4) + 2 * tm * d * 4 + tm * tn * 4
    return pl.pallas_call(
        _norm_matmul_kernel,
        grid=(n // tm, nout // tn),
        in_specs=[pl.BlockSpec((tm, d), lambda i, j: (i, 0)),
                  pl.BlockSpec((None, 1, d), lambda i, j: (gi, 0, 0)),
                  pl.BlockSpec((None, d, tn), lambda i, j: (wi, 0, j))],
        out_specs=pl.BlockSpec((tm, tn), lambda i, j: (i, j)),
        out_shape=jax.ShapeDtypeStruct((n, nout), F32),
        compiler_params=_cparams(("parallel", "parallel"), vmem),
        name=name,
    )(x, g, w)


def _norm_matmul_rope_kernel(x_ref, g_ref, w_ref, hg_ref, grp_ref, cos_ref, sin_ref, *rest,
                             rope_width, cast):
    if cast:
        wgu_i, wdn_i, o_ref, wgu_o, wdn_o = rest
        wgu_o[...] = wgu_i[...].astype(BF16)
        wdn_o[...] = wdn_i[...].astype(BF16)
    else:
        o_ref, = rest
    grp = grp_ref[...]
    lane = lax.broadcasted_iota(jnp.int32, (1, HW), 1)
    first_half = (lane & (HEAD_DIM - 1)) < (HEAD_DIM // 2)
    tm = x_ref.shape[0]
    sub = min(tm, ROPE_SUB_ROWS)
    for r in range(tm // sub):
        rs = slice(r * sub, (r + 1) * sub)
        trs = rs if cos_ref.shape[0] == tm else slice(None)
        x = x_ref[rs, :]
        ms = jnp.mean(x * x, axis=-1, keepdims=True)
        xn = x * lax.rsqrt(ms + EPS) * g_ref[...]
        y = jnp.dot(xn.astype(BF16), w_ref[...], preferred_element_type=F32)
        for c in range(rope_width // HW):
            sl = slice(c * HW, (c + 1) * HW)
            yc = y[:, sl]
            ss = _dot_exact_rhs(yc * yc, grp)
            yc = yc * lax.rsqrt(ss * (1.0 / HEAD_DIM) + EPS) * hg_ref[...]
            fwd = pltpu.roll(yc, HW - HEAD_DIM // 2, 1)
            bwd = pltpu.roll(yc, HEAD_DIM // 2, 1)
            rot = jnp.where(first_half, fwd, bwd)
            o_ref[rs, sl] = yc * cos_ref[trs, :] + rot * sin_ref[trs, :]
        o_ref[rs, rope_width:] = y[:, rope_width:]


def _norm_matmul_rope(x, g, gi, w, wi, head_gain, tabs, *, rope_width, tm, name, pos_blocks=1,
                      cast=None):
    n, d = x.shape
    nout = w.shape[2]
    grp, cos, sin = tabs
    hg = jnp.tile(head_gain.reshape(1, HEAD_DIM), (1, N_HEADS))
    trow = 1 if cos.shape[0] == 1 else tm
    tab_map = (lambda i: (0, 0)) if cos.shape[0] == 1 else (lambda i: (i % pos_blocks, 0))
    vmem = (2 * (tm * d * 4 + tm * nout * 4 + 2 * trow * HW * 4) + d * nout * 2
            + 2 * tm * d * 4 + 3 * tm * nout * 4 + (8 * 1024 * 1024 if cast else 0))
    in_specs = [pl.BlockSpec((tm, d), lambda i: (i, 0)),
                pl.BlockSpec((None, 1, d), lambda i: (gi, 0, 0)),
                pl.BlockSpec((None, d, nout), lambda i: (wi, 0, 0), pipeline_mode=pl.Buffered(1)),
                pl.BlockSpec((1, HW), lambda i: (0, 0)),
                pl.BlockSpec((HW, HW), lambda i: (0, 0)),
                pl.BlockSpec((trow, HW), tab_map),
                pl.BlockSpec((trow, HW), tab_map)]
    args = [x, g, w, hg, grp, cos, sin]
    out_specs = [pl.BlockSpec((tm, nout), lambda i: (i, 0))]
    out_shape = [jax.ShapeDtypeStruct((n, nout), F32)]
    if cast is not None:
        wgu32, wdn32, cli = cast
        for w32 in (wgu32, wdn32):
            i_spec, o_spec, o_shape = _cast_specs(w32, cli, n // tm, lambda i: i)
            in_specs.append(i_spec)
            out_specs.append(o_spec)
            out_shape.append(o_shape)
            args.append(w32)
    out = pl.pallas_call(
        functools.partial(_norm_matmul_rope_kernel, rope_width=rope_width, cast=cast is not None),
        grid=(n // tm,),
        in_specs=in_specs,
        out_specs=out_specs,
        out_shape=out_shape,
        compiler_params=_cparams(("arbitrary" if cast is not None else "parallel",), vmem),
        name=name,
    )(*args)
    return out if cast is not None else out[0]


def _gla_block(q, k, v, g, la, ht, gon, mall_ref, lmask_ref, n_chunks):
    cr = GLA_CHUNK
    chunk = lambda c: slice(c * cr, (c + 1) * cr)
    la_cat = jnp.concatenate([la[chunk(c)] for c in range(n_chunks)], axis=1)
    e_all = jnp.exp2(_dot_exact_lhs(mall_ref[...], la_cat * LOG2E))

    def decay_rows(blk):
        return jnp.concatenate(
            [e_all[blk * cr:(blk + 1) * cr, c * GLA_DK:(c + 1) * GLA_DK] for c in range(n_chunks)],
            axis=0)

    eb = decay_rows(0)
    qb = (q * eb).astype(BF16)
    kdec = (k * decay_rows(1)).astype(BF16)
    q_lv = [q.astype(BF16)]
    k_lv = [k.astype(BF16)]
    for lvl in range(6):
        el = decay_rows(2 + lvl)
        q_lv.append((q * el).astype(BF16))
        k_lv.append((k * el).astype(BF16))
    o_intra, kv_new = [], []
    for c in range(n_chunks):
        r = chunk(c)
        s = lmask_ref[6] * _bdot_nt(q_lv[0][r], k_lv[0][r])
        for lvl in range(6):
            s = s + lmask_ref[lvl] * _bdot_nt(q_lv[1 + lvl][r], k_lv[1 + lvl][r])
        o_intra.append(_bdot(s, v[r]))
        kv_new.append(_bdot_tn(v[r], kdec[r]))
    outs = []
    for c in range(n_chunks):
        outs.append(o_intra[c] + _bdot_nt(qb[chunk(c)], ht))
        ht = ht * eb[c * cr + cr - 1:(c + 1) * cr, :] + kv_new[c]
    o = jnp.concatenate(outs, axis=0)
    ms = jnp.sum(o * o, axis=-1, keepdims=True) * (1.0 / GLA_DV)
    return o * lax.rsqrt(ms + EPS) * gon * _silu(g), ht


def _cast_specs(w32, li, n_steps, step_of):
    rows, cols = w32.shape[1:]
    chunk, rep = rows // n_steps, 1
    assert chunk * n_steps == rows
    while (chunk * rep) % 16:
        rep *= 2
    chunk *= rep
    return (pl.BlockSpec((None, chunk, cols), lambda *ids: (li, step_of(*ids) // rep, 0)),
            pl.BlockSpec((chunk, cols), lambda *ids: (step_of(*ids) // rep, 0)),
            jax.ShapeDtypeStruct((rows, cols), BF16))


def _gla_kernel(x_ref, g1_ref, w_ref, wg2_ref, bg_ref, gon_ref, mall_ref, lmask_ref, h0_ref,
                wgu_i, wdn_i, tok_ref, mq_ref, hout_ref, wgu_o, wdn_o, proj_s, ht_s, *, n_chunks):
    t = pl.program_id(1)
    wgu_o[...] = wgu_i[...].astype(BF16)
    wdn_o[...] = wdn_i[...].astype(BF16)

    @pl.when(t == 0)
    def _():
        ht_s[...] = h0_ref[0]

    x = x_ref[...]
    ms = jnp.mean(x * x, axis=-1, keepdims=True)
    xn = (x * lax.rsqrt(ms + EPS) * g1_ref[...]).astype(BF16)
    proj_s[...] = jnp.dot(xn, w_ref[...], preferred_element_type=F32)
    mq_ref[...] = proj_s[:, A_COL_MQ:A_COL_MQ + HW]
    z = _bdot(proj_s[:, A_COL_GLR:A_COL_GLR + GLA_RANKP], wg2_ref[...]) + bg_ref[...]
    la = _log_sigmoid(z) * (1.0 / GLA_TAU)
    gon = gon_ref[...]
    for h in range(GLA_HEADS):
        kc = slice(h * GLA_DK, (h + 1) * GLA_DK)
        vc = slice(h * GLA_DVP, (h + 1) * GLA_DVP)
        q = proj_s[:, A_COL_Q + h * GLA_DK:A_COL_Q + (h + 1) * GLA_DK] * (GLA_DK ** -0.5)
        k = proj_s[:, A_COL_K + h * GLA_DK:A_COL_K + (h + 1) * GLA_DK]
        v = proj_s[:, A_COL_V + h * GLA_DVP:A_COL_V + (h + 1) * GLA_DVP].astype(BF16)
        g = proj_s[:, A_COL_G + h * GLA_DVP:A_COL_G + (h + 1) * GLA_DVP]
        tok, ht = _gla_block(q, k, v, g, la[:, kc], ht_s[h], gon, mall_ref, lmask_ref, n_chunks)
        tok_ref[:, vc] = tok.astype(tok_ref.dtype)
        ht_s[h] = ht

    @pl.when(t == pl.num_programs(1) - 1)
    def _():
        hout_ref[0] = ht_s[...]


def _gla_prompt(x, g1, w_in, li, h0t, wg2, bg, gon, mall, lmask, wgu32, wdn32, *, batch, seq, tb):
    n, d = x.shape
    nt = seq // tb
    const2 = lambda shape: pl.BlockSpec(shape, lambda b, t: (0,) * len(shape))
    st_spec = pl.BlockSpec((1, GLA_HEADS, GLA_DVP, GLA_DK), lambda b, t: (b, 0, 0, 0))
    step_of = lambda b, t: b * nt + t
    gu_in, gu_out, gu_shape = _cast_specs(wgu32, li, batch * nt, step_of)
    dn_in, dn_out, dn_shape = _cast_specs(wdn32, li, batch * nt, step_of)
    vmem = (2 * tb * d * 4 + d * A_IN_P * 2 + tb * A_IN_P * 4 + 2 * tb * (GLA_VP * 2 + HW * 4)
            + 6 * GLA_HEADS * GLA_DVP * GLA_DK * 4 + 24 * tb * GLA_DVP * 4 + 8 * 1024 * 1024)
    return pl.pallas_call(
        functools.partial(_gla_kernel, n_chunks=tb // GLA_CHUNK),
        grid=(batch, nt),
        in_specs=[
            pl.BlockSpec((tb, d), lambda b, t: (b * nt + t, 0)),
            pl.BlockSpec((None, 1, d), lambda b, t: (li, 0, 0)),
            pl.BlockSpec((None, d, A_IN_P), lambda b, t: (li, 0, 0), pipeline_mode=pl.Buffered(1)),
            const2((GLA_RANKP, GLA_QK)), const2((1, GLA_QK)), const2((1, GLA_DVP)),
            const2(mall.shape), const2(lmask.shape), st_spec, gu_in, dn_in,
        ],
        out_specs=[
            pl.BlockSpec((tb, GLA_VP), lambda b, t: (b * nt + t, 0)),
            pl.BlockSpec((tb, HW), lambda b, t: (b * nt + t, 0)),
            st_spec, gu_out, dn_out,
        ],
        out_shape=[jax.ShapeDtypeStruct((n, GLA_VP), BF16),
                   jax.ShapeDtypeStruct((n, HW), F32),
                   jax.ShapeDtypeStruct((batch, GLA_HEADS, GLA_DVP, GLA_DK), F32),
                   gu_shape, dn_shape],
        scratch_shapes=[pltpu.VMEM((tb, A_IN_P), F32),
                        pltpu.VMEM((GLA_HEADS, GLA_DVP, GLA_DK), F32)],
        compiler_params=_cparams(("parallel", "arbitrary"), vmem),
        name="gla_prompt",
    )(x, g1, w_in, wg2, bg, gon, mall, lmask, h0t, wgu32, wdn32)


STEP_ROWS = 8


def _gla_step_kernel(q_ref, k_ref, v_ref, g_ref, glr_ref, wg2_ref, bg_ref, gon_ref, st_ref, *rest,
                     out_layer):
    tok_ref, so_ref = rest[-2:]
    nb = q_ref.shape[0]
    for other in range(so_ref.shape[0]):
        if other != out_layer:
            so_ref[other] = jnp.zeros(so_ref.shape[1:], F32)
    z = _bdot(glr_ref[...], wg2_ref[...]) + bg_ref[...]
    a = jnp.exp(_log_sigmoid(z) * (1.0 / GLA_TAU))
    q = q_ref[...] * (GLA_DK ** -0.5)
    k = k_ref[...]
    v = v_ref[...]
    g = g_ref[...]
    gon = gon_ref[...][:, :GLA_DV]
    row = lax.broadcasted_iota(jnp.int32, (nb, 1), 0)
    for h in range(GLA_HEADS):
        ks = slice(h * GLA_DK, (h + 1) * GLA_DK)
        vs = slice(h * GLA_DVP, h * GLA_DVP + GLA_DV)
        o_rows = jnp.zeros((nb, GLA_DV), F32)
        for b in range(nb):
            only_b = row == b
            outer = _bdot_tn(jnp.where(only_b, v[:, vs], 0.0), k[:, ks])
            s_new = st_ref[0, b, h] * a[b:b + 1, ks] + outer
            so_ref[out_layer, b, h] = s_new
            o_rows = o_rows + jnp.where(only_b, _bdot_nt(q[:, ks], s_new), 0.0)
        ms = jnp.sum(o_rows * o_rows, axis=-1, keepdims=True) * (1.0 / GLA_DV)
        tok_ref[:, vs] = o_rows * lax.rsqrt(ms + EPS) * gon * _silu(g[:, vs])
        tok_ref[:, h * GLA_DVP + GLA_DV:(h + 1) * GLA_DVP] = jnp.zeros(
            (nb, GLA_DVP - GLA_DV), F32)


def _gla_step(proj, state_t, li, prev_out, wg2, bg, gon):
    bsz = proj.shape[0]
    nb = STEP_ROWS
    col = lambda w, c: pl.BlockSpec((nb, w), lambda i: (i, c))
    st_spec = pl.BlockSpec((1, nb, GLA_HEADS, GLA_DV, GLA_DK), lambda i: (li, i, 0, 0, 0))
    in_specs = [col(GLA_QK, A_COL_Q // GLA_QK), col(GLA_QK, A_COL_K // GLA_QK),
                col(GLA_VP, A_COL_V // GLA_VP), col(GLA_VP, A_COL_G // GLA_VP),
                col(GLA_RANKP, A_COL_GLR // GLA_RANKP),
                pl.BlockSpec((GLA_RANKP, GLA_QK), lambda i: (0, 0)),
                pl.BlockSpec((1, GLA_QK), lambda i: (0, 0)),
                pl.BlockSpec((1, GLA_DVP), lambda i: (0, 0)),
                st_spec]
    args = [proj, proj, proj, proj, proj, wg2, bg, gon, state_t]
    if prev_out is None:
        aliases, out_layer = {}, li
        so_spec = pl.BlockSpec((state_t.shape[0], nb, GLA_HEADS, GLA_DV, GLA_DK),
                               lambda i: (0, i, 0, 0, 0))
    else:
        in_specs.append(pl.BlockSpec(memory_space=pl.ANY))
        args.append(prev_out)
        aliases, out_layer, so_spec = {len(args) - 1: 1}, 0, st_spec
    return pl.pallas_call(
        functools.partial(_gla_step_kernel, out_layer=out_layer),
        grid=(bsz // nb,),
        in_specs=in_specs,
        out_specs=[pl.BlockSpec((nb, GLA_VP), lambda i: (i, 0)), so_spec],
        out_shape=[jax.ShapeDtypeStruct((bsz, GLA_VP), F32),
                   jax.ShapeDtypeStruct(state_t.shape, F32)],
        input_output_aliases=aliases,
        compiler_params=_cparams(("parallel",), 32 * 1024 * 1024),
        name="gla_step",
    )(*args)


def _mem_qnorm(q, gq_ref, grp_ref):
    ss = _dot_exact_rhs(q * q, grp_ref[...])
    return q * lax.rsqrt(ss * (1.0 / HEAD_DIM) + EPS) * gq_ref[...] * ATTN_SCALE


def _mem_attn_kernel(q_ref, kt_ref, vt_ref, gq_ref, grp_ref, hmask_ref, o_ref):
    qn = _mem_qnorm(q_ref[0], gq_ref, grp_ref)
    tq = qn.shape[0]
    kt = kt_ref[0, 0].astype(BF16)
    vt = vt_ref[0, 0].astype(BF16)
    q4 = jnp.concatenate([qn * hmask_ref[h] for h in range(N_HEADS)], axis=0)
    s = _bdot(q4, kt)
    m = jnp.max(s, axis=-1, keepdims=True)
    p = jnp.exp(s - m)
    o4 = _bdot_nt(p, vt) / jnp.sum(p, axis=-1, keepdims=True)
    out = jnp.zeros(qn.shape, F32)
    for h in range(N_HEADS):
        out = out + hmask_ref[h] * o4[h * tq:(h + 1) * tq]
    o_ref[0] = out.astype(o_ref.dtype)


def _mem_attn(q3, qcol, kt, vt, li, gq, grp, hmask, *, tq):
    bsz, t, _ = q3.shape
    g = jnp.tile(gq.reshape(1, HEAD_DIM), (1, N_HEADS))
    vmem = 2 * (tq * HW * 4 + 2 * MEM_TOKENS * HW * 4 + tq * HW * 2) + 16 * tq * HW * 4
    kv_spec = pl.BlockSpec((1, 1, HW, MEM_TOKENS), lambda b, i: (li, b, 0, 0))
    return pl.pallas_call(
        _mem_attn_kernel,
        grid=(bsz, t // tq),
        in_specs=[pl.BlockSpec((1, tq, HW), lambda b, i: (b, i, qcol)),
                  kv_spec, kv_spec,
                  pl.BlockSpec((1, HW), lambda b, i: (0, 0)),
                  pl.BlockSpec((HW, HW), lambda b, i: (0, 0)),
                  pl.BlockSpec((N_HEADS, 1, HW), lambda b, i: (0, 0, 0))],
        out_specs=pl.BlockSpec((1, tq, HW), lambda b, i: (b, i, 0)),
        out_shape=jax.ShapeDtypeStruct((bsz, t, HW), BF16),
        compiler_params=_cparams(("parallel", "parallel"), vmem),
        name="mem_attn",
    )(q3, kt, vt, g, grp, hmask)


def _mem_step_kernel(q_ref, kt_ref, vt_ref, gq_ref, grp_ref, hrow_ref, o_ref):
    nb = q_ref.shape[0]
    qn = _mem_qnorm(q_ref[...], gq_ref, grp_ref)
    hrow = hrow_ref[...]
    row = lax.broadcasted_iota(jnp.int32, (nb, 1), 0)
    out = jnp.zeros((nb, HW), F32)
    for b in range(nb):
        q4 = hrow * qn[b:b + 1]
        s = _bdot(q4, kt_ref[0, b])
        m = jnp.max(s, axis=-1, keepdims=True)
        p = jnp.exp(s - m)
        l = jnp.sum(p, axis=-1, keepdims=True)
        o4 = _bdot_nt(p, vt_ref[0, b]) / l
        o_b = jnp.sum(hrow * o4, axis=0, keepdims=True)
        out = out + jnp.where(row == b, o_b, 0.0)
    o_ref[...] = out


def _mem_step(q, qcol, kt, vt, li, gq, grp, hrow):
    bsz = q.shape[0]
    nb = STEP_ROWS
    g = jnp.tile(gq.reshape(1, HEAD_DIM), (1, N_HEADS))
    kv_spec = pl.BlockSpec((1, nb, HW, MEM_TOKENS), lambda i: (li, i, 0, 0))
    return pl.pallas_call(
        _mem_step_kernel,
        grid=(bsz // nb,),
        in_specs=[pl.BlockSpec((nb, HW), lambda i: (i, qcol)),
                  kv_spec, kv_spec,
                  pl.BlockSpec((1, HW), lambda i: (0, 0)),
                  pl.BlockSpec((HW, HW), lambda i: (0, 0)),
                  pl.BlockSpec((8, HW), lambda i: (0, 0))],
        out_specs=pl.BlockSpec((nb, HW), lambda i: (i, 0)),
        out_shape=jax.ShapeDtypeStruct((bsz, HW), F32),
        compiler_params=_cparams(("parallel",), 24 * 1024 * 1024),
        name="mem_step",
    )(q, kt, vt, g, grp, hrow)


def _mem_kv_kernel(x_ref, gn_ref, w_ref, gk_ref, grp_ref, kt_ref, vt_ref):
    x = x_ref[0]
    ms = jnp.mean(x * x, axis=-1, keepdims=True)
    y = x * lax.rsqrt(ms + EPS) * gn_ref[0]
    kv = jnp.dot(y.astype(BF16), w_ref[0], preferred_element_type=F32)
    k = kv[:, :HW]
    ss = _dot_exact_rhs(k * k, grp_ref[...])
    k = k * lax.rsqrt(ss * (1.0 / HEAD_DIM) + EPS) * gk_ref[0]
    kt_ref[0, 0] = k.T
    vt_ref[0, 0] = kv[:, HW:].T


def _mem_kv(mem, gn, w, gk, grp):
    bsz, m, d = mem.shape
    nl = w.shape[0]
    out = jax.ShapeDtypeStruct((nl, bsz, HW, m), F32)
    o_spec = pl.BlockSpec((1, 1, HW, m), lambda l, b: (l, b, 0, 0))
    return pl.pallas_call(
        _mem_kv_kernel,
        grid=(nl, bsz),
        in_specs=[pl.BlockSpec((1, m, d), lambda l, b: (b, 0, 0)),
                  pl.BlockSpec((1, 1, d), lambda l, b: (l, 0, 0)),
                  pl.BlockSpec((1, d, 2 * HW), lambda l, b: (l, 0, 0)),
                  pl.BlockSpec((1, 1, HW), lambda l, b: (l, 0, 0)),
                  pl.BlockSpec((HW, HW), lambda l, b: (0, 0))],
        out_specs=[o_spec, o_spec],
        out_shape=[out, out],
        compiler_params=_cparams(("parallel", "parallel"), 24 * 1024 * 1024),
        name="mem_kv",
    )(mem, gn.reshape(nl, 1, d), w, jnp.tile(gk, (1, N_HEADS)).reshape(nl, 1, HW), grp)


def _band_attn_kernel(q0_ref, q1_ref, k0_ref, k1_ref, v0_ref, v1_ref, bias_ref, hmask_ref,
                      o_ref, og_s, ls_s, *, seq):
    g = pl.program_id(1)
    n_units = seq // BAND

    def run_group(gi, dil):
        nb = seq // dil // BAND
        shift = nb.bit_length() - 1
        lane = lax.broadcasted_iota(jnp.int32, (1, 128), 1)
        low_lanes = lane < HEAD_DIM

        def idx(start):
            if dil == 1:
                return pl.ds(pl.multiple_of(start, BAND), BAND)
            return pl.ds(start, BAND, stride=dil)

        def rows(ref0, ref1, start):
            return jnp.concatenate([ref0[idx(start), :], ref1[idx(start), :]], axis=1)

        def unit(u, carry):
            r = lax.shift_right_logical(u, shift)
            nblk = u & (nb - 1)
            qs = r + dil * BAND * nblk
            ks = r + dil * BAND * jnp.maximum(nblk - 1, 0)
            q = rows(q0_ref, q1_ref, qs) * (ATTN_SCALE * LOG2E)
            kk = jnp.concatenate([rows(k0_ref, k1_ref, ks), rows(k0_ref, k1_ref, qs)],
                                 axis=0).astype(BF16)
            vv = jnp.concatenate([rows(v0_ref, v1_ref, ks), rows(v0_ref, v1_ref, qs)],
                                 axis=0).astype(BF16)
            q4 = jnp.concatenate([q * hmask_ref[h] for h in range(N_HEADS)], axis=0)
            bias = bias_ref[jnp.minimum(nblk, 1)]
            s = _bdot_nt(q4, kk).reshape(N_HEADS, BAND, 2 * BAND) + bias[None]
            s = s.reshape(N_HEADS * BAND, 2 * BAND)
            m = jnp.max(s, axis=-1, keepdims=True)
            p = jnp.exp2(s - m)
            l = jnp.sum(p, axis=-1, keepdims=True)
            oe = _bdot(p, vv)
            hd = [slice(h * BAND, (h + 1) * BAND) for h in range(N_HEADS)]
            for half in range(2):
                ha, hb = hd[2 * half], hd[2 * half + 1]
                cols = slice(half * 128, (half + 1) * 128)
                o_h = jnp.where(low_lanes, oe[ha, cols], oe[hb, cols])
                l_h = jnp.where(low_lanes, l[ha], l[hb])
                m_h = jnp.where(low_lanes, m[ha], m[hb])
                og_s[2 * gi + half, idx(qs), :] = o_h / l_h
                ls_s[2 * gi + half, idx(qs), :] = (m_h + jnp.log2(l_h)) * LN2
            return carry

        lax.fori_loop(0, n_units, unit, 0, unroll=8)

    for gi, (_, dil) in enumerate(DIL_PAIRS):
        pl.when(g == gi)(functools.partial(run_group, gi, dil))

    @pl.when(g == N_GROUPS - 1)
    def _merge():
        tr = 512
        for c in range(seq // tr):
            rows = slice(c * tr, (c + 1) * tr)
            for half in range(2):
                ls = [ls_s[2 * gi + half, rows, :] for gi in range(N_GROUPS)]
                m = jnp.maximum(jnp.maximum(ls[0], ls[1]), ls[2])
                es = [jnp.exp(x - m) for x in ls]
                inv = 1.0 / (es[0] + es[1] + es[2])
                acc = jnp.zeros((tr, 128), F32)
                for gi in range(N_GROUPS):
                    acc = acc + (es[gi] * inv) * og_s[2 * gi + half, rows, :]
                o_ref[rows, half * 128:(half + 1) * 128] = acc.astype(o_ref.dtype)


def _band_attn(qd, kv, band, hmask, *, batch, seq):
    n = qd.shape[0]
    half = lambda cmap: pl.BlockSpec((seq, 128), cmap)
    qhalf = half
    vmem = (12 * seq * 128 * 4 + 2 * seq * HW * 2 + 12 * seq * 128 * 4 + 6 * 1024 * 1024)
    return pl.pallas_call(
        functools.partial(_band_attn_kernel, seq=seq),
        grid=(batch, N_GROUPS),
        in_specs=[qhalf(lambda b, g: (b, 2 * g)), qhalf(lambda b, g: (b, 2 * g + 1)),
                  half(lambda b, g: (b, 0)), half(lambda b, g: (b, 1)),
                  half(lambda b, g: (b, 2)), half(lambda b, g: (b, 3)),
                  pl.BlockSpec((2, BAND, 2 * BAND), lambda b, g: (0, 0, 0)),
                  pl.BlockSpec((N_HEADS, 1, HW), lambda b, g: (0, 0, 0))],
        out_specs=pl.BlockSpec((seq, HW), lambda b, g: (b, 0)),
        out_shape=jax.ShapeDtypeStruct((n, HW), BF16),
        scratch_shapes=[pltpu.VMEM((2 * N_GROUPS, seq, 128), F32),
                        pltpu.VMEM((2 * N_GROUPS, seq, 128), F32)],
        compiler_params=_cparams(("parallel", "arbitrary"), vmem),
        name="band_attn",
    )(qd, qd, kv, kv, kv, kv, band, hmask)


def _step_bias(cache_len):
    t = np.arange(cache_len)
    rows = np.zeros((16, cache_len), np.float32)
    for gi, (_, dil) in enumerate(DIL_PAIRS):
        keep = (t >= cache_len - dil * BAND) & ((cache_len - t) % dil == 0)
        rows[4 * gi:4 * gi + 4] = np.where(keep, 0.0, NEG)[None]
    return rows


def _step_attn_kernel(q_ref, kn_ref, vn_ref, kt_ref, vt_ref, bias_ref, hrow_ref, o_ref):
    j = pl.program_id(1)
    q = q_ref[pl.ds(j, 1), :] * ATTN_SCALE
    kn = kn_ref[pl.ds(j, 1), :]
    vn = vn_ref[pl.ds(j, 1), :]
    hrow = hrow_ref[...]
    row = lax.broadcasted_iota(jnp.int32, (16, 1), 0)
    q16 = hrow * jnp.where(row < 4, q[:, :HW], jnp.where(row < 8, q[:, HW:2 * HW], q[:, 2 * HW:]))
    s = _bdot(q16, kt_ref[0]) + bias_ref[...]
    s_n = jnp.sum(q16 * kn, axis=-1, keepdims=True)
    m = jnp.maximum(jnp.max(s, axis=-1, keepdims=True), s_n)
    p = jnp.exp(s - m)
    p_n = jnp.exp(s_n - m)
    l = jnp.sum(p, axis=-1, keepdims=True) + p_n
    o16 = (_bdot_nt(p, vt_ref[0]) + p_n * vn) / l
    lse = m + jnp.log(l)
    lg = [lse[4 * g:4 * g + 4] for g in range(N_GROUPS)]
    mm = jnp.maximum(jnp.maximum(lg[0], lg[1]), lg[2])
    es = [jnp.exp(x - mm) for x in lg]
    inv = 1.0 / (es[0] + es[1] + es[2])
    acc = jnp.zeros((4, HW), F32)
    for g in range(N_GROUPS):
        acc = acc + (es[g] * inv) * (hrow[0:4] * o16[4 * g:4 * g + 4])
    o_ref[pl.ds(j, 1), :] = jnp.sum(acc, axis=0, keepdims=True)


def _step_attn(q, kn, kvn, kt, vt, bias, hrow16):
    bsz, _, cache_len = kt.shape
    assert cache_len >= max(d for _, d in DIL_PAIRS) * BAND
    nb = STEP_ROWS
    rows = lambda w: pl.BlockSpec((nb, w), lambda i, j: (i, 0))
    cache = pl.BlockSpec((1, HW, cache_len), lambda i, j: (i * nb + j, 0, 0))
    vmem = 4 * HW * cache_len * 4 + 8 * 16 * cache_len * 4 + 4 * 1024 * 1024
    return pl.pallas_call(
        _step_attn_kernel,
        grid=(bsz // nb, nb),
        in_specs=[rows(N_GROUPS * HW), rows(HW),
                  pl.BlockSpec((nb, HW), lambda i, j: (i, 1)), cache, cache,
                  pl.BlockSpec((16, cache_len), lambda i, j: (0, 0)),
                  pl.BlockSpec((16, HW), lambda i, j: (0, 0))],
        out_specs=rows(HW),
        out_shape=jax.ShapeDtypeStruct((bsz, HW), F32),
        compiler_params=_cparams(("parallel", "arbitrary"), vmem),
        name="step_attn",
    )(q, kn, kvn, kt, vt, bias, hrow16)


def _out_ffn_kernel(h_ref, tok_ref, mem_ref, wo1_ref, wo2_ref, g2_ref, wgu_ref, wdn_ref, o_ref):
    h1 = (h_ref[...]
          + jnp.dot(tok_ref[...].astype(BF16), wo1_ref[...], preferred_element_type=F32)
          + jnp.dot(mem_ref[...].astype(BF16), wo2_ref[...], preferred_element_type=F32))
    ms = jnp.mean(h1 * h1, axis=-1, keepdims=True)
    hn = (h1 * lax.rsqrt(ms + EPS) * g2_ref[...]).astype(BF16)
    gate = jnp.dot(hn, wgu_ref[:, :D_FF], preferred_element_type=F32)
    up = jnp.dot(hn, wgu_ref[:, D_FF:], preferred_element_type=F32)
    act = (_silu(gate) * up).astype(BF16)
    o_ref[...] = h1 + jnp.dot(act, wdn_ref[...], preferred_element_type=F32)


def _out_ffn(h, tok, mem, wo, woi, g2, li, wgu, wdn, *, tm):
    n, d = h.shape
    wt = tok.shape[1]
    assert wo.shape[1] == wt + HW and wt % HW == 0
    once = pl.Buffered(1)
    vmem = ((wt + HW) * d * 2 + 3 * D_FF * d * 2
            + 2 * tm * (2 * d * 4 + (wt + HW) * 4) + 3 * tm * d * 4 + 3 * tm * D_FF * 4
            + 4 * 1024 * 1024)
    in_specs = [pl.BlockSpec((tm, d), lambda i: (i, 0)),
                pl.BlockSpec((tm, wt), lambda i: (i, 0)),
                pl.BlockSpec((tm, HW), lambda i: (i, 0)),
                pl.BlockSpec((None, wt, d), lambda i: (woi, 0, 0), pipeline_mode=once),
                pl.BlockSpec((None, HW, d), lambda i: (woi, wt // HW, 0), pipeline_mode=once),
                pl.BlockSpec((None, 1, d), lambda i: (li, 0, 0), pipeline_mode=once),
                pl.BlockSpec((d, 2 * D_FF), lambda i: (0, 0), pipeline_mode=once),
                pl.BlockSpec((D_FF, d), lambda i: (0, 0), pipeline_mode=once)]
    args = [h, tok, mem, wo, wo, g2, wgu, wdn]
    return pl.pallas_call(
        _out_ffn_kernel,
        grid=(n // tm,),
        in_specs=in_specs,
        out_specs=pl.BlockSpec((tm, d), lambda i: (i, 0)),
        out_shape=jax.ShapeDtypeStruct((n, d), F32),
        compiler_params=_cparams(("parallel",), vmem),
        name="out_ffn",
    )(*args)


def _tail_transpose_kernel(x_ref, o_ref):
    o_ref[0] = x_ref[...].T


def _tail_transpose(x, col, *, batch, seq, keep, tr):
    per, nk = seq // tr, keep // tr
    return pl.pallas_call(
        _tail_transpose_kernel,
        grid=(batch, nk),
        in_specs=[pl.BlockSpec((tr, HW), lambda b, j: (b * per + per - nk + j, col))],
        out_specs=pl.BlockSpec((1, HW, tr), lambda b, j: (b, 0, j)),
        out_shape=jax.ShapeDtypeStruct((batch, HW, keep), F32),
        compiler_params=_cparams(("parallel", "parallel"), 16 * 1024 * 1024),
        name="tail_transpose",
    )(x)


def _pack_a_in(w):
    nl, d, _ = w.shape
    wt = jnp.swapaxes(w, 1, 2).astype(BF16)

    def pad_heads(x):
        x = x.reshape(nl, GLA_HEADS, GLA_DV, d)
        return jnp.pad(x, ((0, 0), (0, 0), (0, GLA_DVP - GLA_DV), (0, 0))).reshape(nl, GLA_VP, d)

    o_v, o_g, o_r = 2 * GLA_QK, 2 * GLA_QK + GLA_V, 2 * GLA_QK + 2 * GLA_V
    glr = jnp.pad(wt[:, o_r:o_r + GLA_RANK], ((0, 0), (0, GLA_RANKP - GLA_RANK), (0, 0)))
    packed = jnp.concatenate([wt[:, :o_v], pad_heads(wt[:, o_v:o_g]), pad_heads(wt[:, o_g:o_r]),
                              wt[:, o_r + GLA_RANK:], glr], axis=1)
    return jnp.swapaxes(packed, 1, 2)


def _pack_a_out(w):
    tok = w[:GLA_V].reshape(GLA_HEADS, GLA_DV, D_MODEL)
    tok = jnp.pad(tok, ((0, 0), (0, GLA_DVP - GLA_DV), (0, 0))).reshape(GLA_VP, D_MODEL)
    return jnp.concatenate([tok, w[GLA_V:]], axis=0).astype(BF16)


def kernel(x_prompt, x_sample, state_gla, cache_win_k, cache_win_v, cache_mem_k, cache_mem_v,
           mem_prompt, norm1, norm2, a_w_in, a_w_gate2, a_b_gate, a_g_onorm, a_w_out, kv_norm, w_kv,
           g_k, b_w_in, b_g_q, b_w_out, mem_norm, w_mem_kv, g_mem_q, g_mem_k, w_ffn_gu, w_ffn_down):
    batch, seq, d = x_prompt.shape
    dec_b = x_sample.shape[0]
    past_len = 8192
    assert d == D_MODEL and seq % 512 == 0 and x_sample.shape[1] == 1

    mall_np, lmask_np = _gla_tables()
    grp_np, hmask_np = _head_tables()
    mall = jnp.asarray(mall_np, BF16)
    lmask = jnp.asarray(lmask_np, F32)
    grp = jnp.asarray(grp_np, BF16)
    hmask = jnp.asarray(hmask_np, F32)
    band = jnp.asarray(_band_bias(), F32)

    hrow8 = jnp.asarray(np.concatenate([hmask_np[:, 0], np.zeros((4, HW), np.float32)], 0))
    hrow16 = jnp.asarray(np.concatenate([hmask_np[:, 0]] * N_GROUPS
                                        + [np.zeros((4, HW), np.float32)], 0))

    cos_p, sin_p = _rope_tables(np.arange(seq))
    cos_s, sin_s = _rope_tables(past_len + np.arange(1))

    wa_in = _pack_a_in(a_w_in)
    wa_out = jnp.stack([_pack_a_out(a_w_out[i]) for i in range(N_A)], 0)
    wg2 = [jnp.pad(a_w_gate2[i], ((0, GLA_RANKP - GLA_RANK), (0, 0))).astype(BF16)
           for i in range(N_A)]
    bg = [a_b_gate[i].reshape(1, GLA_QK) for i in range(N_A)]
    gon = [jnp.pad(a_g_onorm[i], (0, GLA_DVP - GLA_DV)).reshape(1, GLA_DVP) for i in range(N_A)]
    wb_in = b_w_in.astype(BF16)
    wb_out = b_w_out.astype(BF16)
    w_kv_b = w_kv.astype(BF16).reshape(1, d, 2 * HW)
    w_mem_b = w_mem_kv.astype(BF16)
    ffn_w = {}
    n1 = norm1.reshape(DEPTH, 1, d)
    n2 = norm2.reshape(DEPTH, 1, d)
    nkv = kv_norm.reshape(1, 1, d)

    mem_kt_p, mem_vt_p = _mem_kv(mem_prompt, mem_norm, w_mem_b, g_mem_k, grp)

    cache_len = cache_win_k.shape[1]
    win_kt = jnp.transpose(cache_win_k, (0, 2, 3, 1)).reshape(dec_b, HW, cache_len)
    win_vt = jnp.transpose(cache_win_v, (0, 2, 3, 1)).reshape(dec_b, HW, cache_len)
    mem_kt_s = jnp.transpose(cache_mem_k, (0, 1, 3, 4, 2)).reshape(DEPTH, dec_b, HW, MEM_TOKENS)
    mem_vt_s = jnp.transpose(cache_mem_v, (0, 1, 3, 4, 2)).reshape(DEPTH, dec_b, HW, MEM_TOKENS)
    state_t = jnp.swapaxes(state_gla, 3, 4)
    step_bias = jnp.asarray(_step_bias(cache_len), F32)

    def trunk(x2, bsz, t, prompt):
        n = bsz * t
        tm = 512 if prompt else n
        h = x2
        states = []
        state_out = None
        kv_sh = None
        tabs = (grp, cos_p, sin_p) if prompt else (grp, cos_s, sin_s)
        pos_blocks = max(t // tm, 1)
        for li in range(DEPTH):
            if li < N_A:
                if prompt:
                    h0t = jnp.zeros((bsz, GLA_HEADS, GLA_DVP, GLA_DK), F32)
                    tok, proj, st, wgu_b, wdn_b = _gla_prompt(
                        h, n1, wa_in, li, h0t, wg2[li], bg[li], gon[li], mall, lmask,
                        w_ffn_gu, w_ffn_down, batch=bsz, seq=t, tb=512)
                    ffn_w[li] = (wgu_b, wdn_b)
                    states.append(st[:, :, :GLA_DV])
                    mq_col = 0
                else:
                    proj = _norm_matmul(h, n1, li, wa_in, li, tm=tm, tn=A_IN_P // 3, name="a_in")
                    tok, state_out = _gla_step(proj, state_t, li, state_out, wg2[li], bg[li],
                                               gon[li])
                    mq_col = A_COL_MQ // HW
                wo, woi = wa_out, li
            else:
                bi = li - N_A
                proj = _norm_matmul_rope(h, n1, li, wb_in, bi, b_g_q[bi], tabs,
                                         rope_width=N_GROUPS * HW, tm=tm, name="b_in",
                                         pos_blocks=pos_blocks,
                                         cast=(w_ffn_gu, w_ffn_down, li) if prompt else None)
                if prompt:
                    proj, wgu_b, wdn_b = proj
                    ffn_w[li] = (wgu_b, wdn_b)
                mq_col = (N_GROUPS * HW) // HW
                if prompt:
                    tok = _band_attn(proj, kv_sh, band, hmask, batch=bsz, seq=t)
                else:
                    tok = _step_attn(proj, kv_sh, kv_sh, win_kt, win_vt, step_bias, hrow16)
                wo, woi = wb_out, bi
            if prompt:
                mem_o = _mem_attn(proj.reshape(bsz, t, proj.shape[1]), mq_col, mem_kt_p, mem_vt_p,
                                  li, g_mem_q[li], grp, hmask, tq=512).reshape(n, HW)
            else:
                mem_o = _mem_step(proj, mq_col, mem_kt_s, mem_vt_s, li, g_mem_q[li], grp, hrow8)
            h = _out_ffn(h, tok, mem_o, wo, woi, n2, li, *ffn_w[li], tm=tm)
            if li == N_A - 1:
                kv_sh = _norm_matmul_rope(h, nkv, 0, w_kv_b, 0, g_k, tabs, rope_width=HW, tm=tm,
                                          name="shared_kv", pos_blocks=pos_blocks)
        return h, states, state_out, kv_sh

    y_p, gla_p, _, kv_p = trunk(x_prompt.reshape(batch * seq, d), batch, seq, True)
    y_s, _, gla_s, kv_s = trunk(x_sample.reshape(dec_b, d), dec_b, 1, False)

    keep = min(WIN_MAX, seq)

    def window_out(x, col):
        xt = _tail_transpose(x, col, batch=batch, seq=seq, keep=keep, tr=512)
        return jnp.transpose(xt.reshape(batch, N_HEADS, HEAD_DIM, keep), (0, 3, 1, 2))

    def mem_out(xt):
        return jnp.transpose(xt.reshape(DEPTH, batch, N_HEADS, HEAD_DIM, MEM_TOKENS),
                             (0, 1, 4, 2, 3))

    return (y_p.reshape(batch, seq, d),
            y_s.reshape(dec_b, 1, d),
            jnp.swapaxes(jnp.stack(gla_p, 0), 3, 4),
            jnp.swapaxes(gla_s, 3, 4),
            window_out(kv_p, 0),
            window_out(kv_p, 1),
            kv_s[:, :HW].reshape(dec_b, 1, N_HEADS, HEAD_DIM),
            kv_s[:, HW:].reshape(dec_b, 1, N_HEADS, HEAD_DIM),
            mem_out(mem_kt_p),
            mem_out(mem_vt_p))
```

```python
import functools

import numpy as np
import jax
import jax.numpy as jnp
from jax import lax
from jax.experimental import pallas as pl
from jax.experimental.pallas import tpu as pltpu

F32 = jnp.float32
BF16 = jnp.bfloat16

D_MODEL = 1024
DEPTH = 4
N_A = 2
GLA_HEADS = 4
GLA_DK = 128
GLA_DV = 192
GLA_DVP = 256
GLA_QK = GLA_HEADS * GLA_DK
GLA_V = GLA_HEADS * GLA_DV
GLA_VP = GLA_HEADS * GLA_DVP
GLA_RANK = 16
GLA_RANKP = 128
GLA_TAU = 16.0
GLA_CHUNK = 64
HEAD_DIM = 64
N_HEADS = 4
HW = N_HEADS * HEAD_DIM
DIL_PAIRS = ((128, 1), (512, 4), (2048, 16))
N_GROUPS = 3
BAND = 128
WIN_MAX = 2048
MEM_TOKENS = 256
D_FF = 2816
ROPE_THETA = 10000.0
EPS = 1e-6
ATTN_SCALE = HEAD_DIM ** -0.5
NEG = -1e30
ROPE_SUB_ROWS = 512
LOG2E = 1.4426950408889634
LN2 = 0.6931471805599453

A_COL_Q = 0
A_COL_K = GLA_QK
A_COL_V = 2 * GLA_QK
A_COL_G = A_COL_V + GLA_VP
A_COL_MQ = A_COL_G + GLA_VP
A_COL_GLR = A_COL_MQ + HW
A_IN_P = A_COL_GLR + GLA_RANKP

V7X_VMEM_BYTES = 64 * 1024 * 1024
VMEM_CAP = 56 * 1024 * 1024


def _cparams(sem, vmem_bytes):
    return pltpu.CompilerParams(
        dimension_semantics=sem,
        vmem_limit_bytes=int(min(max(vmem_bytes, 16 * 1024 * 1024), VMEM_CAP)))


def _bdot(a, b):
    return jnp.dot(a.astype(BF16), b.astype(BF16), preferred_element_type=F32)


def _bdot_nt(a, b):
    return lax.dot_general(a.astype(BF16), b.astype(BF16), (((1,), (1,)), ((), ())),
                           preferred_element_type=F32)


def _bdot_tn(a, b):
    return lax.dot_general(a.astype(BF16), b.astype(BF16), (((0,), (0,)), ((), ())),
                           preferred_element_type=F32)


def _split(x):
    hi = x.astype(BF16)
    lo = (x - hi.astype(F32)).astype(BF16)
    return hi, lo


def _dot_exact_rhs(x, m):
    hi, lo = _split(x)
    return (jnp.dot(hi, m, preferred_element_type=F32)
            + jnp.dot(lo, m, preferred_element_type=F32))


def _dot_exact_lhs(m2, x):
    hi, lo = _split(x)
    return jnp.dot(m2, jnp.concatenate([hi, lo], axis=0), preferred_element_type=F32)


def _silu(x):
    return x * jax.nn.sigmoid(x)


def _log_sigmoid(z):
    return jnp.minimum(z, 0.0) - jnp.log1p(jnp.exp(-jnp.abs(z)))


def _gla_tables():
    c = GLA_CHUNK
    i = np.arange(c)[:, None]
    t = np.arange(c)[None, :]
    blocks = [(t <= i), (t > i)]
    masks = []
    for lvl in range(6):
        s = c >> lvl
        half = s // 2
        mid = (i // s) * s + half - 1
        second = (i % s) >= half
        m = np.where(second, (t > mid) & (t <= i), (t > i) & (t <= mid))
        blocks.append(m)
        j = t
        masks.append(((i // s) == (j // s)) & ((i % s) >= half) & ((j % s) < half))
    masks.append(i == t)
    mall = np.concatenate(blocks, axis=0).astype(np.float32)
    mall = np.concatenate([mall, mall], axis=1)
    lmask = np.stack(masks, axis=0).astype(np.float32)
    assert np.array_equal(lmask.sum(0), (t <= i).astype(np.float32))
    return mall, lmask


def _head_tables():
    lane = np.arange(HW)
    group = (lane[:, None] // HEAD_DIM == lane[None, :] // HEAD_DIM).astype(np.float32)
    hmask = (lane[None, :] // HEAD_DIM == np.arange(N_HEADS)[:, None]).astype(np.float32)
    return group, hmask.reshape(N_HEADS, 1, HW)


def _band_bias():
    i = np.arange(BAND)[:, None]
    j = np.arange(2 * BAND)[None, :]
    dist = i + BAND - j
    ok = (dist >= 0) & (dist <= BAND)
    first = ok & (j >= BAND)
    return np.where(np.stack([first, ok], 0), 0.0, NEG).astype(np.float32)


def _rope_tables(pos):
    half = HEAD_DIM // 2
    inv = ROPE_THETA ** (-np.arange(half, dtype=np.float64) / half)
    ang = np.asarray(pos, np.float64)[:, None] * inv[None, :]
    cos = np.cos(ang).astype(np.float32)
    sin = np.sin(ang).astype(np.float32)
    cos64 = np.concatenate([cos, cos], axis=-1)
    sin64 = np.concatenate([-sin, sin], axis=-1)
    return jnp.asarray(np.tile(cos64, (1, N_HEADS))), jnp.asarray(np.tile(sin64, (1, N_HEADS)))


def _norm_matmul_kernel(x_ref, g_ref, w_ref, o_ref):
    x = x_ref[...]
    ms = jnp.mean(x * x, axis=-1, keepdims=True)
    y = x * lax.rsqrt(ms + EPS) * g_ref[...]
    o_ref[...] = jnp.dot(y.astype(BF16), w_ref[...], preferred_element_type=F32)


def _norm_matmul(x, g, gi, w, wi, *, tm, tn, name):
    n, d = x.shape
    nout = w.shape[2]
    vmem = 2 * (tm * d * 4 + d * tn * 2 + tm * tn * 4) + 2 * tm * d * 4 + tm * tn * 4
    return pl.pallas_call(
        _norm_matmul_kernel,
        grid=(n // tm, nout // tn),
        in_specs=[pl.BlockSpec((tm, d), lambda i, j: (i, 0)),
                  pl.BlockSpec((None, 1, d), lambda i, j: (gi, 0, 0)),
                  pl.BlockSpec((None, d, tn), lambda i, j: (wi, 0, j))],
        out_specs=pl.BlockSpec((tm, tn), lambda i, j: (i, j)),
        out_shape=jax.ShapeDtypeStruct((n, nout), F32),
        compiler_params=_cparams(("parallel", "parallel"), vmem),
        name=name,
    )(x, g, w)


def _norm_matmul_rope_kernel(x_ref, g_ref, w_ref, hg_ref, grp_ref, cos_ref, sin_ref, *rest,
                             rope_width, cast):
    if cast:
        wgu_i, wdn_i, o_ref, wgu_o, wdn_o = rest
        wgu_o[...] = wgu_i[...].astype(BF16)
        wdn_o[...] = wdn_i[...].astype(BF16)
    else:
        o_ref, = rest
    grp = grp_ref[...]
    lane = lax.broadcasted_iota(jnp.int32, (1, HW), 1)
    first_half = (lane & (HEAD_DIM - 1)) < (HEAD_DIM // 2)
    tm = x_ref.shape[0]
    sub = min(tm, ROPE_SUB_ROWS)
    for r in range(tm // sub):
        rs = slice(r * sub, (r + 1) * sub)
        trs = rs if cos_ref.shape[0] == tm else slice(None)
        x = x_ref[rs, :]
        ms = jnp.mean(x * x, axis=-1, keepdims=True)
        xn = x * lax.rsqrt(ms + EPS) * g_ref[...]
        y = jnp.dot(xn.astype(BF16), w_ref[...], preferred_element_type=F32)
        for c in range(rope_width // HW):
            sl = slice(c * HW, (c + 1) * HW)
            yc = y[:, sl]
            ss = _dot_exact_rhs(yc * yc, grp)
            yc = yc * lax.rsqrt(ss * (1.0 / HEAD_DIM) + EPS) * hg_ref[...]
            fwd = pltpu.roll(yc, HW - HEAD_DIM // 2, 1)
            bwd = pltpu.roll(yc, HEAD_DIM // 2, 1)
            rot = jnp.where(first_half, fwd, bwd)
            o_ref[rs, sl] = yc * cos_ref[trs, :] + rot * sin_ref[trs, :]
        o_ref[rs, rope_width:] = y[:, rope_width:]


def _norm_matmul_rope(x, g, gi, w, wi, hg, hgi, tabs, *, rope_width, tm, name, pos_blocks=1,
                      cast=None):
    n, d = x.shape
    nout = w.shape[2]
    grp, cos, sin = tabs
    trow = 1 if cos.shape[0] == 1 else tm
    tab_map = (lambda i: (0, 0)) if cos.shape[0] == 1 else (lambda i: (i % pos_blocks, 0))
    vmem = (2 * (tm * d * 4 + tm * nout * 4 + 2 * trow * HW * 4) + d * nout * 2
            + 2 * tm * d * 4 + 3 * tm * nout * 4 + (8 * 1024 * 1024 if cast else 0))
    in_specs = [pl.BlockSpec((tm, d), lambda i: (i, 0)),
                pl.BlockSpec((None, 1, d), lambda i: (gi, 0, 0)),
                pl.BlockSpec((None, d, nout), lambda i: (wi, 0, 0), pipeline_mode=pl.Buffered(1)),
                pl.BlockSpec((None, 1, HW), lambda i: (hgi, 0, 0)),
                pl.BlockSpec((HW, HW), lambda i: (0, 0)),
                pl.BlockSpec((trow, HW), tab_map),
                pl.BlockSpec((trow, HW), tab_map)]
    args = [x, g, w, hg, grp, cos, sin]
    out_specs = [pl.BlockSpec((tm, nout), lambda i: (i, 0))]
    out_shape = [jax.ShapeDtypeStruct((n, nout), F32)]
    if cast is not None:
        wgu32, wdn32, cli = cast
        for w32 in (wgu32, wdn32):
            i_spec, o_spec, o_shape = _cast_specs(w32, cli, n // tm, lambda i: i)
            in_specs.append(i_spec)
            out_specs.append(o_spec)
            out_shape.append(o_shape)
            args.append(w32)
    out = pl.pallas_call(
        functools.partial(_norm_matmul_rope_kernel, rope_width=rope_width, cast=cast is not None),
        grid=(n // tm,),
        in_specs=in_specs,
        out_specs=out_specs,
        out_shape=out_shape,
        compiler_params=_cparams(("arbitrary" if cast is not None else "parallel",), vmem),
        name=name,
    )(*args)
    return out if cast is not None else out[0]


def _gla_block(q, k, v, g, la, ht, gon, mall_ref, lmask_ref, n_chunks):
    cr = GLA_CHUNK
    chunk = lambda c: slice(c * cr, (c + 1) * cr)
    la_cat = jnp.concatenate([la[chunk(c)] for c in range(n_chunks)], axis=1)
    e_all = jnp.exp2(_dot_exact_lhs(mall_ref[...], la_cat * LOG2E))

    def decay_rows(blk):
        return jnp.concatenate(
            [e_all[blk * cr:(blk + 1) * cr, c * GLA_DK:(c + 1) * GLA_DK] for c in range(n_chunks)],
            axis=0)

    eb = decay_rows(0)
    qb = (q * eb).astype(BF16)
    kdec = (k * decay_rows(1)).astype(BF16)
    q_lv = [q.astype(BF16)]
    k_lv = [k.astype(BF16)]
    for lvl in range(6):
        el = decay_rows(2 + lvl)
        q_lv.append((q * el).astype(BF16))
        k_lv.append((k * el).astype(BF16))
    o_intra, kv_new = [], []
    for c in range(n_chunks):
        r = chunk(c)
        s = lmask_ref[6] * _bdot_nt(q_lv[0][r], k_lv[0][r])
        for lvl in range(6):
            s = s + lmask_ref[lvl] * _bdot_nt(q_lv[1 + lvl][r], k_lv[1 + lvl][r])
        o_intra.append(_bdot(s, v[r]))
        kv_new.append(_bdot_tn(v[r], kdec[r]))
    outs = []
    for c in range(n_chunks):
        outs.append(o_intra[c] + _bdot_nt(qb[chunk(c)], ht))
        ht = ht * eb[c * cr + cr - 1:(c + 1) * cr, :] + kv_new[c]
    o = jnp.concatenate(outs, axis=0)
    ms = jnp.sum(o * o, axis=-1, keepdims=True) * (1.0 / GLA_DV)
    return o * lax.rsqrt(ms + EPS) * gon * _silu(g), ht


def _cast_specs(w32, li, n_steps, step_of):
    rows, cols = w32.shape[1:]
    chunk, rep = rows // n_steps, 1
    assert chunk * n_steps == rows
    while (chunk * rep) % 16:
        rep *= 2
    chunk *= rep
    return (pl.BlockSpec((None, chunk, cols), lambda *ids: (li, step_of(*ids) // rep, 0)),
            pl.BlockSpec((chunk, cols), lambda *ids: (step_of(*ids) // rep, 0)),
            jax.ShapeDtypeStruct((rows, cols), BF16))


def _gla_kernel(x_ref, g1_ref, w_ref, wg2_ref, bg_ref, gon_ref, mall_ref, lmask_ref, h0_ref,
                wgu_i, wdn_i, tok_ref, mq_ref, hout_ref, wgu_o, wdn_o, proj_s, ht_s, *, n_chunks):
    t = pl.program_id(1)
    wgu_o[...] = wgu_i[...].astype(BF16)
    wdn_o[...] = wdn_i[...].astype(BF16)

    @pl.when(t == 0)
    def _():
        ht_s[...] = h0_ref[0]

    x = x_ref[...]
    ms = jnp.mean(x * x, axis=-1, keepdims=True)
    xn = (x * lax.rsqrt(ms + EPS) * g1_ref[...]).astype(BF16)
    proj_s[...] = jnp.dot(xn, w_ref[...], preferred_element_type=F32)
    mq_ref[...] = proj_s[:, A_COL_MQ:A_COL_MQ + HW]
    z = _bdot(proj_s[:, A_COL_GLR:A_COL_GLR + GLA_RANKP], wg2_ref[...]) + bg_ref[...]
    la = _log_sigmoid(z) * (1.0 / GLA_TAU)
    gon = gon_ref[...]
    for h in range(GLA_HEADS):
        kc = slice(h * GLA_DK, (h + 1) * GLA_DK)
        vc = slice(h * GLA_DVP, (h + 1) * GLA_DVP)
        q = proj_s[:, A_COL_Q + h * GLA_DK:A_COL_Q + (h + 1) * GLA_DK] * (GLA_DK ** -0.5)
        k = proj_s[:, A_COL_K + h * GLA_DK:A_COL_K + (h + 1) * GLA_DK]
        v = proj_s[:, A_COL_V + h * GLA_DVP:A_COL_V + (h + 1) * GLA_DVP].astype(BF16)
        g = proj_s[:, A_COL_G + h * GLA_DVP:A_COL_G + (h + 1) * GLA_DVP]
        tok, ht = _gla_block(q, k, v, g, la[:, kc], ht_s[h], gon, mall_ref, lmask_ref, n_chunks)
        tok_ref[:, vc] = tok.astype(tok_ref.dtype)
        ht_s[h] = ht

    @pl.when(t == pl.num_programs(1) - 1)
    def _():
        hout_ref[0] = ht_s[...]


def _gla_prompt(x, g1, w_in, li, h0t, wg2, bg, gon, mall, lmask, wgu32, wdn32, *, batch, seq, tb):
    n, d = x.shape
    nt = seq // tb
    const2 = lambda shape: pl.BlockSpec(shape, lambda b, t: (0,) * len(shape))
    st_spec = pl.BlockSpec((1, GLA_HEADS, GLA_DVP, GLA_DK), lambda b, t: (b, 0, 0, 0))
    step_of = lambda b, t: b * nt + t
    gu_in, gu_out, gu_shape = _cast_specs(wgu32, li, batch * nt, step_of)
    dn_in, dn_out, dn_shape = _cast_specs(wdn32, li, batch * nt, step_of)
    vmem = (2 * tb * d * 4 + d * A_IN_P * 2 + tb * A_IN_P * 4 + 2 * tb * (GLA_VP * 2 + HW * 4)
            + 6 * GLA_HEADS * GLA_DVP * GLA_DK * 4 + 24 * tb * GLA_DVP * 4 + 8 * 1024 * 1024)
    return pl.pallas_call(
        functools.partial(_gla_kernel, n_chunks=tb // GLA_CHUNK),
        grid=(batch, nt),
        in_specs=[
            pl.BlockSpec((tb, d), lambda b, t: (b * nt + t, 0)),
            pl.BlockSpec((None, 1, d), lambda b, t: (li, 0, 0)),
            pl.BlockSpec((None, d, A_IN_P), lambda b, t: (li, 0, 0), pipeline_mode=pl.Buffered(1)),
            pl.BlockSpec((None, GLA_RANKP, GLA_QK), lambda b, t: (li, 0, 0)),
            pl.BlockSpec((None, 1, GLA_QK), lambda b, t: (li, 0, 0)),
            pl.BlockSpec((None, 1, GLA_DVP), lambda b, t: (li, 0, 0)),
            const2(mall.shape), const2(lmask.shape), st_spec, gu_in, dn_in,
        ],
        out_specs=[
            pl.BlockSpec((tb, GLA_VP), lambda b, t: (b * nt + t, 0)),
            pl.BlockSpec((tb, HW), lambda b, t: (b * nt + t, 0)),
            st_spec, gu_out, dn_out,
        ],
        out_shape=[jax.ShapeDtypeStruct((n, GLA_VP), BF16),
                   jax.ShapeDtypeStruct((n, HW), F32),
                   jax.ShapeDtypeStruct((batch, GLA_HEADS, GLA_DVP, GLA_DK), F32),
                   gu_shape, dn_shape],
        scratch_shapes=[pltpu.VMEM((tb, A_IN_P), F32),
                        pltpu.VMEM((GLA_HEADS, GLA_DVP, GLA_DK), F32)],
        compiler_params=_cparams(("parallel", "arbitrary"), vmem),
        name="gla_prompt",
    )(x, g1, w_in, wg2, bg, gon, mall, lmask, h0t, wgu32, wdn32)


STEP_ROWS = 8


def _gla_step_kernel(q_ref, k_ref, v_ref, g_ref, glr_ref, wg2_ref, bg_ref, gon_ref, st_ref, *rest,
                     out_layer):
    tok_ref, so_ref = rest[-2:]
    nb = q_ref.shape[0]
    for other in range(so_ref.shape[0]):
        if other != out_layer:
            so_ref[other] = jnp.zeros(so_ref.shape[1:], F32)
    z = _bdot(glr_ref[...], wg2_ref[...]) + bg_ref[...]
    a = jnp.exp(_log_sigmoid(z) * (1.0 / GLA_TAU))
    q = q_ref[...] * (GLA_DK ** -0.5)
    k = k_ref[...]
    v = v_ref[...]
    g = g_ref[...]
    gon = gon_ref[...][:, :GLA_DV]
    row = lax.broadcasted_iota(jnp.int32, (nb, 1), 0)
    for h in range(GLA_HEADS):
        ks = slice(h * GLA_DK, (h + 1) * GLA_DK)
        vs = slice(h * GLA_DVP, h * GLA_DVP + GLA_DV)
        o_rows = jnp.zeros((nb, GLA_DV), F32)
        for b in range(nb):
            only_b = row == b
            outer = _bdot_tn(jnp.where(only_b, v[:, vs], 0.0), k[:, ks])
            s_new = st_ref[0, b, h] * a[b:b + 1, ks] + outer
            so_ref[out_layer, b, h] = s_new
            o_rows = o_rows + jnp.where(only_b, _bdot_nt(q[:, ks], s_new), 0.0)
        ms = jnp.sum(o_rows * o_rows, axis=-1, keepdims=True) * (1.0 / GLA_DV)
        tok_ref[:, vs] = o_rows * lax.rsqrt(ms + EPS) * gon * _silu(g[:, vs])
        tok_ref[:, h * GLA_DVP + GLA_DV:(h + 1) * GLA_DVP] = jnp.zeros(
            (nb, GLA_DVP - GLA_DV), F32)


def _gla_step(proj, state_t, li, prev_out, wg2, bg, gon):
    bsz = proj.shape[0]
    nb = STEP_ROWS
    col = lambda w, c: pl.BlockSpec((nb, w), lambda i: (i, c))
    st_spec = pl.BlockSpec((1, nb, GLA_HEADS, GLA_DV, GLA_DK), lambda i: (li, i, 0, 0, 0))
    in_specs = [col(GLA_QK, A_COL_Q // GLA_QK), col(GLA_QK, A_COL_K // GLA_QK),
                col(GLA_VP, A_COL_V // GLA_VP), col(GLA_VP, A_COL_G // GLA_VP),
                col(GLA_RANKP, A_COL_GLR // GLA_RANKP),
                pl.BlockSpec((None, GLA_RANKP, GLA_QK), lambda i: (li, 0, 0)),
                pl.BlockSpec((None, 1, GLA_QK), lambda i: (li, 0, 0)),
                pl.BlockSpec((None, 1, GLA_DVP), lambda i: (li, 0, 0)),
                st_spec]
    args = [proj, proj, proj, proj, proj, wg2, bg, gon, state_t]
    if prev_out is None:
        aliases, out_layer = {}, li
        so_spec = pl.BlockSpec((state_t.shape[0], nb, GLA_HEADS, GLA_DV, GLA_DK),
                               lambda i: (0, i, 0, 0, 0))
    else:
        in_specs.append(pl.BlockSpec(memory_space=pl.ANY))
        args.append(prev_out)
        aliases, out_layer, so_spec = {len(args) - 1: 1}, 0, st_spec
    return pl.pallas_call(
        functools.partial(_gla_step_kernel, out_layer=out_layer),
        grid=(bsz // nb,),
        in_specs=in_specs,
        out_specs=[pl.BlockSpec((nb, GLA_VP), lambda i: (i, 0)), so_spec],
        out_shape=[jax.ShapeDtypeStruct((bsz, GLA_VP), F32),
                   jax.ShapeDtypeStruct(state_t.shape, F32)],
        input_output_aliases=aliases,
        compiler_params=_cparams(("parallel",), 32 * 1024 * 1024),
        name="gla_step",
    )(*args)


def _mem_qnorm(q, gq_ref, grp_ref):
    ss = _dot_exact_rhs(q * q, grp_ref[...])
    return q * lax.rsqrt(ss * (1.0 / HEAD_DIM) + EPS) * gq_ref[...] * ATTN_SCALE


def _mem_attn_kernel(q_ref, kt_ref, vt_ref, gq_ref, grp_ref, hmask_ref, o_ref):
    qn = _mem_qnorm(q_ref[0], gq_ref, grp_ref)
    tq = qn.shape[0]
    kt = kt_ref[0, 0].astype(BF16)
    vt = vt_ref[0, 0].astype(BF16)
    q4 = jnp.concatenate([qn * hmask_ref[h] for h in range(N_HEADS)], axis=0)
    s = _bdot(q4, kt)
    m = jnp.max(s, axis=-1, keepdims=True)
    p = jnp.exp(s - m)
    o4 = _bdot_nt(p, vt) / jnp.sum(p, axis=-1, keepdims=True)
    out = jnp.zeros(qn.shape, F32)
    for h in range(N_HEADS):
        out = out + hmask_ref[h] * o4[h * tq:(h + 1) * tq]
    o_ref[0] = out.astype(o_ref.dtype)


def _mem_attn(q3, qcol, kt, vt, li, gq, grp, hmask, *, tq):
    bsz, t, _ = q3.shape
    g = gq
    vmem = 2 * (tq * HW * 4 + 2 * MEM_TOKENS * HW * 4 + tq * HW * 2) + 16 * tq * HW * 4
    kv_spec = pl.BlockSpec((1, 1, HW, MEM_TOKENS), lambda b, i: (li, b, 0, 0))
    return pl.pallas_call(
        _mem_attn_kernel,
        grid=(bsz, t // tq),
        in_specs=[pl.BlockSpec((1, tq, HW), lambda b, i: (b, i, qcol)),
                  kv_spec, kv_spec,
                  pl.BlockSpec((None, 1, HW), lambda b, i: (li, 0, 0)),
                  pl.BlockSpec((HW, HW), lambda b, i: (0, 0)),
                  pl.BlockSpec((N_HEADS, 1, HW), lambda b, i: (0, 0, 0))],
        out_specs=pl.BlockSpec((1, tq, HW), lambda b, i: (b, i, 0)),
        out_shape=jax.ShapeDtypeStruct((bsz, t, HW), BF16),
        compiler_params=_cparams(("parallel", "parallel"), vmem),
        name="mem_attn",
    )(q3, kt, vt, g, grp, hmask)


def _mem_step_kernel(q_ref, kt_ref, vt_ref, gq_ref, grp_ref, hrow_ref, o_ref):
    nb = q_ref.shape[0]
    qn = _mem_qnorm(q_ref[...], gq_ref, grp_ref)
    hrow = hrow_ref[...]
    row = lax.broadcasted_iota(jnp.int32, (nb, 1), 0)
    out = jnp.zeros((nb, HW), F32)
    for b in range(nb):
        q4 = hrow * qn[b:b + 1]
        s = _bdot(q4, kt_ref[0, b])
        m = jnp.max(s, axis=-1, keepdims=True)
        p = jnp.exp(s - m)
        l = jnp.sum(p, axis=-1, keepdims=True)
        o4 = _bdot_nt(p, vt_ref[0, b]) / l
        o_b = jnp.sum(hrow * o4, axis=0, keepdims=True)
        out = out + jnp.where(row == b, o_b, 0.0)
    o_ref[...] = out


def _mem_step(q, qcol, kt, vt, li, gq, grp, hrow):
    bsz = q.shape[0]
    nb = STEP_ROWS
    g = gq
    kv_spec = pl.BlockSpec((1, nb, HW, MEM_TOKENS), lambda i: (li, i, 0, 0))
    return pl.pallas_call(
        _mem_step_kernel,
        grid=(bsz // nb,),
        in_specs=[pl.BlockSpec((nb, HW), lambda i: (i, qcol)),
                  kv_spec, kv_spec,
                  pl.BlockSpec((None, 1, HW), lambda i: (li, 0, 0)),
                  pl.BlockSpec((HW, HW), lambda i: (0, 0)),
                  pl.BlockSpec((8, HW), lambda i: (0, 0))],
        out_specs=pl.BlockSpec((nb, HW), lambda i: (i, 0)),
        out_shape=jax.ShapeDtypeStruct((bsz, HW), F32),
        compiler_params=_cparams(("parallel",), 24 * 1024 * 1024),
        name="mem_step",
    )(q, kt, vt, g, grp, hrow)


def _mem_kv_kernel(x_ref, gn_ref, w_ref, gk_ref, grp_ref, kt_ref, vt_ref):
    x = x_ref[0]
    ms = jnp.mean(x * x, axis=-1, keepdims=True)
    y = x * lax.rsqrt(ms + EPS) * gn_ref[0]
    kv = jnp.dot(y.astype(BF16), w_ref[0], preferred_element_type=F32)
    k = kv[:, :HW]
    ss = _dot_exact_rhs(k * k, grp_ref[...])
    k = k * lax.rsqrt(ss * (1.0 / HEAD_DIM) + EPS) * gk_ref[0]
    kt_ref[0, 0] = k.T
    vt_ref[0, 0] = kv[:, HW:].T


def _mem_kv(mem, gn, w, gk, grp):
    bsz, m, d = mem.shape
    nl = w.shape[0]
    out = jax.ShapeDtypeStruct((nl, bsz, HW, m), F32)
    o_spec = pl.BlockSpec((1, 1, HW, m), lambda l, b: (l, b, 0, 0))
    return pl.pallas_call(
        _mem_kv_kernel,
        grid=(nl, bsz),
        in_specs=[pl.BlockSpec((1, m, d), lambda l, b: (b, 0, 0)),
                  pl.BlockSpec((1, 1, d), lambda l, b: (l, 0, 0)),
                  pl.BlockSpec((1, d, 2 * HW), lambda l, b: (l, 0, 0)),
                  pl.BlockSpec((1, 1, HW), lambda l, b: (l, 0, 0)),
                  pl.BlockSpec((HW, HW), lambda l, b: (0, 0))],
        out_specs=[o_spec, o_spec],
        out_shape=[out, out],
        compiler_params=_cparams(("parallel", "parallel"), 24 * 1024 * 1024),
        name="mem_kv",
    )(mem, gn.reshape(nl, 1, d), w, jnp.tile(gk, (1, N_HEADS)).reshape(nl, 1, HW), grp)


def _band_attn_kernel(q0_ref, q1_ref, k0_ref, k1_ref, v0_ref, v1_ref, bias_ref, hmask_ref,
                      o_ref, og_s, ls_s, *, seq):
    g = pl.program_id(1)
    n_units = seq // BAND

    def run_group(gi, dil):
        nb = seq // dil // BAND
        shift = nb.bit_length() - 1
        lane = lax.broadcasted_iota(jnp.int32, (1, 128), 1)
        low_lanes = lane < HEAD_DIM

        def idx(start):
            if dil == 1:
                return pl.ds(pl.multiple_of(start, BAND), BAND)
            return pl.ds(start, BAND, stride=dil)

        def rows(ref0, ref1, start):
            return jnp.concatenate([ref0[idx(start), :], ref1[idx(start), :]], axis=1)

        def unit(u, carry):
            r = lax.shift_right_logical(u, shift)
            nblk = u & (nb - 1)
            qs = r + dil * BAND * nblk
            ks = r + dil * BAND * jnp.maximum(nblk - 1, 0)
            q = rows(q0_ref, q1_ref, qs) * (ATTN_SCALE * LOG2E)
            kk = jnp.concatenate([rows(k0_ref, k1_ref, ks), rows(k0_ref, k1_ref, qs)],
                                 axis=0).astype(BF16)
            vv = jnp.concatenate([rows(v0_ref, v1_ref, ks), rows(v0_ref, v1_ref, qs)],
                                 axis=0).astype(BF16)
            q4 = jnp.concatenate([q * hmask_ref[h] for h in range(N_HEADS)], axis=0)
            bias = bias_ref[jnp.minimum(nblk, 1)]
            s = _bdot_nt(q4, kk).reshape(N_HEADS, BAND, 2 * BAND) + bias[None]
            s = s.reshape(N_HEADS * BAND, 2 * BAND)
            m = jnp.max(s, axis=-1, keepdims=True)
            p = jnp.exp2(s - m)
            l = jnp.sum(p, axis=-1, keepdims=True)
            oe = _bdot(p, vv)
            hd = [slice(h * BAND, (h + 1) * BAND) for h in range(N_HEADS)]
            for half in range(2):
                ha, hb = hd[2 * half], hd[2 * half + 1]
                cols = slice(half * 128, (half + 1) * 128)
                o_h = jnp.where(low_lanes, oe[ha, cols], oe[hb, cols])
                l_h = jnp.where(low_lanes, l[ha], l[hb])
                m_h = jnp.where(low_lanes, m[ha], m[hb])
                og_s[2 * gi + half, idx(qs), :] = o_h / l_h
                ls_s[2 * gi + half, idx(qs), :] = (m_h + jnp.log2(l_h)) * LN2
            return carry

        lax.fori_loop(0, n_units, unit, 0, unroll=8)

    for gi, (_, dil) in enumerate(DIL_PAIRS):
        pl.when(g == gi)(functools.partial(run_group, gi, dil))

    @pl.when(g == N_GROUPS - 1)
    def _merge():
        tr = 512
        for c in range(seq // tr):
            rows = slice(c * tr, (c + 1) * tr)
            for half in range(2):
                ls = [ls_s[2 * gi + half, rows, :] for gi in range(N_GROUPS)]
                m = jnp.maximum(jnp.maximum(ls[0], ls[1]), ls[2])
                es = [jnp.exp(x - m) for x in ls]
                inv = 1.0 / (es[0] + es[1] + es[2])
                acc = jnp.zeros((tr, 128), F32)
                for gi in range(N_GROUPS):
                    acc = acc + (es[gi] * inv) * og_s[2 * gi + half, rows, :]
                o_ref[rows, half * 128:(half + 1) * 128] = acc.astype(o_ref.dtype)


def _band_attn(qd, kv, band, hmask, *, batch, seq):
    n = qd.shape[0]
    half = lambda cmap: pl.BlockSpec((seq, 128), cmap)
    qhalf = half
    vmem = (12 * seq * 128 * 4 + 2 * seq * HW * 2 + 12 * seq * 128 * 4 + 6 * 1024 * 1024)
    return pl.pallas_call(
        functools.partial(_band_attn_kernel, seq=seq),
        grid=(batch, N_GROUPS),
        in_specs=[qhalf(lambda b, g: (b, 2 * g)), qhalf(lambda b, g: (b, 2 * g + 1)),
                  half(lambda b, g: (b, 0)), half(lambda b, g: (b, 1)),
                  half(lambda b, g: (b, 2)), half(lambda b, g: (b, 3)),
                  pl.BlockSpec((2, BAND, 2 * BAND), lambda b, g: (0, 0, 0)),
                  pl.BlockSpec((N_HEADS, 1, HW), lambda b, g: (0, 0, 0))],
        out_specs=pl.BlockSpec((seq, HW), lambda b, g: (b, 0)),
        out_shape=jax.ShapeDtypeStruct((n, HW), BF16),
        scratch_shapes=[pltpu.VMEM((2 * N_GROUPS, seq, 128), F32),
                        pltpu.VMEM((2 * N_GROUPS, seq, 128), F32)],
        compiler_params=_cparams(("parallel", "arbitrary"), vmem),
        name="band_attn",
    )(qd, qd, kv, kv, kv, kv, band, hmask)


def _step_bias(cache_len):
    t = np.arange(cache_len)
    rows = np.zeros((16, cache_len), np.float32)
    for gi, (_, dil) in enumerate(DIL_PAIRS):
        keep = (t >= cache_len - dil * BAND) & ((cache_len - t) % dil == 0)
        rows[4 * gi:4 * gi + 4] = np.where(keep, 0.0, NEG)[None]
    return rows


def _step_attn_kernel(q_ref, kn_ref, vn_ref, kt_ref, vt_ref, bias_ref, hrow_ref, o_ref):
    j = pl.program_id(1)
    q = q_ref[pl.ds(j, 1), :] * ATTN_SCALE
    kn = kn_ref[pl.ds(j, 1), :]
    vn = vn_ref[pl.ds(j, 1), :]
    hrow = hrow_ref[...]
    row = lax.broadcasted_iota(jnp.int32, (16, 1), 0)
    q16 = hrow * jnp.where(row < 4, q[:, :HW], jnp.where(row < 8, q[:, HW:2 * HW], q[:, 2 * HW:]))
    s = _bdot(q16, kt_ref[0]) + bias_ref[...]
    s_n = jnp.sum(q16 * kn, axis=-1, keepdims=True)
    m = jnp.maximum(jnp.max(s, axis=-1, keepdims=True), s_n)
    p = jnp.exp(s - m)
    p_n = jnp.exp(s_n - m)
    l = jnp.sum(p, axis=-1, keepdims=True) + p_n
    o16 = (_bdot_nt(p, vt_ref[0]) + p_n * vn) / l
    lse = m + jnp.log(l)
    lg = [lse[4 * g:4 * g + 4] for g in range(N_GROUPS)]
    mm = jnp.maximum(jnp.maximum(lg[0], lg[1]), lg[2])
    es = [jnp.exp(x - mm) for x in lg]
    inv = 1.0 / (es[0] + es[1] + es[2])
    acc = jnp.zeros((4, HW), F32)
    for g in range(N_GROUPS):
        acc = acc + (es[g] * inv) * (hrow[0:4] * o16[4 * g:4 * g + 4])
    o_ref[pl.ds(j, 1), :] = jnp.sum(acc, axis=0, keepdims=True)


def _step_attn(q, kn, kvn, kt, vt, bias, hrow16):
    bsz, _, cache_len = kt.shape
    assert cache_len >= max(d for _, d in DIL_PAIRS) * BAND
    nb = STEP_ROWS
    rows = lambda w: pl.BlockSpec((nb, w), lambda i, j: (i, 0))
    cache = pl.BlockSpec((1, HW, cache_len), lambda i, j: (i * nb + j, 0, 0))
    vmem = 4 * HW * cache_len * 4 + 8 * 16 * cache_len * 4 + 4 * 1024 * 1024
    return pl.pallas_call(
        _step_attn_kernel,
        grid=(bsz // nb, nb),
        in_specs=[rows(N_GROUPS * HW), rows(HW),
                  pl.BlockSpec((nb, HW), lambda i, j: (i, 1)), cache, cache,
                  pl.BlockSpec((16, cache_len), lambda i, j: (0, 0)),
                  pl.BlockSpec((16, HW), lambda i, j: (0, 0))],
        out_specs=rows(HW),
        out_shape=jax.ShapeDtypeStruct((bsz, HW), F32),
        compiler_params=_cparams(("parallel", "arbitrary"), vmem),
        name="step_attn",
    )(q, kn, kvn, kt, vt, bias, hrow16)


def _out_ffn_kernel(h_ref, tok_ref, mem_ref, wo1_ref, wo2_ref, g2_ref, wgu_ref, wdn_ref, o_ref):
    h1 = (h_ref[...]
          + jnp.dot(tok_ref[...].astype(BF16), wo1_ref[...], preferred_element_type=F32)
          + jnp.dot(mem_ref[...].astype(BF16), wo2_ref[...], preferred_element_type=F32))
    ms = jnp.mean(h1 * h1, axis=-1, keepdims=True)
    hn = (h1 * lax.rsqrt(ms + EPS) * g2_ref[...]).astype(BF16)
    gate = jnp.dot(hn, wgu_ref[:, :D_FF], preferred_element_type=F32)
    up = jnp.dot(hn, wgu_ref[:, D_FF:], preferred_element_type=F32)
    act = (_silu(gate) * up).astype(BF16)
    o_ref[...] = h1 + jnp.dot(act, wdn_ref[...], preferred_element_type=F32)


def _out_ffn(h, tok, mem, wo, woi, g2, li, wgu, wdn, *, tm):
    n, d = h.shape
    wt = tok.shape[1]
    assert wo.shape[1] == wt + HW and wt % HW == 0
    once = pl.Buffered(1)
    vmem = ((wt + HW) * d * 2 + 3 * D_FF * d * 2
            + 2 * tm * (2 * d * 4 + (wt + HW) * 4) + 3 * tm * d * 4 + 3 * tm * D_FF * 4
            + 4 * 1024 * 1024)
    in_specs = [pl.BlockSpec((tm, d), lambda i: (i, 0)),
                pl.BlockSpec((tm, wt), lambda i: (i, 0)),
                pl.BlockSpec((tm, HW), lambda i: (i, 0)),
                pl.BlockSpec((None, wt, d), lambda i: (woi, 0, 0), pipeline_mode=once),
                pl.BlockSpec((None, HW, d), lambda i: (woi, wt // HW, 0), pipeline_mode=once),
                pl.BlockSpec((None, 1, d), lambda i: (li, 0, 0), pipeline_mode=once),
                pl.BlockSpec((d, 2 * D_FF), lambda i: (0, 0), pipeline_mode=once),
                pl.BlockSpec((D_FF, d), lambda i: (0, 0), pipeline_mode=once)]
    args = [h, tok, mem, wo, wo, g2, wgu, wdn]
    return pl.pallas_call(
        _out_ffn_kernel,
        grid=(n // tm,),
        in_specs=in_specs,
        out_specs=pl.BlockSpec((tm, d), lambda i: (i, 0)),
        out_shape=jax.ShapeDtypeStruct((n, d), F32),
        compiler_params=_cparams(("parallel",), vmem),
        name="out_ffn",
    )(*args)


def _tail_transpose_kernel(x_ref, o_ref):
    o_ref[0] = x_ref[...].T


def _tail_transpose(x, col, *, batch, seq, keep, tr):
    per, nk = seq // tr, keep // tr
    return pl.pallas_call(
        _tail_transpose_kernel,
        grid=(batch, nk),
        in_specs=[pl.BlockSpec((tr, HW), lambda b, j: (b * per + per - nk + j, col))],
        out_specs=pl.BlockSpec((1, HW, tr), lambda b, j: (b, 0, j)),
        out_shape=jax.ShapeDtypeStruct((batch, HW, keep), F32),
        compiler_params=_cparams(("parallel", "parallel"), 16 * 1024 * 1024),
        name="tail_transpose",
    )(x)


def _pack_a_in(w):
    nl, d, _ = w.shape
    wt = jnp.swapaxes(w, 1, 2).astype(BF16)

    def pad_heads(x):
        x = x.reshape(nl, GLA_HEADS, GLA_DV, d)
        return jnp.pad(x, ((0, 0), (0, 0), (0, GLA_DVP - GLA_DV), (0, 0))).reshape(nl, GLA_VP, d)

    o_v, o_g, o_r = 2 * GLA_QK, 2 * GLA_QK + GLA_V, 2 * GLA_QK + 2 * GLA_V
    glr = jnp.pad(wt[:, o_r:o_r + GLA_RANK], ((0, 0), (0, GLA_RANKP - GLA_RANK), (0, 0)))
    packed = jnp.concatenate([wt[:, :o_v], pad_heads(wt[:, o_v:o_g]), pad_heads(wt[:, o_g:o_r]),
                              wt[:, o_r + GLA_RANK:], glr], axis=1)
    return jnp.swapaxes(packed, 1, 2)


def _pack_a_out(w):
    tok = w[:GLA_V].reshape(GLA_HEADS, GLA_DV, D_MODEL)
    tok = jnp.pad(tok, ((0, 0), (0, GLA_DVP - GLA_DV), (0, 0))).reshape(GLA_VP, D_MODEL)
    return jnp.concatenate([tok, w[GLA_V:]], axis=0).astype(BF16)


def kernel(x_prompt, x_sample, state_gla, cache_win_k, cache_win_v, cache_mem_k, cache_mem_v,
           mem_prompt, norm1, norm2, a_w_in, a_w_gate2, a_b_gate, a_g_onorm, a_w_out, kv_norm, w_kv,
           g_k, b_w_in, b_g_q, b_w_out, mem_norm, w_mem_kv, g_mem_q, g_mem_k, w_ffn_gu, w_ffn_down):
    batch, seq, d = x_prompt.shape
    dec_b = x_sample.shape[0]
    past_len = 8192
    assert d == D_MODEL and seq % 512 == 0 and x_sample.shape[1] == 1

    mall_np, lmask_np = _gla_tables()
    grp_np, hmask_np = _head_tables()
    mall = jnp.asarray(mall_np, BF16)
    lmask = jnp.asarray(lmask_np, F32)
    grp = jnp.asarray(grp_np, BF16)
    hmask = jnp.asarray(hmask_np, F32)
    band = jnp.asarray(_band_bias(), F32)

    hrow8 = jnp.asarray(np.concatenate([hmask_np[:, 0], np.zeros((4, HW), np.float32)], 0))
    hrow16 = jnp.asarray(np.concatenate([hmask_np[:, 0]] * N_GROUPS
                                        + [np.zeros((4, HW), np.float32)], 0))

    cos_p, sin_p = _rope_tables(np.arange(seq))
    cos_s, sin_s = _rope_tables(past_len + np.arange(1))

    wa_in = _pack_a_in(a_w_in)
    wa_out = jnp.stack([_pack_a_out(a_w_out[i]) for i in range(N_A)], 0)
    wg2 = jnp.pad(a_w_gate2, ((0, 0), (0, GLA_RANKP - GLA_RANK), (0, 0))).astype(BF16)
    bg = a_b_gate.reshape(N_A, 1, GLA_QK)
    gon = jnp.pad(a_g_onorm, ((0, 0), (0, GLA_DVP - GLA_DV))).reshape(N_A, 1, GLA_DVP)
    gq_mem = jnp.tile(g_mem_q, (1, N_HEADS)).reshape(DEPTH, 1, HW)
    hg_rope = jnp.tile(jnp.concatenate([b_g_q, g_k.reshape(1, HEAD_DIM)], 0),
                       (1, N_HEADS)).reshape(DEPTH - N_A + 1, 1, HW)
    h0_zero = jnp.zeros((batch, GLA_HEADS, GLA_DVP, GLA_DK), F32)
    wb_in = b_w_in.astype(BF16)
    wb_out = b_w_out.astype(BF16)
    w_kv_b = w_kv.astype(BF16).reshape(1, d, 2 * HW)
    w_mem_b = w_mem_kv.astype(BF16)
    ffn_w = {}
    n1 = norm1.reshape(DEPTH, 1, d)
    n2 = norm2.reshape(DEPTH, 1, d)
    nkv = kv_norm.reshape(1, 1, d)

    mem_kt_p, mem_vt_p = _mem_kv(mem_prompt, mem_norm, w_mem_b, g_mem_k, grp)

    cache_len = cache_win_k.shape[1]
    win_kt = jnp.transpose(cache_win_k, (0, 2, 3, 1)).reshape(dec_b, HW, cache_len)
    win_vt = jnp.transpose(cache_win_v, (0, 2, 3, 1)).reshape(dec_b, HW, cache_len)
    mem_kt_s = jnp.transpose(cache_mem_k, (0, 1, 3, 4, 2)).reshape(DEPTH, dec_b, HW, MEM_TOKENS)
    mem_vt_s = jnp.transpose(cache_mem_v, (0, 1, 3, 4, 2)).reshape(DEPTH, dec_b, HW, MEM_TOKENS)
    state_t = jnp.swapaxes(state_gla, 3, 4)
    step_bias = jnp.asarray(_step_bias(cache_len), F32)

    def trunk(x2, bsz, t, prompt):
        n = bsz * t
        tm = 512 if prompt else n
        h = x2
        states = []
        state_out = None
        kv_sh = None
        tabs = (grp, cos_p, sin_p) if prompt else (grp, cos_s, sin_s)
        pos_blocks = max(t // tm, 1)
        for li in range(DEPTH):
            if li < N_A:
                if prompt:
                    tok, proj, st, wgu_b, wdn_b = _gla_prompt(
                        h, n1, wa_in, li, h0_zero, wg2, bg, gon, mall, lmask,
                        w_ffn_gu, w_ffn_down, batch=bsz, seq=t, tb=512)
                    ffn_w[li] = (wgu_b, wdn_b)
                    states.append(st[:, :, :GLA_DV])
                    mq_col = 0
                else:
                    proj = _norm_matmul(h, n1, li, wa_in, li, tm=tm, tn=A_IN_P // 3, name="a_in")
                    tok, state_out = _gla_step(proj, state_t, li, state_out, wg2, bg, gon)
                    mq_col = A_COL_MQ // HW
                wo, woi = wa_out, li
            else:
                bi = li - N_A
                proj = _norm_matmul_rope(h, n1, li, wb_in, bi, hg_rope, bi, tabs,
                                         rope_width=N_GROUPS * HW, tm=tm, name="b_in",
                                         pos_blocks=pos_blocks,
                                         cast=(w_ffn_gu, w_ffn_down, li) if prompt else None)
                if prompt:
                    proj, wgu_b, wdn_b = proj
                    ffn_w[li] = (wgu_b, wdn_b)
                mq_col = (N_GROUPS * HW) // HW
                if prompt:
                    tok = _band_attn(proj, kv_sh, band, hmask, batch=bsz, seq=t)
                else:
                    tok = _step_attn(proj, kv_sh, kv_sh, win_kt, win_vt, step_bias, hrow16)
                wo, woi = wb_out, bi
            if prompt:
                mem_o = _mem_attn(proj.reshape(bsz, t, proj.shape[1]), mq_col, mem_kt_p, mem_vt_p,
                                  li, gq_mem, grp, hmask, tq=512).reshape(n, HW)
            else:
                mem_o = _mem_step(proj, mq_col, mem_kt_s, mem_vt_s, li, gq_mem, grp, hrow8)
            h = _out_ffn(h, tok, mem_o, wo, woi, n2, li, *ffn_w[li], tm=tm)
            if li == N_A - 1:
                kv_sh = _norm_matmul_rope(h, nkv, 0, w_kv_b, 0, hg_rope, DEPTH - N_A, tabs,
                                          rope_width=HW, tm=tm,
                                          name="shared_kv", pos_blocks=pos_blocks)
        return h, states, state_out, kv_sh

    y_p, gla_p, _, kv_p = trunk(x_prompt.reshape(batch * seq, d), batch, seq, True)
    y_s, _, gla_s, kv_s = trunk(x_sample.reshape(dec_b, d), dec_b, 1, False)

    keep = min(WIN_MAX, seq)

    def window_out(x, col):
        xt = _tail_transpose(x, col, batch=batch, seq=seq, keep=keep, tr=512)
        return jnp.transpose(xt.reshape(batch, N_HEADS, HEAD_DIM, keep), (0, 3, 1, 2))

    def mem_out(xt):
        return jnp.transpose(xt.reshape(DEPTH, batch, N_HEADS, HEAD_DIM, MEM_TOKENS),
                             (0, 1, 4, 2, 3))

    return (y_p.reshape(batch, seq, d),
            y_s.reshape(dec_b, 1, d),
            jnp.swapaxes(jnp.stack(gla_p, 0), 3, 4),
            jnp.swapaxes(gla_s, 3, 4),
            window_out(kv_p, 0),
            window_out(kv_p, 1),
            kv_s[:, :HW].reshape(dec_b, 1, N_HEADS, HEAD_DIM),
            kv_s[:, HW:].reshape(dec_b, 1, N_HEADS, HEAD_DIM),
            mem_out(mem_kt_p),
            mem_out(mem_vt_p))
```

```python
import functools

import numpy as np
import jax
import jax.numpy as jnp
from jax import lax
from jax.experimental import pallas as pl
from jax.experimental.pallas import tpu as pltpu

F32 = jnp.float32
BF16 = jnp.bfloat16

D_MODEL = 1024
DEPTH = 4
N_A = 2
GLA_HEADS = 4
GLA_DK = 128
GLA_DV = 192
GLA_DVP = 256
GLA_QK = GLA_HEADS * GLA_DK
GLA_V = GLA_HEADS * GLA_DV
GLA_VP = GLA_HEADS * GLA_DVP
GLA_RANK = 16
GLA_RANKP = 128
GLA_TAU = 16.0
GLA_CHUNK = 64
HEAD_DIM = 64
N_HEADS = 4
HW = N_HEADS * HEAD_DIM
DIL_PAIRS = ((128, 1), (512, 4), (2048, 16))
N_GROUPS = 3
BAND = 128
WIN_MAX = 2048
MEM_TOKENS = 256
D_FF = 2816
ROPE_THETA = 10000.0
EPS = 1e-6
ATTN_SCALE = HEAD_DIM ** -0.5
NEG = -1e30
ROPE_ROWS = 1024
ROPE_SUB_ROWS = 1024
LOG2E = 1.4426950408889634
LN2 = 0.6931471805599453

A_COL_Q = 0
A_COL_K = GLA_QK
A_COL_V = 2 * GLA_QK
A_COL_G = A_COL_V + GLA_VP
A_COL_MQ = A_COL_G + GLA_VP
A_COL_GLR = A_COL_MQ + HW
A_IN_P = A_COL_GLR + GLA_RANKP

V7X_VMEM_BYTES = 64 * 1024 * 1024
VMEM_CAP = V7X_VMEM_BYTES - 8 * 1024 * 1024


def _cparams(sem, vmem_bytes):
    return pltpu.CompilerParams(
        dimension_semantics=sem,
        vmem_limit_bytes=int(min(max(vmem_bytes, 16 * 1024 * 1024), VMEM_CAP)))


def _bdot(a, b):
    return jnp.dot(a.astype(BF16), b.astype(BF16), preferred_element_type=F32)


def _bdot_nt(a, b):
    return lax.dot_general(a.astype(BF16), b.astype(BF16), (((1,), (1,)), ((), ())),
                           preferred_element_type=F32)


def _bdot_tn(a, b):
    return lax.dot_general(a.astype(BF16), b.astype(BF16), (((0,), (0,)), ((), ())),
                           preferred_element_type=F32)


def _split(x):
    hi = x.astype(BF16)
    lo = (x - hi.astype(F32)).astype(BF16)
    return hi, lo


def _dot_exact_rhs(x, m):
    hi, lo = _split(x)
    return (jnp.dot(hi, m, preferred_element_type=F32)
            + jnp.dot(lo, m, preferred_element_type=F32))


def _dot_exact_lhs(m2, x):
    hi, lo = _split(x)
    return jnp.dot(m2, jnp.concatenate([hi, lo], axis=0), preferred_element_type=F32)


def _silu(x):
    return x * jax.nn.sigmoid(x)


def _log_sigmoid(z):
    return jnp.minimum(z, 0.0) - jnp.log1p(jnp.exp(-jnp.abs(z)))


def _gla_tables():
    c = GLA_CHUNK
    i = np.arange(c)[:, None]
    t = np.arange(c)[None, :]
    blocks = [(t <= i), (t > i)]
    masks = []
    for lvl in range(6):
        s = c >> lvl
        half = s // 2
        mid = (i // s) * s + half - 1
        second = (i % s) >= half
        m = np.where(second, (t > mid) & (t <= i), (t > i) & (t <= mid))
        blocks.append(m)
        j = t
        masks.append(((i // s) == (j // s)) & ((i % s) >= half) & ((j % s) < half))
    masks.append(i == t)
    mall = np.concatenate(blocks, axis=0).astype(np.float32)
    mall = np.concatenate([mall, mall], axis=1)
    lmask = np.stack(masks, axis=0).astype(np.float32)
    assert np.array_equal(lmask.sum(0), (t <= i).astype(np.float32))
    return mall, lmask


def _head_tables():
    lane = np.arange(HW)
    group = (lane[:, None] // HEAD_DIM == lane[None, :] // HEAD_DIM).astype(np.float32)
    hmask = (lane[None, :] // HEAD_DIM == np.arange(N_HEADS)[:, None]).astype(np.float32)
    return group, hmask.reshape(N_HEADS, 1, HW)


def _band_bias():
    i = np.arange(BAND)[:, None]
    j = np.arange(2 * BAND)[None, :]
    dist = i + BAND - j
    ok = (dist >= 0) & (dist <= BAND)
    first = ok & (j >= BAND)
    return np.where(np.stack([first, ok], 0), 0.0, NEG).astype(np.float32)


def _rope_tables(pos):
    half = HEAD_DIM // 2
    inv = ROPE_THETA ** (-np.arange(half, dtype=np.float64) / half)
    ang = np.asarray(pos, np.float64)[:, None] * inv[None, :]
    cos = np.cos(ang).astype(np.float32)
    sin = np.sin(ang).astype(np.float32)
    cos64 = np.concatenate([cos, cos], axis=-1)
    sin64 = np.concatenate([-sin, sin], axis=-1)
    return jnp.asarray(np.tile(cos64, (1, N_HEADS))), jnp.asarray(np.tile(sin64, (1, N_HEADS)))


def _norm_matmul_kernel(x_ref, g_ref, w_ref, o_ref):
    x = x_ref[...]
    ms = jnp.mean(x * x, axis=-1, keepdims=True)
    y = x * lax.rsqrt(ms + EPS) * g_ref[...]
    o_ref[...] = jnp.dot(y.astype(BF16), w_ref[...], preferred_element_type=F32)


def _norm_matmul(x, g, gi, w, wi, *, tm, tn, name):
    n, d = x.shape
    nout = w.shape[2]
    vmem = 2 * (tm * d * 4 + d * tn * 2 + tm * tn * 4) + 2 * tm * d * 4 + tm * tn * 4
    return pl.pallas_call(
        _norm_matmul_kernel,
        grid=(n // tm, nout // tn),
        in_specs=[pl.BlockSpec((tm, d), lambda i, j: (i, 0)),
                  pl.BlockSpec((None, 1, d), lambda i, j: (gi, 0, 0)),
                  pl.BlockSpec((None, d, tn), lambda i, j: (wi, 0, j))],
        out_specs=pl.BlockSpec((tm, tn), lambda i, j: (i, j)),
        out_shape=jax.ShapeDtypeStruct((n, nout), F32),
        compiler_params=_cparams(("parallel", "parallel"), vmem),
        name=name,
    )(x, g, w)


def _norm_matmul_rope_kernel(x_ref, g_ref, w_ref, hg_ref, grp_ref, cos_ref, sin_ref, *rest,
                             rope_width, cast):
    if cast:
        wgu_i, wdn_i, o_ref, wgu_o, wdn_o = rest
        wgu_o[...] = wgu_i[...].astype(BF16)
        wdn_o[...] = wdn_i[...].astype(BF16)
    else:
        o_ref, = rest
    grp = grp_ref[...]
    lane = lax.broadcasted_iota(jnp.int32, (1, HW), 1)
    first_half = (lane & (HEAD_DIM - 1)) < (HEAD_DIM // 2)
    tm = x_ref.shape[0]
    sub = min(tm, ROPE_SUB_ROWS)
    for r in range(tm // sub):
        rs = slice(r * sub, (r + 1) * sub)
        trs = rs if cos_ref.shape[0] == tm else slice(None)
        x = x_ref[rs, :]
        ms = jnp.mean(x * x, axis=-1, keepdims=True)
        xn = x * lax.rsqrt(ms + EPS) * g_ref[...]
        y = jnp.dot(xn.astype(BF16), w_ref[...], preferred_element_type=F32)
        for c in range(rope_width // HW):
            sl = slice(c * HW, (c + 1) * HW)
            yc = y[:, sl]
            ss = _dot_exact_rhs(yc * yc, grp)
            yc = yc * lax.rsqrt(ss * (1.0 / HEAD_DIM) + EPS) * hg_ref[...]
            fwd = pltpu.roll(yc, HW - HEAD_DIM // 2, 1)
            bwd = pltpu.roll(yc, HEAD_DIM // 2, 1)
            rot = jnp.where(first_half, fwd, bwd)
            o_ref[rs, sl] = yc * cos_ref[trs, :] + rot * sin_ref[trs, :]
        o_ref[rs, rope_width:] = y[:, rope_width:]


def _norm_matmul_rope(x, g, gi, w, wi, hg, hgi, tabs, *, rope_width, tm, name, pos_blocks=1,
                      cast=None):
    n, d = x.shape
    nout = w.shape[2]
    grp, cos, sin = tabs
    trow = 1 if cos.shape[0] == 1 else tm
    tab_map = (lambda i: (0, 0)) if cos.shape[0] == 1 else (lambda i: (i % pos_blocks, 0))
    vmem = (2 * (tm * d * 4 + tm * nout * 4 + 2 * trow * HW * 4) + d * nout * 2
            + 2 * tm * d * 4 + 3 * tm * nout * 4 + (8 * 1024 * 1024 if cast else 0))
    in_specs = [pl.BlockSpec((tm, d), lambda i: (i, 0)),
                pl.BlockSpec((None, 1, d), lambda i: (gi, 0, 0)),
                pl.BlockSpec((None, d, nout), lambda i: (wi, 0, 0), pipeline_mode=pl.Buffered(1)),
                pl.BlockSpec((None, 1, HW), lambda i: (hgi, 0, 0)),
                pl.BlockSpec((HW, HW), lambda i: (0, 0)),
                pl.BlockSpec((trow, HW), tab_map),
                pl.BlockSpec((trow, HW), tab_map)]
    args = [x, g, w, hg, grp, cos, sin]
    out_specs = [pl.BlockSpec((tm, nout), lambda i: (i, 0))]
    out_shape = [jax.ShapeDtypeStruct((n, nout), F32)]
    if cast is not None:
        wgu32, wdn32, cli = cast
        for w32 in (wgu32, wdn32):
            i_spec, o_spec, o_shape = _cast_specs(w32, cli, n // tm, lambda i: i)
            in_specs.append(i_spec)
            out_specs.append(o_spec)
            out_shape.append(o_shape)
            args.append(w32)
    out = pl.pallas_call(
        functools.partial(_norm_matmul_rope_kernel, rope_width=rope_width, cast=cast is not None),
        grid=(n // tm,),
        in_specs=in_specs,
        out_specs=out_specs,
        out_shape=out_shape,
        compiler_params=_cparams(("arbitrary" if cast is not None else "parallel",), vmem),
        name=name,
    )(*args)
    return out if cast is not None else out[0]


def _gla_block(q, k, v, g, la, ht, gon, mall_ref, lmask_ref, n_chunks):
    cr = GLA_CHUNK
    chunk = lambda c: slice(c * cr, (c + 1) * cr)
    la_cat = jnp.concatenate([la[chunk(c)] for c in range(n_chunks)], axis=1)
    e_all = jnp.exp2(_dot_exact_lhs(mall_ref[...], la_cat * LOG2E))

    def decay_rows(blk):
        return jnp.concatenate(
            [e_all[blk * cr:(blk + 1) * cr, c * GLA_DK:(c + 1) * GLA_DK] for c in range(n_chunks)],
            axis=0)

    eb = decay_rows(0)
    qb = (q * eb).astype(BF16)
    kdec = (k * decay_rows(1)).astype(BF16)
    q_lv = [q.astype(BF16)]
    k_lv = [k.astype(BF16)]
    for lvl in range(6):
        el = decay_rows(2 + lvl)
        q_lv.append((q * el).astype(BF16))
        k_lv.append((k * el).astype(BF16))
    o_intra, kv_new = [], []
    for c in range(n_chunks):
        r = chunk(c)
        s = lmask_ref[6] * _bdot_nt(q_lv[0][r], k_lv[0][r])
        for lvl in range(6):
            s = s + lmask_ref[lvl] * _bdot_nt(q_lv[1 + lvl][r], k_lv[1 + lvl][r])
        o_intra.append(_bdot(s, v[r]))
        kv_new.append(_bdot_tn(v[r], kdec[r]))
    outs = []
    for c in range(n_chunks):
        outs.append(o_intra[c] + _bdot_nt(qb[chunk(c)], ht))
        ht = ht * eb[c * cr + cr - 1:(c + 1) * cr, :] + kv_new[c]
    o = jnp.concatenate(outs, axis=0)
    ms = jnp.sum(o * o, axis=-1, keepdims=True) * (1.0 / GLA_DV)
    return o * lax.rsqrt(ms + EPS) * gon * _silu(g), ht


def _cast_specs(w32, li, n_steps, step_of):
    rows, cols = w32.shape[1:]
    chunk, rep = rows // n_steps, 1
    assert chunk * n_steps == rows
    while (chunk * rep) % 16:
        rep *= 2
    chunk *= rep
    return (pl.BlockSpec((None, chunk, cols), lambda *ids: (li, step_of(*ids) // rep, 0)),
            pl.BlockSpec((chunk, cols), lambda *ids: (step_of(*ids) // rep, 0)),
            jax.ShapeDtypeStruct((rows, cols), BF16))


def _gla_kernel(x_ref, g1_ref, w_ref, wg2_ref, bg_ref, gon_ref, mall_ref, lmask_ref, h0_ref,
                wgu_i, wdn_i, tok_ref, mq_ref, hout_ref, wgu_o, wdn_o, proj_s, ht_s, *, n_chunks):
    t = pl.program_id(1)
    wgu_o[...] = wgu_i[...].astype(BF16)
    wdn_o[...] = wdn_i[...].astype(BF16)

    @pl.when(t == 0)
    def _():
        ht_s[...] = h0_ref[0]

    x = x_ref[...]
    ms = jnp.mean(x * x, axis=-1, keepdims=True)
    xn = (x * lax.rsqrt(ms + EPS) * g1_ref[...]).astype(BF16)
    proj_s[...] = jnp.dot(xn, w_ref[...], preferred_element_type=F32)
    mq_ref[...] = proj_s[:, A_COL_MQ:A_COL_MQ + HW]
    z = _bdot(proj_s[:, A_COL_GLR:A_COL_GLR + GLA_RANKP], wg2_ref[...]) + bg_ref[...]
    la = _log_sigmoid(z) * (1.0 / GLA_TAU)
    gon = gon_ref[...]
    for h in range(GLA_HEADS):
        kc = slice(h * GLA_DK, (h + 1) * GLA_DK)
        vc = slice(h * GLA_DVP, (h + 1) * GLA_DVP)
        q = proj_s[:, A_COL_Q + h * GLA_DK:A_COL_Q + (h + 1) * GLA_DK] * (GLA_DK ** -0.5)
        k = proj_s[:, A_COL_K + h * GLA_DK:A_COL_K + (h + 1) * GLA_DK]
        v = proj_s[:, A_COL_V + h * GLA_DVP:A_COL_V + (h + 1) * GLA_DVP].astype(BF16)
        g = proj_s[:, A_COL_G + h * GLA_DVP:A_COL_G + (h + 1) * GLA_DVP]
        tok, ht = _gla_block(q, k, v, g, la[:, kc], ht_s[h], gon, mall_ref, lmask_ref, n_chunks)
        tok_ref[:, vc] = tok.astype(tok_ref.dtype)
        ht_s[h] = ht

    @pl.when(t == pl.num_programs(1) - 1)
    def _():
        hout_ref[0] = ht_s[...]


def _gla_prompt(x, g1, w_in, li, h0t, wg2, bg, gon, mall, lmask, wgu32, wdn32, *, batch, seq, tb):
    n, d = x.shape
    nt = seq // tb
    const2 = lambda shape: pl.BlockSpec(shape, lambda b, t: (0,) * len(shape))
    st_spec = pl.BlockSpec((1, GLA_HEADS, GLA_DVP, GLA_DK), lambda b, t: (b, 0, 0, 0))
    step_of = lambda b, t: b * nt + t
    gu_in, gu_out, gu_shape = _cast_specs(wgu32, li, batch * nt, step_of)
    dn_in, dn_out, dn_shape = _cast_specs(wdn32, li, batch * nt, step_of)
    vmem = (2 * tb * d * 4 + d * A_IN_P * 2 + tb * A_IN_P * 4 + 2 * tb * (GLA_VP * 2 + HW * 4)
            + 6 * GLA_HEADS * GLA_DVP * GLA_DK * 4 + 24 * tb * GLA_DVP * 4 + 8 * 1024 * 1024)
    return pl.pallas_call(
        functools.partial(_gla_kernel, n_chunks=tb // GLA_CHUNK),
        grid=(batch, nt),
        in_specs=[
            pl.BlockSpec((tb, d), lambda b, t: (b * nt + t, 0)),
            pl.BlockSpec((None, 1, d), lambda b, t: (li, 0, 0)),
            pl.BlockSpec((None, d, A_IN_P), lambda b, t: (li, 0, 0), pipeline_mode=pl.Buffered(1)),
            pl.BlockSpec((None, GLA_RANKP, GLA_QK), lambda b, t: (li, 0, 0)),
            pl.BlockSpec((None, 1, GLA_QK), lambda b, t: (li, 0, 0)),
            pl.BlockSpec((None, 1, GLA_DVP), lambda b, t: (li, 0, 0)),
            const2(mall.shape), const2(lmask.shape), st_spec, gu_in, dn_in,
        ],
        out_specs=[
            pl.BlockSpec((tb, GLA_VP), lambda b, t: (b * nt + t, 0)),
            pl.BlockSpec((tb, HW), lambda b, t: (b * nt + t, 0)),
            st_spec, gu_out, dn_out,
        ],
        out_shape=[jax.ShapeDtypeStruct((n, GLA_VP), BF16),
                   jax.ShapeDtypeStruct((n, HW), F32),
                   jax.ShapeDtypeStruct((batch, GLA_HEADS, GLA_DVP, GLA_DK), F32),
                   gu_shape, dn_shape],
        scratch_shapes=[pltpu.VMEM((tb, A_IN_P), F32),
                        pltpu.VMEM((GLA_HEADS, GLA_DVP, GLA_DK), F32)],
        compiler_params=_cparams(("parallel", "arbitrary"), vmem),
        name="gla_prompt",
    )(x, g1, w_in, wg2, bg, gon, mall, lmask, h0t, wgu32, wdn32)


STEP_ROWS = 8


def _gla_step_kernel(q_ref, k_ref, v_ref, g_ref, glr_ref, wg2_ref, bg_ref, gon_ref, st_ref, *rest,
                     out_layer):
    tok_ref, so_ref = rest[-2:]
    nb = q_ref.shape[0]
    for other in range(so_ref.shape[0]):
        if other != out_layer:
            so_ref[other] = jnp.zeros(so_ref.shape[1:], F32)
    z = _bdot(glr_ref[...], wg2_ref[...]) + bg_ref[...]
    a = jnp.exp(_log_sigmoid(z) * (1.0 / GLA_TAU))
    q = q_ref[...] * (GLA_DK ** -0.5)
    k = k_ref[...]
    v = v_ref[...]
    g = g_ref[...]
    gon = gon_ref[...][:, :GLA_DV]
    row = lax.broadcasted_iota(jnp.int32, (nb, 1), 0)
    for h in range(GLA_HEADS):
        ks = slice(h * GLA_DK, (h + 1) * GLA_DK)
        vs = slice(h * GLA_DVP, h * GLA_DVP + GLA_DV)
        o_rows = jnp.zeros((nb, GLA_DV), F32)
        for b in range(nb):
            only_b = row == b
            outer = _bdot_tn(jnp.where(only_b, v[:, vs], 0.0), k[:, ks])
            s_new = st_ref[0, b, h] * a[b:b + 1, ks] + outer
            so_ref[out_layer, b, h] = s_new
            o_rows = o_rows + jnp.where(only_b, _bdot_nt(q[:, ks], s_new), 0.0)
        ms = jnp.sum(o_rows * o_rows, axis=-1, keepdims=True) * (1.0 / GLA_DV)
        tok_ref[:, vs] = o_rows * lax.rsqrt(ms + EPS) * gon * _silu(g[:, vs])
        tok_ref[:, h * GLA_DVP + GLA_DV:(h + 1) * GLA_DVP] = jnp.zeros(
            (nb, GLA_DVP - GLA_DV), F32)


def _gla_step(proj, state_t, li, prev_out, wg2, bg, gon):
    bsz = proj.shape[0]
    nb = STEP_ROWS
    col = lambda w, c: pl.BlockSpec((nb, w), lambda i: (i, c))
    st_spec = pl.BlockSpec((1, nb, GLA_HEADS, GLA_DV, GLA_DK), lambda i: (li, i, 0, 0, 0))
    in_specs = [col(GLA_QK, A_COL_Q // GLA_QK), col(GLA_QK, A_COL_K // GLA_QK),
                col(GLA_VP, A_COL_V // GLA_VP), col(GLA_VP, A_COL_G // GLA_VP),
                col(GLA_RANKP, A_COL_GLR // GLA_RANKP),
                pl.BlockSpec((None, GLA_RANKP, GLA_QK), lambda i: (li, 0, 0)),
                pl.BlockSpec((None, 1, GLA_QK), lambda i: (li, 0, 0)),
                pl.BlockSpec((None, 1, GLA_DVP), lambda i: (li, 0, 0)),
                st_spec]
    args = [proj, proj, proj, proj, proj, wg2, bg, gon, state_t]
    if prev_out is None:
        aliases, out_layer = {}, li
        so_spec = pl.BlockSpec((state_t.shape[0], nb, GLA_HEADS, GLA_DV, GLA_DK),
                               lambda i: (0, i, 0, 0, 0))
    else:
        in_specs.append(pl.BlockSpec(memory_space=pl.ANY))
        args.append(prev_out)
        aliases, out_layer, so_spec = {len(args) - 1: 1}, 0, st_spec
    return pl.pallas_call(
        functools.partial(_gla_step_kernel, out_layer=out_layer),
        grid=(bsz // nb,),
        in_specs=in_specs,
        out_specs=[pl.BlockSpec((nb, GLA_VP), lambda i: (i, 0)), so_spec],
        out_shape=[jax.ShapeDtypeStruct((bsz, GLA_VP), F32),
                   jax.ShapeDtypeStruct(state_t.shape, F32)],
        input_output_aliases=aliases,
        compiler_params=_cparams(("parallel",), 32 * 1024 * 1024),
        name="gla_step",
    )(*args)


def _mem_qnorm(q, gq_ref, grp_ref):
    ss = _dot_exact_rhs(q * q, grp_ref[...])
    return q * lax.rsqrt(ss * (1.0 / HEAD_DIM) + EPS) * gq_ref[...] * ATTN_SCALE


def _mem_attn_kernel(q_ref, kt_ref, vt_ref, gq_ref, grp_ref, hmask_ref, o_ref):
    qn = _mem_qnorm(q_ref[0], gq_ref, grp_ref)
    tq = qn.shape[0]
    kt = kt_ref[0, 0].astype(BF16)
    vt = vt_ref[0, 0].astype(BF16)
    q4 = jnp.concatenate([qn * hmask_ref[h] for h in range(N_HEADS)], axis=0)
    s = _bdot(q4, kt)
    m = jnp.max(s, axis=-1, keepdims=True)
    p = jnp.exp(s - m)
    o4 = _bdot_nt(p, vt) / jnp.sum(p, axis=-1, keepdims=True)
    out = jnp.zeros(qn.shape, F32)
    for h in range(N_HEADS):
        out = out + hmask_ref[h] * o4[h * tq:(h + 1) * tq]
    o_ref[0] = out.astype(o_ref.dtype)


def _mem_attn(q3, qcol, kt, vt, li, gq, grp, hmask, *, tq):
    bsz, t, _ = q3.shape
    g = gq
    vmem = 2 * (tq * HW * 4 + 2 * MEM_TOKENS * HW * 4 + tq * HW * 2) + 16 * tq * HW * 4
    kv_spec = pl.BlockSpec((1, 1, HW, MEM_TOKENS), lambda b, i: (li, b, 0, 0))
    return pl.pallas_call(
        _mem_attn_kernel,
        grid=(bsz, t // tq),
        in_specs=[pl.BlockSpec((1, tq, HW), lambda b, i: (b, i, qcol)),
                  kv_spec, kv_spec,
                  pl.BlockSpec((None, 1, HW), lambda b, i: (li, 0, 0)),
                  pl.BlockSpec((HW, HW), lambda b, i: (0, 0)),
                  pl.BlockSpec((N_HEADS, 1, HW), lambda b, i: (0, 0, 0))],
        out_specs=pl.BlockSpec((1, tq, HW), lambda b, i: (b, i, 0)),
        out_shape=jax.ShapeDtypeStruct((bsz, t, HW), BF16),
        compiler_params=_cparams(("parallel", "parallel"), vmem),
        name="mem_attn",
    )(q3, kt, vt, g, grp, hmask)


def _mem_step_kernel(q_ref, kt_ref, vt_ref, gq_ref, grp_ref, hrow_ref, o_ref):
    nb = q_ref.shape[0]
    qn = _mem_qnorm(q_ref[...], gq_ref, grp_ref)
    hrow = hrow_ref[...]
    row = lax.broadcasted_iota(jnp.int32, (nb, 1), 0)
    out = jnp.zeros((nb, HW), F32)
    for b in range(nb):
        q4 = hrow * qn[b:b + 1]
        s = _bdot(q4, kt_ref[0, b])
        m = jnp.max(s, axis=-1, keepdims=True)
        p = jnp.exp(s - m)
        l = jnp.sum(p, axis=-1, keepdims=True)
        o4 = _bdot_nt(p, vt_ref[0, b]) / l
        o_b = jnp.sum(hrow * o4, axis=0, keepdims=True)
        out = out + jnp.where(row == b, o_b, 0.0)
    o_ref[...] = out


def _mem_step(q, qcol, kt, vt, li, gq, grp, hrow):
    bsz = q.shape[0]
    nb = STEP_ROWS
    g = gq
    kv_spec = pl.BlockSpec((1, nb, HW, MEM_TOKENS), lambda i: (li, i, 0, 0))
    return pl.pallas_call(
        _mem_step_kernel,
        grid=(bsz // nb,),
        in_specs=[pl.BlockSpec((nb, HW), lambda i: (i, qcol)),
                  kv_spec, kv_spec,
                  pl.BlockSpec((None, 1, HW), lambda i: (li, 0, 0)),
                  pl.BlockSpec((HW, HW), lambda i: (0, 0)),
                  pl.BlockSpec((8, HW), lambda i: (0, 0))],
        out_specs=pl.BlockSpec((nb, HW), lambda i: (i, 0)),
        out_shape=jax.ShapeDtypeStruct((bsz, HW), F32),
        compiler_params=_cparams(("parallel",), 24 * 1024 * 1024),
        name="mem_step",
    )(q, kt, vt, g, grp, hrow)


def _mem_kv_kernel(x_ref, gn_ref, w_ref, gk_ref, grp_ref, kt_ref, vt_ref):
    x = x_ref[0]
    ms = jnp.mean(x * x, axis=-1, keepdims=True)
    y = x * lax.rsqrt(ms + EPS) * gn_ref[0]
    kv = jnp.dot(y.astype(BF16), w_ref[0], preferred_element_type=F32)
    k = kv[:, :HW]
    ss = _dot_exact_rhs(k * k, grp_ref[...])
    k = k * lax.rsqrt(ss * (1.0 / HEAD_DIM) + EPS) * gk_ref[0]
    kt_ref[0, 0] = k.T
    vt_ref[0, 0] = kv[:, HW:].T


def _mem_kv(mem, gn, w, gk, grp):
    bsz, m, d = mem.shape
    nl = w.shape[0]
    out = jax.ShapeDtypeStruct((nl, bsz, HW, m), F32)
    o_spec = pl.BlockSpec((1, 1, HW, m), lambda l, b: (l, b, 0, 0))
    return pl.pallas_call(
        _mem_kv_kernel,
        grid=(nl, bsz),
        in_specs=[pl.BlockSpec((1, m, d), lambda l, b: (b, 0, 0)),
                  pl.BlockSpec((1, 1, d), lambda l, b: (l, 0, 0)),
                  pl.BlockSpec((1, d, 2 * HW), lambda l, b: (l, 0, 0)),
                  pl.BlockSpec((1, 1, HW), lambda l, b: (l, 0, 0)),
                  pl.BlockSpec((HW, HW), lambda l, b: (0, 0))],
        out_specs=[o_spec, o_spec],
        out_shape=[out, out],
        compiler_params=_cparams(("parallel", "parallel"), 24 * 1024 * 1024),
        name="mem_kv",
    )(mem, gn.reshape(nl, 1, d), w, jnp.tile(gk, (1, N_HEADS)).reshape(nl, 1, HW), grp)


def _band_attn_kernel(q0_ref, q1_ref, k0_ref, k1_ref, v0_ref, v1_ref, bias_ref, hmask_ref,
                      o_ref, og_s, ls_s, *, seq):
    g = pl.program_id(1)
    n_units = seq // BAND

    def run_group(gi, dil):
        nb = seq // dil // BAND
        shift = nb.bit_length() - 1
        lane = lax.broadcasted_iota(jnp.int32, (1, 128), 1)
        low_lanes = lane < HEAD_DIM

        def idx(start):
            if dil == 1:
                return pl.ds(pl.multiple_of(start, BAND), BAND)
            return pl.ds(start, BAND, stride=dil)

        def rows(ref0, ref1, start):
            return jnp.concatenate([ref0[idx(start), :], ref1[idx(start), :]], axis=1)

        def unit(u, carry):
            r = lax.shift_right_logical(u, shift)
            nblk = u & (nb - 1)
            qs = r + dil * BAND * nblk
            ks = r + dil * BAND * jnp.maximum(nblk - 1, 0)
            q = rows(q0_ref, q1_ref, qs) * (ATTN_SCALE * LOG2E)
            kk = jnp.concatenate([rows(k0_ref, k1_ref, ks), rows(k0_ref, k1_ref, qs)],
                                 axis=0).astype(BF16)
            vv = jnp.concatenate([rows(v0_ref, v1_ref, ks), rows(v0_ref, v1_ref, qs)],
                                 axis=0).astype(BF16)
            q4 = jnp.concatenate([q * hmask_ref[h] for h in range(N_HEADS)], axis=0)
            bias = bias_ref[jnp.minimum(nblk, 1)]
            s = _bdot_nt(q4, kk).reshape(N_HEADS, BAND, 2 * BAND) + bias[None]
            s = s.reshape(N_HEADS * BAND, 2 * BAND)
            m = jnp.max(s, axis=-1, keepdims=True)
            p = jnp.exp2(s - m)
            l = jnp.sum(p, axis=-1, keepdims=True)
            oe = _bdot(p, vv)
            hd = [slice(h * BAND, (h + 1) * BAND) for h in range(N_HEADS)]
            for half in range(2):
                ha, hb = hd[2 * half], hd[2 * half + 1]
                cols = slice(half * 128, (half + 1) * 128)
                o_h = jnp.where(low_lanes, oe[ha, cols], oe[hb, cols])
                l_h = jnp.where(low_lanes, l[ha], l[hb])
                m_h = jnp.where(low_lanes, m[ha], m[hb])
                og_s[2 * gi + half, idx(qs), :] = o_h / l_h
                ls_s[2 * gi + half, idx(qs), :] = (m_h + jnp.log2(l_h)) * LN2
            return carry

        lax.fori_loop(0, n_units, unit, 0, unroll=8)

    for gi, (_, dil) in enumerate(DIL_PAIRS):
        pl.when(g == gi)(functools.partial(run_group, gi, dil))

    @pl.when(g == N_GROUPS - 1)
    def _merge():
        tr = 512
        for c in range(seq // tr):
            rows = slice(c * tr, (c + 1) * tr)
            for half in range(2):
                ls = [ls_s[2 * gi + half, rows, :] for gi in range(N_GROUPS)]
                m = jnp.maximum(jnp.maximum(ls[0], ls[1]), ls[2])
                es = [jnp.exp(x - m) for x in ls]
                inv = 1.0 / (es[0] + es[1] + es[2])
                acc = jnp.zeros((tr, 128), F32)
                for gi in range(N_GROUPS):
                    acc = acc + (es[gi] * inv) * og_s[2 * gi + half, rows, :]
                o_ref[rows, half * 128:(half + 1) * 128] = acc.astype(o_ref.dtype)


def _band_attn(qd, kv, band, hmask, *, batch, seq):
    n = qd.shape[0]
    half = lambda cmap: pl.BlockSpec((seq, 128), cmap)
    qhalf = half
    vmem = (12 * seq * 128 * 4 + 2 * seq * HW * 2 + 12 * seq * 128 * 4 + 6 * 1024 * 1024)
    return pl.pallas_call(
        functools.partial(_band_attn_kernel, seq=seq),
        grid=(batch, N_GROUPS),
        in_specs=[qhalf(lambda b, g: (b, 2 * g)), qhalf(lambda b, g: (b, 2 * g + 1)),
                  half(lambda b, g: (b, 0)), half(lambda b, g: (b, 1)),
                  half(lambda b, g: (b, 2)), half(lambda b, g: (b, 3)),
                  pl.BlockSpec((2, BAND, 2 * BAND), lambda b, g: (0, 0, 0)),
                  pl.BlockSpec((N_HEADS, 1, HW), lambda b, g: (0, 0, 0))],
        out_specs=pl.BlockSpec((seq, HW), lambda b, g: (b, 0)),
        out_shape=jax.ShapeDtypeStruct((n, HW), BF16),
        scratch_shapes=[pltpu.VMEM((2 * N_GROUPS, seq, 128), F32),
                        pltpu.VMEM((2 * N_GROUPS, seq, 128), F32)],
        compiler_params=_cparams(("parallel", "arbitrary"), vmem),
        name="band_attn",
    )(qd, qd, kv, kv, kv, kv, band, hmask)


def _step_bias(cache_len):
    t = np.arange(cache_len)
    rows = np.zeros((16, cache_len), np.float32)
    for gi, (_, dil) in enumerate(DIL_PAIRS):
        keep = (t >= cache_len - dil * BAND) & ((cache_len - t) % dil == 0)
        rows[4 * gi:4 * gi + 4] = np.where(keep, 0.0, NEG)[None]
    return rows


def _step_attn_kernel(q_ref, kn_ref, vn_ref, kt_ref, vt_ref, bias_ref, hrow_ref, o_ref):
    j = pl.program_id(1)
    q = q_ref[pl.ds(j, 1), :] * ATTN_SCALE
    kn = kn_ref[pl.ds(j, 1), :]
    vn = vn_ref[pl.ds(j, 1), :]
    hrow = hrow_ref[...]
    row = lax.broadcasted_iota(jnp.int32, (16, 1), 0)
    q16 = hrow * jnp.where(row < 4, q[:, :HW], jnp.where(row < 8, q[:, HW:2 * HW], q[:, 2 * HW:]))
    s = _bdot(q16, kt_ref[0]) + bias_ref[...]
    s_n = jnp.sum(q16 * kn, axis=-1, keepdims=True)
    m = jnp.maximum(jnp.max(s, axis=-1, keepdims=True), s_n)
    p = jnp.exp(s - m)
    p_n = jnp.exp(s_n - m)
    l = jnp.sum(p, axis=-1, keepdims=True) + p_n
    o16 = (_bdot_nt(p, vt_ref[0]) + p_n * vn) / l
    lse = m + jnp.log(l)
    lg = [lse[4 * g:4 * g + 4] for g in range(N_GROUPS)]
    mm = jnp.maximum(jnp.maximum(lg[0], lg[1]), lg[2])
    es = [jnp.exp(x - mm) for x in lg]
    inv = 1.0 / (es[0] + es[1] + es[2])
    acc = jnp.zeros((4, HW), F32)
    for g in range(N_GROUPS):
        acc = acc + (es[g] * inv) * (hrow[0:4] * o16[4 * g:4 * g + 4])
    o_ref[pl.ds(j, 1), :] = jnp.sum(acc, axis=0, keepdims=True)


def _step_attn(q, kn, kvn, kt, vt, bias, hrow16):
    bsz, _, cache_len = kt.shape
    assert cache_len >= max(d for _, d in DIL_PAIRS) * BAND
    nb = STEP_ROWS
    rows = lambda w: pl.BlockSpec((nb, w), lambda i, j: (i, 0))
    cache = pl.BlockSpec((1, HW, cache_len), lambda i, j: (i * nb + j, 0, 0))
    vmem = 4 * HW * cache_len * 4 + 8 * 16 * cache_len * 4 + 4 * 1024 * 1024
    return pl.pallas_call(
        _step_attn_kernel,
        grid=(bsz // nb, nb),
        in_specs=[rows(N_GROUPS * HW), rows(HW),
                  pl.BlockSpec((nb, HW), lambda i, j: (i, 1)), cache, cache,
                  pl.BlockSpec((16, cache_len), lambda i, j: (0, 0)),
                  pl.BlockSpec((16, HW), lambda i, j: (0, 0))],
        out_specs=rows(HW),
        out_shape=jax.ShapeDtypeStruct((bsz, HW), F32),
        compiler_params=_cparams(("parallel", "arbitrary"), vmem),
        name="step_attn",
    )(q, kn, kvn, kt, vt, bias, hrow16)


def _out_ffn_kernel(h_ref, tok_ref, mem_ref, wo1_ref, wo2_ref, g2_ref, wgu_ref, wdn_ref, o_ref):
    h1 = (h_ref[...]
          + jnp.dot(tok_ref[...].astype(BF16), wo1_ref[...], preferred_element_type=F32)
          + jnp.dot(mem_ref[...].astype(BF16), wo2_ref[...], preferred_element_type=F32))
    ms = jnp.mean(h1 * h1, axis=-1, keepdims=True)
    hn = (h1 * lax.rsqrt(ms + EPS) * g2_ref[...]).astype(BF16)
    gate = jnp.dot(hn, wgu_ref[:, :D_FF], preferred_element_type=F32)
    up = jnp.dot(hn, wgu_ref[:, D_FF:], preferred_element_type=F32)
    act = (_silu(gate) * up).astype(BF16)
    o_ref[...] = h1 + jnp.dot(act, wdn_ref[...], preferred_element_type=F32)


def _out_ffn(h, tok, mem, wo, woi, g2, li, wgu, wdn, *, tm):
    n, d = h.shape
    wt = tok.shape[1]
    assert wo.shape[1] == wt + HW and wt % HW == 0
    once = pl.Buffered(1)
    vmem = ((wt + HW) * d * 2 + 3 * D_FF * d * 2
            + 2 * tm * (2 * d * 4 + (wt + HW) * 4) + 3 * tm * d * 4 + 3 * tm * D_FF * 4
            + 4 * 1024 * 1024)
    in_specs = [pl.BlockSpec((tm, d), lambda i: (i, 0)),
                pl.BlockSpec((tm, wt), lambda i: (i, 0)),
                pl.BlockSpec((tm, HW), lambda i: (i, 0)),
                pl.BlockSpec((None, wt, d), lambda i: (woi, 0, 0), pipeline_mode=once),
                pl.BlockSpec((None, HW, d), lambda i: (woi, wt // HW, 0), pipeline_mode=once),
                pl.BlockSpec((None, 1, d), lambda i: (li, 0, 0), pipeline_mode=once),
                pl.BlockSpec((d, 2 * D_FF), lambda i: (0, 0), pipeline_mode=once),
                pl.BlockSpec((D_FF, d), lambda i: (0, 0), pipeline_mode=once)]
    args = [h, tok, mem, wo, wo, g2, wgu, wdn]
    return pl.pallas_call(
        _out_ffn_kernel,
        grid=(n // tm,),
        in_specs=in_specs,
        out_specs=pl.BlockSpec((tm, d), lambda i: (i, 0)),
        out_shape=jax.ShapeDtypeStruct((n, d), F32),
        compiler_params=_cparams(("parallel",), vmem),
        name="out_ffn",
    )(*args)


def _tail_transpose_kernel(x_ref, o_ref):
    o_ref[0] = x_ref[...].T


def _tail_transpose(x, col, *, batch, seq, keep, tr):
    per, nk = seq // tr, keep // tr
    return pl.pallas_call(
        _tail_transpose_kernel,
        grid=(batch, nk),
        in_specs=[pl.BlockSpec((tr, HW), lambda b, j: (b * per + per - nk + j, col))],
        out_specs=pl.BlockSpec((1, HW, tr), lambda b, j: (b, 0, j)),
        out_shape=jax.ShapeDtypeStruct((batch, HW, keep), F32),
        compiler_params=_cparams(("parallel", "parallel"), 16 * 1024 * 1024),
        name="tail_transpose",
    )(x)


def _pack_a_in(w):
    nl, d, _ = w.shape
    wt = jnp.swapaxes(w, 1, 2).astype(BF16)

    def pad_heads(x):
        x = x.reshape(nl, GLA_HEADS, GLA_DV, d)
        return jnp.pad(x, ((0, 0), (0, 0), (0, GLA_DVP - GLA_DV), (0, 0))).reshape(nl, GLA_VP, d)

    o_v, o_g, o_r = 2 * GLA_QK, 2 * GLA_QK + GLA_V, 2 * GLA_QK + 2 * GLA_V
    glr = jnp.pad(wt[:, o_r:o_r + GLA_RANK], ((0, 0), (0, GLA_RANKP - GLA_RANK), (0, 0)))
    packed = jnp.concatenate([wt[:, :o_v], pad_heads(wt[:, o_v:o_g]), pad_heads(wt[:, o_g:o_r]),
                              wt[:, o_r + GLA_RANK:], glr], axis=1)
    return jnp.swapaxes(packed, 1, 2)


def _pack_a_out(w):
    tok = w[:GLA_V].reshape(GLA_HEADS, GLA_DV, D_MODEL)
    tok = jnp.pad(tok, ((0, 0), (0, GLA_DVP - GLA_DV), (0, 0))).reshape(GLA_VP, D_MODEL)
    return jnp.concatenate([tok, w[GLA_V:]], axis=0).astype(BF16)


def kernel(x_prompt, x_sample, state_gla, cache_win_k, cache_win_v, cache_mem_k, cache_mem_v,
           mem_prompt, norm1, norm2, a_w_in, a_w_gate2, a_b_gate, a_g_onorm, a_w_out, kv_norm, w_kv,
           g_k, b_w_in, b_g_q, b_w_out, mem_norm, w_mem_kv, g_mem_q, g_mem_k, w_ffn_gu, w_ffn_down):
    batch, seq, d = x_prompt.shape
    dec_b = x_sample.shape[0]
    past_len = 8192
    assert d == D_MODEL and seq % 512 == 0 and x_sample.shape[1] == 1

    mall_np, lmask_np = _gla_tables()
    grp_np, hmask_np = _head_tables()
    mall = jnp.asarray(mall_np, BF16)
    lmask = jnp.asarray(lmask_np, F32)
    grp = jnp.asarray(grp_np, BF16)
    hmask = jnp.asarray(hmask_np, F32)
    band = jnp.asarray(_band_bias(), F32)

    hrow8 = jnp.asarray(np.concatenate([hmask_np[:, 0], np.zeros((4, HW), np.float32)], 0))
    hrow16 = jnp.asarray(np.concatenate([hmask_np[:, 0]] * N_GROUPS
                                        + [np.zeros((4, HW), np.float32)], 0))

    cos_p, sin_p = _rope_tables(np.arange(seq))
    cos_s, sin_s = _rope_tables(past_len + np.arange(1))

    wa_in = _pack_a_in(a_w_in)
    wa_out = jnp.stack([_pack_a_out(a_w_out[i]) for i in range(N_A)], 0)
    wg2 = jnp.pad(a_w_gate2, ((0, 0), (0, GLA_RANKP - GLA_RANK), (0, 0))).astype(BF16)
    bg = a_b_gate.reshape(N_A, 1, GLA_QK)
    gon = jnp.pad(a_g_onorm, ((0, 0), (0, GLA_DVP - GLA_DV))).reshape(N_A, 1, GLA_DVP)
    gq_mem = jnp.tile(g_mem_q, (1, N_HEADS)).reshape(DEPTH, 1, HW)
    hg_rope = jnp.tile(jnp.concatenate([b_g_q, g_k.reshape(1, HEAD_DIM)], 0),
                       (1, N_HEADS)).reshape(DEPTH - N_A + 1, 1, HW)
    h0_zero = jnp.zeros((batch, GLA_HEADS, GLA_DVP, GLA_DK), F32)
    wb_in = b_w_in.astype(BF16)
    wb_out = b_w_out.astype(BF16)
    w_kv_b = w_kv.astype(BF16).reshape(1, d, 2 * HW)
    w_mem_b = w_mem_kv.astype(BF16)
    ffn_w = {}
    n1 = norm1.reshape(DEPTH, 1, d)
    n2 = norm2.reshape(DEPTH, 1, d)
    nkv = kv_norm.reshape(1, 1, d)

    mem_kt_p, mem_vt_p = _mem_kv(mem_prompt, mem_norm, w_mem_b, g_mem_k, grp)

    cache_len = cache_win_k.shape[1]
    win_kt = jnp.transpose(cache_win_k, (0, 2, 3, 1)).reshape(dec_b, HW, cache_len)
    win_vt = jnp.transpose(cache_win_v, (0, 2, 3, 1)).reshape(dec_b, HW, cache_len)
    mem_kt_s = jnp.transpose(cache_mem_k, (0, 1, 3, 4, 2)).reshape(DEPTH, dec_b, HW, MEM_TOKENS)
    mem_vt_s = jnp.transpose(cache_mem_v, (0, 1, 3, 4, 2)).reshape(DEPTH, dec_b, HW, MEM_TOKENS)
    state_t = jnp.swapaxes(state_gla, 3, 4)
    step_bias = jnp.asarray(_step_bias(cache_len), F32)

    def trunk(x2, bsz, t, prompt):
        n = bsz * t
        tm = 512 if prompt else n
        h = x2
        states = []
        state_out = None
        kv_sh = None
        tabs = (grp, cos_p, sin_p) if prompt else (grp, cos_s, sin_s)
        tr = ROPE_ROWS if prompt else n
        pos_blocks = max(t // tr, 1)
        for li in range(DEPTH):
            if li < N_A:
                if prompt:
                    tok, proj, st, wgu_b, wdn_b = _gla_prompt(
                        h, n1, wa_in, li, h0_zero, wg2, bg, gon, mall, lmask,
                        w_ffn_gu, w_ffn_down, batch=bsz, seq=t, tb=512)
                    ffn_w[li] = (wgu_b, wdn_b)
                    states.append(st[:, :, :GLA_DV])
                    mq_col = 0
                else:
                    proj = _norm_matmul(h, n1, li, wa_in, li, tm=tm, tn=A_IN_P // 3, name="a_in")
                    tok, state_out = _gla_step(proj, state_t, li, state_out, wg2, bg, gon)
                    mq_col = A_COL_MQ // HW
                wo, woi = wa_out, li
            else:
                bi = li - N_A
                proj = _norm_matmul_rope(h, n1, li, wb_in, bi, hg_rope, bi, tabs,
                                         rope_width=N_GROUPS * HW, tm=tr, name="b_in",
                                         pos_blocks=pos_blocks,
                                         cast=(w_ffn_gu, w_ffn_down, li) if prompt else None)
                if prompt:
                    proj, wgu_b, wdn_b = proj
                    ffn_w[li] = (wgu_b, wdn_b)
                mq_col = (N_GROUPS * HW) // HW
                if prompt:
                    tok = _band_attn(proj, kv_sh, band, hmask, batch=bsz, seq=t)
                else:
                    tok = _step_attn(proj, kv_sh, kv_sh, win_kt, win_vt, step_bias, hrow16)
                wo, woi = wb_out, bi
            if prompt:
                mem_o = _mem_attn(proj.reshape(bsz, t, proj.shape[1]), mq_col, mem_kt_p, mem_vt_p,
                                  li, gq_mem, grp, hmask, tq=512).reshape(n, HW)
            else:
                mem_o = _mem_step(proj, mq_col, mem_kt_s, mem_vt_s, li, gq_mem, grp, hrow8)
            h = _out_ffn(h, tok, mem_o, wo, woi, n2, li, *ffn_w[li], tm=tm)
            if li == N_A - 1:
                kv_sh = _norm_matmul_rope(h, nkv, 0, w_kv_b, 0, hg_rope, DEPTH - N_A, tabs,
                                          rope_width=HW, tm=tr,
                                          name="shared_kv", pos_blocks=pos_blocks)
        return h, states, state_out, kv_sh

    y_p, gla_p, _, kv_p = trunk(x_prompt.reshape(batch * seq, d), batch, seq, True)
    y_s, _, gla_s, kv_s = trunk(x_sample.reshape(dec_b, d), dec_b, 1, False)

    keep = min(WIN_MAX, seq)

    def window_out(x, col):
        xt = _tail_transpose(x, col, batch=batch, seq=seq, keep=keep, tr=512)
        return jnp.transpose(xt.reshape(batch, N_HEADS, HEAD_DIM, keep), (0, 3, 1, 2))

    def mem_out(xt):
        return jnp.transpose(xt.reshape(DEPTH, batch, N_HEADS, HEAD_DIM, MEM_TOKENS),
                             (0, 1, 4, 2, 3))

    return (y_p.reshape(batch, seq, d),
            y_s.reshape(dec_b, 1, d),
            jnp.swapaxes(jnp.stack(gla_p, 0), 3, 4),
            jnp.swapaxes(gla_s, 3, 4),
            window_out(kv_p, 0),
            window_out(kv_p, 1),
            kv_s[:, :HW].reshape(dec_b, 1, N_HEADS, HEAD_DIM),
            kv_s[:, HW:].reshape(dec_b, 1, N_HEADS, HEAD_DIM),
            mem_out(mem_kt_p),
            mem_out(mem_vt_p))
```

```python
import functools

import numpy as np
import jax
import jax.numpy as jnp
from jax import lax
from jax.experimental import pallas as pl
from jax.experimental.pallas import tpu as pltpu

F32 = jnp.float32
BF16 = jnp.bfloat16

D_MODEL = 1024
DEPTH = 4
N_A = 2
GLA_HEADS = 4
GLA_DK = 128
GLA_DV = 192
GLA_DVP = 256
GLA_QK = GLA_HEADS * GLA_DK
GLA_V = GLA_HEADS * GLA_DV
GLA_VP = GLA_HEADS * GLA_DVP
GLA_RANK = 16
GLA_RANKP = 128
GLA_TAU = 16.0
GLA_CHUNK = 64
HEAD_DIM = 64
N_HEADS = 4
HW = N_HEADS * HEAD_DIM
DIL_PAIRS = ((128, 1), (512, 4), (2048, 16))
N_GROUPS = 3
BAND = 128
WIN_MAX = 2048
MEM_TOKENS = 256
D_FF = 2816
ROPE_THETA = 10000.0
EPS = 1e-6
ATTN_SCALE = HEAD_DIM ** -0.5
NEG = -1e30
ROPE_ROWS = 1024
ROPE_SUB_ROWS = 1024
LOG2E = 1.4426950408889634
LN2 = 0.6931471805599453

A_COL_Q = 0
A_COL_K = GLA_QK
A_COL_V = 2 * GLA_QK
A_COL_G = A_COL_V + GLA_VP
A_COL_MQ = A_COL_G + GLA_VP
A_COL_GLR = A_COL_MQ + HW
A_IN_P = A_COL_GLR + GLA_RANKP

V7X_VMEM_BYTES = 64 * 1024 * 1024
VMEM_CAP = V7X_VMEM_BYTES - 8 * 1024 * 1024


def _cparams(sem, vmem_bytes):
    return pltpu.CompilerParams(
        dimension_semantics=sem,
        vmem_limit_bytes=int(min(max(vmem_bytes, 16 * 1024 * 1024), VMEM_CAP)))


def _bdot(a, b):
    return jnp.dot(a.astype(BF16), b.astype(BF16), preferred_element_type=F32)


def _bdot_nt(a, b):
    return lax.dot_general(a.astype(BF16), b.astype(BF16), (((1,), (1,)), ((), ())),
                           preferred_element_type=F32)


def _bdot_tn(a, b):
    return lax.dot_general(a.astype(BF16), b.astype(BF16), (((0,), (0,)), ((), ())),
                           preferred_element_type=F32)


def _split(x):
    hi = x.astype(BF16)
    lo = (x - hi.astype(F32)).astype(BF16)
    return hi, lo


def _dot_exact_rhs(x, m):
    hi, lo = _split(x)
    return (jnp.dot(hi, m, preferred_element_type=F32)
            + jnp.dot(lo, m, preferred_element_type=F32))


def _dot_exact_lhs(m2, x):
    hi, lo = _split(x)
    return jnp.dot(m2, jnp.concatenate([hi, lo], axis=0), preferred_element_type=F32)


def _silu(x):
    return x * jax.nn.sigmoid(x)


def _log_sigmoid(z):
    return jnp.minimum(z, 0.0) - jnp.log1p(jnp.exp(-jnp.abs(z)))


def _gla_tables():
    c = GLA_CHUNK
    i = np.arange(c)[:, None]
    t = np.arange(c)[None, :]
    blocks = [(t <= i), (t > i)]
    masks = []
    for lvl in range(6):
        s = c >> lvl
        half = s // 2
        mid = (i // s) * s + half - 1
        second = (i % s) >= half
        m = np.where(second, (t > mid) & (t <= i), (t > i) & (t <= mid))
        blocks.append(m)
        j = t
        masks.append(((i // s) == (j // s)) & ((i % s) >= half) & ((j % s) < half))
    masks.append(i == t)
    mall = np.concatenate(blocks, axis=0).astype(np.float32)
    mall = np.concatenate([mall, mall], axis=1)
    lmask = np.stack(masks, axis=0).astype(np.float32)
    assert np.array_equal(lmask.sum(0), (t <= i).astype(np.float32))
    return mall, lmask


def _head_tables():
    lane = np.arange(HW)
    group = (lane[:, None] // HEAD_DIM == lane[None, :] // HEAD_DIM).astype(np.float32)
    hmask = (lane[None, :] // HEAD_DIM == np.arange(N_HEADS)[:, None]).astype(np.float32)
    return group, hmask.reshape(N_HEADS, 1, HW)


def _band_bias():
    i = np.arange(BAND)[:, None]
    j = np.arange(2 * BAND)[None, :]
    dist = i + BAND - j
    ok = (dist >= 0) & (dist <= BAND)
    first = ok & (j >= BAND)
    return np.where(np.stack([first, ok], 0), 0.0, NEG).astype(np.float32)


def _rope_tables(pos):
    half = HEAD_DIM // 2
    inv = ROPE_THETA ** (-np.arange(half, dtype=np.float64) / half)
    ang = np.asarray(pos, np.float64)[:, None] * inv[None, :]
    cos = np.cos(ang).astype(np.float32)
    sin = np.sin(ang).astype(np.float32)
    cos64 = np.concatenate([cos, cos], axis=-1)
    sin64 = np.concatenate([-sin, sin], axis=-1)
    return jnp.asarray(np.tile(cos64, (1, N_HEADS))), jnp.asarray(np.tile(sin64, (1, N_HEADS)))


def _norm_matmul_kernel(x_ref, g_ref, w_ref, o_ref):
    x = x_ref[...]
    ms = jnp.mean(x * x, axis=-1, keepdims=True)
    y = x * lax.rsqrt(ms + EPS) * g_ref[...]
    o_ref[...] = jnp.dot(y.astype(BF16), w_ref[...], preferred_element_type=F32)


def _norm_matmul(x, g, gi, w, wi, *, tm, tn, name):
    n, d = x.shape
    nout = w.shape[2]
    vmem = 2 * (tm * d * 4 + d * tn * 2 + tm * tn * 4) + 2 * tm * d * 4 + tm * tn * 4
    return pl.pallas_call(
        _norm_matmul_kernel,
        grid=(n // tm, nout // tn),
        in_specs=[pl.BlockSpec((tm, d), lambda i, j: (i, 0)),
                  pl.BlockSpec((None, 1, d), lambda i, j: (gi, 0, 0)),
                  pl.BlockSpec((None, d, tn), lambda i, j: (wi, 0, j))],
        out_specs=pl.BlockSpec((tm, tn), lambda i, j: (i, j)),
        out_shape=jax.ShapeDtypeStruct((n, nout), F32),
        compiler_params=_cparams(("parallel", "parallel"), vmem),
        name=name,
    )(x, g, w)


def _norm_matmul_rope_kernel(x_ref, g_ref, w_ref, hg_ref, grp_ref, cos_ref, sin_ref, *rest,
                             rope_width, cast):
    if cast:
        wgu_i, wdn_i, o_ref, wgu_o, wdn_o = rest
        wgu_o[...] = wgu_i[...].astype(BF16)
        wdn_o[...] = wdn_i[...].astype(BF16)
    else:
        o_ref, = rest
    grp = grp_ref[...]
    lane = lax.broadcasted_iota(jnp.int32, (1, HW), 1)
    first_half = (lane & (HEAD_DIM - 1)) < (HEAD_DIM // 2)
    tm = x_ref.shape[0]
    sub = min(tm, ROPE_SUB_ROWS)
    for r in range(tm // sub):
        rs = slice(r * sub, (r + 1) * sub)
        trs = rs if cos_ref.shape[0] == tm else slice(None)
        x = x_ref[rs, :]
        ms = jnp.mean(x * x, axis=-1, keepdims=True)
        xn = x * lax.rsqrt(ms + EPS) * g_ref[...]
        y = jnp.dot(xn.astype(BF16), w_ref[...], preferred_element_type=F32)
        for c in range(rope_width // HW):
            sl = slice(c * HW, (c + 1) * HW)
            yc = y[:, sl]
            ss = _dot_exact_rhs(yc * yc, grp)
            yc = yc * lax.rsqrt(ss * (1.0 / HEAD_DIM) + EPS) * hg_ref[...]
            fwd = pltpu.roll(yc, HW - HEAD_DIM // 2, 1)
            bwd = pltpu.roll(yc, HEAD_DIM // 2, 1)
            rot = jnp.where(first_half, fwd, bwd)
            o_ref[rs, sl] = yc * cos_ref[trs, :] + rot * sin_ref[trs, :]
        o_ref[rs, rope_width:] = y[:, rope_width:]


def _norm_matmul_rope(x, g, gi, w, wi, hg, hgi, tabs, *, rope_width, tm, name, pos_blocks=1,
                      cast=None):
    n, d = x.shape
    nout = w.shape[2]
    grp, cos, sin = tabs
    trow = 1 if cos.shape[0] == 1 else tm
    tab_map = (lambda i: (0, 0)) if cos.shape[0] == 1 else (lambda i: (i % pos_blocks, 0))
    vmem = (2 * (tm * d * 4 + tm * nout * 4 + 2 * trow * HW * 4) + d * nout * 2
            + 2 * tm * d * 4 + 3 * tm * nout * 4 + (8 * 1024 * 1024 if cast else 0))
    in_specs = [pl.BlockSpec((tm, d), lambda i: (i, 0)),
                pl.BlockSpec((None, 1, d), lambda i: (gi, 0, 0)),
                pl.BlockSpec((None, d, nout), lambda i: (wi, 0, 0), pipeline_mode=pl.Buffered(1)),
                pl.BlockSpec((None, 1, HW), lambda i: (hgi, 0, 0)),
                pl.BlockSpec((HW, HW), lambda i: (0, 0)),
                pl.BlockSpec((trow, HW), tab_map),
                pl.BlockSpec((trow, HW), tab_map)]
    args = [x, g, w, hg, grp, cos, sin]
    out_specs = [pl.BlockSpec((tm, nout), lambda i: (i, 0))]
    out_shape = [jax.ShapeDtypeStruct((n, nout), F32)]
    if cast is not None:
        wgu32, wdn32, cli = cast
        for w32 in (wgu32, wdn32):
            i_spec, o_spec, o_shape = _cast_specs(w32, cli, n // tm, lambda i: i)
            in_specs.append(i_spec)
            out_specs.append(o_spec)
            out_shape.append(o_shape)
            args.append(w32)
    out = pl.pallas_call(
        functools.partial(_norm_matmul_rope_kernel, rope_width=rope_width, cast=cast is not None),
        grid=(n // tm,),
        in_specs=in_specs,
        out_specs=out_specs,
        out_shape=out_shape,
        compiler_params=_cparams(("arbitrary" if cast is not None else "parallel",), vmem),
        name=name,
    )(*args)
    return out if cast is not None else out[0]


def _gla_block(q, k, v, g, la, ht, gon, mall_ref, lmask_ref, n_chunks):
    cr = GLA_CHUNK
    chunk = lambda c: slice(c * cr, (c + 1) * cr)
    la_cat = jnp.concatenate([la[chunk(c)] for c in range(n_chunks)], axis=1)
    e_all = jnp.exp2(_dot_exact_lhs(mall_ref[...], la_cat * LOG2E))

    def decay_rows(blk):
        return jnp.concatenate(
            [e_all[blk * cr:(blk + 1) * cr, c * GLA_DK:(c + 1) * GLA_DK] for c in range(n_chunks)],
            axis=0)

    eb = decay_rows(0)
    qb = (q * eb).astype(BF16)
    kdec = (k * decay_rows(1)).astype(BF16)
    q_lv = [q.astype(BF16)]
    k_lv = [k.astype(BF16)]
    for lvl in range(6):
        el = decay_rows(2 + lvl)
        q_lv.append((q * el).astype(BF16))
        k_lv.append((k * el).astype(BF16))
    o_intra, kv_new = [], []
    for c in range(n_chunks):
        r = chunk(c)
        s = lmask_ref[6] * _bdot_nt(q_lv[0][r], k_lv[0][r])
        for lvl in range(6):
            s = s + lmask_ref[lvl] * _bdot_nt(q_lv[1 + lvl][r], k_lv[1 + lvl][r])
        o_intra.append(_bdot(s, v[r]))
        kv_new.append(_bdot_tn(v[r], kdec[r]))
    outs = []
    for c in range(n_chunks):
        outs.append(o_intra[c] + _bdot_nt(qb[chunk(c)], ht))
        ht = ht * eb[c * cr + cr - 1:(c + 1) * cr, :] + kv_new[c]
    o = jnp.concatenate(outs, axis=0)
    ms = jnp.sum(o * o, axis=-1, keepdims=True) * (1.0 / GLA_DV)
    return o * lax.rsqrt(ms + EPS) * gon * _silu(g), ht


def _cast_specs(w32, li, n_steps, step_of):
    rows, cols = w32.shape[1:]
    chunk, rep = rows // n_steps, 1
    assert chunk * n_steps == rows
    while (chunk * rep) % 16:
        rep *= 2
    chunk *= rep
    return (pl.BlockSpec((None, chunk, cols), lambda *ids: (li, step_of(*ids) // rep, 0)),
            pl.BlockSpec((chunk, cols), lambda *ids: (step_of(*ids) // rep, 0)),
            jax.ShapeDtypeStruct((rows, cols), BF16))


def _gla_kernel(x_ref, g1_ref, w_ref, wg2_ref, bg_ref, gon_ref, mall_ref, lmask_ref, h0_ref,
                wgu_i, wdn_i, tok_ref, mq_ref, hout_ref, wgu_o, wdn_o, proj_s, ht_s, *, n_chunks):
    t = pl.program_id(1)
    wgu_o[...] = wgu_i[...].astype(BF16)
    wdn_o[...] = wdn_i[...].astype(BF16)

    @pl.when(t == 0)
    def _():
        ht_s[...] = h0_ref[0]

    x = x_ref[...]
    ms = jnp.mean(x * x, axis=-1, keepdims=True)
    xn = (x * lax.rsqrt(ms + EPS) * g1_ref[...]).astype(BF16)
    proj_s[...] = jnp.dot(xn, w_ref[...], preferred_element_type=F32)
    mq_ref[...] = proj_s[:, A_COL_MQ:A_COL_MQ + HW]
    z = _bdot(proj_s[:, A_COL_GLR:A_COL_GLR + GLA_RANKP], wg2_ref[...]) + bg_ref[...]
    la = _log_sigmoid(z) * (1.0 / GLA_TAU)
    gon = gon_ref[...]
    for h in range(GLA_HEADS):
        kc = slice(h * GLA_DK, (h + 1) * GLA_DK)
        vc = slice(h * GLA_DVP, (h + 1) * GLA_DVP)
        q = proj_s[:, A_COL_Q + h * GLA_DK:A_COL_Q + (h + 1) * GLA_DK] * (GLA_DK ** -0.5)
        k = proj_s[:, A_COL_K + h * GLA_DK:A_COL_K + (h + 1) * GLA_DK]
        v = proj_s[:, A_COL_V + h * GLA_DVP:A_COL_V + (h + 1) * GLA_DVP].astype(BF16)
        g = proj_s[:, A_COL_G + h * GLA_DVP:A_COL_G + (h + 1) * GLA_DVP]
        tok, ht = _gla_block(q, k, v, g, la[:, kc], ht_s[h], gon, mall_ref, lmask_ref, n_chunks)
        tok_ref[:, vc] = tok.astype(tok_ref.dtype)
        ht_s[h] = ht

    @pl.when(t == pl.num_programs(1) - 1)
    def _():
        hout_ref[0] = ht_s[...]


def _gla_prompt(x, g1, w_in, li, h0t, wg2, bg, gon, mall, lmask, wgu32, wdn32, *, batch, seq, tb):
    n, d = x.shape
    nt = seq // tb
    const2 = lambda shape: pl.BlockSpec(shape, lambda b, t: (0,) * len(shape))
    st_spec = pl.BlockSpec((1, GLA_HEADS, GLA_DVP, GLA_DK), lambda b, t: (b, 0, 0, 0))
    step_of = lambda b, t: b * nt + t
    gu_in, gu_out, gu_shape = _cast_specs(wgu32, li, batch * nt, step_of)
    dn_in, dn_out, dn_shape = _cast_specs(wdn32, li, batch * nt, step_of)
    vmem = (2 * tb * d * 4 + d * A_IN_P * 2 + tb * A_IN_P * 4 + 2 * tb * (GLA_VP * 2 + HW * 4)
            + 6 * GLA_HEADS * GLA_DVP * GLA_DK * 4 + 24 * tb * GLA_DVP * 4 + 8 * 1024 * 1024)
    return pl.pallas_call(
        functools.partial(_gla_kernel, n_chunks=tb // GLA_CHUNK),
        grid=(batch, nt),
        in_specs=[
            pl.BlockSpec((tb, d), lambda b, t: (b * nt + t, 0)),
            pl.BlockSpec((None, 1, d), lambda b, t: (li, 0, 0)),
            pl.BlockSpec((None, d, A_IN_P), lambda b, t: (li, 0, 0), pipeline_mode=pl.Buffered(1)),
            pl.BlockSpec((None, GLA_RANKP, GLA_QK), lambda b, t: (li, 0, 0)),
            pl.BlockSpec((None, 1, GLA_QK), lambda b, t: (li, 0, 0)),
            pl.BlockSpec((None, 1, GLA_DVP), lambda b, t: (li, 0, 0)),
            const2(mall.shape), const2(lmask.shape), st_spec, gu_in, dn_in,
        ],
        out_specs=[
            pl.BlockSpec((tb, GLA_VP), lambda b, t: (b * nt + t, 0)),
            pl.BlockSpec((tb, HW), lambda b, t: (b * nt + t, 0)),
            st_spec, gu_out, dn_out,
        ],
        out_shape=[jax.ShapeDtypeStruct((n, GLA_VP), BF16),
                   jax.ShapeDtypeStruct((n, HW), F32),
                   jax.ShapeDtypeStruct((batch, GLA_HEADS, GLA_DVP, GLA_DK), F32),
                   gu_shape, dn_shape],
        scratch_shapes=[pltpu.VMEM((tb, A_IN_P), F32),
                        pltpu.VMEM((GLA_HEADS, GLA_DVP, GLA_DK), F32)],
        compiler_params=_cparams(("parallel", "arbitrary"), vmem),
        name="gla_prompt",
    )(x, g1, w_in, wg2, bg, gon, mall, lmask, h0t, wgu32, wdn32)


STEP_ROWS = 8


def _gla_step_kernel(q_ref, k_ref, v_ref, g_ref, glr_ref, wg2_ref, bg_ref, gon_ref, st_ref, *rest,
                     out_layer):
    tok_ref, so_ref = rest[-2:]
    nb = q_ref.shape[0]
    for other in range(so_ref.shape[0]):
        if other != out_layer:
            so_ref[other] = jnp.zeros(so_ref.shape[1:], F32)
    z = _bdot(glr_ref[...], wg2_ref[...]) + bg_ref[...]
    a = jnp.exp(_log_sigmoid(z) * (1.0 / GLA_TAU))
    q = q_ref[...] * (GLA_DK ** -0.5)
    k = k_ref[...]
    v = v_ref[...]
    g = g_ref[...]
    gon = gon_ref[...][:, :GLA_DV]
    row = lax.broadcasted_iota(jnp.int32, (nb, 1), 0)
    for h in range(GLA_HEADS):
        ks = slice(h * GLA_DK, (h + 1) * GLA_DK)
        vs = slice(h * GLA_DVP, h * GLA_DVP + GLA_DV)
        o_rows = jnp.zeros((nb, GLA_DV), F32)
        for b in range(nb):
            only_b = row == b
            outer = _bdot_tn(jnp.where(only_b, v[:, vs], 0.0), k[:, ks])
            s_new = st_ref[0, b, h] * a[b:b + 1, ks] + outer
            so_ref[out_layer, b, h] = s_new
            o_rows = o_rows + jnp.where(only_b, _bdot_nt(q[:, ks], s_new), 0.0)
        ms = jnp.sum(o_rows * o_rows, axis=-1, keepdims=True) * (1.0 / GLA_DV)
        tok_ref[:, vs] = o_rows * lax.rsqrt(ms + EPS) * gon * _silu(g[:, vs])
        tok_ref[:, h * GLA_DVP + GLA_DV:(h + 1) * GLA_DVP] = jnp.zeros(
            (nb, GLA_DVP - GLA_DV), F32)


def _gla_step(proj, state_t, li, prev_out, wg2, bg, gon):
    bsz = proj.shape[0]
    nb = STEP_ROWS
    col = lambda w, c: pl.BlockSpec((nb, w), lambda i: (i, c))
    st_spec = pl.BlockSpec((1, nb, GLA_HEADS, GLA_DV, GLA_DK), lambda i: (li, i, 0, 0, 0))
    in_specs = [col(GLA_QK, A_COL_Q // GLA_QK), col(GLA_QK, A_COL_K // GLA_QK),
                col(GLA_VP, A_COL_V // GLA_VP), col(GLA_VP, A_COL_G // GLA_VP),
                col(GLA_RANKP, A_COL_GLR // GLA_RANKP),
                pl.BlockSpec((None, GLA_RANKP, GLA_QK), lambda i: (li, 0, 0)),
                pl.BlockSpec((None, 1, GLA_QK), lambda i: (li, 0, 0)),
                pl.BlockSpec((None, 1, GLA_DVP), lambda i: (li, 0, 0)),
                st_spec]
    args = [proj, proj, proj, proj, proj, wg2, bg, gon, state_t]
    if prev_out is None:
        aliases, out_layer = {}, li
        so_spec = pl.BlockSpec((state_t.shape[0], nb, GLA_HEADS, GLA_DV, GLA_DK),
                               lambda i: (0, i, 0, 0, 0))
    else:
        in_specs.append(pl.BlockSpec(memory_space=pl.ANY))
        args.append(prev_out)
        aliases, out_layer, so_spec = {len(args) - 1: 1}, 0, st_spec
    return pl.pallas_call(
        functools.partial(_gla_step_kernel, out_layer=out_layer),
        grid=(bsz // nb,),
        in_specs=in_specs,
        out_specs=[pl.BlockSpec((nb, GLA_VP), lambda i: (i, 0)), so_spec],
        out_shape=[jax.ShapeDtypeStruct((bsz, GLA_VP), F32),
                   jax.ShapeDtypeStruct(state_t.shape, F32)],
        input_output_aliases=aliases,
        compiler_params=_cparams(("parallel",), 32 * 1024 * 1024),
        name="gla_step",
    )(*args)


def _mem_qnorm(q, gq_ref, grp_ref):
    ss = _dot_exact_rhs(q * q, grp_ref[...])
    return q * lax.rsqrt(ss * (1.0 / HEAD_DIM) + EPS) * gq_ref[...] * ATTN_SCALE


def _mem_attn_kernel(q_ref, kt_ref, vt_ref, gq_ref, grp_ref, hmask_ref, o_ref):
    qn = _mem_qnorm(q_ref[0], gq_ref, grp_ref)
    tq = qn.shape[0]
    kt = kt_ref[0, 0].astype(BF16)
    vt = vt_ref[0, 0].astype(BF16)
    q4 = jnp.concatenate([qn * hmask_ref[h] for h in range(N_HEADS)], axis=0)
    s = _bdot(q4, kt)
    m = jnp.max(s, axis=-1, keepdims=True)
    p = jnp.exp(s - m)
    o4 = _bdot_nt(p, vt) / jnp.sum(p, axis=-1, keepdims=True)
    out = jnp.zeros(qn.shape, F32)
    for h in range(N_HEADS):
        out = out + hmask_ref[h] * o4[h * tq:(h + 1) * tq]
    o_ref[0] = out.astype(o_ref.dtype)


def _mem_attn(q3, qcol, kt, vt, li, gq, grp, hmask, *, tq):
    bsz, t, _ = q3.shape
    g = gq
    vmem = 2 * (tq * HW * 4 + 2 * MEM_TOKENS * HW * 4 + tq * HW * 2) + 16 * tq * HW * 4
    kv_spec = pl.BlockSpec((1, 1, HW, MEM_TOKENS), lambda b, i: (li, b, 0, 0))
    return pl.pallas_call(
        _mem_attn_kernel,
        grid=(bsz, t // tq),
        in_specs=[pl.BlockSpec((1, tq, HW), lambda b, i: (b, i, qcol)),
                  kv_spec, kv_spec,
                  pl.BlockSpec((None, 1, HW), lambda b, i: (li, 0, 0)),
                  pl.BlockSpec((HW, HW), lambda b, i: (0, 0)),
                  pl.BlockSpec((N_HEADS, 1, HW), lambda b, i: (0, 0, 0))],
        out_specs=pl.BlockSpec((1, tq, HW), lambda b, i: (b, i, 0)),
        out_shape=jax.ShapeDtypeStruct((bsz, t, HW), BF16),
        compiler_params=_cparams(("parallel", "parallel"), vmem),
        name="mem_attn",
    )(q3, kt, vt, g, grp, hmask)


def _mem_step_kernel(q_ref, kt_ref, vt_ref, gq_ref, grp_ref, hrow_ref, o_ref):
    nb = q_ref.shape[0]
    qn = _mem_qnorm(q_ref[...], gq_ref, grp_ref)
    hrow = hrow_ref[...]
    row = lax.broadcasted_iota(jnp.int32, (nb, 1), 0)
    out = jnp.zeros((nb, HW), F32)
    for b in range(nb):
        q4 = hrow * qn[b:b + 1]
        s = _bdot(q4, kt_ref[0, b])
        m = jnp.max(s, axis=-1, keepdims=True)
        p = jnp.exp(s - m)
        l = jnp.sum(p, axis=-1, keepdims=True)
        o4 = _bdot_nt(p, vt_ref[0, b]) / l
        o_b = jnp.sum(hrow * o4, axis=0, keepdims=True)
        out = out + jnp.where(row == b, o_b, 0.0)
    o_ref[...] = out


def _mem_step(q, qcol, kt, vt, li, gq, grp, hrow):
    bsz = q.shape[0]
    nb = STEP_ROWS
    g = gq
    kv_spec = pl.BlockSpec((1, nb, HW, MEM_TOKENS), lambda i: (li, i, 0, 0))
    return pl.pallas_call(
        _mem_step_kernel,
        grid=(bsz // nb,),
        in_specs=[pl.BlockSpec((nb, HW), lambda i: (i, qcol)),
                  kv_spec, kv_spec,
                  pl.BlockSpec((None, 1, HW), lambda i: (li, 0, 0)),
                  pl.BlockSpec((HW, HW), lambda i: (0, 0)),
                  pl.BlockSpec((8, HW), lambda i: (0, 0))],
        out_specs=pl.BlockSpec((nb, HW), lambda i: (i, 0)),
        out_shape=jax.ShapeDtypeStruct((bsz, HW), F32),
        compiler_params=_cparams(("parallel",), 24 * 1024 * 1024),
        name="mem_step",
    )(q, kt, vt, g, grp, hrow)


def _mem_kv_kernel(x_ref, gn_ref, w_ref, gk_ref, grp_ref, kt_ref, vt_ref):
    x = x_ref[0]
    ms = jnp.mean(x * x, axis=-1, keepdims=True)
    y = x * lax.rsqrt(ms + EPS) * gn_ref[0]
    kv = jnp.dot(y.astype(BF16), w_ref[0], preferred_element_type=F32)
    k = kv[:, :HW]
    ss = _dot_exact_rhs(k * k, grp_ref[...])
    k = k * lax.rsqrt(ss * (1.0 / HEAD_DIM) + EPS) * gk_ref[0]
    kt_ref[0, 0] = k.T
    vt_ref[0, 0] = kv[:, HW:].T


def _mem_kv(mem, gn, w, gk, grp):
    bsz, m, d = mem.shape
    nl = w.shape[0]
    out = jax.ShapeDtypeStruct((nl, bsz, HW, m), F32)
    o_spec = pl.BlockSpec((1, 1, HW, m), lambda l, b: (l, b, 0, 0))
    return pl.pallas_call(
        _mem_kv_kernel,
        grid=(nl, bsz),
        in_specs=[pl.BlockSpec((1, m, d), lambda l, b: (b, 0, 0)),
                  pl.BlockSpec((1, 1, d), lambda l, b: (l, 0, 0)),
                  pl.BlockSpec((1, d, 2 * HW), lambda l, b: (l, 0, 0)),
                  pl.BlockSpec((1, 1, HW), lambda l, b: (l, 0, 0)),
                  pl.BlockSpec((HW, HW), lambda l, b: (0, 0))],
        out_specs=[o_spec, o_spec],
        out_shape=[out, out],
        compiler_params=_cparams(("parallel", "parallel"), 24 * 1024 * 1024),
        name="mem_kv",
    )(mem, gn.reshape(nl, 1, d), w, jnp.tile(gk, (1, N_HEADS)).reshape(nl, 1, HW), grp)


def _band_attn_kernel(q0_ref, q1_ref, k0_ref, k1_ref, v0_ref, v1_ref, bias_ref, hmask_ref,
                      o_ref, og_s, ls_s, *, seq):
    g = pl.program_id(1)
    n_units = seq // BAND

    def run_group(gi, dil):
        nb = seq // dil // BAND
        shift = nb.bit_length() - 1
        lane = lax.broadcasted_iota(jnp.int32, (1, 128), 1)
        low_lanes = lane < HEAD_DIM

        def idx(start):
            if dil == 1:
                return pl.ds(pl.multiple_of(start, BAND), BAND)
            return pl.ds(start, BAND, stride=dil)

        def rows(ref0, ref1, start):
            return jnp.concatenate([ref0[idx(start), :], ref1[idx(start), :]], axis=1)

        def unit(u, carry):
            r = lax.shift_right_logical(u, shift)
            nblk = u & (nb - 1)
            qs = r + dil * BAND * nblk
            ks = r + dil * BAND * jnp.maximum(nblk - 1, 0)
            q = rows(q0_ref, q1_ref, qs) * (ATTN_SCALE * LOG2E)
            kk = jnp.concatenate([rows(k0_ref, k1_ref, ks), rows(k0_ref, k1_ref, qs)],
                                 axis=0).astype(BF16)
            vv = jnp.concatenate([rows(v0_ref, v1_ref, ks), rows(v0_ref, v1_ref, qs)],
                                 axis=0).astype(BF16)
            q4 = jnp.concatenate([q * hmask_ref[h] for h in range(N_HEADS)], axis=0)
            bias = bias_ref[jnp.minimum(nblk, 1)]
            s = _bdot_nt(q4, kk).reshape(N_HEADS, BAND, 2 * BAND) + bias[None]
            s = s.reshape(N_HEADS * BAND, 2 * BAND)
            m = jnp.max(s, axis=-1, keepdims=True)
            p = jnp.exp2(s - m)
            l = jnp.sum(p, axis=-1, keepdims=True)
            oe = _bdot(p, vv)
            hd = [slice(h * BAND, (h + 1) * BAND) for h in range(N_HEADS)]
            for half in range(2):
                ha, hb = hd[2 * half], hd[2 * half + 1]
                cols = slice(half * 128, (half + 1) * 128)
                o_h = jnp.where(low_lanes, oe[ha, cols], oe[hb, cols])
                l_h = jnp.where(low_lanes, l[ha], l[hb])
                m_h = jnp.where(low_lanes, m[ha], m[hb])
                og_s[2 * gi + half, idx(qs), :] = o_h / l_h
                ls_s[2 * gi + half, idx(qs), :] = (m_h + jnp.log2(l_h)) * LN2
            return carry

        lax.fori_loop(0, n_units, unit, 0, unroll=8)

    for gi, (_, dil) in enumerate(DIL_PAIRS):
        pl.when(g == gi)(functools.partial(run_group, gi, dil))

    @pl.when(g == N_GROUPS - 1)
    def _merge():
        tr = 512
        for c in range(seq // tr):
            rows = slice(c * tr, (c + 1) * tr)
            for half in range(2):
                ls = [ls_s[2 * gi + half, rows, :] for gi in range(N_GROUPS)]
                m = jnp.maximum(jnp.maximum(ls[0], ls[1]), ls[2])
                es = [jnp.exp(x - m) for x in ls]
                inv = 1.0 / (es[0] + es[1] + es[2])
                acc = jnp.zeros((tr, 128), F32)
                for gi in range(N_GROUPS):
                    acc = acc + (es[gi] * inv) * og_s[2 * gi + half, rows, :]
                o_ref[rows, half * 128:(half + 1) * 128] = acc.astype(o_ref.dtype)


def _band_attn(qd, kv, band, hmask, *, batch, seq):
    n = qd.shape[0]
    half = lambda cmap: pl.BlockSpec((seq, 128), cmap)
    qhalf = half
    vmem = (12 * seq * 128 * 4 + 2 * seq * HW * 2 + 12 * seq * 128 * 4 + 6 * 1024 * 1024)
    return pl.pallas_call(
        functools.partial(_band_attn_kernel, seq=seq),
        grid=(batch, N_GROUPS),
        in_specs=[qhalf(lambda b, g: (b, 2 * g)), qhalf(lambda b, g: (b, 2 * g + 1)),
                  half(lambda b, g: (b, 0)), half(lambda b, g: (b, 1)),
                  half(lambda b, g: (b, 2)), half(lambda b, g: (b, 3)),
                  pl.BlockSpec((2, BAND, 2 * BAND), lambda b, g: (0, 0, 0)),
                  pl.BlockSpec((N_HEADS, 1, HW), lambda b, g: (0, 0, 0))],
        out_specs=pl.BlockSpec((seq, HW), lambda b, g: (b, 0)),
        out_shape=jax.ShapeDtypeStruct((n, HW), BF16),
        scratch_shapes=[pltpu.VMEM((2 * N_GROUPS, seq, 128), F32),
                        pltpu.VMEM((2 * N_GROUPS, seq, 128), F32)],
        compiler_params=_cparams(("parallel", "arbitrary"), vmem),
        name="band_attn",
    )(qd, qd, kv, kv, kv, kv, band, hmask)


def _step_bias(cache_len):
    t = np.arange(cache_len)
    rows = np.zeros((16, cache_len), np.float32)
    for gi, (_, dil) in enumerate(DIL_PAIRS):
        keep = (t >= cache_len - dil * BAND) & ((cache_len - t) % dil == 0)
        rows[4 * gi:4 * gi + 4] = np.where(keep, 0.0, NEG)[None]
    return rows


def _step_attn_kernel(q_ref, kn_ref, vn_ref, kt_ref, vt_ref, bias_ref, hrow_ref, o_ref):
    j = pl.program_id(1)
    q = q_ref[pl.ds(j, 1), :] * ATTN_SCALE
    kn = kn_ref[pl.ds(j, 1), :]
    vn = vn_ref[pl.ds(j, 1), :]
    hrow = hrow_ref[...]
    row = lax.broadcasted_iota(jnp.int32, (16, 1), 0)
    q16 = hrow * jnp.where(row < 4, q[:, :HW], jnp.where(row < 8, q[:, HW:2 * HW], q[:, 2 * HW:]))
    s = _bdot(q16, kt_ref[0]) + bias_ref[...]
    s_n = jnp.sum(q16 * kn, axis=-1, keepdims=True)
    m = jnp.maximum(jnp.max(s, axis=-1, keepdims=True), s_n)
    p = jnp.exp(s - m)
    p_n = jnp.exp(s_n - m)
    l = jnp.sum(p, axis=-1, keepdims=True) + p_n
    o16 = (_bdot_nt(p, vt_ref[0]) + p_n * vn) / l
    lse = m + jnp.log(l)
    lg = [lse[4 * g:4 * g + 4] for g in range(N_GROUPS)]
    mm = jnp.maximum(jnp.maximum(lg[0], lg[1]), lg[2])
    es = [jnp.exp(x - mm) for x in lg]
    inv = 1.0 / (es[0] + es[1] + es[2])
    acc = jnp.zeros((4, HW), F32)
    for g in range(N_GROUPS):
        acc = acc + (es[g] * inv) * (hrow[0:4] * o16[4 * g:4 * g + 4])
    o_ref[pl.ds(j, 1), :] = jnp.sum(acc, axis=0, keepdims=True)


def _step_attn(q, kn, kvn, kt, vt, bias, hrow16):
    bsz, _, cache_len = kt.shape
    assert cache_len >= max(d for _, d in DIL_PAIRS) * BAND
    nb = STEP_ROWS
    rows = lambda w: pl.BlockSpec((nb, w), lambda i, j: (i, 0))
    cache = pl.BlockSpec((1, HW, cache_len), lambda i, j: (i * nb + j, 0, 0))
    vmem = 4 * HW * cache_len * 4 + 8 * 16 * cache_len * 4 + 4 * 1024 * 1024
    return pl.pallas_call(
        _step_attn_kernel,
        grid=(bsz // nb, nb),
        in_specs=[rows(N_GROUPS * HW), rows(HW),
                  pl.BlockSpec((nb, HW), lambda i, j: (i, 1)), cache, cache,
                  pl.BlockSpec((16, cache_len), lambda i, j: (0, 0)),
                  pl.BlockSpec((16, HW), lambda i, j: (0, 0))],
        out_specs=rows(HW),
        out_shape=jax.ShapeDtypeStruct((bsz, HW), F32),
        compiler_params=_cparams(("parallel", "arbitrary"), vmem),
        name="step_attn",
    )(q, kn, kvn, kt, vt, bias, hrow16)


def _out_ffn_kernel(h_ref, tok_ref, mem_ref, wo1_ref, wo2_ref, g2_ref, wgu_ref, wdn_ref, *rest,
                    cast):
    if cast:
        ngu_i, ndn_i, o_ref, ngu_o, ndn_o = rest
        ngu_o[...] = ngu_i[...].astype(BF16)
        ndn_o[...] = ndn_i[...].astype(BF16)
    else:
        o_ref, = rest
    h1 = (h_ref[...]
          + jnp.dot(tok_ref[...].astype(BF16), wo1_ref[...], preferred_element_type=F32)
          + jnp.dot(mem_ref[...].astype(BF16), wo2_ref[...], preferred_element_type=F32))
    ms = jnp.mean(h1 * h1, axis=-1, keepdims=True)
    hn = (h1 * lax.rsqrt(ms + EPS) * g2_ref[...]).astype(BF16)
    gate = jnp.dot(hn, wgu_ref[:, :D_FF], preferred_element_type=F32)
    up = jnp.dot(hn, wgu_ref[:, D_FF:], preferred_element_type=F32)
    act = (_silu(gate) * up).astype(BF16)
    o_ref[...] = h1 + jnp.dot(act, wdn_ref[...], preferred_element_type=F32)


def _out_ffn(h, tok, mem, wo, woi, g2, li, wgu, wdn, *, tm, cast=None):
    n, d = h.shape
    wt = tok.shape[1]
    assert wo.shape[1] == wt + HW and wt % HW == 0
    once = pl.Buffered(1)
    vmem = ((wt + HW) * d * 2 + 3 * D_FF * d * 2
            + 2 * tm * (2 * d * 4 + (wt + HW) * 4) + 3 * tm * d * 4 + 3 * tm * D_FF * 4
            + 4 * 1024 * 1024)
    in_specs = [pl.BlockSpec((tm, d), lambda i: (i, 0)),
                pl.BlockSpec((tm, wt), lambda i: (i, 0)),
                pl.BlockSpec((tm, HW), lambda i: (i, 0)),
                pl.BlockSpec((None, wt, d), lambda i: (woi, 0, 0), pipeline_mode=once),
                pl.BlockSpec((None, HW, d), lambda i: (woi, wt // HW, 0), pipeline_mode=once),
                pl.BlockSpec((None, 1, d), lambda i: (li, 0, 0), pipeline_mode=once),
                pl.BlockSpec((d, 2 * D_FF), lambda i: (0, 0), pipeline_mode=once),
                pl.BlockSpec((D_FF, d), lambda i: (0, 0), pipeline_mode=once)]
    args = [h, tok, mem, wo, wo, g2, wgu, wdn]
    out_specs = [pl.BlockSpec((tm, d), lambda i: (i, 0))]
    out_shape = [jax.ShapeDtypeStruct((n, d), F32)]
    if cast is not None:
        wgu32, wdn32, cli = cast
        for w32 in (wgu32, wdn32):
            i_spec, o_spec, o_shape = _cast_specs(w32, cli, n // tm, lambda i: i)
            in_specs.append(i_spec)
            out_specs.append(o_spec)
            out_shape.append(o_shape)
            args.append(w32)
    out = pl.pallas_call(
        functools.partial(_out_ffn_kernel, cast=cast is not None),
        grid=(n // tm,),
        in_specs=in_specs,
        out_specs=out_specs,
        out_shape=out_shape,
        compiler_params=_cparams(("arbitrary" if cast is not None else "parallel",), vmem),
        name="out_ffn",
    )(*args)
    return out if cast is not None else out[0]


def _tail_transpose_kernel(x_ref, o_ref):
    o_ref[0] = x_ref[...].T


def _tail_transpose(x, col, *, batch, seq, keep, tr):
    per, nk = seq // tr, keep // tr
    return pl.pallas_call(
        _tail_transpose_kernel,
        grid=(batch, nk),
        in_specs=[pl.BlockSpec((tr, HW), lambda b, j: (b * per + per - nk + j, col))],
        out_specs=pl.BlockSpec((1, HW, tr), lambda b, j: (b, 0, j)),
        out_shape=jax.ShapeDtypeStruct((batch, HW, keep), F32),
        compiler_params=_cparams(("parallel", "parallel"), 16 * 1024 * 1024),
        name="tail_transpose",
    )(x)


def _pack_a_in(w):
    nl, d, _ = w.shape
    wt = jnp.swapaxes(w, 1, 2).astype(BF16)

    def pad_heads(x):
        x = x.reshape(nl, GLA_HEADS, GLA_DV, d)
        return jnp.pad(x, ((0, 0), (0, 0), (0, GLA_DVP - GLA_DV), (0, 0))).reshape(nl, GLA_VP, d)

    o_v, o_g, o_r = 2 * GLA_QK, 2 * GLA_QK + GLA_V, 2 * GLA_QK + 2 * GLA_V
    glr = jnp.pad(wt[:, o_r:o_r + GLA_RANK], ((0, 0), (0, GLA_RANKP - GLA_RANK), (0, 0)))
    packed = jnp.concatenate([wt[:, :o_v], pad_heads(wt[:, o_v:o_g]), pad_heads(wt[:, o_g:o_r]),
                              wt[:, o_r + GLA_RANK:], glr], axis=1)
    return jnp.swapaxes(packed, 1, 2)


def _pack_a_out(w):
    tok = w[:GLA_V].reshape(GLA_HEADS, GLA_DV, D_MODEL)
    tok = jnp.pad(tok, ((0, 0), (0, GLA_DVP - GLA_DV), (0, 0))).reshape(GLA_VP, D_MODEL)
    return jnp.concatenate([tok, w[GLA_V:]], axis=0).astype(BF16)


def kernel(x_prompt, x_sample, state_gla, cache_win_k, cache_win_v, cache_mem_k, cache_mem_v,
           mem_prompt, norm1, norm2, a_w_in, a_w_gate2, a_b_gate, a_g_onorm, a_w_out, kv_norm, w_kv,
           g_k, b_w_in, b_g_q, b_w_out, mem_norm, w_mem_kv, g_mem_q, g_mem_k, w_ffn_gu, w_ffn_down):
    batch, seq, d = x_prompt.shape
    dec_b = x_sample.shape[0]
    past_len = 8192
    assert d == D_MODEL and seq % 512 == 0 and x_sample.shape[1] == 1

    mall_np, lmask_np = _gla_tables()
    grp_np, hmask_np = _head_tables()
    mall = jnp.asarray(mall_np, BF16)
    lmask = jnp.asarray(lmask_np, F32)
    grp = jnp.asarray(grp_np, BF16)
    hmask = jnp.asarray(hmask_np, F32)
    band = jnp.asarray(_band_bias(), F32)

    hrow8 = jnp.asarray(np.concatenate([hmask_np[:, 0], np.zeros((4, HW), np.float32)], 0))
    hrow16 = jnp.asarray(np.concatenate([hmask_np[:, 0]] * N_GROUPS
                                        + [np.zeros((4, HW), np.float32)], 0))

    cos_p, sin_p = _rope_tables(np.arange(seq))
    cos_s, sin_s = _rope_tables(past_len + np.arange(1))

    wa_in = _pack_a_in(a_w_in)
    wa_out = jnp.stack([_pack_a_out(a_w_out[i]) for i in range(N_A)], 0)
    wg2 = jnp.pad(a_w_gate2, ((0, 0), (0, GLA_RANKP - GLA_RANK), (0, 0))).astype(BF16)
    bg = a_b_gate.reshape(N_A, 1, GLA_QK)
    gon = jnp.pad(a_g_onorm, ((0, 0), (0, GLA_DVP - GLA_DV))).reshape(N_A, 1, GLA_DVP)
    gq_mem = jnp.tile(g_mem_q, (1, N_HEADS)).reshape(DEPTH, 1, HW)
    hg_rope = jnp.tile(jnp.concatenate([b_g_q, g_k.reshape(1, HEAD_DIM)], 0),
                       (1, N_HEADS)).reshape(DEPTH - N_A + 1, 1, HW)
    h0_zero = jnp.zeros((batch, GLA_HEADS, GLA_DVP, GLA_DK), F32)
    wb_in = b_w_in.astype(BF16)
    wb_out = b_w_out.astype(BF16)
    w_kv_b = w_kv.astype(BF16).reshape(1, d, 2 * HW)
    w_mem_b = w_mem_kv.astype(BF16)
    ffn_w = {}
    n1 = norm1.reshape(DEPTH, 1, d)
    n2 = norm2.reshape(DEPTH, 1, d)
    nkv = kv_norm.reshape(1, 1, d)

    mem_kt_p, mem_vt_p = _mem_kv(mem_prompt, mem_norm, w_mem_b, g_mem_k, grp)

    cache_len = cache_win_k.shape[1]
    win_kt = jnp.transpose(cache_win_k, (0, 2, 3, 1)).reshape(dec_b, HW, cache_len)
    win_vt = jnp.transpose(cache_win_v, (0, 2, 3, 1)).reshape(dec_b, HW, cache_len)
    mem_kt_s = jnp.transpose(cache_mem_k, (0, 1, 3, 4, 2)).reshape(DEPTH, dec_b, HW, MEM_TOKENS)
    mem_vt_s = jnp.transpose(cache_mem_v, (0, 1, 3, 4, 2)).reshape(DEPTH, dec_b, HW, MEM_TOKENS)
    state_t = jnp.swapaxes(state_gla, 3, 4)
    step_bias = jnp.asarray(_step_bias(cache_len), F32)

    def trunk(x2, bsz, t, prompt):
        n = bsz * t
        tm = 512 if prompt else n
        h = x2
        states = []
        state_out = None
        kv_sh = None
        tabs = (grp, cos_p, sin_p) if prompt else (grp, cos_s, sin_s)
        tr = ROPE_ROWS if prompt else n
        pos_blocks = max(t // tr, 1)
        for li in range(DEPTH):
            if li < N_A:
                if prompt:
                    tok, proj, st, wgu_b, wdn_b = _gla_prompt(
                        h, n1, wa_in, li, h0_zero, wg2, bg, gon, mall, lmask,
                        w_ffn_gu, w_ffn_down, batch=bsz, seq=t, tb=512)
                    ffn_w[li] = (wgu_b, wdn_b)
                    states.append(st[:, :, :GLA_DV])
                    mq_col = 0
                else:
                    proj = _norm_matmul(h, n1, li, wa_in, li, tm=tm, tn=A_IN_P // 3, name="a_in")
                    tok, state_out = _gla_step(proj, state_t, li, state_out, wg2, bg, gon)
                    mq_col = A_COL_MQ // HW
                wo, woi = wa_out, li
            else:
                bi = li - N_A
                proj = _norm_matmul_rope(h, n1, li, wb_in, bi, hg_rope, bi, tabs,
                                         rope_width=N_GROUPS * HW, tm=tr, name="b_in",
                                         pos_blocks=pos_blocks)
                mq_col = (N_GROUPS * HW) // HW
                if prompt:
                    tok = _band_attn(proj, kv_sh, band, hmask, batch=bsz, seq=t)
                else:
                    tok = _step_attn(proj, kv_sh, kv_sh, win_kt, win_vt, step_bias, hrow16)
                wo, woi = wb_out, bi
            if prompt:
                mem_o = _mem_attn(proj.reshape(bsz, t, proj.shape[1]), mq_col, mem_kt_p, mem_vt_p,
                                  li, gq_mem, grp, hmask, tq=512).reshape(n, HW)
            else:
                mem_o = _mem_step(proj, mq_col, mem_kt_s, mem_vt_s, li, gq_mem, grp, hrow8)
            if prompt and N_A - 1 <= li < DEPTH - 1:
                h, wgu_b, wdn_b = _out_ffn(h, tok, mem_o, wo, woi, n2, li, *ffn_w[li], tm=tm,
                                           cast=(w_ffn_gu, w_ffn_down, li + 1))
                ffn_w[li + 1] = (wgu_b, wdn_b)
            else:
                h = _out_ffn(h, tok, mem_o, wo, woi, n2, li, *ffn_w[li], tm=tm)
            if li == N_A - 1:
                kv_sh = _norm_matmul_rope(h, nkv, 0, w_kv_b, 0, hg_rope, DEPTH - N_A, tabs,
                                          rope_width=HW, tm=tr,
                                          name="shared_kv", pos_blocks=pos_blocks)
        return h, states, state_out, kv_sh

    y_p, gla_p, _, kv_p = trunk(x_prompt.reshape(batch * seq, d), batch, seq, True)
    y_s, _, gla_s, kv_s = trunk(x_sample.reshape(dec_b, d), dec_b, 1, False)

    keep = min(WIN_MAX, seq)

    def window_out(x, col):
        xt = _tail_transpose(x, col, batch=batch, seq=seq, keep=keep, tr=512)
        return jnp.transpose(xt.reshape(batch, N_HEADS, HEAD_DIM, keep), (0, 3, 1, 2))

    def mem_out(xt):
        return jnp.transpose(xt.reshape(DEPTH, batch, N_HEADS, HEAD_DIM, MEM_TOKENS),
                             (0, 1, 4, 2, 3))

    return (y_p.reshape(batch, seq, d),
            y_s.reshape(dec_b, 1, d),
            jnp.swapaxes(jnp.stack(gla_p, 0), 3, 4),
            jnp.swapaxes(gla_s, 3, 4),
            window_out(kv_p, 0),
            window_out(kv_p, 1),
            kv_s[:, :HW].reshape(dec_b, 1, N_HEADS, HEAD_DIM),
            kv_s[:, HW:].reshape(dec_b, 1, N_HEADS, HEAD_DIM),
            mem_out(mem_kt_p),
            mem_out(mem_vt_p))
```

```python
import functools

import numpy as np
import jax
import jax.numpy as jnp
from jax import lax
from jax.experimental import pallas as pl
from jax.experimental.pallas import tpu as pltpu

F32 = jnp.float32
BF16 = jnp.bfloat16

D_MODEL = 1024
DEPTH = 4
N_A = 2
GLA_HEADS = 4
GLA_DK = 128
GLA_DV = 192
GLA_DVP = 256
GLA_QK = GLA_HEADS * GLA_DK
GLA_V = GLA_HEADS * GLA_DV
GLA_VP = GLA_HEADS * GLA_DVP
GLA_RANK = 16
GLA_RANKP = 128
GLA_TAU = 16.0
GLA_CHUNK = 64
HEAD_DIM = 64
N_HEADS = 4
HW = N_HEADS * HEAD_DIM
DIL_PAIRS = ((128, 1), (512, 4), (2048, 16))
N_GROUPS = 3
BAND = 128
WIN_MAX = 2048
MEM_TOKENS = 256
D_FF = 2816
ROPE_THETA = 10000.0
EPS = 1e-6
ATTN_SCALE = HEAD_DIM ** -0.5
NEG = -1e30
ROPE_ROWS = 1024
ROPE_SUB_ROWS = 1024
LOG2E = 1.4426950408889634
LN2 = 0.6931471805599453

A_COL_Q = 0
A_COL_K = GLA_QK
A_COL_V = 2 * GLA_QK
A_COL_G = A_COL_V + GLA_VP
A_COL_MQ = A_COL_G + GLA_VP
A_COL_GLR = A_COL_MQ + HW
A_IN_P = A_COL_GLR + GLA_RANKP

V7X_VMEM_BYTES = 64 * 1024 * 1024
VMEM_CAP = V7X_VMEM_BYTES - 8 * 1024 * 1024


def _cparams(sem, vmem_bytes):
    return pltpu.CompilerParams(
        dimension_semantics=sem,
        vmem_limit_bytes=int(min(max(vmem_bytes, 16 * 1024 * 1024), VMEM_CAP)))


def _bdot(a, b):
    return jnp.dot(a.astype(BF16), b.astype(BF16), preferred_element_type=F32)


def _bdot_nt(a, b):
    return lax.dot_general(a.astype(BF16), b.astype(BF16), (((1,), (1,)), ((), ())),
                           preferred_element_type=F32)


def _bdot_tn(a, b):
    return lax.dot_general(a.astype(BF16), b.astype(BF16), (((0,), (0,)), ((), ())),
                           preferred_element_type=F32)


def _split(x):
    hi = x.astype(BF16)
    lo = (x - hi.astype(F32)).astype(BF16)
    return hi, lo


def _dot_exact_rhs(x, m):
    hi, lo = _split(x)
    return (jnp.dot(hi, m, preferred_element_type=F32)
            + jnp.dot(lo, m, preferred_element_type=F32))


def _dot_exact_lhs(m2, x):
    hi, lo = _split(x)
    return jnp.dot(m2, jnp.concatenate([hi, lo], axis=0), preferred_element_type=F32)


def _silu(x):
    return x * jax.nn.sigmoid(x)


def _log_sigmoid(z):
    return jnp.minimum(z, 0.0) - jnp.log1p(jnp.exp(-jnp.abs(z)))


def _gla_tables():
    c = GLA_CHUNK
    i = np.arange(c)[:, None]
    t = np.arange(c)[None, :]
    blocks = [(t <= i), (t > i)]
    masks = []
    for lvl in range(6):
        s = c >> lvl
        half = s // 2
        mid = (i // s) * s + half - 1
        second = (i % s) >= half
        m = np.where(second, (t > mid) & (t <= i), (t > i) & (t <= mid))
        blocks.append(m)
        j = t
        masks.append(((i // s) == (j // s)) & ((i % s) >= half) & ((j % s) < half))
    masks.append(i == t)
    mall = np.concatenate(blocks, axis=0).astype(np.float32)
    mall = np.concatenate([mall, mall], axis=1)
    lmask = np.stack(masks, axis=0).astype(np.float32)
    assert np.array_equal(lmask.sum(0), (t <= i).astype(np.float32))
    return mall, lmask


def _head_tables():
    lane = np.arange(HW)
    group = (lane[:, None] // HEAD_DIM == lane[None, :] // HEAD_DIM).astype(np.float32)
    hmask = (lane[None, :] // HEAD_DIM == np.arange(N_HEADS)[:, None]).astype(np.float32)
    return group, hmask.reshape(N_HEADS, 1, HW)


def _band_bias():
    i = np.arange(BAND)[:, None]
    j = np.arange(2 * BAND)[None, :]
    dist = i + BAND - j
    ok = (dist >= 0) & (dist <= BAND)
    first = ok & (j >= BAND)
    return np.where(np.stack([first, ok], 0), 0.0, NEG).astype(np.float32)


def _rope_tables(pos):
    half = HEAD_DIM // 2
    inv = ROPE_THETA ** (-np.arange(half, dtype=np.float64) / half)
    ang = np.asarray(pos, np.float64)[:, None] * inv[None, :]
    cos = np.cos(ang).astype(np.float32)
    sin = np.sin(ang).astype(np.float32)
    cos64 = np.concatenate([cos, cos], axis=-1)
    sin64 = np.concatenate([-sin, sin], axis=-1)
    return jnp.asarray(np.tile(cos64, (1, N_HEADS))), jnp.asarray(np.tile(sin64, (1, N_HEADS)))


def _norm_matmul_kernel(x_ref, g_ref, w_ref, o_ref):
    x = x_ref[...]
    ms = jnp.mean(x * x, axis=-1, keepdims=True)
    y = x * lax.rsqrt(ms + EPS) * g_ref[...]
    o_ref[...] = jnp.dot(y.astype(BF16), w_ref[...], preferred_element_type=F32)


def _norm_matmul(x, g, gi, w, wi, *, tm, tn, name):
    n, d = x.shape
    nout = w.shape[2]
    vmem = 2 * (tm * d * 4 + d * tn * 2 + tm * tn * 4) + 2 * tm * d * 4 + tm * tn * 4
    return pl.pallas_call(
        _norm_matmul_kernel,
        grid=(n // tm, nout // tn),
        in_specs=[pl.BlockSpec((tm, d), lambda i, j: (i, 0)),
                  pl.BlockSpec((None, 1, d), lambda i, j: (gi, 0, 0)),
                  pl.BlockSpec((None, d, tn), lambda i, j: (wi, 0, j))],
        out_specs=pl.BlockSpec((tm, tn), lambda i, j: (i, j)),
        out_shape=jax.ShapeDtypeStruct((n, nout), F32),
        compiler_params=_cparams(("parallel", "parallel"), vmem),
        name=name,
    )(x, g, w)


def _norm_matmul_rope_kernel(x_ref, g_ref, w_ref, hg_ref, grp_ref, cos_ref, sin_ref, *rest,
                             rope_width, cast):
    if cast:
        wgu_i, wdn_i, o_ref, wgu_o, wdn_o = rest
        wgu_o[...] = wgu_i[...].astype(BF16)
        wdn_o[...] = wdn_i[...].astype(BF16)
    else:
        o_ref, = rest
    grp = grp_ref[...]
    lane = lax.broadcasted_iota(jnp.int32, (1, HW), 1)
    first_half = (lane & (HEAD_DIM - 1)) < (HEAD_DIM // 2)
    tm = x_ref.shape[0]
    sub = min(tm, ROPE_SUB_ROWS)
    for r in range(tm // sub):
        rs = slice(r * sub, (r + 1) * sub)
        trs = rs if cos_ref.shape[0] == tm else slice(None)
        x = x_ref[rs, :]
        ms = jnp.mean(x * x, axis=-1, keepdims=True)
        xn = x * lax.rsqrt(ms + EPS) * g_ref[...]
        y = jnp.dot(xn.astype(BF16), w_ref[...], preferred_element_type=F32)
        for c in range(rope_width // HW):
            sl = slice(c * HW, (c + 1) * HW)
            yc = y[:, sl]
            ss = _dot_exact_rhs(yc * yc, grp)
            yc = yc * lax.rsqrt(ss * (1.0 / HEAD_DIM) + EPS) * hg_ref[...]
            fwd = pltpu.roll(yc, HW - HEAD_DIM // 2, 1)
            bwd = pltpu.roll(yc, HEAD_DIM // 2, 1)
            rot = jnp.where(first_half, fwd, bwd)
            o_ref[rs, sl] = yc * cos_ref[trs, :] + rot * sin_ref[trs, :]
        o_ref[rs, rope_width:] = y[:, rope_width:]


def _norm_matmul_rope(x, g, gi, w, wi, hg, hgi, tabs, *, rope_width, tm, name, pos_blocks=1,
                      cast=None):
    n, d = x.shape
    nout = w.shape[2]
    grp, cos, sin = tabs
    trow = 1 if cos.shape[0] == 1 else tm
    tab_map = (lambda i: (0, 0)) if cos.shape[0] == 1 else (lambda i: (i % pos_blocks, 0))
    vmem = (2 * (tm * d * 4 + tm * nout * 4 + 2 * trow * HW * 4) + d * nout * 2
            + 2 * tm * d * 4 + 3 * tm * nout * 4 + (8 * 1024 * 1024 if cast else 0))
    in_specs = [pl.BlockSpec((tm, d), lambda i: (i, 0)),
                pl.BlockSpec((None, 1, d), lambda i: (gi, 0, 0)),
                pl.BlockSpec((None, d, nout), lambda i: (wi, 0, 0), pipeline_mode=pl.Buffered(1)),
                pl.BlockSpec((None, 1, HW), lambda i: (hgi, 0, 0)),
                pl.BlockSpec((HW, HW), lambda i: (0, 0)),
                pl.BlockSpec((trow, HW), tab_map),
                pl.BlockSpec((trow, HW), tab_map)]
    args = [x, g, w, hg, grp, cos, sin]
    out_specs = [pl.BlockSpec((tm, nout), lambda i: (i, 0))]
    out_shape = [jax.ShapeDtypeStruct((n, nout), F32)]
    if cast is not None:
        wgu32, wdn32, cli = cast
        for w32 in (wgu32, wdn32):
            i_spec, o_spec, o_shape = _cast_specs(w32, cli, n // tm, lambda i: i)
            in_specs.append(i_spec)
            out_specs.append(o_spec)
            out_shape.append(o_shape)
            args.append(w32)
    out = pl.pallas_call(
        functools.partial(_norm_matmul_rope_kernel, rope_width=rope_width, cast=cast is not None),
        grid=(n // tm,),
        in_specs=in_specs,
        out_specs=out_specs,
        out_shape=out_shape,
        compiler_params=_cparams(("arbitrary" if cast is not None else "parallel",), vmem),
        name=name,
    )(*args)
    return out if cast is not None else out[0]


def _gla_block(q, k, v, g, la, ht, gon, mall_ref, lmask_ref, n_chunks):
    cr = GLA_CHUNK
    chunk = lambda c: slice(c * cr, (c + 1) * cr)
    la_cat = jnp.concatenate([la[chunk(c)] for c in range(n_chunks)], axis=1)
    e_all = jnp.exp2(_dot_exact_lhs(mall_ref[...], la_cat * LOG2E))

    def decay_rows(blk):
        return jnp.concatenate(
            [e_all[blk * cr:(blk + 1) * cr, c * GLA_DK:(c + 1) * GLA_DK] for c in range(n_chunks)],
            axis=0)

    eb = decay_rows(0)
    qb = (q * eb).astype(BF16)
    kdec = (k * decay_rows(1)).astype(BF16)
    q_lv = [q.astype(BF16)]
    k_lv = [k.astype(BF16)]
    for lvl in range(6):
        el = decay_rows(2 + lvl)
        q_lv.append((q * el).astype(BF16))
        k_lv.append((k * el).astype(BF16))
    o_intra, kv_new = [], []
    for c in range(n_chunks):
        r = chunk(c)
        s = lmask_ref[6] * _bdot_nt(q_lv[0][r], k_lv[0][r])
        for lvl in range(6):
            s = s + lmask_ref[lvl] * _bdot_nt(q_lv[1 + lvl][r], k_lv[1 + lvl][r])
        o_intra.append(_bdot(s, v[r]))
        kv_new.append(_bdot_tn(v[r], kdec[r]))
    outs = []
    for c in range(n_chunks):
        outs.append(o_intra[c] + _bdot_nt(qb[chunk(c)], ht))
        ht = ht * eb[c * cr + cr - 1:(c + 1) * cr, :] + kv_new[c]
    o = jnp.concatenate(outs, axis=0)
    ms = jnp.sum(o * o, axis=-1, keepdims=True) * (1.0 / GLA_DV)
    return o * lax.rsqrt(ms + EPS) * gon * _silu(g), ht


def _cast_specs(w32, li, n_steps, step_of):
    rows, cols = w32.shape[1:]
    chunk, rep = rows // n_steps, 1
    assert chunk * n_steps == rows
    while (chunk * rep) % 16:
        rep *= 2
    chunk *= rep
    return (pl.BlockSpec((None, chunk, cols), lambda *ids: (li, step_of(*ids) // rep, 0)),
            pl.BlockSpec((chunk, cols), lambda *ids: (step_of(*ids) // rep, 0)),
            jax.ShapeDtypeStruct((rows, cols), BF16))


def _gla_kernel(x_ref, g1_ref, w_ref, wg2_ref, bg_ref, gon_ref, mall_ref, lmask_ref, h0_ref,
                wgu_i, wdn_i, tok_ref, mq_ref, hout_ref, wgu_o, wdn_o, proj_s, ht_s, *, n_chunks):
    t = pl.program_id(1)
    wgu_o[...] = wgu_i[...].astype(BF16)
    wdn_o[...] = wdn_i[...].astype(BF16)

    @pl.when(t == 0)
    def _():
        ht_s[...] = h0_ref[0]

    x = x_ref[...]
    ms = jnp.mean(x * x, axis=-1, keepdims=True)
    xn = (x * lax.rsqrt(ms + EPS) * g1_ref[...]).astype(BF16)
    proj_s[...] = jnp.dot(xn, w_ref[...], preferred_element_type=F32)
    mq_ref[...] = proj_s[:, A_COL_MQ:A_COL_MQ + HW]
    z = _bdot(proj_s[:, A_COL_GLR:A_COL_GLR + GLA_RANKP], wg2_ref[...]) + bg_ref[...]
    la = _log_sigmoid(z) * (1.0 / GLA_TAU)
    gon = gon_ref[...]
    for h in range(GLA_HEADS):
        kc = slice(h * GLA_DK, (h + 1) * GLA_DK)
        vc = slice(h * GLA_DVP, (h + 1) * GLA_DVP)
        q = proj_s[:, A_COL_Q + h * GLA_DK:A_COL_Q + (h + 1) * GLA_DK] * (GLA_DK ** -0.5)
        k = proj_s[:, A_COL_K + h * GLA_DK:A_COL_K + (h + 1) * GLA_DK]
        v = proj_s[:, A_COL_V + h * GLA_DVP:A_COL_V + (h + 1) * GLA_DVP].astype(BF16)
        g = proj_s[:, A_COL_G + h * GLA_DVP:A_COL_G + (h + 1) * GLA_DVP]
        tok, ht = _gla_block(q, k, v, g, la[:, kc], ht_s[h], gon, mall_ref, lmask_ref, n_chunks)
        tok_ref[:, vc] = tok.astype(tok_ref.dtype)
        ht_s[h] = ht

    @pl.when(t == pl.num_programs(1) - 1)
    def _():
        hout_ref[0] = ht_s[...]


def _gla_prompt(x, g1, w_in, li, h0t, wg2, bg, gon, mall, lmask, wgu32, wdn32, *, batch, seq, tb):
    n, d = x.shape
    nt = seq // tb
    const2 = lambda shape: pl.BlockSpec(shape, lambda b, t: (0,) * len(shape))
    st_spec = pl.BlockSpec((1, GLA_HEADS, GLA_DVP, GLA_DK), lambda b, t: (b, 0, 0, 0))
    step_of = lambda b, t: b * nt + t
    gu_in, gu_out, gu_shape = _cast_specs(wgu32, li, batch * nt, step_of)
    dn_in, dn_out, dn_shape = _cast_specs(wdn32, li, batch * nt, step_of)
    vmem = (2 * tb * d * 4 + d * A_IN_P * 2 + tb * A_IN_P * 4 + 2 * tb * (GLA_VP * 2 + HW * 4)
            + 6 * GLA_HEADS * GLA_DVP * GLA_DK * 4 + 24 * tb * GLA_DVP * 4 + 8 * 1024 * 1024)
    return pl.pallas_call(
        functools.partial(_gla_kernel, n_chunks=tb // GLA_CHUNK),
        grid=(batch, nt),
        in_specs=[
            pl.BlockSpec((tb, d), lambda b, t: (b * nt + t, 0)),
            pl.BlockSpec((None, 1, d), lambda b, t: (li, 0, 0)),
            pl.BlockSpec((None, d, A_IN_P), lambda b, t: (li, 0, 0), pipeline_mode=pl.Buffered(1)),
            pl.BlockSpec((None, GLA_RANKP, GLA_QK), lambda b, t: (li, 0, 0)),
            pl.BlockSpec((None, 1, GLA_QK), lambda b, t: (li, 0, 0)),
            pl.BlockSpec((None, 1, GLA_DVP), lambda b, t: (li, 0, 0)),
            const2(mall.shape), const2(lmask.shape), st_spec, gu_in, dn_in,
        ],
        out_specs=[
            pl.BlockSpec((tb, GLA_VP), lambda b, t: (b * nt + t, 0)),
            pl.BlockSpec((tb, HW), lambda b, t: (b * nt + t, 0)),
            st_spec, gu_out, dn_out,
        ],
        out_shape=[jax.ShapeDtypeStruct((n, GLA_VP), BF16),
                   jax.ShapeDtypeStruct((n, HW), F32),
                   jax.ShapeDtypeStruct((batch, GLA_HEADS, GLA_DVP, GLA_DK), F32),
                   gu_shape, dn_shape],
        scratch_shapes=[pltpu.VMEM((tb, A_IN_P), F32),
                        pltpu.VMEM((GLA_HEADS, GLA_DVP, GLA_DK), F32)],
        compiler_params=_cparams(("parallel", "arbitrary"), vmem),
        name="gla_prompt",
    )(x, g1, w_in, wg2, bg, gon, mall, lmask, h0t, wgu32, wdn32)


STEP_ROWS = 8


def _gla_step_kernel(q_ref, k_ref, v_ref, g_ref, glr_ref, wg2_ref, bg_ref, gon_ref, st_ref, *rest,
                     out_layer):
    tok_ref, so_ref = rest[-2:]
    nb = q_ref.shape[0]
    for other in range(so_ref.shape[0]):
        if other != out_layer:
            so_ref[other] = jnp.zeros(so_ref.shape[1:], F32)
    z = _bdot(glr_ref[...], wg2_ref[...]) + bg_ref[...]
    a = jnp.exp(_log_sigmoid(z) * (1.0 / GLA_TAU))
    q = q_ref[...] * (GLA_DK ** -0.5)
    k = k_ref[...]
    v = v_ref[...]
    g = g_ref[...]
    gon = gon_ref[...][:, :GLA_DV]
    row = lax.broadcasted_iota(jnp.int32, (nb, 1), 0)
    for h in range(GLA_HEADS):
        ks = slice(h * GLA_DK, (h + 1) * GLA_DK)
        vs = slice(h * GLA_DVP, h * GLA_DVP + GLA_DV)
        o_rows = jnp.zeros((nb, GLA_DV), F32)
        for b in range(nb):
            only_b = row == b
            outer = _bdot_tn(jnp.where(only_b, v[:, vs], 0.0), k[:, ks])
            s_new = st_ref[0, b, h] * a[b:b + 1, ks] + outer
            so_ref[out_layer, b, h] = s_new
            o_rows = o_rows + jnp.where(only_b, _bdot_nt(q[:, ks], s_new), 0.0)
        ms = jnp.sum(o_rows * o_rows, axis=-1, keepdims=True) * (1.0 / GLA_DV)
        tok_ref[:, vs] = o_rows * lax.rsqrt(ms + EPS) * gon * _silu(g[:, vs])
        tok_ref[:, h * GLA_DVP + GLA_DV:(h + 1) * GLA_DVP] = jnp.zeros(
            (nb, GLA_DVP - GLA_DV), F32)


def _gla_step(proj, state_t, li, prev_out, wg2, bg, gon):
    bsz = proj.shape[0]
    nb = STEP_ROWS
    col = lambda w, c: pl.BlockSpec((nb, w), lambda i: (i, c))
    st_spec = pl.BlockSpec((1, nb, GLA_HEADS, GLA_DV, GLA_DK), lambda i: (li, i, 0, 0, 0))
    in_specs = [col(GLA_QK, A_COL_Q // GLA_QK), col(GLA_QK, A_COL_K // GLA_QK),
                col(GLA_VP, A_COL_V // GLA_VP), col(GLA_VP, A_COL_G // GLA_VP),
                col(GLA_RANKP, A_COL_GLR // GLA_RANKP),
                pl.BlockSpec((None, GLA_RANKP, GLA_QK), lambda i: (li, 0, 0)),
                pl.BlockSpec((None, 1, GLA_QK), lambda i: (li, 0, 0)),
                pl.BlockSpec((None, 1, GLA_DVP), lambda i: (li, 0, 0)),
                st_spec]
    args = [proj, proj, proj, proj, proj, wg2, bg, gon, state_t]
    if prev_out is None:
        aliases, out_layer = {}, li
        so_spec = pl.BlockSpec((state_t.shape[0], nb, GLA_HEADS, GLA_DV, GLA_DK),
                               lambda i: (0, i, 0, 0, 0))
    else:
        in_specs.append(pl.BlockSpec(memory_space=pl.ANY))
        args.append(prev_out)
        aliases, out_layer, so_spec = {len(args) - 1: 1}, 0, st_spec
    return pl.pallas_call(
        functools.partial(_gla_step_kernel, out_layer=out_layer),
        grid=(bsz // nb,),
        in_specs=in_specs,
        out_specs=[pl.BlockSpec((nb, GLA_VP), lambda i: (i, 0)), so_spec],
        out_shape=[jax.ShapeDtypeStruct((bsz, GLA_VP), F32),
                   jax.ShapeDtypeStruct(state_t.shape, F32)],
        input_output_aliases=aliases,
        compiler_params=_cparams(("parallel",), 32 * 1024 * 1024),
        name="gla_step",
    )(*args)


def _mem_qnorm(q, gq_ref, grp_ref):
    ss = _dot_exact_rhs(q * q, grp_ref[...])
    return q * lax.rsqrt(ss * (1.0 / HEAD_DIM) + EPS) * gq_ref[...] * ATTN_SCALE


def _mem_attn_kernel(q_ref, kt_ref, vt_ref, gq_ref, grp_ref, hmask_ref, o_ref):
    qn = _mem_qnorm(q_ref[0], gq_ref, grp_ref)
    tq = qn.shape[0]
    kt = kt_ref[0, 0].astype(BF16)
    vt = vt_ref[0, 0].astype(BF16)
    q4 = jnp.concatenate([qn * hmask_ref[h] for h in range(N_HEADS)], axis=0)
    s = _bdot(q4, kt)
    m = jnp.max(s, axis=-1, keepdims=True)
    p = jnp.exp(s - m)
    o4 = _bdot_nt(p, vt) / jnp.sum(p, axis=-1, keepdims=True)
    out = jnp.zeros(qn.shape, F32)
    for h in range(N_HEADS):
        out = out + hmask_ref[h] * o4[h * tq:(h + 1) * tq]
    o_ref[0] = out.astype(o_ref.dtype)


def _mem_attn(q3, qcol, kt, vt, li, gq, grp, hmask, *, tq):
    bsz, t, _ = q3.shape
    g = gq
    vmem = 2 * (tq * HW * 4 + 2 * MEM_TOKENS * HW * 4 + tq * HW * 2) + 16 * tq * HW * 4
    kv_spec = pl.BlockSpec((1, 1, HW, MEM_TOKENS), lambda b, i: (li, b, 0, 0))
    return pl.pallas_call(
        _mem_attn_kernel,
        grid=(bsz, t // tq),
        in_specs=[pl.BlockSpec((1, tq, HW), lambda b, i: (b, i, qcol)),
                  kv_spec, kv_spec,
                  pl.BlockSpec((None, 1, HW), lambda b, i: (li, 0, 0)),
                  pl.BlockSpec((HW, HW), lambda b, i: (0, 0)),
                  pl.BlockSpec((N_HEADS, 1, HW), lambda b, i: (0, 0, 0))],
        out_specs=pl.BlockSpec((1, tq, HW), lambda b, i: (b, i, 0)),
        out_shape=jax.ShapeDtypeStruct((bsz, t, HW), BF16),
        compiler_params=_cparams(("parallel", "parallel"), vmem),
        name="mem_attn",
    )(q3, kt, vt, g, grp, hmask)


def _mem_step_kernel(q_ref, kt_ref, vt_ref, gq_ref, grp_ref, hrow_ref, o_ref):
    nb = q_ref.shape[0]
    qn = _mem_qnorm(q_ref[...], gq_ref, grp_ref)
    hrow = hrow_ref[...]
    row = lax.broadcasted_iota(jnp.int32, (nb, 1), 0)
    out = jnp.zeros((nb, HW), F32)
    for b in range(nb):
        q4 = hrow * qn[b:b + 1]
        s = _bdot(q4, kt_ref[0, b])
        m = jnp.max(s, axis=-1, keepdims=True)
        p = jnp.exp(s - m)
        l = jnp.sum(p, axis=-1, keepdims=True)
        o4 = _bdot_nt(p, vt_ref[0, b]) / l
        o_b = jnp.sum(hrow * o4, axis=0, keepdims=True)
        out = out + jnp.where(row == b, o_b, 0.0)
    o_ref[...] = out


def _mem_step(q, qcol, kt, vt, li, gq, grp, hrow):
    bsz = q.shape[0]
    nb = STEP_ROWS
    g = gq
    kv_spec = pl.BlockSpec((1, nb, HW, MEM_TOKENS), lambda i: (li, i, 0, 0))
    return pl.pallas_call(
        _mem_step_kernel,
        grid=(bsz // nb,),
        in_specs=[pl.BlockSpec((nb, HW), lambda i: (i, qcol)),
                  kv_spec, kv_spec,
                  pl.BlockSpec((None, 1, HW), lambda i: (li, 0, 0)),
                  pl.BlockSpec((HW, HW), lambda i: (0, 0)),
                  pl.BlockSpec((8, HW), lambda i: (0, 0))],
        out_specs=pl.BlockSpec((nb, HW), lambda i: (i, 0)),
        out_shape=jax.ShapeDtypeStruct((bsz, HW), F32),
        compiler_params=_cparams(("parallel",), 24 * 1024 * 1024),
        name="mem_step",
    )(q, kt, vt, g, grp, hrow)


def _mem_kv_kernel(x_ref, gn_ref, w_ref, gk_ref, grp_ref, kt_ref, vt_ref):
    x = x_ref[0]
    ms = jnp.mean(x * x, axis=-1, keepdims=True)
    y = x * lax.rsqrt(ms + EPS) * gn_ref[0]
    kv = jnp.dot(y.astype(BF16), w_ref[0], preferred_element_type=F32)
    k = kv[:, :HW]
    ss = _dot_exact_rhs(k * k, grp_ref[...])
    k = k * lax.rsqrt(ss * (1.0 / HEAD_DIM) + EPS) * gk_ref[0]
    kt_ref[0, 0] = k.T
    vt_ref[0, 0] = kv[:, HW:].T


def _mem_kv(mem, gn, w, gk, grp):
    bsz, m, d = mem.shape
    nl = w.shape[0]
    out = jax.ShapeDtypeStruct((nl, bsz, HW, m), F32)
    o_spec = pl.BlockSpec((1, 1, HW, m), lambda l, b: (l, b, 0, 0))
    return pl.pallas_call(
        _mem_kv_kernel,
        grid=(nl, bsz),
        in_specs=[pl.BlockSpec((1, m, d), lambda l, b: (b, 0, 0)),
                  pl.BlockSpec((1, 1, d), lambda l, b: (l, 0, 0)),
                  pl.BlockSpec((1, d, 2 * HW), lambda l, b: (l, 0, 0)),
                  pl.BlockSpec((1, 1, HW), lambda l, b: (l, 0, 0)),
                  pl.BlockSpec((HW, HW), lambda l, b: (0, 0))],
        out_specs=[o_spec, o_spec],
        out_shape=[out, out],
        compiler_params=_cparams(("parallel", "parallel"), 24 * 1024 * 1024),
        name="mem_kv",
    )(mem, gn.reshape(nl, 1, d), w, jnp.tile(gk, (1, N_HEADS)).reshape(nl, 1, HW), grp)


def _band_attn_kernel(q0_ref, q1_ref, k0_ref, k1_ref, v0_ref, v1_ref, bias_ref, hmask_ref,
                      o_ref, og_s, ls_s, *, seq):
    g = pl.program_id(1)
    n_units = seq // BAND

    def run_group(gi, dil):
        nb = seq // dil // BAND
        shift = nb.bit_length() - 1
        lane = lax.broadcasted_iota(jnp.int32, (1, 128), 1)
        low_lanes = lane < HEAD_DIM

        def idx(start):
            if dil == 1:
                return pl.ds(pl.multiple_of(start, BAND), BAND)
            return pl.ds(start, BAND, stride=dil)

        def rows(ref0, ref1, start):
            return jnp.concatenate([ref0[idx(start), :], ref1[idx(start), :]], axis=1)

        def unit(u, carry):
            r = lax.shift_right_logical(u, shift)
            nblk = u & (nb - 1)
            qs = r + dil * BAND * nblk
            ks = r + dil * BAND * jnp.maximum(nblk - 1, 0)
            q = rows(q0_ref, q1_ref, qs) * (ATTN_SCALE * LOG2E)
            kk = jnp.concatenate([rows(k0_ref, k1_ref, ks), rows(k0_ref, k1_ref, qs)],
                                 axis=0).astype(BF16)
            vv = jnp.concatenate([rows(v0_ref, v1_ref, ks), rows(v0_ref, v1_ref, qs)],
                                 axis=0).astype(BF16)
            q4 = jnp.concatenate([q * hmask_ref[h] for h in range(N_HEADS)], axis=0)
            bias = bias_ref[jnp.minimum(nblk, 1)]
            s = _bdot_nt(q4, kk).reshape(N_HEADS, BAND, 2 * BAND) + bias[None]
            s = s.reshape(N_HEADS * BAND, 2 * BAND)
            m = jnp.max(s, axis=-1, keepdims=True)
            p = jnp.exp2(s - m)
            l = jnp.sum(p, axis=-1, keepdims=True)
            oe = _bdot(p, vv)
            hd = [slice(h * BAND, (h + 1) * BAND) for h in range(N_HEADS)]
            for half in range(2):
                ha, hb = hd[2 * half], hd[2 * half + 1]
                cols = slice(half * 128, (half + 1) * 128)
                o_h = jnp.where(low_lanes, oe[ha, cols], oe[hb, cols])
                l_h = jnp.where(low_lanes, l[ha], l[hb])
                m_h = jnp.where(low_lanes, m[ha], m[hb])
                og_s[2 * gi + half, idx(qs), :] = o_h / l_h
                ls_s[2 * gi + half, idx(qs), :] = (m_h + jnp.log2(l_h)) * LN2
            return carry

        lax.fori_loop(0, n_units, unit, 0, unroll=8)

    for gi, (_, dil) in enumerate(DIL_PAIRS):
        pl.when(g == gi)(functools.partial(run_group, gi, dil))

    @pl.when(g == N_GROUPS - 1)
    def _merge():
        tr = 512
        for c in range(seq // tr):
            rows = slice(c * tr, (c + 1) * tr)
            for half in range(2):
                ls = [ls_s[2 * gi + half, rows, :] for gi in range(N_GROUPS)]
                m = jnp.maximum(jnp.maximum(ls[0], ls[1]), ls[2])
                es = [jnp.exp(x - m) for x in ls]
                inv = 1.0 / (es[0] + es[1] + es[2])
                acc = jnp.zeros((tr, 128), F32)
                for gi in range(N_GROUPS):
                    acc = acc + (es[gi] * inv) * og_s[2 * gi + half, rows, :]
                o_ref[rows, half * 128:(half + 1) * 128] = acc.astype(o_ref.dtype)


def _band_attn(qd, kv, band, hmask, *, batch, seq):
    n = qd.shape[0]
    half = lambda cmap: pl.BlockSpec((seq, 128), cmap)
    qhalf = half
    vmem = (12 * seq * 128 * 4 + 2 * seq * HW * 2 + 12 * seq * 128 * 4 + 6 * 1024 * 1024)
    return pl.pallas_call(
        functools.partial(_band_attn_kernel, seq=seq),
        grid=(batch, N_GROUPS),
        in_specs=[qhalf(lambda b, g: (b, 2 * g)), qhalf(lambda b, g: (b, 2 * g + 1)),
                  half(lambda b, g: (b, 0)), half(lambda b, g: (b, 1)),
                  half(lambda b, g: (b, 2)), half(lambda b, g: (b, 3)),
                  pl.BlockSpec((2, BAND, 2 * BAND), lambda b, g: (0, 0, 0)),
                  pl.BlockSpec((N_HEADS, 1, HW), lambda b, g: (0, 0, 0))],
        out_specs=pl.BlockSpec((seq, HW), lambda b, g: (b, 0)),
        out_shape=jax.ShapeDtypeStruct((n, HW), BF16),
        scratch_shapes=[pltpu.VMEM((2 * N_GROUPS, seq, 128), F32),
                        pltpu.VMEM((2 * N_GROUPS, seq, 128), F32)],
        compiler_params=_cparams(("parallel", "arbitrary"), vmem),
        name="band_attn",
    )(qd, qd, kv, kv, kv, kv, band, hmask)


def _step_bias(cache_len):
    t = np.arange(cache_len)
    rows = np.zeros((16, cache_len), np.float32)
    for gi, (_, dil) in enumerate(DIL_PAIRS):
        keep = (t >= cache_len - dil * BAND) & ((cache_len - t) % dil == 0)
        rows[4 * gi:4 * gi + 4] = np.where(keep, 0.0, NEG)[None]
    return rows


def _sel_bias():
    rows = np.zeros((16, N_GROUPS * BAND), np.float32)
    for gi in range(N_GROUPS):
        keep = np.arange(N_GROUPS * BAND) // BAND == gi
        rows[4 * gi:4 * gi + 4] = np.where(keep, 0.0, NEG)[None]
    return rows


def _step_attn_kernel(q_ref, kn_ref, vn_ref, kt_ref, vt_ref, bias_ref, hrow_ref, *rest,
                      transposed, emit_sel):
    if emit_sel:
        o_ref, ksel_ref, vsel_ref, tk_s, tv_s = rest
        cache_len = kt_ref.shape[2]
        for src, scr, dst in ((kt_ref, tk_s, ksel_ref), (vt_ref, tv_s, vsel_ref)):
            xt = src[0].T
            scr[0] = xt[:, :128]
            scr[1] = xt[:, 128:]
            for gi, (_, dil) in enumerate(DIL_PAIRS):
                start = cache_len - dil * BAND
                idx = pl.ds(start, BAND) if dil == 1 else pl.ds(start, BAND, stride=dil)
                dst[0, gi * BAND:(gi + 1) * BAND, 0:128] = scr[0, idx, :]
                dst[0, gi * BAND:(gi + 1) * BAND, 128:HW] = scr[1, idx, :]
    else:
        o_ref, = rest
    j = pl.program_id(1)
    q = q_ref[pl.ds(j, 1), :] * ATTN_SCALE
    kn = kn_ref[pl.ds(j, 1), :]
    vn = vn_ref[pl.ds(j, 1), :]
    hrow = hrow_ref[...]
    row = lax.broadcasted_iota(jnp.int32, (16, 1), 0)
    q16 = hrow * jnp.where(row < 4, q[:, :HW], jnp.where(row < 8, q[:, HW:2 * HW], q[:, 2 * HW:]))
    if transposed:
        s = _bdot(q16, kt_ref[0]) + bias_ref[...]
    else:
        s = _bdot_nt(q16, kt_ref[0]) + bias_ref[...]
    s_n = jnp.sum(q16 * kn, axis=-1, keepdims=True)
    m = jnp.maximum(jnp.max(s, axis=-1, keepdims=True), s_n)
    p = jnp.exp(s - m)
    p_n = jnp.exp(s_n - m)
    l = jnp.sum(p, axis=-1, keepdims=True) + p_n
    pv = _bdot_nt(p, vt_ref[0]) if transposed else _bdot(p, vt_ref[0])
    o16 = (pv + p_n * vn) / l
    lse = m + jnp.log(l)
    lg = [lse[4 * g:4 * g + 4] for g in range(N_GROUPS)]
    mm = jnp.maximum(jnp.maximum(lg[0], lg[1]), lg[2])
    es = [jnp.exp(x - mm) for x in lg]
    inv = 1.0 / (es[0] + es[1] + es[2])
    acc = jnp.zeros((4, HW), F32)
    for g in range(N_GROUPS):
        acc = acc + (es[g] * inv) * (hrow[0:4] * o16[4 * g:4 * g + 4])
    o_ref[pl.ds(j, 1), :] = jnp.sum(acc, axis=0, keepdims=True)


def _step_attn(q, kn, kvn, kc, vc, bias, hrow16, *, transposed):
    bsz = q.shape[0]
    nb = STEP_ROWS
    rows = lambda w: pl.BlockSpec((nb, w), lambda i, j: (i, 0))
    cache = pl.BlockSpec((1,) + kc.shape[1:], lambda i, j: (i * nb + j, 0, 0))
    n_sel = N_GROUPS * BAND
    out_specs = [rows(HW)]
    out_shape = [jax.ShapeDtypeStruct((bsz, HW), F32)]
    scratch = []
    if transposed:
        cache_len = kc.shape[2]
        assert cache_len >= max(d for _, d in DIL_PAIRS) * BAND
        sel = pl.BlockSpec((1, n_sel, HW), lambda i, j: (i * nb + j, 0, 0))
        out_specs += [sel, sel]
        out_shape += [jax.ShapeDtypeStruct((bsz, n_sel, HW), F32)] * 2
        scratch = [pltpu.VMEM((2, cache_len, 128), F32)] * 2
    vmem = (4 * kc.shape[1] * kc.shape[2] * 4 + 8 * 16 * bias.shape[1] * 4
            + (6 * kc.shape[1] * kc.shape[2] * 4 if transposed else 0) + 4 * 1024 * 1024)
    out = pl.pallas_call(
        functools.partial(_step_attn_kernel, transposed=transposed, emit_sel=transposed),
        grid=(bsz // nb, nb),
        in_specs=[rows(N_GROUPS * HW), rows(HW),
                  pl.BlockSpec((nb, HW), lambda i, j: (i, 1)), cache, cache,
                  pl.BlockSpec(bias.shape, lambda i, j: (0, 0)),
                  pl.BlockSpec((16, HW), lambda i, j: (0, 0))],
        out_specs=out_specs,
        out_shape=out_shape,
        scratch_shapes=scratch,
        compiler_params=_cparams(("parallel", "arbitrary"), vmem),
        name="step_attn",
    )(q, kn, kvn, kc, vc, bias, hrow16)
    return out if transposed else out[0]


def _out_ffn_kernel(h_ref, tok_ref, mem_ref, wo1_ref, wo2_ref, g2_ref, wgu_ref, wdn_ref, *rest,
                    cast):
    if cast:
        ngu_i, ndn_i, o_ref, ngu_o, ndn_o = rest
        ngu_o[...] = ngu_i[...].astype(BF16)
        ndn_o[...] = ndn_i[...].astype(BF16)
    else:
        o_ref, = rest
    h1 = (h_ref[...]
          + jnp.dot(tok_ref[...].astype(BF16), wo1_ref[...], preferred_element_type=F32)
          + jnp.dot(mem_ref[...].astype(BF16), wo2_ref[...], preferred_element_type=F32))
    ms = jnp.mean(h1 * h1, axis=-1, keepdims=True)
    hn = (h1 * lax.rsqrt(ms + EPS) * g2_ref[...]).astype(BF16)
    gate = jnp.dot(hn, wgu_ref[:, :D_FF], preferred_element_type=F32)
    up = jnp.dot(hn, wgu_ref[:, D_FF:], preferred_element_type=F32)
    act = (_silu(gate) * up).astype(BF16)
    o_ref[...] = h1 + jnp.dot(act, wdn_ref[...], preferred_element_type=F32)


def _out_ffn(h, tok, mem, wo, woi, g2, li, wgu, wdn, *, tm, cast=None):
    n, d = h.shape
    wt = tok.shape[1]
    assert wo.shape[1] == wt + HW and wt % HW == 0
    once = pl.Buffered(1)
    vmem = ((wt + HW) * d * 2 + 3 * D_FF * d * 2
            + 2 * tm * (2 * d * 4 + (wt + HW) * 4) + 3 * tm * d * 4 + 3 * tm * D_FF * 4
            + 4 * 1024 * 1024)
    in_specs = [pl.BlockSpec((tm, d), lambda i: (i, 0)),
                pl.BlockSpec((tm, wt), lambda i: (i, 0)),
                pl.BlockSpec((tm, HW), lambda i: (i, 0)),
                pl.BlockSpec((None, wt, d), lambda i: (woi, 0, 0), pipeline_mode=once),
                pl.BlockSpec((None, HW, d), lambda i: (woi, wt // HW, 0), pipeline_mode=once),
                pl.BlockSpec((None, 1, d), lambda i: (li, 0, 0), pipeline_mode=once),
                pl.BlockSpec((d, 2 * D_FF), lambda i: (0, 0), pipeline_mode=once),
                pl.BlockSpec((D_FF, d), lambda i: (0, 0), pipeline_mode=once)]
    args = [h, tok, mem, wo, wo, g2, wgu, wdn]
    out_specs = [pl.BlockSpec((tm, d), lambda i: (i, 0))]
    out_shape = [jax.ShapeDtypeStruct((n, d), F32)]
    if cast is not None:
        wgu32, wdn32, cli = cast
        for w32 in (wgu32, wdn32):
            i_spec, o_spec, o_shape = _cast_specs(w32, cli, n // tm, lambda i: i)
            in_specs.append(i_spec)
            out_specs.append(o_spec)
            out_shape.append(o_shape)
            args.append(w32)
    out = pl.pallas_call(
        functools.partial(_out_ffn_kernel, cast=cast is not None),
        grid=(n // tm,),
        in_specs=in_specs,
        out_specs=out_specs,
        out_shape=out_shape,
        compiler_params=_cparams(("arbitrary" if cast is not None else "parallel",), vmem),
        name="out_ffn",
    )(*args)
    return out if cast is not None else out[0]


def _tail_transpose_kernel(x_ref, o_ref):
    o_ref[0] = x_ref[...].T


def _tail_transpose(x, col, *, batch, seq, keep, tr):
    per, nk = seq // tr, keep // tr
    return pl.pallas_call(
        _tail_transpose_kernel,
        grid=(batch, nk),
        in_specs=[pl.BlockSpec((tr, HW), lambda b, j: (b * per + per - nk + j, col))],
        out_specs=pl.BlockSpec((1, HW, tr), lambda b, j: (b, 0, j)),
        out_shape=jax.ShapeDtypeStruct((batch, HW, keep), F32),
        compiler_params=_cparams(("parallel", "parallel"), 16 * 1024 * 1024),
        name="tail_transpose",
    )(x)


def _pack_a_in(w):
    nl, d, _ = w.shape
    wt = jnp.swapaxes(w, 1, 2).astype(BF16)

    def pad_heads(x):
        x = x.reshape(nl, GLA_HEADS, GLA_DV, d)
        return jnp.pad(x, ((0, 0), (0, 0), (0, GLA_DVP - GLA_DV), (0, 0))).reshape(nl, GLA_VP, d)

    o_v, o_g, o_r = 2 * GLA_QK, 2 * GLA_QK + GLA_V, 2 * GLA_QK + 2 * GLA_V
    glr = jnp.pad(wt[:, o_r:o_r + GLA_RANK], ((0, 0), (0, GLA_RANKP - GLA_RANK), (0, 0)))
    packed = jnp.concatenate([wt[:, :o_v], pad_heads(wt[:, o_v:o_g]), pad_heads(wt[:, o_g:o_r]),
                              wt[:, o_r + GLA_RANK:], glr], axis=1)
    return jnp.swapaxes(packed, 1, 2)


def _pack_a_out(w):
    tok = w[:GLA_V].reshape(GLA_HEADS, GLA_DV, D_MODEL)
    tok = jnp.pad(tok, ((0, 0), (0, GLA_DVP - GLA_DV), (0, 0))).reshape(GLA_VP, D_MODEL)
    return jnp.concatenate([tok, w[GLA_V:]], axis=0).astype(BF16)


def kernel(x_prompt, x_sample, state_gla, cache_win_k, cache_win_v, cache_mem_k, cache_mem_v,
           mem_prompt, norm1, norm2, a_w_in, a_w_gate2, a_b_gate, a_g_onorm, a_w_out, kv_norm, w_kv,
           g_k, b_w_in, b_g_q, b_w_out, mem_norm, w_mem_kv, g_mem_q, g_mem_k, w_ffn_gu, w_ffn_down):
    batch, seq, d = x_prompt.shape
    dec_b = x_sample.shape[0]
    past_len = 8192
    assert d == D_MODEL and seq % 512 == 0 and x_sample.shape[1] == 1

    mall_np, lmask_np = _gla_tables()
    grp_np, hmask_np = _head_tables()
    mall = jnp.asarray(mall_np, BF16)
    lmask = jnp.asarray(lmask_np, F32)
    grp = jnp.asarray(grp_np, BF16)
    hmask = jnp.asarray(hmask_np, F32)
    band = jnp.asarray(_band_bias(), F32)

    hrow8 = jnp.asarray(np.concatenate([hmask_np[:, 0], np.zeros((4, HW), np.float32)], 0))
    hrow16 = jnp.asarray(np.concatenate([hmask_np[:, 0]] * N_GROUPS
                                        + [np.zeros((4, HW), np.float32)], 0))

    cos_p, sin_p = _rope_tables(np.arange(seq))
    cos_s, sin_s = _rope_tables(past_len + np.arange(1))

    wa_in = _pack_a_in(a_w_in)
    wa_out = jnp.stack([_pack_a_out(a_w_out[i]) for i in range(N_A)], 0)
    wg2 = jnp.pad(a_w_gate2, ((0, 0), (0, GLA_RANKP - GLA_RANK), (0, 0))).astype(BF16)
    bg = a_b_gate.reshape(N_A, 1, GLA_QK)
    gon = jnp.pad(a_g_onorm, ((0, 0), (0, GLA_DVP - GLA_DV))).reshape(N_A, 1, GLA_DVP)
    gq_mem = jnp.tile(g_mem_q, (1, N_HEADS)).reshape(DEPTH, 1, HW)
    hg_rope = jnp.tile(jnp.concatenate([b_g_q, g_k.reshape(1, HEAD_DIM)], 0),
                       (1, N_HEADS)).reshape(DEPTH - N_A + 1, 1, HW)
    h0_zero = jnp.zeros((batch, GLA_HEADS, GLA_DVP, GLA_DK), F32)
    wb_in = b_w_in.astype(BF16)
    wb_out = b_w_out.astype(BF16)
    w_kv_b = w_kv.astype(BF16).reshape(1, d, 2 * HW)
    w_mem_b = w_mem_kv.astype(BF16)
    ffn_w = {}
    n1 = norm1.reshape(DEPTH, 1, d)
    n2 = norm2.reshape(DEPTH, 1, d)
    nkv = kv_norm.reshape(1, 1, d)

    mem_kt_p, mem_vt_p = _mem_kv(mem_prompt, mem_norm, w_mem_b, g_mem_k, grp)

    cache_len = cache_win_k.shape[1]
    win_kt = jnp.transpose(cache_win_k, (0, 2, 3, 1)).reshape(dec_b, HW, cache_len)
    win_vt = jnp.transpose(cache_win_v, (0, 2, 3, 1)).reshape(dec_b, HW, cache_len)
    mem_kt_s = jnp.transpose(cache_mem_k, (0, 1, 3, 4, 2)).reshape(DEPTH, dec_b, HW, MEM_TOKENS)
    mem_vt_s = jnp.transpose(cache_mem_v, (0, 1, 3, 4, 2)).reshape(DEPTH, dec_b, HW, MEM_TOKENS)
    state_t = jnp.swapaxes(state_gla, 3, 4)
    step_bias = jnp.asarray(_step_bias(cache_len), F32)
    sel_bias = jnp.asarray(_sel_bias(), F32)

    def trunk(x2, bsz, t, prompt):
        n = bsz * t
        tm = 512 if prompt else n
        h = x2
        states = []
        state_out = None
        win_sel = None
        kv_sh = None
        tabs = (grp, cos_p, sin_p) if prompt else (grp, cos_s, sin_s)
        tr = ROPE_ROWS if prompt else n
        pos_blocks = max(t // tr, 1)
        for li in range(DEPTH):
            if li < N_A:
                if prompt:
                    tok, proj, st, wgu_b, wdn_b = _gla_prompt(
                        h, n1, wa_in, li, h0_zero, wg2, bg, gon, mall, lmask,
                        w_ffn_gu, w_ffn_down, batch=bsz, seq=t, tb=512)
                    ffn_w[li] = (wgu_b, wdn_b)
                    states.append(st[:, :, :GLA_DV])
                    mq_col = 0
                else:
                    proj = _norm_matmul(h, n1, li, wa_in, li, tm=tm, tn=A_IN_P // 3, name="a_in")
                    tok, state_out = _gla_step(proj, state_t, li, state_out, wg2, bg, gon)
                    mq_col = A_COL_MQ // HW
                wo, woi = wa_out, li
            else:
                bi = li - N_A
                proj = _norm_matmul_rope(h, n1, li, wb_in, bi, hg_rope, bi, tabs,
                                         rope_width=N_GROUPS * HW, tm=tr, name="b_in",
                                         pos_blocks=pos_blocks)
                mq_col = (N_GROUPS * HW) // HW
                if prompt:
                    tok = _band_attn(proj, kv_sh, band, hmask, batch=bsz, seq=t)
                else:
                    if win_sel is None:
                        tok, k_sel, v_sel = _step_attn(proj, kv_sh, kv_sh, win_kt, win_vt,
                                                       step_bias, hrow16, transposed=True)
                        win_sel = (k_sel, v_sel)
                    else:
                        tok = _step_attn(proj, kv_sh, kv_sh, *win_sel, sel_bias, hrow16,
                                         transposed=False)
                wo, woi = wb_out, bi
            if prompt:
                mem_o = _mem_attn(proj.reshape(bsz, t, proj.shape[1]), mq_col, mem_kt_p, mem_vt_p,
                                  li, gq_mem, grp, hmask, tq=512).reshape(n, HW)
            else:
                mem_o = _mem_step(proj, mq_col, mem_kt_s, mem_vt_s, li, gq_mem, grp, hrow8)
            if prompt and N_A - 1 <= li < DEPTH - 1:
                h, wgu_b, wdn_b = _out_ffn(h, tok, mem_o, wo, woi, n2, li, *ffn_w[li], tm=tm,
                                           cast=(w_ffn_gu, w_ffn_down, li + 1))
                ffn_w[li + 1] = (wgu_b, wdn_b)
            else:
                h = _out_ffn(h, tok, mem_o, wo, woi, n2, li, *ffn_w[li], tm=tm)
            if li == N_A - 1:
                kv_sh = _norm_matmul_rope(h, nkv, 0, w_kv_b, 0, hg_rope, DEPTH - N_A, tabs,
                                          rope_width=HW, tm=tr,
                                          name="shared_kv", pos_blocks=pos_blocks)
        return h, states, state_out, kv_sh

    y_p, gla_p, _, kv_p = trunk(x_prompt.reshape(batch * seq, d), batch, seq, True)
    y_s, _, gla_s, kv_s = trunk(x_sample.reshape(dec_b, d), dec_b, 1, False)

    keep = min(WIN_MAX, seq)

    def window_out(x, col):
        xt = _tail_transpose(x, col, batch=batch, seq=seq, keep=keep, tr=512)
        return jnp.transpose(xt.reshape(batch, N_HEADS, HEAD_DIM, keep), (0, 3, 1, 2))

    def mem_out(xt):
        return jnp.transpose(xt.reshape(DEPTH, batch, N_HEADS, HEAD_DIM, MEM_TOKENS),
                             (0, 1, 4, 2, 3))

    return (y_p.reshape(batch, seq, d),
            y_s.reshape(dec_b, 1, d),
            jnp.swapaxes(jnp.stack(gla_p, 0), 3, 4),
            jnp.swapaxes(gla_s, 3, 4),
            window_out(kv_p, 0),
            window_out(kv_p, 1),
            kv_s[:, :HW].reshape(dec_b, 1, N_HEADS, HEAD_DIM),
            kv_s[:, HW:].reshape(dec_b, 1, N_HEADS, HEAD_DIM),
            mem_out(mem_kt_p),
            mem_out(mem_vt_p))
```

```python
import functools

import numpy as np
import jax
import jax.numpy as jnp
from jax import lax
from jax.experimental import pallas as pl
from jax.experimental.pallas import tpu as pltpu

F32 = jnp.float32
BF16 = jnp.bfloat16

D_MODEL = 1024
DEPTH = 4
N_A = 2
GLA_HEADS = 4
GLA_DK = 128
GLA_DV = 192
GLA_DVP = 256
GLA_QK = GLA_HEADS * GLA_DK
GLA_V = GLA_HEADS * GLA_DV
GLA_VP = GLA_HEADS * GLA_DVP
GLA_RANK = 16
GLA_RANKP = 128
GLA_TAU = 16.0
GLA_CHUNK = 64
HEAD_DIM = 64
N_HEADS = 4
HW = N_HEADS * HEAD_DIM
DIL_PAIRS = ((128, 1), (512, 4), (2048, 16))
N_GROUPS = 3
BAND = 128
WIN_MAX = 2048
MEM_TOKENS = 256
D_FF = 2816
ROPE_THETA = 10000.0
EPS = 1e-6
ATTN_SCALE = HEAD_DIM ** -0.5
NEG = -1e30
ROPE_ROWS = 1024
ROPE_SUB_ROWS = 1024
LOG2E = 1.4426950408889634
LN2 = 0.6931471805599453

A_COL_Q = 0
A_COL_K = GLA_QK
A_COL_V = 2 * GLA_QK
A_COL_G = A_COL_V + GLA_VP
A_COL_MQ = A_COL_G + GLA_VP
A_COL_GLR = A_COL_MQ + HW
A_IN_P = A_COL_GLR + GLA_RANKP

V7X_VMEM_BYTES = 64 * 1024 * 1024
VMEM_CAP = V7X_VMEM_BYTES - 8 * 1024 * 1024


def _cparams(sem, vmem_bytes):
    return pltpu.CompilerParams(
        dimension_semantics=sem,
        vmem_limit_bytes=int(min(max(vmem_bytes, 16 * 1024 * 1024), VMEM_CAP)))


def _bdot(a, b):
    return jnp.dot(a.astype(BF16), b.astype(BF16), preferred_element_type=F32)


def _bdot_nt(a, b):
    return lax.dot_general(a.astype(BF16), b.astype(BF16), (((1,), (1,)), ((), ())),
                           preferred_element_type=F32)


def _bdot_tn(a, b):
    return lax.dot_general(a.astype(BF16), b.astype(BF16), (((0,), (0,)), ((), ())),
                           preferred_element_type=F32)


def _split(x):
    hi = x.astype(BF16)
    lo = (x - hi.astype(F32)).astype(BF16)
    return hi, lo


def _dot_exact_rhs(x, m):
    hi, lo = _split(x)
    return (jnp.dot(hi, m, preferred_element_type=F32)
            + jnp.dot(lo, m, preferred_element_type=F32))


def _dot_exact_lhs(m2, x):
    hi, lo = _split(x)
    return jnp.dot(m2, jnp.concatenate([hi, lo], axis=0), preferred_element_type=F32)


def _silu(x):
    return x * jax.nn.sigmoid(x)


def _log_sigmoid(z):
    return jnp.minimum(z, 0.0) - jnp.log1p(jnp.exp(-jnp.abs(z)))


def _gla_tables():
    c = GLA_CHUNK
    i = np.arange(c)[:, None]
    t = np.arange(c)[None, :]
    blocks = [(t <= i), (t > i)]
    masks = []
    for lvl in range(6):
        s = c >> lvl
        half = s // 2
        mid = (i // s) * s + half - 1
        second = (i % s) >= half
        m = np.where(second, (t > mid) & (t <= i), (t > i) & (t <= mid))
        blocks.append(m)
        j = t
        masks.append(((i // s) == (j // s)) & ((i % s) >= half) & ((j % s) < half))
    masks.append(i == t)
    mall = np.concatenate(blocks, axis=0).astype(np.float32)
    mall = np.concatenate([mall, mall], axis=1)
    lmask = np.stack(masks, axis=0).astype(np.float32)
    assert np.array_equal(lmask.sum(0), (t <= i).astype(np.float32))
    return mall, lmask


def _head_tables():
    lane = np.arange(HW)
    group = (lane[:, None] // HEAD_DIM == lane[None, :] // HEAD_DIM).astype(np.float32)
    hmask = (lane[None, :] // HEAD_DIM == np.arange(N_HEADS)[:, None]).astype(np.float32)
    return group, hmask.reshape(N_HEADS, 1, HW)


def _band_bias():
    i = np.arange(BAND)[:, None]
    j = np.arange(2 * BAND)[None, :]
    dist = i + BAND - j
    ok = (dist >= 0) & (dist <= BAND)
    first = ok & (j >= BAND)
    return np.where(np.stack([first, ok], 0), 0.0, NEG).astype(np.float32)


def _rope_tables(pos):
    half = HEAD_DIM // 2
    inv = ROPE_THETA ** (-np.arange(half, dtype=np.float64) / half)
    ang = np.asarray(pos, np.float64)[:, None] * inv[None, :]
    cos = np.cos(ang).astype(np.float32)
    sin = np.sin(ang).astype(np.float32)
    cos64 = np.concatenate([cos, cos], axis=-1)
    sin64 = np.concatenate([-sin, sin], axis=-1)
    return jnp.asarray(np.tile(cos64, (1, N_HEADS))), jnp.asarray(np.tile(sin64, (1, N_HEADS)))


def _norm_matmul_kernel(x_ref, g_ref, w_ref, o_ref):
    x = x_ref[...]
    ms = jnp.mean(x * x, axis=-1, keepdims=True)
    y = x * lax.rsqrt(ms + EPS) * g_ref[...]
    o_ref[...] = jnp.dot(y.astype(BF16), w_ref[...], preferred_element_type=F32)


def _norm_matmul(x, g, gi, w, wi, *, tm, tn, name):
    n, d = x.shape
    nout = w.shape[2]
    vmem = 2 * (tm * d * 4 + d * tn * 2 + tm * tn * 4) + 2 * tm * d * 4 + tm * tn * 4
    return pl.pallas_call(
        _norm_matmul_kernel,
        grid=(n // tm, nout // tn),
        in_specs=[pl.BlockSpec((tm, d), lambda i, j: (i, 0)),
                  pl.BlockSpec((None, 1, d), lambda i, j: (gi, 0, 0)),
                  pl.BlockSpec((None, d, tn), lambda i, j: (wi, 0, j))],
        out_specs=pl.BlockSpec((tm, tn), lambda i, j: (i, j)),
        out_shape=jax.ShapeDtypeStruct((n, nout), F32),
        compiler_params=_cparams(("parallel", "parallel"), vmem),
        name=name,
    )(x, g, w)


def _norm_matmul_rope_kernel(x_ref, g_ref, w_ref, hg_ref, grp_ref, cos_ref, sin_ref, *rest,
                             rope_width, cast):
    if cast:
        wgu_i, wdn_i, o_ref, wgu_o, wdn_o = rest
        wgu_o[...] = wgu_i[...].astype(BF16)
        wdn_o[...] = wdn_i[...].astype(BF16)
    else:
        o_ref, = rest
    grp = grp_ref[...]
    lane = lax.broadcasted_iota(jnp.int32, (1, HW), 1)
    first_half = (lane & (HEAD_DIM - 1)) < (HEAD_DIM // 2)
    tm = x_ref.shape[0]
    sub = min(tm, ROPE_SUB_ROWS)
    for r in range(tm // sub):
        rs = slice(r * sub, (r + 1) * sub)
        trs = rs if cos_ref.shape[0] == tm else slice(None)
        x = x_ref[rs, :]
        ms = jnp.mean(x * x, axis=-1, keepdims=True)
        xn = x * lax.rsqrt(ms + EPS) * g_ref[...]
        y = jnp.dot(xn.astype(BF16), w_ref[...], preferred_element_type=F32)
        for c in range(rope_width // HW):
            sl = slice(c * HW, (c + 1) * HW)
            yc = y[:, sl]
            ss = _dot_exact_rhs(yc * yc, grp)
            yc = yc * lax.rsqrt(ss * (1.0 / HEAD_DIM) + EPS) * hg_ref[...]
            fwd = pltpu.roll(yc, HW - HEAD_DIM // 2, 1)
            bwd = pltpu.roll(yc, HEAD_DIM // 2, 1)
            rot = jnp.where(first_half, fwd, bwd)
            o_ref[rs, sl] = yc * cos_ref[trs, :] + rot * sin_ref[trs, :]
        o_ref[rs, rope_width:] = y[:, rope_width:]


def _norm_matmul_rope(x, g, gi, w, wi, hg, hgi, tabs, *, rope_width, tm, name, pos_blocks=1,
                      cast=None):
    n, d = x.shape
    nout = w.shape[2]
    grp, cos, sin = tabs
    trow = 1 if cos.shape[0] == 1 else tm
    tab_map = (lambda i: (0, 0)) if cos.shape[0] == 1 else (lambda i: (i % pos_blocks, 0))
    vmem = (2 * (tm * d * 4 + tm * nout * 4 + 2 * trow * HW * 4) + d * nout * 2
            + 2 * tm * d * 4 + 3 * tm * nout * 4 + (8 * 1024 * 1024 if cast else 0))
    in_specs = [pl.BlockSpec((tm, d), lambda i: (i, 0)),
                pl.BlockSpec((None, 1, d), lambda i: (gi, 0, 0)),
                pl.BlockSpec((None, d, nout), lambda i: (wi, 0, 0), pipeline_mode=pl.Buffered(1)),
                pl.BlockSpec((None, 1, HW), lambda i: (hgi, 0, 0)),
                pl.BlockSpec((HW, HW), lambda i: (0, 0)),
                pl.BlockSpec((trow, HW), tab_map),
                pl.BlockSpec((trow, HW), tab_map)]
    args = [x, g, w, hg, grp, cos, sin]
    out_specs = [pl.BlockSpec((tm, nout), lambda i: (i, 0))]
    out_shape = [jax.ShapeDtypeStruct((n, nout), F32)]
    if cast is not None:
        wgu32, wdn32, cli = cast
        for w32 in (wgu32, wdn32):
            i_spec, o_spec, o_shape = _cast_specs(w32, cli, n // tm, lambda i: i)
            in_specs.append(i_spec)
            out_specs.append(o_spec)
            out_shape.append(o_shape)
            args.append(w32)
    out = pl.pallas_call(
        functools.partial(_norm_matmul_rope_kernel, rope_width=rope_width, cast=cast is not None),
        grid=(n // tm,),
        in_specs=in_specs,
        out_specs=out_specs,
        out_shape=out_shape,
        compiler_params=_cparams(("arbitrary" if cast is not None else "parallel",), vmem),
        name=name,
    )(*args)
    return out if cast is not None else out[0]


def _gla_block(q, k, v, g, la, ht, gon, mall_ref, lmask_ref, n_chunks):
    cr = GLA_CHUNK
    chunk = lambda c: slice(c * cr, (c + 1) * cr)
    la_cat = jnp.concatenate([la[chunk(c)] for c in range(n_chunks)], axis=1)
    e_all = jnp.exp2(_dot_exact_lhs(mall_ref[...], la_cat * LOG2E))

    def decay_rows(blk):
        return jnp.concatenate(
            [e_all[blk * cr:(blk + 1) * cr, c * GLA_DK:(c + 1) * GLA_DK] for c in range(n_chunks)],
            axis=0)

    eb = decay_rows(0)
    qb = (q * eb).astype(BF16)
    kdec = (k * decay_rows(1)).astype(BF16)
    q_lv = [q.astype(BF16)]
    k_lv = [k.astype(BF16)]
    for lvl in range(6):
        el = decay_rows(2 + lvl)
        q_lv.append((q * el).astype(BF16))
        k_lv.append((k * el).astype(BF16))
    o_intra, kv_new = [], []
    for c in range(n_chunks):
        r = chunk(c)
        s = lmask_ref[6] * _bdot_nt(q_lv[0][r], k_lv[0][r])
        for lvl in range(6):
            s = s + lmask_ref[lvl] * _bdot_nt(q_lv[1 + lvl][r], k_lv[1 + lvl][r])
        o_intra.append(_bdot(s, v[r]))
        kv_new.append(_bdot_tn(v[r], kdec[r]))
    outs = []
    for c in range(n_chunks):
        outs.append(o_intra[c] + _bdot_nt(qb[chunk(c)], ht))
        ht = ht * eb[c * cr + cr - 1:(c + 1) * cr, :] + kv_new[c]
    o = jnp.concatenate(outs, axis=0)
    ms = jnp.sum(o * o, axis=-1, keepdims=True) * (1.0 / GLA_DV)
    return o * lax.rsqrt(ms + EPS) * gon * _silu(g), ht


def _cast_specs(w32, li, n_steps, step_of):
    rows, cols = w32.shape[1:]
    chunk, rep = rows // n_steps, 1
    assert chunk * n_steps == rows
    while (chunk * rep) % 16:
        rep *= 2
    chunk *= rep
    return (pl.BlockSpec((None, chunk, cols), lambda *ids: (li, step_of(*ids) // rep, 0)),
            pl.BlockSpec((chunk, cols), lambda *ids: (step_of(*ids) // rep, 0)),
            jax.ShapeDtypeStruct((rows, cols), BF16))


def _gla_kernel(x_ref, g1_ref, w_ref, wg2_ref, bg_ref, gon_ref, mall_ref, lmask_ref, h0_ref,
                wgu_i, wdn_i, tok_ref, mq_ref, hout_ref, wgu_o, wdn_o, proj_s, ht_s, *, n_chunks):
    t = pl.program_id(1)
    wgu_o[...] = wgu_i[...].astype(BF16)
    wdn_o[...] = wdn_i[...].astype(BF16)

    @pl.when(t == 0)
    def _():
        ht_s[...] = h0_ref[0]

    x = x_ref[...]
    ms = jnp.mean(x * x, axis=-1, keepdims=True)
    xn = (x * lax.rsqrt(ms + EPS) * g1_ref[...]).astype(BF16)
    proj_s[...] = jnp.dot(xn, w_ref[...], preferred_element_type=F32)
    mq_ref[...] = proj_s[:, A_COL_MQ:A_COL_MQ + HW]
    z = _bdot(proj_s[:, A_COL_GLR:A_COL_GLR + GLA_RANKP], wg2_ref[...]) + bg_ref[...]
    la = _log_sigmoid(z) * (1.0 / GLA_TAU)
    gon = gon_ref[...]
    for h in range(GLA_HEADS):
        kc = slice(h * GLA_DK, (h + 1) * GLA_DK)
        vc = slice(h * GLA_DVP, (h + 1) * GLA_DVP)
        q = proj_s[:, A_COL_Q + h * GLA_DK:A_COL_Q + (h + 1) * GLA_DK] * (GLA_DK ** -0.5)
        k = proj_s[:, A_COL_K + h * GLA_DK:A_COL_K + (h + 1) * GLA_DK]
        v = proj_s[:, A_COL_V + h * GLA_DVP:A_COL_V + (h + 1) * GLA_DVP].astype(BF16)
        g = proj_s[:, A_COL_G + h * GLA_DVP:A_COL_G + (h + 1) * GLA_DVP]
        tok, ht = _gla_block(q, k, v, g, la[:, kc], ht_s[h], gon, mall_ref, lmask_ref, n_chunks)
        tok_ref[:, vc] = tok.astype(tok_ref.dtype)
        ht_s[h] = ht

    @pl.when(t == pl.num_programs(1) - 1)
    def _():
        hout_ref[0] = ht_s[...]


def _gla_prompt(x, g1, w_in, li, h0t, wg2, bg, gon, mall, lmask, wgu32, wdn32, *, batch, seq, tb):
    n, d = x.shape
    nt = seq // tb
    const2 = lambda shape: pl.BlockSpec(shape, lambda b, t: (0,) * len(shape))
    st_spec = pl.BlockSpec((1, GLA_HEADS, GLA_DVP, GLA_DK), lambda b, t: (b, 0, 0, 0))
    step_of = lambda b, t: b * nt + t
    gu_in, gu_out, gu_shape = _cast_specs(wgu32, li, batch * nt, step_of)
    dn_in, dn_out, dn_shape = _cast_specs(wdn32, li, batch * nt, step_of)
    vmem = (2 * tb * d * 4 + d * A_IN_P * 2 + tb * A_IN_P * 4 + 2 * tb * (GLA_VP * 2 + HW * 4)
            + 6 * GLA_HEADS * GLA_DVP * GLA_DK * 4 + 24 * tb * GLA_DVP * 4 + 8 * 1024 * 1024)
    return pl.pallas_call(
        functools.partial(_gla_kernel, n_chunks=tb // GLA_CHUNK),
        grid=(batch, nt),
        in_specs=[
            pl.BlockSpec((tb, d), lambda b, t: (b * nt + t, 0)),
            pl.BlockSpec((None, 1, d), lambda b, t: (li, 0, 0)),
            pl.BlockSpec((None, d, A_IN_P), lambda b, t: (li, 0, 0), pipeline_mode=pl.Buffered(1)),
            pl.BlockSpec((None, GLA_RANKP, GLA_QK), lambda b, t: (li, 0, 0)),
            pl.BlockSpec((None, 1, GLA_QK), lambda b, t: (li, 0, 0)),
            pl.BlockSpec((None, 1, GLA_DVP), lambda b, t: (li, 0, 0)),
            const2(mall.shape), const2(lmask.shape), st_spec, gu_in, dn_in,
        ],
        out_specs=[
            pl.BlockSpec((tb, GLA_VP), lambda b, t: (b * nt + t, 0)),
            pl.BlockSpec((tb, HW), lambda b, t: (b * nt + t, 0)),
            st_spec, gu_out, dn_out,
        ],
        out_shape=[jax.ShapeDtypeStruct((n, GLA_VP), BF16),
                   jax.ShapeDtypeStruct((n, HW), F32),
                   jax.ShapeDtypeStruct((batch, GLA_HEADS, GLA_DVP, GLA_DK), F32),
                   gu_shape, dn_shape],
        scratch_shapes=[pltpu.VMEM((tb, A_IN_P), F32),
                        pltpu.VMEM((GLA_HEADS, GLA_DVP, GLA_DK), F32)],
        compiler_params=_cparams(("parallel", "arbitrary"), vmem),
        name="gla_prompt",
    )(x, g1, w_in, wg2, bg, gon, mall, lmask, h0t, wgu32, wdn32)


STEP_ROWS = 8


def _gla_step_kernel(q_ref, k_ref, v_ref, g_ref, glr_ref, wg2_ref, bg_ref, gon_ref, st_ref, *rest,
                     out_layer):
    tok_ref, so_ref = rest[-2:]
    nb = q_ref.shape[0]
    for other in range(so_ref.shape[0]):
        if other != out_layer:
            so_ref[other] = jnp.zeros(so_ref.shape[1:], F32)
    z = _bdot(glr_ref[...], wg2_ref[...]) + bg_ref[...]
    a = jnp.exp(_log_sigmoid(z) * (1.0 / GLA_TAU))
    q = q_ref[...] * (GLA_DK ** -0.5)
    k = k_ref[...]
    v = v_ref[...]
    g = g_ref[...]
    gon = gon_ref[...][:, :GLA_DV]
    row = lax.broadcasted_iota(jnp.int32, (nb, 1), 0)
    for h in range(GLA_HEADS):
        ks = slice(h * GLA_DK, (h + 1) * GLA_DK)
        vs = slice(h * GLA_DVP, h * GLA_DVP + GLA_DV)
        o_rows = jnp.zeros((nb, GLA_DV), F32)
        for b in range(nb):
            only_b = row == b
            outer = _bdot_tn(jnp.where(only_b, v[:, vs], 0.0), k[:, ks])
            s_new = st_ref[0, b, h] * a[b:b + 1, ks] + outer
            so_ref[out_layer, b, h] = s_new
            o_rows = o_rows + jnp.where(only_b, _bdot_nt(q[:, ks], s_new), 0.0)
        ms = jnp.sum(o_rows * o_rows, axis=-1, keepdims=True) * (1.0 / GLA_DV)
        tok_ref[:, vs] = o_rows * lax.rsqrt(ms + EPS) * gon * _silu(g[:, vs])
        tok_ref[:, h * GLA_DVP + GLA_DV:(h + 1) * GLA_DVP] = jnp.zeros(
            (nb, GLA_DVP - GLA_DV), F32)


def _gla_step(proj, state_t, li, prev_out, wg2, bg, gon):
    bsz = proj.shape[0]
    nb = STEP_ROWS
    col = lambda w, c: pl.BlockSpec((nb, w), lambda i: (i, c))
    st_spec = pl.BlockSpec((1, nb, GLA_HEADS, GLA_DV, GLA_DK), lambda i: (li, i, 0, 0, 0))
    in_specs = [col(GLA_QK, A_COL_Q // GLA_QK), col(GLA_QK, A_COL_K // GLA_QK),
                col(GLA_VP, A_COL_V // GLA_VP), col(GLA_VP, A_COL_G // GLA_VP),
                col(GLA_RANKP, A_COL_GLR // GLA_RANKP),
                pl.BlockSpec((None, GLA_RANKP, GLA_QK), lambda i: (li, 0, 0)),
                pl.BlockSpec((None, 1, GLA_QK), lambda i: (li, 0, 0)),
                pl.BlockSpec((None, 1, GLA_DVP), lambda i: (li, 0, 0)),
                st_spec]
    args = [proj, proj, proj, proj, proj, wg2, bg, gon, state_t]
    if prev_out is None:
        aliases, out_layer = {}, li
        so_spec = pl.BlockSpec((state_t.shape[0], nb, GLA_HEADS, GLA_DV, GLA_DK),
                               lambda i: (0, i, 0, 0, 0))
    else:
        in_specs.append(pl.BlockSpec(memory_space=pl.ANY))
        args.append(prev_out)
        aliases, out_layer, so_spec = {len(args) - 1: 1}, 0, st_spec
    return pl.pallas_call(
        functools.partial(_gla_step_kernel, out_layer=out_layer),
        grid=(bsz // nb,),
        in_specs=in_specs,
        out_specs=[pl.BlockSpec((nb, GLA_VP), lambda i: (i, 0)), so_spec],
        out_shape=[jax.ShapeDtypeStruct((bsz, GLA_VP), F32),
                   jax.ShapeDtypeStruct(state_t.shape, F32)],
        input_output_aliases=aliases,
        compiler_params=_cparams(("parallel",), 32 * 1024 * 1024),
        name="gla_step",
    )(*args)


def _mem_qnorm(q, gq_ref, grp_ref):
    ss = _dot_exact_rhs(q * q, grp_ref[...])
    return q * lax.rsqrt(ss * (1.0 / HEAD_DIM) + EPS) * gq_ref[...] * ATTN_SCALE


def _mem_attn_kernel(q_ref, kt_ref, vt_ref, gq_ref, grp_ref, hmask_ref, o_ref):
    qn = _mem_qnorm(q_ref[0], gq_ref, grp_ref)
    tq = qn.shape[0]
    kt = kt_ref[0, 0].astype(BF16)
    vt = vt_ref[0, 0].astype(BF16)
    q4 = jnp.concatenate([qn * hmask_ref[h] for h in range(N_HEADS)], axis=0)
    s = _bdot(q4, kt)
    m = jnp.max(s, axis=-1, keepdims=True)
    p = jnp.exp(s - m)
    o4 = _bdot_nt(p, vt) / jnp.sum(p, axis=-1, keepdims=True)
    out = jnp.zeros(qn.shape, F32)
    for h in range(N_HEADS):
        out = out + hmask_ref[h] * o4[h * tq:(h + 1) * tq]
    o_ref[0] = out.astype(o_ref.dtype)


def _mem_attn(q3, qcol, kt, vt, li, gq, grp, hmask, *, tq):
    bsz, t, _ = q3.shape
    g = gq
    vmem = 2 * (tq * HW * 4 + 2 * MEM_TOKENS * HW * 4 + tq * HW * 2) + 16 * tq * HW * 4
    kv_spec = pl.BlockSpec((1, 1, HW, MEM_TOKENS), lambda b, i: (li, b, 0, 0))
    return pl.pallas_call(
        _mem_attn_kernel,
        grid=(bsz, t // tq),
        in_specs=[pl.BlockSpec((1, tq, HW), lambda b, i: (b, i, qcol)),
                  kv_spec, kv_spec,
                  pl.BlockSpec((None, 1, HW), lambda b, i: (li, 0, 0)),
                  pl.BlockSpec((HW, HW), lambda b, i: (0, 0)),
                  pl.BlockSpec((N_HEADS, 1, HW), lambda b, i: (0, 0, 0))],
        out_specs=pl.BlockSpec((1, tq, HW), lambda b, i: (b, i, 0)),
        out_shape=jax.ShapeDtypeStruct((bsz, t, HW), BF16),
        compiler_params=_cparams(("parallel", "parallel"), vmem),
        name="mem_attn",
    )(q3, kt, vt, g, grp, hmask)


def _mem_step_kernel(q_ref, kt_ref, vt_ref, gq_ref, grp_ref, hrow_ref, o_ref):
    nb = q_ref.shape[0]
    qn = _mem_qnorm(q_ref[...], gq_ref, grp_ref)
    hrow = hrow_ref[...]
    row = lax.broadcasted_iota(jnp.int32, (nb, 1), 0)
    out = jnp.zeros((nb, HW), F32)
    for b in range(nb):
        q4 = hrow * qn[b:b + 1]
        s = _bdot(q4, kt_ref[0, b])
        m = jnp.max(s, axis=-1, keepdims=True)
        p = jnp.exp(s - m)
        l = jnp.sum(p, axis=-1, keepdims=True)
        o4 = _bdot_nt(p, vt_ref[0, b]) / l
        o_b = jnp.sum(hrow * o4, axis=0, keepdims=True)
        out = out + jnp.where(row == b, o_b, 0.0)
    o_ref[...] = out


def _mem_step(q, qcol, kt, vt, li, gq, grp, hrow):
    bsz = q.shape[0]
    nb = STEP_ROWS
    g = gq
    kv_spec = pl.BlockSpec((1, nb, HW, MEM_TOKENS), lambda i: (li, i, 0, 0))
    return pl.pallas_call(
        _mem_step_kernel,
        grid=(bsz // nb,),
        in_specs=[pl.BlockSpec((nb, HW), lambda i: (i, qcol)),
                  kv_spec, kv_spec,
                  pl.BlockSpec((None, 1, HW), lambda i: (li, 0, 0)),
                  pl.BlockSpec((HW, HW), lambda i: (0, 0)),
                  pl.BlockSpec((8, HW), lambda i: (0, 0))],
        out_specs=pl.BlockSpec((nb, HW), lambda i: (i, 0)),
        out_shape=jax.ShapeDtypeStruct((bsz, HW), F32),
        compiler_params=_cparams(("parallel",), 24 * 1024 * 1024),
        name="mem_step",
    )(q, kt, vt, g, grp, hrow)


def _mem_kv_kernel(x_ref, gn_ref, w_ref, gk_ref, grp_ref, kt_ref, vt_ref):
    x = x_ref[0]
    ms = jnp.mean(x * x, axis=-1, keepdims=True)
    y = x * lax.rsqrt(ms + EPS) * gn_ref[0]
    kv = jnp.dot(y.astype(BF16), w_ref[0], preferred_element_type=F32)
    k = kv[:, :HW]
    ss = _dot_exact_rhs(k * k, grp_ref[...])
    k = k * lax.rsqrt(ss * (1.0 / HEAD_DIM) + EPS) * gk_ref[0]
    kt_ref[0, 0] = k.T
    vt_ref[0, 0] = kv[:, HW:].T


def _mem_kv(mem, gn, w, gk, grp):
    bsz, m, d = mem.shape
    nl = w.shape[0]
    out = jax.ShapeDtypeStruct((nl, bsz, HW, m), F32)
    o_spec = pl.BlockSpec((1, 1, HW, m), lambda l, b: (l, b, 0, 0))
    return pl.pallas_call(
        _mem_kv_kernel,
        grid=(nl, bsz),
        in_specs=[pl.BlockSpec((1, m, d), lambda l, b: (b, 0, 0)),
                  pl.BlockSpec((1, 1, d), lambda l, b: (l, 0, 0)),
                  pl.BlockSpec((1, d, 2 * HW), lambda l, b: (l, 0, 0)),
                  pl.BlockSpec((1, 1, HW), lambda l, b: (l, 0, 0)),
                  pl.BlockSpec((HW, HW), lambda l, b: (0, 0))],
        out_specs=[o_spec, o_spec],
        out_shape=[out, out],
        compiler_params=_cparams(("parallel", "parallel"), 24 * 1024 * 1024),
        name="mem_kv",
    )(mem, gn.reshape(nl, 1, d), w, jnp.tile(gk, (1, N_HEADS)).reshape(nl, 1, HW), grp)


def _band_attn_kernel(q0_ref, q1_ref, k0_ref, k1_ref, v0_ref, v1_ref, bias_ref, hmask_ref,
                      o_ref, og_s, ls_s, *, seq):
    g = pl.program_id(1)

    def run_group(gi, dil):
        nb = seq // dil // BAND
        lane = lax.broadcasted_iota(jnp.int32, (1, 128), 1)
        low_lanes = lane < HEAD_DIM

        def idx(start):
            if dil == 1:
                return pl.ds(pl.multiple_of(start, BAND), BAND)
            return pl.ds(start, BAND, stride=dil)

        def rows(ref0, ref1, start):
            return jnp.concatenate([ref0[idx(start), :], ref1[idx(start), :]], axis=1)

        def attend(qs, first):
            q = rows(q0_ref, q1_ref, qs) * (ATTN_SCALE * LOG2E)
            if first:
                kk = rows(k0_ref, k1_ref, qs).astype(BF16)
                vv = rows(v0_ref, v1_ref, qs).astype(BF16)
                bias = bias_ref[0, :, BAND:]
            else:
                ks = qs - dil * BAND
                kk = jnp.concatenate([rows(k0_ref, k1_ref, ks), rows(k0_ref, k1_ref, qs)],
                                     axis=0).astype(BF16)
                vv = jnp.concatenate([rows(v0_ref, v1_ref, ks), rows(v0_ref, v1_ref, qs)],
                                     axis=0).astype(BF16)
                bias = bias_ref[1]
            nk = kk.shape[0]
            q4 = jnp.concatenate([q * hmask_ref[h] for h in range(N_HEADS)], axis=0)
            s = _bdot_nt(q4, kk).reshape(N_HEADS, BAND, nk) + bias[None]
            s = s.reshape(N_HEADS * BAND, nk)
            m = jnp.max(s, axis=-1, keepdims=True)
            p = jnp.exp2(s - m)
            l = jnp.sum(p, axis=-1, keepdims=True)
            oe = _bdot(p, vv)
            hd = [slice(h * BAND, (h + 1) * BAND) for h in range(N_HEADS)]
            for half in range(2):
                ha, hb = hd[2 * half], hd[2 * half + 1]
                cols = slice(half * 128, (half + 1) * 128)
                o_h = jnp.where(low_lanes, oe[ha, cols], oe[hb, cols])
                l_h = jnp.where(low_lanes, l[ha], l[hb])
                m_h = jnp.where(low_lanes, m[ha], m[hb])
                og_s[2 * gi + half, idx(qs), :] = o_h / l_h
                ls_s[2 * gi + half, idx(qs), :] = (m_h + jnp.log2(l_h)) * LN2

        def first_unit(r, carry):
            attend(r, True)
            return carry

        def later_unit(u, carry):
            r = lax.div(u, jnp.int32(nb - 1))
            nblk = 1 + lax.rem(u, jnp.int32(nb - 1))
            attend(r + dil * BAND * nblk, False)
            return carry

        lax.fori_loop(0, dil, first_unit, 0, unroll=min(dil, 8))
        lax.fori_loop(0, dil * (nb - 1), later_unit, 0, unroll=8)

    for gi, (_, dil) in enumerate(DIL_PAIRS):
        pl.when(g == gi)(functools.partial(run_group, gi, dil))

    @pl.when(g == N_GROUPS - 1)
    def _merge():
        tr = 512
        for c in range(seq // tr):
            rows = slice(c * tr, (c + 1) * tr)
            for half in range(2):
                ls = [ls_s[2 * gi + half, rows, :] for gi in range(N_GROUPS)]
                m = jnp.maximum(jnp.maximum(ls[0], ls[1]), ls[2])
                es = [jnp.exp(x - m) for x in ls]
                inv = 1.0 / (es[0] + es[1] + es[2])
                acc = jnp.zeros((tr, 128), F32)
                for gi in range(N_GROUPS):
                    acc = acc + (es[gi] * inv) * og_s[2 * gi + half, rows, :]
                o_ref[rows, half * 128:(half + 1) * 128] = acc.astype(o_ref.dtype)


def _band_attn(qd, kv, band, hmask, *, batch, seq):
    n = qd.shape[0]
    half = lambda cmap: pl.BlockSpec((seq, 128), cmap)
    qhalf = half
    vmem = (12 * seq * 128 * 4 + 2 * seq * HW * 2 + 12 * seq * 128 * 4 + 6 * 1024 * 1024)
    return pl.pallas_call(
        functools.partial(_band_attn_kernel, seq=seq),
        grid=(batch, N_GROUPS),
        in_specs=[qhalf(lambda b, g: (b, 2 * g)), qhalf(lambda b, g: (b, 2 * g + 1)),
                  half(lambda b, g: (b, 0)), half(lambda b, g: (b, 1)),
                  half(lambda b, g: (b, 2)), half(lambda b, g: (b, 3)),
                  pl.BlockSpec((2, BAND, 2 * BAND), lambda b, g: (0, 0, 0)),
                  pl.BlockSpec((N_HEADS, 1, HW), lambda b, g: (0, 0, 0))],
        out_specs=pl.BlockSpec((seq, HW), lambda b, g: (b, 0)),
        out_shape=jax.ShapeDtypeStruct((n, HW), BF16),
        scratch_shapes=[pltpu.VMEM((2 * N_GROUPS, seq, 128), F32),
                        pltpu.VMEM((2 * N_GROUPS, seq, 128), F32)],
        compiler_params=_cparams(("parallel", "arbitrary"), vmem),
        name="band_attn",
    )(qd, qd, kv, kv, kv, kv, band, hmask)


def _step_bias(cache_len):
    t = np.arange(cache_len)
    rows = np.zeros((16, cache_len), np.float32)
    for gi, (_, dil) in enumerate(DIL_PAIRS):
        keep = (t >= cache_len - dil * BAND) & ((cache_len - t) % dil == 0)
        rows[4 * gi:4 * gi + 4] = np.where(keep, 0.0, NEG)[None]
    return rows


def _sel_bias():
    rows = np.zeros((16, N_GROUPS * BAND), np.float32)
    for gi in range(N_GROUPS):
        keep = np.arange(N_GROUPS * BAND) // BAND == gi
        rows[4 * gi:4 * gi + 4] = np.where(keep, 0.0, NEG)[None]
    return rows


def _step_attn_kernel(q_ref, kn_ref, vn_ref, kt_ref, vt_ref, bias_ref, hrow_ref, *rest,
                      transposed, emit_sel):
    if emit_sel:
        o_ref, ksel_ref, vsel_ref, tk_s, tv_s = rest
        cache_len = kt_ref.shape[2]
        for src, scr, dst in ((kt_ref, tk_s, ksel_ref), (vt_ref, tv_s, vsel_ref)):
            xt = src[0].T
            scr[0] = xt[:, :128]
            scr[1] = xt[:, 128:]
            for gi, (_, dil) in enumerate(DIL_PAIRS):
                start = cache_len - dil * BAND
                idx = pl.ds(start, BAND) if dil == 1 else pl.ds(start, BAND, stride=dil)
                dst[0, gi * BAND:(gi + 1) * BAND, 0:128] = scr[0, idx, :]
                dst[0, gi * BAND:(gi + 1) * BAND, 128:HW] = scr[1, idx, :]
    else:
        o_ref, = rest
    j = pl.program_id(1)
    q = q_ref[pl.ds(j, 1), :] * ATTN_SCALE
    kn = kn_ref[pl.ds(j, 1), :]
    vn = vn_ref[pl.ds(j, 1), :]
    hrow = hrow_ref[...]
    row = lax.broadcasted_iota(jnp.int32, (16, 1), 0)
    q16 = hrow * jnp.where(row < 4, q[:, :HW], jnp.where(row < 8, q[:, HW:2 * HW], q[:, 2 * HW:]))
    if transposed:
        s = _bdot(q16, kt_ref[0]) + bias_ref[...]
    else:
        s = _bdot_nt(q16, kt_ref[0]) + bias_ref[...]
    s_n = jnp.sum(q16 * kn, axis=-1, keepdims=True)
    m = jnp.maximum(jnp.max(s, axis=-1, keepdims=True), s_n)
    p = jnp.exp(s - m)
    p_n = jnp.exp(s_n - m)
    l = jnp.sum(p, axis=-1, keepdims=True) + p_n
    pv = _bdot_nt(p, vt_ref[0]) if transposed else _bdot(p, vt_ref[0])
    o16 = (pv + p_n * vn) / l
    lse = m + jnp.log(l)
    lg = [lse[4 * g:4 * g + 4] for g in range(N_GROUPS)]
    mm = jnp.maximum(jnp.maximum(lg[0], lg[1]), lg[2])
    es = [jnp.exp(x - mm) for x in lg]
    inv = 1.0 / (es[0] + es[1] + es[2])
    acc = jnp.zeros((4, HW), F32)
    for g in range(N_GROUPS):
        acc = acc + (es[g] * inv) * (hrow[0:4] * o16[4 * g:4 * g + 4])
    o_ref[pl.ds(j, 1), :] = jnp.sum(acc, axis=0, keepdims=True)


def _step_attn(q, kn, kvn, kc, vc, bias, hrow16, *, transposed):
    bsz = q.shape[0]
    nb = STEP_ROWS
    rows = lambda w: pl.BlockSpec((nb, w), lambda i, j: (i, 0))
    cache = pl.BlockSpec((1,) + kc.shape[1:], lambda i, j: (i * nb + j, 0, 0))
    n_sel = N_GROUPS * BAND
    out_specs = [rows(HW)]
    out_shape = [jax.ShapeDtypeStruct((bsz, HW), F32)]
    scratch = []
    if transposed:
        cache_len = kc.shape[2]
        assert cache_len >= max(d for _, d in DIL_PAIRS) * BAND
        sel = pl.BlockSpec((1, n_sel, HW), lambda i, j: (i * nb + j, 0, 0))
        out_specs += [sel, sel]
        out_shape += [jax.ShapeDtypeStruct((bsz, n_sel, HW), F32)] * 2
        scratch = [pltpu.VMEM((2, cache_len, 128), F32)] * 2
    vmem = (4 * kc.shape[1] * kc.shape[2] * 4 + 8 * 16 * bias.shape[1] * 4
            + (6 * kc.shape[1] * kc.shape[2] * 4 if transposed else 0) + 4 * 1024 * 1024)
    out = pl.pallas_call(
        functools.partial(_step_attn_kernel, transposed=transposed, emit_sel=transposed),
        grid=(bsz // nb, nb),
        in_specs=[rows(N_GROUPS * HW), rows(HW),
                  pl.BlockSpec((nb, HW), lambda i, j: (i, 1)), cache, cache,
                  pl.BlockSpec(bias.shape, lambda i, j: (0, 0)),
                  pl.BlockSpec((16, HW), lambda i, j: (0, 0))],
        out_specs=out_specs,
        out_shape=out_shape,
        scratch_shapes=scratch,
        compiler_params=_cparams(("parallel", "arbitrary"), vmem),
        name="step_attn",
    )(q, kn, kvn, kc, vc, bias, hrow16)
    return out if transposed else out[0]


def _out_ffn_kernel(h_ref, tok_ref, mem_ref, wo1_ref, wo2_ref, g2_ref, wgu_ref, wdn_ref, *rest,
                    cast):
    if cast:
        ngu_i, ndn_i, o_ref, ngu_o, ndn_o = rest
        ngu_o[...] = ngu_i[...].astype(BF16)
        ndn_o[...] = ndn_i[...].astype(BF16)
    else:
        o_ref, = rest
    h1 = (h_ref[...]
          + jnp.dot(tok_ref[...].astype(BF16), wo1_ref[...], preferred_element_type=F32)
          + jnp.dot(mem_ref[...].astype(BF16), wo2_ref[...], preferred_element_type=F32))
    ms = jnp.mean(h1 * h1, axis=-1, keepdims=True)
    hn = (h1 * lax.rsqrt(ms + EPS) * g2_ref[...]).astype(BF16)
    gate = jnp.dot(hn, wgu_ref[:, :D_FF], preferred_element_type=F32)
    up = jnp.dot(hn, wgu_ref[:, D_FF:], preferred_element_type=F32)
    act = (_silu(gate) * up).astype(BF16)
    o_ref[...] = h1 + jnp.dot(act, wdn_ref[...], preferred_element_type=F32)


def _out_ffn(h, tok, mem, wo, woi, g2, li, wgu, wdn, *, tm, cast=None):
    n, d = h.shape
    wt = tok.shape[1]
    assert wo.shape[1] == wt + HW and wt % HW == 0
    once = pl.Buffered(1)
    vmem = ((wt + HW) * d * 2 + 3 * D_FF * d * 2
            + 2 * tm * (2 * d * 4 + (wt + HW) * 4) + 3 * tm * d * 4 + 3 * tm * D_FF * 4
            + 4 * 1024 * 1024)
    in_specs = [pl.BlockSpec((tm, d), lambda i: (i, 0)),
                pl.BlockSpec((tm, wt), lambda i: (i, 0)),
                pl.BlockSpec((tm, HW), lambda i: (i, 0)),
                pl.BlockSpec((None, wt, d), lambda i: (woi, 0, 0), pipeline_mode=once),
                pl.BlockSpec((None, HW, d), lambda i: (woi, wt // HW, 0), pipeline_mode=once),
                pl.BlockSpec((None, 1, d), lambda i: (li, 0, 0), pipeline_mode=once),
                pl.BlockSpec((d, 2 * D_FF), lambda i: (0, 0), pipeline_mode=once),
                pl.BlockSpec((D_FF, d), lambda i: (0, 0), pipeline_mode=once)]
    args = [h, tok, mem, wo, wo, g2, wgu, wdn]
    out_specs = [pl.BlockSpec((tm, d), lambda i: (i, 0))]
    out_shape = [jax.ShapeDtypeStruct((n, d), F32)]
    if cast is not None:
        wgu32, wdn32, cli = cast
        for w32 in (wgu32, wdn32):
            i_spec, o_spec, o_shape = _cast_specs(w32, cli, n // tm, lambda i: i)
            in_specs.append(i_spec)
            out_specs.append(o_spec)
            out_shape.append(o_shape)
            args.append(w32)
    out = pl.pallas_call(
        functools.partial(_out_ffn_kernel, cast=cast is not None),
        grid=(n // tm,),
        in_specs=in_specs,
        out_specs=out_specs,
        out_shape=out_shape,
        compiler_params=_cparams(("arbitrary" if cast is not None else "parallel",), vmem),
        name="out_ffn",
    )(*args)
    return out if cast is not None else out[0]


def _tail_transpose_kernel(x_ref, o_ref):
    o_ref[0] = x_ref[...].T


def _tail_transpose(x, col, *, batch, seq, keep, tr):
    per, nk = seq // tr, keep // tr
    return pl.pallas_call(
        _tail_transpose_kernel,
        grid=(batch, nk),
        in_specs=[pl.BlockSpec((tr, HW), lambda b, j: (b * per + per - nk + j, col))],
        out_specs=pl.BlockSpec((1, HW, tr), lambda b, j: (b, 0, j)),
        out_shape=jax.ShapeDtypeStruct((batch, HW, keep), F32),
        compiler_params=_cparams(("parallel", "parallel"), 16 * 1024 * 1024),
        name="tail_transpose",
    )(x)


def _pack_a_in(w):
    nl, d, _ = w.shape
    wt = jnp.swapaxes(w, 1, 2).astype(BF16)

    def pad_heads(x):
        x = x.reshape(nl, GLA_HEADS, GLA_DV, d)
        return jnp.pad(x, ((0, 0), (0, 0), (0, GLA_DVP - GLA_DV), (0, 0))).reshape(nl, GLA_VP, d)

    o_v, o_g, o_r = 2 * GLA_QK, 2 * GLA_QK + GLA_V, 2 * GLA_QK + 2 * GLA_V
    glr = jnp.pad(wt[:, o_r:o_r + GLA_RANK], ((0, 0), (0, GLA_RANKP - GLA_RANK), (0, 0)))
    packed = jnp.concatenate([wt[:, :o_v], pad_heads(wt[:, o_v:o_g]), pad_heads(wt[:, o_g:o_r]),
                              wt[:, o_r + GLA_RANK:], glr], axis=1)
    return jnp.swapaxes(packed, 1, 2)


def _pack_a_out(w):
    tok = w[:GLA_V].reshape(GLA_HEADS, GLA_DV, D_MODEL)
    tok = jnp.pad(tok, ((0, 0), (0, GLA_DVP - GLA_DV), (0, 0))).reshape(GLA_VP, D_MODEL)
    return jnp.concatenate([tok, w[GLA_V:]], axis=0).astype(BF16)


def kernel(x_prompt, x_sample, state_gla, cache_win_k, cache_win_v, cache_mem_k, cache_mem_v,
           mem_prompt, norm1, norm2, a_w_in, a_w_gate2, a_b_gate, a_g_onorm, a_w_out, kv_norm, w_kv,
           g_k, b_w_in, b_g_q, b_w_out, mem_norm, w_mem_kv, g_mem_q, g_mem_k, w_ffn_gu, w_ffn_down):
    batch, seq, d = x_prompt.shape
    dec_b = x_sample.shape[0]
    past_len = 8192
    assert d == D_MODEL and seq % 512 == 0 and x_sample.shape[1] == 1

    mall_np, lmask_np = _gla_tables()
    grp_np, hmask_np = _head_tables()
    mall = jnp.asarray(mall_np, BF16)
    lmask = jnp.asarray(lmask_np, F32)
    grp = jnp.asarray(grp_np, BF16)
    hmask = jnp.asarray(hmask_np, F32)
    band = jnp.asarray(_band_bias(), F32)

    hrow8 = jnp.asarray(np.concatenate([hmask_np[:, 0], np.zeros((4, HW), np.float32)], 0))
    hrow16 = jnp.asarray(np.concatenate([hmask_np[:, 0]] * N_GROUPS
                                        + [np.zeros((4, HW), np.float32)], 0))

    cos_p, sin_p = _rope_tables(np.arange(seq))
    cos_s, sin_s = _rope_tables(past_len + np.arange(1))

    wa_in = _pack_a_in(a_w_in)
    wa_out = jnp.stack([_pack_a_out(a_w_out[i]) for i in range(N_A)], 0)
    wg2 = jnp.pad(a_w_gate2, ((0, 0), (0, GLA_RANKP - GLA_RANK), (0, 0))).astype(BF16)
    bg = a_b_gate.reshape(N_A, 1, GLA_QK)
    gon = jnp.pad(a_g_onorm, ((0, 0), (0, GLA_DVP - GLA_DV))).reshape(N_A, 1, GLA_DVP)
    gq_mem = jnp.tile(g_mem_q, (1, N_HEADS)).reshape(DEPTH, 1, HW)
    hg_rope = jnp.tile(jnp.concatenate([b_g_q, g_k.reshape(1, HEAD_DIM)], 0),
                       (1, N_HEADS)).reshape(DEPTH - N_A + 1, 1, HW)
    h0_zero = jnp.zeros((batch, GLA_HEADS, GLA_DVP, GLA_DK), F32)
    wb_in = b_w_in.astype(BF16)
    wb_out = b_w_out.astype(BF16)
    w_kv_b = w_kv.astype(BF16).reshape(1, d, 2 * HW)
    w_mem_b = w_mem_kv.astype(BF16)
    ffn_w = {}
    n1 = norm1.reshape(DEPTH, 1, d)
    n2 = norm2.reshape(DEPTH, 1, d)
    nkv = kv_norm.reshape(1, 1, d)

    mem_kt_p, mem_vt_p = _mem_kv(mem_prompt, mem_norm, w_mem_b, g_mem_k, grp)

    cache_len = cache_win_k.shape[1]
    win_kt = jnp.transpose(cache_win_k, (0, 2, 3, 1)).reshape(dec_b, HW, cache_len)
    win_vt = jnp.transpose(cache_win_v, (0, 2, 3, 1)).reshape(dec_b, HW, cache_len)
    mem_kt_s = jnp.transpose(cache_mem_k, (0, 1, 3, 4, 2)).reshape(DEPTH, dec_b, HW, MEM_TOKENS)
    mem_vt_s = jnp.transpose(cache_mem_v, (0, 1, 3, 4, 2)).reshape(DEPTH, dec_b, HW, MEM_TOKENS)
    state_t = jnp.swapaxes(state_gla, 3, 4)
    step_bias = jnp.asarray(_step_bias(cache_len), F32)
    sel_bias = jnp.asarray(_sel_bias(), F32)

    def trunk(x2, bsz, t, prompt):
        n = bsz * t
        tm = 512 if prompt else n
        h = x2
        states = []
        state_out = None
        win_sel = None
        kv_sh = None
        tabs = (grp, cos_p, sin_p) if prompt else (grp, cos_s, sin_s)
        tr = ROPE_ROWS if prompt else n
        pos_blocks = max(t // tr, 1)
        for li in range(DEPTH):
            if li < N_A:
                if prompt:
                    tok, proj, st, wgu_b, wdn_b = _gla_prompt(
                        h, n1, wa_in, li, h0_zero, wg2, bg, gon, mall, lmask,
                        w_ffn_gu, w_ffn_down, batch=bsz, seq=t, tb=512)
                    ffn_w[li] = (wgu_b, wdn_b)
                    states.append(st[:, :, :GLA_DV])
                    mq_col = 0
                else:
                    proj = _norm_matmul(h, n1, li, wa_in, li, tm=tm, tn=A_IN_P // 3, name="a_in")
                    tok, state_out = _gla_step(proj, state_t, li, state_out, wg2, bg, gon)
                    mq_col = A_COL_MQ // HW
                wo, woi = wa_out, li
            else:
                bi = li - N_A
                proj = _norm_matmul_rope(h, n1, li, wb_in, bi, hg_rope, bi, tabs,
                                         rope_width=N_GROUPS * HW, tm=tr, name="b_in",
                                         pos_blocks=pos_blocks)
                mq_col = (N_GROUPS * HW) // HW
                if prompt:
                    tok = _band_attn(proj, kv_sh, band, hmask, batch=bsz, seq=t)
                else:
                    if win_sel is None:
                        tok, k_sel, v_sel = _step_attn(proj, kv_sh, kv_sh, win_kt, win_vt,
                                                       step_bias, hrow16, transposed=True)
                        win_sel = (k_sel, v_sel)
                    else:
                        tok = _step_attn(proj, kv_sh, kv_sh, *win_sel, sel_bias, hrow16,
                                         transposed=False)
                wo, woi = wb_out, bi
            if prompt:
                mem_o = _mem_attn(proj.reshape(bsz, t, proj.shape[1]), mq_col, mem_kt_p, mem_vt_p,
                                  li, gq_mem, grp, hmask, tq=512).reshape(n, HW)
            else:
                mem_o = _mem_step(proj, mq_col, mem_kt_s, mem_vt_s, li, gq_mem, grp, hrow8)
            if prompt and N_A - 1 <= li < DEPTH - 1:
                h, wgu_b, wdn_b = _out_ffn(h, tok, mem_o, wo, woi, n2, li, *ffn_w[li], tm=tm,
                                           cast=(w_ffn_gu, w_ffn_down, li + 1))
                ffn_w[li + 1] = (wgu_b, wdn_b)
            else:
                h = _out_ffn(h, tok, mem_o, wo, woi, n2, li, *ffn_w[li], tm=tm)
            if li == N_A - 1:
                kv_sh = _norm_matmul_rope(h, nkv, 0, w_kv_b, 0, hg_rope, DEPTH - N_A, tabs,
                                          rope_width=HW, tm=tr,
                                          name="shared_kv", pos_blocks=pos_blocks)
        return h, states, state_out, kv_sh

    y_p, gla_p, _, kv_p = trunk(x_prompt.reshape(batch * seq, d), batch, seq, True)
    y_s, _, gla_s, kv_s = trunk(x_sample.reshape(dec_b, d), dec_b, 1, False)

    keep = min(WIN_MAX, seq)

    def window_out(x, col):
        xt = _tail_transpose(x, col, batch=batch, seq=seq, keep=keep, tr=512)
        return jnp.transpose(xt.reshape(batch, N_HEADS, HEAD_DIM, keep), (0, 3, 1, 2))

    def mem_out(xt):
        return jnp.transpose(xt.reshape(DEPTH, batch, N_HEADS, HEAD_DIM, MEM_TOKENS),
                             (0, 1, 4, 2, 3))

    return (y_p.reshape(batch, seq, d),
            y_s.reshape(dec_b, 1, d),
            jnp.swapaxes(jnp.stack(gla_p, 0), 3, 4),
            jnp.swapaxes(gla_s, 3, 4),
            window_out(kv_p, 0),
            window_out(kv_p, 1),
            kv_s[:, :HW].reshape(dec_b, 1, N_HEADS, HEAD_DIM),
            kv_s[:, HW:].reshape(dec_b, 1, N_HEADS, HEAD_DIM),
            mem_out(mem_kt_p),
            mem_out(mem_vt_p))
```
